```python
import math
import jax
import jax.numpy as jnp
from jax import lax
import numpy as np

D_MODEL = 1024
BATCH = 16
SEQ = 256
DEPTH = 2
DEC_BATCH = 2
DEC_SEQ = 2048
PAST_LEN = 512

GRID_W = 64
EPS = 1e-6
NEG_BIG = -1e30

HY_WIDTH = D_MODEL // 2
HY_SHORT = 3
HY_EMB_DIM = 33
HY_BANDS = (HY_EMB_DIM - 1) // 2
HY_FILTER_HIDDEN = 64
HY_DECAY_TARGET = 1e-2
HY_MIN_DECAY = -math.log(HY_DECAY_TARGET) / 1.5
HY_MAX_DECAY = -math.log(HY_DECAY_TARGET) / 0.3

ATT_HEADS = 8
ATT_KV_HEADS = 2
ATT_GROUP = ATT_HEADS // ATT_KV_HEADS
HEAD_DIM = 64
ATT_WIDTH = ATT_HEADS * HEAD_DIM
KV_WIDTH = ATT_KV_HEADS * HEAD_DIM
ATT_SCALE = HEAD_DIM ** -0.5
WINDOW = 128
BLOCK = 128
ROPE_THETA = 10000.0
ROPE_FREQS = HEAD_DIM // 4
EVEN_IN = 3 * HY_WIDTH + ATT_WIDTH + 2 * KV_WIDTH
EVEN_MIX = HY_WIDTH + ATT_WIDTH

RET_HEADS = 4
RET_DK = D_MODEL // RET_HEADS
RET_DV = 2 * RET_DK
RET_CHUNK = 128
RET_QK_WIDTH = RET_HEADS * RET_DK
RET_V_WIDTH = RET_HEADS * RET_DV
ODD_IN = 2 * RET_QK_WIDTH + 3 * RET_V_WIDTH

N_EXPERTS = 32
TOP_K = 4
D_FF = D_MODEL
SWIGLU_ALPHA = 1.702
SWIGLU_LIMIT = 7.0

kernel_name = 'hybrid_diffusion_prefix_step'


def rms_norm(x, gain=None):
    xf = x.astype(jnp.float32)
    y = xf * lax.rsqrt(jnp.mean(xf * xf, axis=-1, keepdims=True) + EPS)
    if gain is not None:
        y = y * gain.astype(jnp.float32)
    return y.astype(x.dtype)


def adaln_params(cond, w, b):
    m = jax.nn.silu(cond) @ w + b
    return jnp.split(m[:, None, :], 6, axis=-1)


def modulate(x, shift, scale):
    return x * (1.0 + scale) + shift


def short_conv(u, w, b):
    l = u.shape[1]
    up = jnp.pad(u, ((0, 0), (1, 1), (0, 0)))
    return up[:, :l] * w[0] + up[:, 1:l + 1] * w[1] + up[:, 2:] * w[2] + b


def hyena_filter_taps(length, w1, b1, freq, w2, b2, w3, deltas):
    f32 = jnp.float32
    t = jnp.linspace(0.0, 1.0, length, dtype=f32)[:, None]
    w = 2.0 * math.pi * jnp.arange(length, dtype=f32)[:, None] / length
    f = jnp.linspace(1e-4, HY_BANDS - 1, HY_BANDS, dtype=f32)[None, :]
    z = jnp.concatenate([t, jnp.cos(f * w), -jnp.sin(f * w)], axis=-1)
    fr = freq.astype(f32)
    h = jnp.sin(fr * (z @ w1.astype(f32) + b1.astype(f32)))
    h = jnp.sin(fr * (h @ w2.astype(f32) + b2.astype(f32)))
    h = h @ w3.astype(f32)
    window = jnp.exp(-t * jnp.abs(deltas.astype(f32)))
    h_fwd = h[:, :HY_WIDTH] * window
    h_bwd = h[:, HY_WIDTH:] * window
    taps = jnp.concatenate([h_fwd, jnp.zeros((1, HY_WIDTH), f32), h_bwd[:0:-1]], axis=0)
    return taps / jnp.sum(jnp.abs(taps), axis=0, keepdims=True)


def hyena(x0, x1, v, taps, skip):
    l = v.shape[1]
    z = x1 * v
    zf = jnp.fft.rfft(z.astype(jnp.float32), n=2 * l, axis=1)
    tf = jnp.fft.rfft(taps, axis=0)
    conv = jnp.fft.irfft(zf * tf[None], n=2 * l, axis=1)[:, :l]
    return x0 * (conv.astype(z.dtype) + z * skip)


def axial_rope_tables(length):
    rows = length // GRID_W
    row = jnp.repeat(jnp.arange(rows, dtype=jnp.float32), GRID_W)
    col = jnp.tile(jnp.arange(GRID_W, dtype=jnp.float32), rows)
    inv = ROPE_THETA ** (-jnp.arange(ROPE_FREQS, dtype=jnp.float32) / ROPE_FREQS)
    ang = jnp.concatenate([row[:, None] * inv, col[:, None] * inv], axis=-1)
    return jnp.cos(ang), jnp.sin(ang)


def apply_axial_rope(x, cos, sin):
    n = ROPE_FREQS
    c_r, c_c = cos[:, None, :n], cos[:, None, n:]
    s_r, s_c = sin[:, None, :n], sin[:, None, n:]
    r1, r2, k1, k2 = jnp.split(x.astype(jnp.float32), 4, axis=-1)
    out = jnp.concatenate([r1 * c_r - r2 * s_r, r2 * c_r + r1 * s_r,
                           k1 * c_c - k2 * s_c, k2 * c_c + k1 * s_c], axis=-1)
    return out.astype(x.dtype)


def context_attention(q, k, v, sink):
    b, l = q.shape[:2]
    nq = l // BLOCK
    qb = q.reshape(b, nq, BLOCK, ATT_KV_HEADS, ATT_GROUP, HEAD_DIM).swapaxes(0, 1)
    sink_col = jnp.broadcast_to(sink.astype(jnp.float32).reshape(1, ATT_KV_HEADS, ATT_GROUP, 1, 1),
                                (b, ATT_KV_HEADS, ATT_GROUP, BLOCK, 1))

    def block_out(q_blk):
        s = jnp.einsum('bqkgd,bskd->bkgqs', q_blk, k).astype(jnp.float32) * ATT_SCALE
        p = jax.nn.softmax(jnp.concatenate([s, sink_col], axis=-1), axis=-1)[..., :-1]
        return jnp.einsum('bkgqs,bskd->bqkgd', p.astype(v.dtype), v)

    o = lax.map(block_out, qb)
    return o.swapaxes(0, 1).reshape(b, l, ATT_WIDTH)


def latent_attention(q, k, v, k_ctx, v_ctx, sink):
    b, l = q.shape[:2]
    nb = l // BLOCK
    span = 3 * BLOCK
    qb = q.reshape(b, nb, BLOCK, ATT_KV_HEADS, ATT_GROUP, HEAD_DIM)

    def band(t):
        tp = jnp.pad(t, ((0, 0), (BLOCK, BLOCK), (0, 0), (0, 0)))
        tp = tp.reshape(b, nb + 2, BLOCK, ATT_KV_HEADS, HEAD_DIM)
        return jnp.concatenate([tp[:, :-2], tp[:, 1:-1], tp[:, 2:]], axis=2)

    kw, vw = band(k), band(v)
    q_pos = jnp.arange(nb)[:, None] * BLOCK + jnp.arange(BLOCK)[None, :]
    k_pos = jnp.arange(nb)[:, None] * BLOCK - BLOCK + jnp.arange(span)[None, :]
    valid = ((jnp.abs(q_pos[:, :, None] - k_pos[:, None, :]) <= WINDOW)
             & (k_pos[:, None, :] >= 0) & (k_pos[:, None, :] < l))
    s_loc = jnp.einsum('bnqkgd,bnskd->bnkgqs', qb, kw).astype(jnp.float32) * ATT_SCALE
    s_loc = jnp.where(valid[None, :, None, None], s_loc, NEG_BIG)
    s_ctx = jnp.einsum('bnqkgd,bskd->bnkgqs', qb, k_ctx).astype(jnp.float32) * ATT_SCALE
    sink_col = jnp.broadcast_to(sink.astype(jnp.float32).reshape(1, 1, ATT_KV_HEADS, ATT_GROUP, 1, 1),
                                (b, nb, ATT_KV_HEADS, ATT_GROUP, BLOCK, 1))
    p = jax.nn.softmax(jnp.concatenate([s_loc, s_ctx, sink_col], axis=-1), axis=-1).astype(v.dtype)
    n_ctx = k_ctx.shape[1]
    o = (jnp.einsum('bnkgqs,bnskd->bnqkgd', p[..., :span], vw)
         + jnp.einsum('bnkgqs,bskd->bnqkgd', p[..., span:span + n_ctx], v_ctx))
    return o.reshape(b, l, ATT_WIDTH)


def even_project(h, w_in, conv_w, conv_b, q_gain, k_gain):
    b, l = h.shape[:2]
    u = h @ w_in
    hy, q, k, v = jnp.split(u, [3 * HY_WIDTH, 3 * HY_WIDTH + ATT_WIDTH,
                                3 * HY_WIDTH + ATT_WIDTH + KV_WIDTH], axis=-1)
    x0, x1, hv = jnp.split(short_conv(hy, conv_w, conv_b), 3, axis=-1)
    q = rms_norm(q.reshape(b, l, ATT_HEADS, HEAD_DIM), q_gain)
    k = rms_norm(k.reshape(b, l, ATT_KV_HEADS, HEAD_DIM), k_gain)
    v = v.reshape(b, l, ATT_KV_HEADS, HEAD_DIM)
    return x0, x1, hv, q, k, v


def even_mixer(h_ctx, h_lat, cache_k, cache_v, w_in, conv_w, conv_b, filt_w1, filt_b1, filt_freq,
               filt_w2, filt_b2, filt_w3, filt_deltas, hy_skip, q_gain, k_gain, sink, w_out):
    filt = (filt_w1, filt_b1, filt_freq, filt_w2, filt_b2, filt_w3, filt_deltas)
    x0, x1, hv, q, k_ctx, v_ctx = even_project(h_ctx, w_in, conv_w, conv_b, q_gain, k_gain)
    a = hyena(x0, x1, hv, hyena_filter_taps(h_ctx.shape[1], *filt), hy_skip)
    att = context_attention(q, k_ctx, v_ctx, sink)
    out_ctx = jnp.concatenate([a, att], axis=-1) @ w_out
    l_lat = h_lat.shape[1]
    x0, x1, hv, q, k, v = even_project(h_lat, w_in, conv_w, conv_b, q_gain, k_gain)
    cos, sin = axial_rope_tables(l_lat)
    q = apply_axial_rope(q, cos, sin)
    k = apply_axial_rope(k, cos, sin)
    a = hyena(x0, x1, hv, hyena_filter_taps(l_lat, *filt), hy_skip)
    att = latent_attention(q, k, v, cache_k, cache_v, sink)
    out_lat = jnp.concatenate([a, att], axis=-1) @ w_out
    return out_ctx, out_lat, k_ctx, v_ctx


def odd_project(h, w_in):
    b, l = h.shape[:2]
    u = h @ w_in
    q, k, v, g_f, g_b = jnp.split(u, [RET_QK_WIDTH, 2 * RET_QK_WIDTH, 2 * RET_QK_WIDTH + RET_V_WIDTH,
                                      2 * RET_QK_WIDTH + 2 * RET_V_WIDTH], axis=-1)
    q = q.reshape(b, l, RET_HEADS, RET_DK)
    k = k.reshape(b, l, RET_HEADS, RET_DK) * (RET_DK ** -0.5)
    v = v.reshape(b, l, RET_HEADS, RET_DV)
    return q, k, v, g_f, g_b


def retention_scan(q, k, v, log_gamma, s0):
    f32 = jnp.float32
    b, l = q.shape[:2]
    nc = l // RET_CHUNK
    idx = jnp.arange(RET_CHUNK, dtype=f32)
    lg = log_gamma.astype(f32)
    diff = idx[:, None] - idx[None, :]
    inner_decay = jnp.where(diff >= 0, jnp.exp(lg[:, None, None] * jnp.maximum(diff, 0.0)), 0.0)
    q_decay = jnp.exp(lg[None, :] * (idx[:, None] + 1.0))
    k_decay = jnp.exp(lg[:, None] * (RET_CHUNK - 1.0 - idx[None, :]))
    chunk_decay = jnp.exp(lg * RET_CHUNK)

    def chunks(t):
        return t.astype(f32).reshape(b, nc, RET_CHUNK, *t.shape[2:]).swapaxes(0, 1)

    def step(s, inp):
        qc, kc, vc = inp
        att = jnp.einsum('bihd,bjhd->bhij', qc, kc) * inner_decay
        o = (jnp.einsum('bhij,bjhe->bihe', att, vc)
             + jnp.einsum('bihd,bhde->bihe', qc, s) * q_decay[None, :, :, None])
        s = s * chunk_decay[None, :, None, None] + jnp.einsum('bjhd,hj,bjhe->bhde', kc, k_decay, vc)
        return s, o

    s_fin, o = lax.scan(step, s0.astype(f32), (chunks(q), chunks(k), chunks(v)))
    return o.swapaxes(0, 1).reshape(b, l, RET_HEADS, RET_DV).astype(v.dtype), s_fin


def bidir_retention(q, k, v, g_f, g_b, log_gamma, s0):
    b, l = q.shape[:2]
    o_f, s_f = retention_scan(q, k, v, log_gamma[0], s0[:, 0])
    o_b, s_b = retention_scan(q[:, ::-1], k[:, ::-1], v[:, ::-1], log_gamma[1], s0[:, 1])
    o_b = o_b[:, ::-1]
    y = (jax.nn.silu(g_f) * rms_norm(o_f).reshape(b, l, RET_V_WIDTH)
         + jax.nn.silu(g_b) * rms_norm(o_b).reshape(b, l, RET_V_WIDTH))
    return y, jnp.stack([s_f, s_b], axis=1)


def odd_mixer(h_ctx, h_lat, state, w_in, ret_decay_logit, w_out):
    log_gamma = -jnp.exp(ret_decay_logit.astype(jnp.float32))
    s_zero = jnp.zeros((h_ctx.shape[0], 2, RET_HEADS, RET_DK, RET_DV), jnp.float32)
    y_ctx, s_ctx = bidir_retention(*odd_project(h_ctx, w_in), log_gamma, s_zero)
    y_lat, _ = bidir_retention(*odd_project(h_lat, w_in), log_gamma, state)
    return y_ctx @ w_out, y_lat @ w_out, s_ctx


def moe_ffn(h, router_w, router_b, moe_w1, moe_b1, moe_w2, moe_b2):
    b, l, d = h.shape
    xt = h.reshape(b * l, d)
    logits = (xt @ router_w + router_b).astype(jnp.float32)
    top_val, top_idx = lax.top_k(logits, TOP_K)
    top_w = jax.nn.softmax(top_val, axis=-1)
    combine = jnp.sum(jax.nn.one_hot(top_idx, N_EXPERTS, dtype=jnp.float32) * top_w[..., None], axis=1)
    combine = combine.astype(xt.dtype)
    out = jnp.zeros_like(xt)
    for e in range(N_EXPERTS):
        hh = xt @ moe_w1[e] + moe_b1[e]
        glu, lin = jnp.split(hh, 2, axis=-1)
        glu = jnp.minimum(glu, SWIGLU_LIMIT)
        lin = jnp.clip(lin, -SWIGLU_LIMIT, SWIGLU_LIMIT)
        act = glu * jax.nn.sigmoid(SWIGLU_ALPHA * glu) * (lin + 1.0)
        out = out + combine[:, e:e + 1] * (act @ moe_w2[e] + moe_b2[e])
    return out.reshape(b, l, d)


def setup_inputs(seed: int = 0) -> dict:
    key = jax.random.key(seed)
    ks = iter(jax.random.split(key, 64))
    f32 = jnp.float32

    def nrm(shape, scale=1.0):
        return jax.random.normal(next(ks), shape, f32) * scale

    def gain(n):
        return 1.0 + nrm((n,), 0.05)

    inp = {}
    inp['x_prompt'] = nrm((BATCH, SEQ, D_MODEL))
    inp['x_sample'] = nrm((DEC_BATCH, DEC_SEQ, D_MODEL))
    inp['cache_k0'] = nrm((DEC_BATCH, PAST_LEN, ATT_KV_HEADS, HEAD_DIM))
    inp['cache_v0'] = nrm((DEC_BATCH, PAST_LEN, ATT_KV_HEADS, HEAD_DIM))
    inp['state_ret1'] = nrm((DEC_BATCH, 2, RET_HEADS, RET_DK, RET_DV), 0.5)
    inp['c'] = nrm((DEC_BATCH, D_MODEL))
    inp['c_ctx'] = nrm((D_MODEL,))

    def add_moe(prefix):
        inp[prefix + 'norm_ffn'] = gain(D_MODEL)
        inp[prefix + 'router_w'] = nrm((D_MODEL, N_EXPERTS), D_MODEL ** -0.5)
        inp[prefix + 'router_b'] = nrm((N_EXPERTS,), 0.01)
        inp[prefix + 'moe_w1'] = nrm((N_EXPERTS, D_MODEL, 2 * D_FF), D_MODEL ** -0.5)
        inp[prefix + 'moe_b1'] = nrm((N_EXPERTS, 2 * D_FF), 0.01)
        inp[prefix + 'moe_w2'] = nrm((N_EXPERTS, D_FF, D_MODEL), D_FF ** -0.5)
        inp[prefix + 'moe_b2'] = nrm((N_EXPERTS, D_MODEL), 0.01)

    inp['l0_norm_mix'] = gain(D_MODEL)
    inp['l0_ada_w'] = nrm((D_MODEL, 6 * D_MODEL), 0.3 * D_MODEL ** -0.5)
    inp['l0_ada_b'] = nrm((6 * D_MODEL,), 0.02)
    inp['l0_w_in'] = nrm((D_MODEL, EVEN_IN), D_MODEL ** -0.5)
    inp['l0_conv_w'] = nrm((HY_SHORT, 3 * HY_WIDTH), HY_SHORT ** -0.5)
    inp['l0_conv_b'] = nrm((3 * HY_WIDTH,), 0.02)
    inp['l0_filt_w1'] = nrm((HY_EMB_DIM, HY_FILTER_HIDDEN), HY_EMB_DIM ** -0.5)
    inp['l0_filt_b1'] = nrm((HY_FILTER_HIDDEN,), 0.1)
    inp['l0_filt_freq'] = 1.0 + nrm((HY_FILTER_HIDDEN,), 0.1)
    inp['l0_filt_w2'] = nrm((HY_FILTER_HIDDEN, HY_FILTER_HIDDEN), HY_FILTER_HIDDEN ** -0.5)
    inp['l0_filt_b2'] = nrm((HY_FILTER_HIDDEN,), 0.1)
    inp['l0_filt_w3'] = nrm((HY_FILTER_HIDDEN, 2 * HY_WIDTH), HY_FILTER_HIDDEN ** -0.5)
    inp['l0_filt_deltas'] = jax.random.uniform(next(ks), (HY_WIDTH,), f32, HY_MIN_DECAY, HY_MAX_DECAY)
    inp['l0_hy_skip'] = nrm((HY_WIDTH,))
    inp['l0_q_gain'] = gain(HEAD_DIM)
    inp['l0_k_gain'] = gain(HEAD_DIM)
    inp['l0_sink'] = nrm((ATT_HEADS,), 0.5)
    inp['l0_w_out'] = nrm((EVEN_MIX, D_MODEL), EVEN_MIX ** -0.5)
    add_moe('l0_')
    inp['l1_norm_mix'] = gain(D_MODEL)
    inp['l1_ada_w'] = nrm((D_MODEL, 6 * D_MODEL), 0.3 * D_MODEL ** -0.5)
    inp['l1_ada_b'] = nrm((6 * D_MODEL,), 0.02)
    inp['l1_w_in'] = nrm((D_MODEL, ODD_IN), D_MODEL ** -0.5)
    base = jnp.log(-jnp.log1p(-(2.0 ** (-5.0 - jnp.arange(RET_HEADS, dtype=f32)))))
    inp['l1_ret_decay_logit'] = base[None, :] + nrm((2, RET_HEADS), 0.1)
    inp['l1_w_out'] = nrm((RET_V_WIDTH, D_MODEL), RET_V_WIDTH ** -0.5)
    add_moe('l1_')
    return inp


def reference(x_prompt, x_sample, cache_k0, cache_v0, state_ret1, c, c_ctx,
              l0_norm_mix, l0_ada_w, l0_ada_b, l0_w_in, l0_conv_w, l0_conv_b,
              l0_filt_w1, l0_filt_b1, l0_filt_freq, l0_filt_w2, l0_filt_b2, l0_filt_w3, l0_filt_deltas,
              l0_hy_skip, l0_q_gain, l0_k_gain, l0_sink, l0_w_out,
              l0_norm_ffn, l0_router_w, l0_router_b, l0_moe_w1, l0_moe_b1, l0_moe_w2, l0_moe_b2,
              l1_norm_mix, l1_ada_w, l1_ada_b, l1_w_in, l1_ret_decay_logit, l1_w_out,
              l1_norm_ffn, l1_router_w, l1_router_b, l1_moe_w1, l1_moe_b1, l1_moe_w2, l1_moe_b2):
    layers = (
        dict(norm_mix=l0_norm_mix, ada_w=l0_ada_w, ada_b=l0_ada_b, norm_ffn=l0_norm_ffn,
             cache=dict(cache_k=cache_k0, cache_v=cache_v0),
             mix=dict(w_in=l0_w_in, conv_w=l0_conv_w, conv_b=l0_conv_b, filt_w1=l0_filt_w1,
                      filt_b1=l0_filt_b1, filt_freq=l0_filt_freq, filt_w2=l0_filt_w2, filt_b2=l0_filt_b2,
                      filt_w3=l0_filt_w3, filt_deltas=l0_filt_deltas, hy_skip=l0_hy_skip,
                      q_gain=l0_q_gain, k_gain=l0_k_gain, sink=l0_sink, w_out=l0_w_out),
             moe=dict(router_w=l0_router_w, router_b=l0_router_b, moe_w1=l0_moe_w1, moe_b1=l0_moe_b1,
                      moe_w2=l0_moe_w2, moe_b2=l0_moe_b2)),
        dict(norm_mix=l1_norm_mix, ada_w=l1_ada_w, ada_b=l1_ada_b, norm_ffn=l1_norm_ffn,
             cache=dict(state=state_ret1),
             mix=dict(w_in=l1_w_in, ret_decay_logit=l1_ret_decay_logit, w_out=l1_w_out),
             moe=dict(router_w=l1_router_w, router_b=l1_router_b, moe_w1=l1_moe_w1, moe_b1=l1_moe_b1,
                      moe_w2=l1_moe_w2, moe_b2=l1_moe_b2)),
    )
    x_ctx, x_lat = x_prompt, x_sample
    new_state = []
    for layer in range(DEPTH):
        p = layers[layer]
        mixer = even_mixer if layer % 2 == 0 else odd_mixer
        sh_m_c, sc_m_c, g_m_c, sh_f_c, sc_f_c, g_f_c = adaln_params(c_ctx[None, :], p['ada_w'], p['ada_b'])
        sh_m_l, sc_m_l, g_m_l, sh_f_l, sc_f_l, g_f_l = adaln_params(c, p['ada_w'], p['ada_b'])
        h_ctx = modulate(rms_norm(x_ctx, p['norm_mix']), sh_m_c, sc_m_c)
        h_lat = modulate(rms_norm(x_lat, p['norm_mix']), sh_m_l, sc_m_l)
        mix_ctx, mix_lat, *ctx_tensors = mixer(h_ctx, h_lat, **p['cache'], **p['mix'])
        new_state.extend(ctx_tensors)
        x_ctx = x_ctx + g_m_c * mix_ctx
        x_lat = x_lat + g_m_l * mix_lat
        x_ctx = x_ctx + g_f_c * moe_ffn(modulate(rms_norm(x_ctx, p['norm_ffn']), sh_f_c, sc_f_c), **p['moe'])
        x_lat = x_lat + g_f_l * moe_ffn(modulate(rms_norm(x_lat, p['norm_ffn']), sh_f_l, sc_f_l), **p['moe'])
    y_prompt, y_sample = x_ctx, x_lat
    new_k0, new_v0, new_state_ret1 = new_state
    return (y_prompt, y_sample, new_k0, new_v0, new_state_ret1)
```

```python
import functools
import math

import jax
import jax.numpy as jnp
from jax import lax
from jax.experimental import pallas as pl
from jax.experimental.pallas import tpu as pltpu

F32 = jnp.float32
BF16 = jnp.bfloat16
HIGHEST = lax.Precision.HIGHEST

D_MODEL = 1024
N_CTX_SEQ, L_CTX = 16, 256
N_LAT_SEQ, L_LAT = 2, 2048
N_CTX_TOK = N_CTX_SEQ * L_CTX
N_TOK = N_CTX_TOK + N_LAT_SEQ * L_LAT
PAST_LEN = 512
EPS = 1e-6
NEG_BIG = -1e30

HY_WIDTH = 512
HY_BANDS = 16
HY_FILTER_HIDDEN = 64
HY_FEAT_PAD = 64

ATT_HEADS, ATT_KV_HEADS, HEAD_DIM = 8, 2, 64
ATT_GROUP = ATT_HEADS // ATT_KV_HEADS
ATT_WIDTH = ATT_HEADS * HEAD_DIM
KV_WIDTH = ATT_KV_HEADS * HEAD_DIM
ATT_SCALE = HEAD_DIM ** -0.5
WINDOW = 128
ATT_BLOCK = 128
ROPE_THETA = 10000.0
ROPE_FREQS = HEAD_DIM // 4
GRID_W = 64
EVEN_IN = 3 * HY_WIDTH + ATT_WIDTH + 2 * KV_WIDTH

RET_HEADS = 4
RET_DK = 256
RET_DV = 512
RET_CHUNK = 128
RET_QK_WIDTH = RET_HEADS * RET_DK
RET_V_WIDTH = RET_HEADS * RET_DV
ODD_IN = 2 * RET_QK_WIDTH + 3 * RET_V_WIDTH

N_EXPERTS = 32
TOP_K = 4
D_FF = 1024
SWIGLU_ALPHA = 1.702
SWIGLU_LIMIT = 7.0

SUBLANES = 8
LANES = 128
TOK_CHUNKS = D_MODEL // LANES

MOE_TILE = 256
N_PAIRS = N_TOK * TOP_K
MOE_NUM_TILES = N_PAIRS // MOE_TILE + N_EXPERTS

ROW_TILE = 512


def _vmem(mib):
    return pltpu.CompilerParams(vmem_limit_bytes=mib * 1024 * 1024)


def _cond_of_tile(i, tm):
    row = i * tm
    return jnp.where(row < N_CTX_TOK, 0, 1 + (row - N_CTX_TOK) // L_LAT)


def _sigmoid(x):
    return 1.0 / (1.0 + jnp.exp(-x))


def _norm_mod(x, gain, shift, scale):
    ms = jnp.mean(x * x, axis=-1, keepdims=True)
    return (x * lax.rsqrt(ms + EPS) * gain) * (1.0 + scale) + shift


def _adaln_kernel(c_ref, w_ref, b_ref, o_ref):
    c = c_ref[...]
    s = c * _sigmoid(c)
    o_ref[...] = jnp.dot(s, w_ref[...], preferred_element_type=F32, precision=HIGHEST) + b_ref[...]


def _adaln(cond, w, b):
    n = w.shape[1]
    tn = 1024
    out = pl.pallas_call(
        _adaln_kernel,
        grid=(n // tn,),
        in_specs=[
            pl.BlockSpec((SUBLANES, D_MODEL), lambda j: (0, 0)),
            pl.BlockSpec((D_MODEL, tn), lambda j: (0, j)),
            pl.BlockSpec((1, tn), lambda j: (0, j)),
        ],
        out_specs=pl.BlockSpec((SUBLANES, tn), lambda j: (0, j)),
        out_shape=jax.ShapeDtypeStruct((SUBLANES, n), F32),
        name="adaln",
    )(cond, w, b.reshape(1, n))
    return out.reshape(SUBLANES, 6, D_MODEL)


def _in_proj_kernel(x_ref, gain_ref, mod_ref, w_ref, o_ref, wb_ref):
    @pl.when(pl.program_id(1) == 0)
    def _():
        wb_ref[...] = w_ref[...].astype(BF16)

    h = _norm_mod(x_ref[...], gain_ref[...], mod_ref[0, 0:1, :], mod_ref[0, 1:2, :])
    o_ref[...] = jnp.dot(h.astype(BF16), wb_ref[...], preferred_element_type=F32).astype(o_ref.dtype)


def _in_proj(x, gain, mod, w, tn, out_dtype):
    n = w.shape[1]
    tm = ROW_TILE
    return pl.pallas_call(
        _in_proj_kernel,
        grid=(n // tn, N_TOK // tm),
        in_specs=[
            pl.BlockSpec((tm, D_MODEL), lambda j, i: (i, 0)),
            pl.BlockSpec((1, D_MODEL), lambda j, i: (0, 0)),
            pl.BlockSpec((1, 6, D_MODEL), lambda j, i: (_cond_of_tile(i, tm), 0, 0)),
            pl.BlockSpec((D_MODEL, tn), lambda j, i: (0, j)),
        ],
        out_specs=pl.BlockSpec((tm, tn), lambda j, i: (i, j)),
        out_shape=jax.ShapeDtypeStruct((N_TOK, n), out_dtype),
        scratch_shapes=[pltpu.VMEM((D_MODEL, tn), BF16)],
        compiler_params=_vmem(48),
        name="in_proj",
    )(x, gain.reshape(1, D_MODEL), mod, w)


def _dft_matrices(length):
    n = 2 * length
    lo = 16
    s = jnp.arange(length, dtype=jnp.int32)
    k1 = jnp.arange(length // lo, dtype=jnp.int32) * lo
    k0 = jnp.arange(lo, dtype=jnp.int32)
    ang1 = (2.0 * math.pi / n) * ((k1[:, None] * s[None, :]) % n).astype(F32)
    ang0 = (2.0 * math.pi / n) * ((k0[:, None] * s[None, :]) % n).astype(F32)
    c1, s1 = jnp.cos(ang1)[:, None, :], jnp.sin(ang1)[:, None, :]
    c0, s0 = jnp.cos(ang0)[None, :, :], jnp.sin(ang0)[None, :, :]
    cmat = (c1 * c0 - s1 * s0).reshape(length, length)
    smat = (s1 * c0 + c1 * s0).reshape(length, length)
    sign = jnp.where(s % 2 == 0, 1.0, -1.0).astype(F32)
    row = lax.broadcasted_iota(jnp.int32, (length, length), 0)
    col = lax.broadcasted_iota(jnp.int32, (length, length), 1)
    s_nyq = jnp.where(row == 0, sign[None, :], smat)
    st_nyq = jnp.where(col == 0, sign[:, None], smat)
    return cmat.astype(BF16), s_nyq.astype(BF16), st_nyq.astype(BF16)


def _filter_features(length):
    t = jnp.linspace(0.0, 1.0, length, dtype=F32)[:, None]
    w = 2.0 * math.pi * jnp.arange(length, dtype=F32)[:, None] / length
    f = jnp.linspace(1e-4, HY_BANDS - 1, HY_BANDS, dtype=F32)[None, :]
    z = jnp.concatenate([t, jnp.cos(f * w), -jnp.sin(f * w)], axis=-1)
    return jnp.pad(z, ((0, 0), (0, HY_FEAT_PAD - z.shape[1])))


def _filter_kernel(z_ref, w1_ref, b1_ref, fr_ref, w2_ref, b2_ref, w3_ref, dl_ref, c_ref, s_ref,
                   tc_ref, ts_ref, taps_ref, *, length, rb):
    r = pl.program_id(0)

    @pl.when(r == 0)
    def _():
        z = z_ref[...]
        fr = fr_ref[...]
        h = jnp.sin(fr * (jnp.dot(z, w1_ref[...], preferred_element_type=F32, precision=HIGHEST) + b1_ref[...]))
        h = jnp.sin(fr * (jnp.dot(h, w2_ref[...], preferred_element_type=F32, precision=HIGHEST) + b2_ref[...]))
        h = jnp.dot(h, w3_ref[...], preferred_element_type=F32, precision=HIGHEST)
        win = jnp.exp(-z[:, 0:1] * jnp.abs(dl_ref[...]))
        hf = h[:, :HY_WIDTH] * win
        hb = h[:, HY_WIDTH:] * win
        row = lax.broadcasted_iota(jnp.int32, (length, HY_WIDTH), 0)
        hb = jnp.where(row == 0, 0.0, hb)
        l1 = jnp.sum(jnp.abs(hf), axis=0, keepdims=True) + jnp.sum(jnp.abs(hb), axis=0, keepdims=True)
        inv = 1.0 / l1
        taps_ref[:, :HY_WIDTH] = (hf * inv).astype(BF16)
        taps_ref[:, HY_WIDTH:] = (hb * inv).astype(BF16)

    taps = taps_ref[...]
    rc = jnp.dot(c_ref[...], taps, preferred_element_type=F32)
    rs = jnp.dot(s_ref[...], taps, preferred_element_type=F32)
    tc = rc[:, :HY_WIDTH] + rc[:, HY_WIDTH:]
    ts = rs[:, :HY_WIDTH] - rs[:, HY_WIDTH:]
    grow = r * rb + lax.broadcasted_iota(jnp.int32, (rb, HY_WIDTH), 0)
    is0 = grow == 0
    ts = jnp.where(is0, rs[:, :HY_WIDTH] + rs[:, HY_WIDTH:], ts)
    wgt = jnp.where(is0, 1.0 / (2 * length), 2.0 / (2 * length))
    tc_ref[...] = tc * wgt
    ts_ref[...] = ts * wgt


def _hyena_filter(length, feats, cmat, smat, w1, b1, freq, w2, b2, w3, deltas):
    rb = min(length, 512)
    w1p = jnp.pad(w1, ((0, HY_FEAT_PAD - w1.shape[0]), (0, 0)))
    full = lambda shape: pl.BlockSpec(shape, lambda r: (0,) * len(shape))
    hid = HY_FILTER_HIDDEN
    return pl.pallas_call(
        functools.partial(_filter_kernel, length=length, rb=rb),
        grid=(length // rb,),
        in_specs=[
            full((length, HY_FEAT_PAD)), full((HY_FEAT_PAD, hid)), full((1, hid)), full((1, hid)),
            full((hid, hid)), full((1, hid)), full((hid, 2 * HY_WIDTH)), full((1, HY_WIDTH)),
            pl.BlockSpec((rb, length), lambda r: (r, 0)),
            pl.BlockSpec((rb, length), lambda r: (r, 0)),
        ],
        out_specs=[pl.BlockSpec((rb, HY_WIDTH), lambda r: (r, 0)),
                   pl.BlockSpec((rb, HY_WIDTH), lambda r: (r, 0))],
        out_shape=[jax.ShapeDtypeStruct((length, HY_WIDTH), F32)] * 2,
        scratch_shapes=[pltpu.VMEM((length, 2 * HY_WIDTH), BF16)],
        compiler_params=_vmem(48),
        name=f"hyena_filter_{length}",
    )(feats, w1p, b1.reshape(1, hid), freq.reshape(1, hid), w2, b2.reshape(1, hid), w3,
      deltas.reshape(1, HY_WIDTH), cmat, smat)


def _hyena_kernel(x0_ref, x1_ref, v_ref, w0_ref, w1_ref, w2_ref, b0_ref, b1_ref, b2_ref, skip_ref,
                  tc_ref, ts_ref, c_ref, s_ref, ct_ref, st_ref, o_ref,
                  zb_ref, zs_ref, x0c_ref, acc_ref, *, nseq, length, cb, fb):
    f = pl.program_id(1)
    nf = pl.num_programs(1)

    def short_conv(u, w_ref, b_ref):
        row = lax.broadcasted_iota(jnp.int32, u.shape, 0)
        prev = jnp.where(row == 0, 0.0, pltpu.roll(u, 1, 0))
        nxt = jnp.where(row == length - 1, 0.0, pltpu.roll(u, length - 1, 0))
        return prev * w_ref[0:1, :] + u * w_ref[1:2, :] + nxt * w_ref[2:3, :] + b_ref[...]

    @pl.when(f == 0)
    def _():
        for b in range(nseq):
            cols = slice(b * cb, (b + 1) * cb)
            x0c_ref[:, cols] = short_conv(x0_ref[b], w0_ref, b0_ref)
            z = short_conv(x1_ref[b], w1_ref, b1_ref) * short_conv(v_ref[b], w2_ref, b2_ref)
            zb_ref[:, cols] = z.astype(BF16)
            zs_ref[:, cols] = z * skip_ref[...]
        acc_ref[...] = jnp.zeros_like(acc_ref)

    zb = zb_ref[...]
    zc = jnp.dot(c_ref[...], zb, preferred_element_type=F32)
    zsn = jnp.dot(s_ref[...], zb, preferred_element_type=F32)
    tc = jnp.concatenate([tc_ref[...]] * nseq, axis=1)
    ts = jnp.concatenate([ts_ref[...]] * nseq, axis=1)
    grow = f * fb + lax.broadcasted_iota(jnp.int32, zc.shape, 0)
    is0 = grow == 0
    yc = jnp.where(is0, zc * tc, zc * tc - zsn * ts)
    ys = jnp.where(is0, zsn * ts, zc * ts + zsn * tc)
    acc_ref[...] += (jnp.dot(ct_ref[...], yc.astype(BF16), preferred_element_type=F32)
                     + jnp.dot(st_ref[...], ys.astype(BF16), preferred_element_type=F32))

    @pl.when(f == nf - 1)
    def _():
        for b in range(nseq):
            cols = slice(b * cb, (b + 1) * cb)
            o_ref[b] = (x0c_ref[:, cols] * (acc_ref[:, cols] + zs_ref[:, cols])).astype(o_ref.dtype)


def _hyena(u, first_seq_block, nseq, length, conv_w, conv_b, skip, tc, ts, cmat, smat, stmat):
    cb = 128
    fb = min(length, 512)
    u3 = u.reshape(N_TOK // length, length, EVEN_IN)
    ncb = HY_WIDTH // cb
    width = nseq * cb

    def ublock(part):
        return pl.BlockSpec((nseq, length, cb), lambda c, f: (first_seq_block, 0, part * ncb + c))

    def wblock(part, rows):
        return pl.BlockSpec((rows, cb), lambda c, f: (0, part * ncb + c))

    return pl.pallas_call(
        functools.partial(_hyena_kernel, nseq=nseq, length=length, cb=cb, fb=fb),
        grid=(ncb, length // fb),
        in_specs=[
            ublock(0), ublock(1), ublock(2),
            wblock(0, 3), wblock(1, 3), wblock(2, 3),
            wblock(0, 1), wblock(1, 1), wblock(2, 1),
            pl.BlockSpec((1, cb), lambda c, f: (0, c)),
            pl.BlockSpec((fb, cb), lambda c, f: (f, c)),
            pl.BlockSpec((fb, cb), lambda c, f: (f, c)),
            pl.BlockSpec((fb, length), lambda c, f: (f, 0)),
            pl.BlockSpec((fb, length), lambda c, f: (f, 0)),
            pl.BlockSpec((length, fb), lambda c, f: (0, f)),
            pl.BlockSpec((length, fb), lambda c, f: (0, f)),
        ],
        out_specs=pl.BlockSpec((nseq, length, cb), lambda c, f: (0, 0, c)),
        out_shape=jax.ShapeDtypeStruct((nseq, length, HY_WIDTH), BF16),
        scratch_shapes=[pltpu.VMEM((length, width), BF16), pltpu.VMEM((length, width), F32),
                        pltpu.VMEM((length, width), F32), pltpu.VMEM((length, width), F32)],
        compiler_params=_vmem(48),
        name=f"hyena_{length}",
    )(u3, u3, u3, conv_w, conv_w, conv_w, conv_b.reshape(1, -1), conv_b.reshape(1, -1),
      conv_b.reshape(1, -1), skip.reshape(1, HY_WIDTH), tc, ts, cmat, smat, cmat, stmat
      ).reshape(nseq * length, HY_WIDTH)


def _head_rms(x, seg, gain):
    x2 = x * x
    hi = x2.astype(BF16)
    lo = (x2 - hi.astype(F32)).astype(BF16)
    ss = jnp.dot(hi, seg, preferred_element_type=F32) + jnp.dot(lo, seg, preferred_element_type=F32)
    return x * lax.rsqrt(ss * (1.0 / HEAD_DIM) + EPS) * gain


def _rope(x, cos, sin_signed):
    width = x.shape[1]
    lane = lax.broadcasted_iota(jnp.int32, x.shape, 1)
    first = (lane // ROPE_FREQS) % 2 == 0
    partner = jnp.where(first, pltpu.roll(x, width - ROPE_FREQS, 1), pltpu.roll(x, ROPE_FREQS, 1))
    return x * cos + partner * sin_signed


def _stack_heads(x, g):
    return jnp.concatenate(
        [x[:, (g * ATT_GROUP + j) * HEAD_DIM:(g * ATT_GROUP + j + 1) * HEAD_DIM] for j in range(ATT_GROUP)], axis=0)


def _sink_column(sink_ref, g, rows):
    return jnp.concatenate([jnp.full((rows, 1), sink_ref[g * ATT_GROUP + j], F32) for j in range(ATT_GROUP)], axis=0)


def _unstack_heads(outs, rows):
    return jnp.concatenate([outs[g][j * rows:(j + 1) * rows, :]
                            for g in range(ATT_KV_HEADS) for j in range(ATT_GROUP)], axis=1)


_DOT_NT = (((1,), (1,)), ((), ()))


def _ctx_attn_kernel(sink_ref, q_ref, k_ref, v_ref, seg_ref, qg_ref, kg_ref, o_ref, ko_ref, vo_ref):
    seg = seg_ref[...]
    qn = _head_rms(q_ref[...], seg, qg_ref[...]) * ATT_SCALE
    kn = _head_rms(k_ref[...], seg[:KV_WIDTH, :KV_WIDTH], kg_ref[...])
    v = v_ref[...]
    ko_ref[...] = kn
    vo_ref[...] = v
    outs = []
    for g in range(ATT_KV_HEADS):
        cols = slice(g * HEAD_DIM, (g + 1) * HEAD_DIM)
        q = _stack_heads(qn, g).astype(BF16)
        s = lax.dot_general(q, kn[:, cols].astype(BF16), _DOT_NT, preferred_element_type=F32)
        sink = _sink_column(sink_ref, g, L_CTX)
        m = jnp.maximum(jnp.max(s, axis=-1, keepdims=True), sink)
        p = jnp.exp(s - m)
        den = jnp.sum(p, axis=-1, keepdims=True) + jnp.exp(sink - m)
        o = jnp.dot(p.astype(BF16), v[:, cols].astype(BF16), preferred_element_type=F32)
        outs.append(o / den)
    o_ref[...] = _unstack_heads(outs, L_CTX).astype(o_ref.dtype)


def _ctx_attention(u, seg, q_gain, k_gain, sink):
    qcol = 3 * HY_WIDTH // ATT_WIDTH
    kcol = (3 * HY_WIDTH + ATT_WIDTH) // KV_WIDTH
    return pl.pallas_call(
        _ctx_attn_kernel,
        grid_spec=pltpu.PrefetchScalarGridSpec(
            num_scalar_prefetch=1,
            grid=(N_CTX_SEQ,),
            in_specs=[
                pl.BlockSpec((L_CTX, ATT_WIDTH), lambda b, s: (b, qcol)),
                pl.BlockSpec((L_CTX, KV_WIDTH), lambda b, s: (b, kcol)),
                pl.BlockSpec((L_CTX, KV_WIDTH), lambda b, s: (b, kcol + 1)),
                pl.BlockSpec((ATT_WIDTH, ATT_WIDTH), lambda b, s: (0, 0)),
                pl.BlockSpec((1, ATT_WIDTH), lambda b, s: (0, 0)),
                pl.BlockSpec((1, KV_WIDTH), lambda b, s: (0, 0)),
            ],
            out_specs=[
                pl.BlockSpec((L_CTX, ATT_WIDTH), lambda b, s: (b, 0)),
                pl.BlockSpec((L_CTX, KV_WIDTH), lambda b, s: (b, 0)),
                pl.BlockSpec((L_CTX, KV_WIDTH), lambda b, s: (b, 0)),
            ],
        ),
        out_shape=[jax.ShapeDtypeStruct((N_CTX_TOK, ATT_WIDTH), BF16),
                   jax.ShapeDtypeStruct((N_CTX_TOK, KV_WIDTH), F32),
                   jax.ShapeDtypeStruct((N_CTX_TOK, KV_WIDTH), F32)],
        name="ctx_attention",
    )(sink, u, u, u, seg, jnp.tile(q_gain, ATT_HEADS).reshape(1, ATT_WIDTH),
      jnp.tile(k_gain, ATT_KV_HEADS).reshape(1, KV_WIDTH))


def _rope_tables():
    pos = jnp.arange(L_LAT, dtype=jnp.int32)
    row = (pos // GRID_W).astype(F32)
    col = (pos % GRID_W).astype(F32)
    inv = ROPE_THETA ** (-jnp.arange(ROPE_FREQS, dtype=F32) / ROPE_FREQS)
    ar, ac = row[:, None] * inv, col[:, None] * inv
    cos = jnp.concatenate([jnp.cos(ar), jnp.cos(ar), jnp.cos(ac), jnp.cos(ac)], axis=-1)
    sin = jnp.concatenate([-jnp.sin(ar), jnp.sin(ar), -jnp.sin(ac), jnp.sin(ac)], axis=-1)
    return jnp.tile(cos, (1, ATT_HEADS)), jnp.tile(sin, (1, ATT_HEADS))


def _lat_attn_kernel(sink_ref, q_ref, k_ref, v_ref, ck_ref, cv_ref, cosq_ref, sinq_ref, cosk_ref, sink_k_ref,
                     seg_ref, qg_ref, kg_ref, o_ref, kn_ref):
    n = pl.program_id(1)
    seg = seg_ref[...]

    @pl.when(n == 0)
    def _():
        kn = _head_rms(k_ref[...], seg[:KV_WIDTH, :KV_WIDTH], kg_ref[...])
        kn_ref[...] = _rope(kn, cosk_ref[...], sink_k_ref[...]).astype(BF16)

    qn = _head_rms(q_ref[...], seg, qg_ref[...])
    qn = _rope(qn, cosq_ref[...], sinq_ref[...]) * ATT_SCALE
    span = 3 * ATT_BLOCK
    start = pl.multiple_of(jnp.clip((n - 1) * ATT_BLOCK, 0, L_LAT - span), ATT_BLOCK)
    rows = ATT_GROUP * ATT_BLOCK
    q_pos = n * ATT_BLOCK + (lax.broadcasted_iota(jnp.int32, (rows, span), 0) % ATT_BLOCK)
    k_pos = start + lax.broadcasted_iota(jnp.int32, (rows, span), 1)
    valid = jnp.abs(q_pos - k_pos) <= WINDOW
    k_loc = kn_ref[pl.ds(start, span), :]
    v_loc = v_ref[pl.ds(start, span), :].astype(BF16)
    k_ctx = ck_ref[0].astype(BF16)
    v_ctx = cv_ref[0].astype(BF16)
    outs = []
    for g in range(ATT_KV_HEADS):
        cols = slice(g * HEAD_DIM, (g + 1) * HEAD_DIM)
        q = _stack_heads(qn, g).astype(BF16)
        s_loc = lax.dot_general(q, k_loc[:, cols], _DOT_NT, preferred_element_type=F32)
        s_loc = jnp.where(valid, s_loc, NEG_BIG)
        s_ctx = lax.dot_general(q, k_ctx[:, cols], _DOT_NT, preferred_element_type=F32)
        sink = _sink_column(sink_ref, g, ATT_BLOCK)
        m = jnp.maximum(jnp.maximum(jnp.max(s_loc, axis=-1, keepdims=True),
                                    jnp.max(s_ctx, axis=-1, keepdims=True)), sink)
        p_loc = jnp.exp(s_loc - m)
        p_ctx = jnp.exp(s_ctx - m)
        den = (jnp.sum(p_loc, axis=-1, keepdims=True) + jnp.sum(p_ctx, axis=-1, keepdims=True)
               + jnp.exp(sink - m))
        o = (jnp.dot(p_loc.astype(BF16), v_loc[:, cols], preferred_element_type=F32)
             + jnp.dot(p_ctx.astype(BF16), v_ctx[:, cols], preferred_element_type=F32))
        outs.append(o / den)
    o_ref[...] = _unstack_heads(outs, ATT_BLOCK).astype(o_ref.dtype)


def _lat_attention(u, cache_k, cache_v, seg, q_gain, k_gain, sink):
    qcol = 3 * HY_WIDTH // ATT_WIDTH
    kcol = (3 * HY_WIDTH + ATT_WIDTH) // KV_WIDTH
    nblk = L_LAT // ATT_BLOCK
    first_q_block = N_CTX_TOK // ATT_BLOCK
    first_seq = N_CTX_TOK // L_LAT
    cos, sin = _rope_tables()
    return pl.pallas_call(
        _lat_attn_kernel,
        grid_spec=pltpu.PrefetchScalarGridSpec(
            num_scalar_prefetch=1,
            grid=(N_LAT_SEQ, nblk),
            in_specs=[
                pl.BlockSpec((ATT_BLOCK, ATT_WIDTH), lambda b, n, s: (first_q_block + b * nblk + n, qcol)),
                pl.BlockSpec((L_LAT, KV_WIDTH), lambda b, n, s: (first_seq + b, kcol)),
                pl.BlockSpec((L_LAT, KV_WIDTH), lambda b, n, s: (first_seq + b, kcol + 1)),
                pl.BlockSpec((1, PAST_LEN, KV_WIDTH), lambda b, n, s: (b, 0, 0)),
                pl.BlockSpec((1, PAST_LEN, KV_WIDTH), lambda b, n, s: (b, 0, 0)),
                pl.BlockSpec((ATT_BLOCK, ATT_WIDTH), lambda b, n, s: (n, 0)),
                pl.BlockSpec((ATT_BLOCK, ATT_WIDTH), lambda b, n, s: (n, 0)),
                pl.BlockSpec((L_LAT, KV_WIDTH), lambda b, n, s: (0, 0)),
                pl.BlockSpec((L_LAT, KV_WIDTH), lambda b, n, s: (0, 0)),
                pl.BlockSpec((ATT_WIDTH, ATT_WIDTH), lambda b, n, s: (0, 0)),
                pl.BlockSpec((1, ATT_WIDTH), lambda b, n, s: (0, 0)),
                pl.BlockSpec((1, KV_WIDTH), lambda b, n, s: (0, 0)),
            ],
            out_specs=pl.BlockSpec((ATT_BLOCK, ATT_WIDTH), lambda b, n, s: (b * nblk + n, 0)),
            scratch_shapes=[pltpu.VMEM((L_LAT, KV_WIDTH), BF16)],
        ),
        out_shape=jax.ShapeDtypeStruct((N_LAT_SEQ * L_LAT, ATT_WIDTH), BF16),
        name="lat_attention",
    )(sink, u, u, u, cache_k.reshape(N_LAT_SEQ, PAST_LEN, KV_WIDTH), cache_v.reshape(N_LAT_SEQ, PAST_LEN, KV_WIDTH),
      cos, sin, cos[:, :KV_WIDTH], sin[:, :KV_WIDTH], seg,
      jnp.tile(q_gain, ATT_HEADS).reshape(1, ATT_WIDTH), jnp.tile(k_gain, ATT_KV_HEADS).reshape(1, KV_WIDTH))


def _columns_to_lanes(cols):
    rows = cols[0].shape[0]
    lane = lax.broadcasted_iota(jnp.int32, (rows, len(cols)), 1)
    out = jnp.broadcast_to(cols[-1], (rows, len(cols)))
    for k in range(len(cols) - 1):
        out = jnp.where(lane == k, cols[k], out)
    return out


def _post_mixer(x, mix, wb, gain, mod_ref, rw_ref, rb_ref, x1_ref, xt_ref, idx_ref, tw_ref, tm):
    y = jnp.dot(mix, wb, preferred_element_type=F32)
    x1 = x + mod_ref[0, 2:3, :] * y
    x1_ref[...] = x1
    xt = _norm_mod(x1, gain, mod_ref[0, 3:4, :], mod_ref[0, 4:5, :])
    for s in range(TOK_CHUNKS):
        xt_ref[pl.ds(s, tm, stride=TOK_CHUNKS), :] = xt[:, s * LANES:(s + 1) * LANES]
    logits = jnp.dot(xt, rw_ref[...], preferred_element_type=F32, precision=HIGHEST) + rb_ref[...]
    lane = lax.broadcasted_iota(jnp.int32, logits.shape, 1)
    vals, idxs = [], []
    for _ in range(TOP_K):
        m = jnp.max(logits, axis=-1, keepdims=True)
        sel = jnp.min(jnp.where(logits == m, lane, N_EXPERTS), axis=-1, keepdims=True)
        vals.append(m)
        idxs.append(sel)
        logits = jnp.where(lane == sel, -jnp.inf, logits)
    es = [jnp.exp(v - vals[0]) for v in vals]
    den = es[0] + es[1] + es[2] + es[3]
    idx_ref[...] = _columns_to_lanes(idxs)
    tw_ref[...] = _columns_to_lanes([e / den for e in es])


def _post_even_kernel(x_ref, ac_ref, al_ref, tc_ref, tl_ref, w_ref, gain_ref, mod_ref, rw_ref, rb_ref,
                      x1_ref, xt_ref, idx_ref, tw_ref, wb_ref, *, tm):
    i = pl.program_id(0)

    @pl.when(i == 0)
    def _():
        wb_ref[...] = w_ref[...].astype(BF16)

    is_ctx = i * tm < N_CTX_TOK
    mix = jnp.concatenate([jnp.where(is_ctx, ac_ref[...], al_ref[...]),
                           jnp.where(is_ctx, tc_ref[...], tl_ref[...])], axis=1)
    _post_mixer(x_ref[...], mix, wb_ref[...], gain_ref[...], mod_ref, rw_ref, rb_ref,
                x1_ref, xt_ref, idx_ref, tw_ref, tm)


def _post_odd_kernel(x_ref, yc_ref, yl_ref, w_ref, gain_ref, mod_ref, rw_ref, rb_ref,
                     x1_ref, xt_ref, idx_ref, tw_ref, wb_ref, *, tm):
    i = pl.program_id(0)

    @pl.when(i == 0)
    def _():
        wb_ref[...] = w_ref[...].astype(BF16)

    mix = jnp.where(i * tm < N_CTX_TOK, yc_ref[...], yl_ref[...])
    _post_mixer(x_ref[...], mix, wb_ref[...], gain_ref[...], mod_ref, rw_ref, rb_ref,
                x1_ref, xt_ref, idx_ref, tw_ref, tm)


def _post_out_specs(tm):
    specs = [
        pl.BlockSpec((tm, D_MODEL), lambda i: (i, 0)),
        pl.BlockSpec((tm * TOK_CHUNKS, LANES), lambda i: (i, 0)),
        pl.BlockSpec((tm, TOP_K), lambda i: (i, 0)),
        pl.BlockSpec((tm, TOP_K), lambda i: (i, 0)),
    ]
    shapes = [
        jax.ShapeDtypeStruct((N_TOK, D_MODEL), F32),
        jax.ShapeDtypeStruct((N_TOK * TOK_CHUNKS, LANES), F32),
        jax.ShapeDtypeStruct((N_TOK, TOP_K), jnp.int32),
        jax.ShapeDtypeStruct((N_TOK, TOP_K), F32),
    ]
    return specs, shapes


def _post_tail_specs(k_in, tm):
    return [
        pl.BlockSpec((k_in, D_MODEL), lambda i: (0, 0)),
        pl.BlockSpec((1, D_MODEL), lambda i: (0, 0)),
        pl.BlockSpec((1, 6, D_MODEL), lambda i: (_cond_of_tile(i, tm), 0, 0)),
        pl.BlockSpec((D_MODEL, N_EXPERTS), lambda i: (0, 0)),
        pl.BlockSpec((1, N_EXPERTS), lambda i: (0, 0)),
    ]


def _post_even(x, a_ctx, a_lat, t_ctx, t_lat, w_out, gain, mod, router_w, router_b):
    tm = ROW_TILE
    nctx = N_CTX_TOK // tm
    ctx_map = lambda i: (jnp.minimum(i, nctx - 1), 0)
    lat_map = lambda i: (jnp.maximum(i - nctx, 0), 0)
    out_specs, out_shapes = _post_out_specs(tm)
    return pl.pallas_call(
        functools.partial(_post_even_kernel, tm=tm),
        grid=(N_TOK // tm,),
        in_specs=[
            pl.BlockSpec((tm, D_MODEL), lambda i: (i, 0)),
            pl.BlockSpec((tm, HY_WIDTH), ctx_map), pl.BlockSpec((tm, HY_WIDTH), lat_map),
            pl.BlockSpec((tm, ATT_WIDTH), ctx_map), pl.BlockSpec((tm, ATT_WIDTH), lat_map),
        ] + _post_tail_specs(HY_WIDTH + ATT_WIDTH, tm),
        out_specs=out_specs,
        out_shape=out_shapes,
        scratch_shapes=[pltpu.VMEM((HY_WIDTH + ATT_WIDTH, D_MODEL), BF16)],
        compiler_params=_vmem(48),
        name="post_even",
    )(x, a_ctx, a_lat, t_ctx, t_lat, w_out, gain.reshape(1, D_MODEL), mod, router_w,
      router_b.reshape(1, N_EXPERTS))


def _post_odd(x, y_ctx, y_lat, w_out, gain, mod, router_w, router_b):
    tm = ROW_TILE
    nctx = N_CTX_TOK // tm
    out_specs, out_shapes = _post_out_specs(tm)
    return pl.pallas_call(
        functools.partial(_post_odd_kernel, tm=tm),
        grid=(N_TOK // tm,),
        in_specs=[
            pl.BlockSpec((tm, D_MODEL), lambda i: (i, 0)),
            pl.BlockSpec((tm, RET_V_WIDTH), lambda i: (jnp.minimum(i, nctx - 1), 0)),
            pl.BlockSpec((tm, RET_V_WIDTH), lambda i: (jnp.maximum(i - nctx, 0), 0)),
        ] + _post_tail_specs(RET_V_WIDTH, tm),
        out_specs=out_specs,
        out_shape=out_shapes,
        scratch_shapes=[pltpu.VMEM((RET_V_WIDTH, D_MODEL), BF16)],
        compiler_params=_vmem(48),
        name="post_odd",
    )(x, y_ctx, y_lat, w_out, gain.reshape(1, D_MODEL), mod, router_w, router_b.reshape(1, N_EXPERTS))


def _moe_plan(top_idx):
    t = MOE_TILE
    e_flat = top_idx.reshape(N_PAIRS)
    order = jnp.argsort(e_flat, stable=True).astype(jnp.int32)
    counts = jnp.sum(jax.nn.one_hot(e_flat, N_EXPERTS, dtype=jnp.int32), axis=0)
    offsets = jnp.cumsum(counts) - counts
    ntiles = (counts + t - 1) // t
    tile_end = jnp.cumsum(ntiles)
    tile_base = tile_end - ntiles
    total = tile_end[-1]
    i = jnp.arange(MOE_NUM_TILES, dtype=jnp.int32)
    last_valid = jnp.searchsorted(tile_end, total - 1, side="right").astype(jnp.int32)
    e_of = jnp.minimum(jnp.searchsorted(tile_end, i, side="right").astype(jnp.int32), N_EXPERTS - 1)
    e_of = jnp.where(i < total, e_of, last_valid)
    j = i - tile_base[e_of]
    start = jnp.where(i < total, offsets[e_of] + j * t, 0)
    count = jnp.where(i < total, jnp.clip(counts[e_of] - j * t, 0, t), 0)
    dest = (order % TOP_K) * N_TOK + order // TOP_K
    return e_of.astype(jnp.int32), start.astype(jnp.int32), count.astype(jnp.int32), dest.astype(jnp.int32)


def _moe_kernel(te_ref, ts_ref, tcnt_ref, dest_ref, xt_hbm, w1_ref, b1_ref, w2_ref, b2_ref, y_hbm,
                xbuf, obuf, w1b, w2b, sem_in, sem_out):
    i = pl.program_id(0)
    nt = pl.num_programs(0)
    t = MOE_TILE
    slot = i % 2
    rows = TOK_CHUNKS

    def gather_copy(tile, r, sl):
        p = jnp.minimum(ts_ref[tile] + r, N_PAIRS - 1)
        tok = dest_ref[p] % N_TOK
        return pltpu.make_async_copy(xt_hbm.at[pl.ds(tok * rows, rows), :],
                                     xbuf.at[sl, pl.ds(r * rows, rows), :], sem_in.at[sl])

    def scatter_copy(tile, r, sl):
        d = dest_ref[ts_ref[tile] + r]
        return pltpu.make_async_copy(obuf.at[sl, pl.ds(r * rows, rows), :],
                                     y_hbm.at[pl.ds(d * rows, rows), :], sem_out.at[sl])

    def start_gather(tile, sl):
        def body(r, c):
            gather_copy(tile, r, sl).start()
            return c
        lax.fori_loop(0, t, body, 0)

    def wait_scatter(tile, sl):
        def body(r, c):
            scatter_copy(tile, r, sl).wait()
            return c
        lax.fori_loop(0, tcnt_ref[tile], body, 0)

    @pl.when(i == 0)
    def _():
        start_gather(0, 0)

    @pl.when(jnp.logical_and(i + 1 < nt, tcnt_ref[jnp.minimum(i + 1, nt - 1)] > 0))
    def _():
        start_gather(i + 1, 1 - slot)

    @pl.when(i >= 2)
    def _():
        wait_scatter(i - 2, slot)

    cnt = tcnt_ref[i]

    @pl.when(jnp.logical_or(cnt > 0, i == 0))
    def _():
        def wbody(r, c):
            gather_copy(i, r, slot).wait()
            return c
        lax.fori_loop(0, t, wbody, 0)

    @pl.when(cnt > 0)
    def _():
        first = jnp.logical_or(i == 0, te_ref[i] != te_ref[jnp.maximum(i - 1, 0)])

        @pl.when(first)
        def _():
            w1b[...] = w1_ref[0].astype(BF16)
            w2b[...] = w2_ref[0].astype(BF16)

        xs = jnp.concatenate([xbuf[slot, pl.ds(s, t, stride=rows), :] for s in range(rows)], axis=1)
        h = jnp.dot(xs.astype(BF16), w1b[...], preferred_element_type=F32) + b1_ref[0]
        glu = jnp.minimum(h[:, :D_FF], SWIGLU_LIMIT)
        lin = jnp.clip(h[:, D_FF:], -SWIGLU_LIMIT, SWIGLU_LIMIT)
        act = glu * _sigmoid(SWIGLU_ALPHA * glu) * (lin + 1.0)
        y = jnp.dot(act.astype(BF16), w2b[...], preferred_element_type=F32) + b2_ref[0]
        for s in range(rows):
            obuf[slot, pl.ds(s, t, stride=rows), :] = y[:, s * LANES:(s + 1) * LANES]

        def sbody(r, c):
            scatter_copy(i, r, slot).start()
            return c
        lax.fori_loop(0, cnt, sbody, 0)

    @pl.when(i == nt - 1)
    def _():
        @pl.when(i >= 1)
        def _():
            wait_scatter(i - 1, 1 - slot)
        wait_scatter(i, slot)


def _moe_experts(xt_tiles, plan, w1, b1, w2, b2):
    te, ts, tcnt, dest = plan
    t = MOE_TILE
    return pl.pallas_call(
        _moe_kernel,
        grid_spec=pltpu.PrefetchScalarGridSpec(
            num_scalar_prefetch=4,
            grid=(MOE_NUM_TILES,),
            in_specs=[
                pl.BlockSpec(memory_space=pl.ANY),
                pl.BlockSpec((1, D_MODEL, 2 * D_FF), lambda i, te, ts, tc, d: (te[i], 0, 0)),
                pl.BlockSpec((1, 1, 2 * D_FF), lambda i, te, ts, tc, d: (te[i], 0, 0)),
                pl.BlockSpec((1, D_FF, D_MODEL), lambda i, te, ts, tc, d: (te[i], 0, 0)),
                pl.BlockSpec((1, 1, D_MODEL), lambda i, te, ts, tc, d: (te[i], 0, 0)),
            ],
            out_specs=pl.BlockSpec(memory_space=pl.ANY),
            scratch_shapes=[
                pltpu.VMEM((2, t * TOK_CHUNKS, LANES), F32),
                pltpu.VMEM((2, t * TOK_CHUNKS, LANES), F32),
                pltpu.VMEM((D_MODEL, 2 * D_FF), BF16),
                pltpu.VMEM((D_FF, D_MODEL), BF16),
                pltpu.SemaphoreType.DMA((2,)),
                pltpu.SemaphoreType.DMA((2,)),
            ],
        ),
        out_shape=jax.ShapeDtypeStruct((TOP_K * N_TOK * TOK_CHUNKS, LANES), F32),
        compiler_params=_vmem(56),
        name="moe_experts",
    )(te, ts, tcnt, dest, xt_tiles, w1, b1.reshape(N_EXPERTS, 1, 2 * D_FF), w2,
      b2.reshape(N_EXPERTS, 1, D_MODEL))


def _combine_value(x1_ref, y_ref, tw_ref, mod_ref, tm):
    tw = tw_ref[...]
    chunks = []
    for s in range(TOK_CHUNKS):
        acc = tw[:, 0:1] * y_ref[0, pl.ds(s, tm, stride=TOK_CHUNKS), :]
        for k in range(1, TOP_K):
            acc = acc + tw[:, k:k + 1] * y_ref[k, pl.ds(s, tm, stride=TOK_CHUNKS), :]
        chunks.append(acc)
    return x1_ref[...] + mod_ref[0, 5:6, :] * jnp.concatenate(chunks, axis=1)


def _combine_kernel(x1_ref, y_ref, tw_ref, mod_ref, o_ref, *, tm):
    o_ref[...] = _combine_value(x1_ref, y_ref, tw_ref, mod_ref, tm)


def _combine_split_kernel(x1_ref, y_ref, tw_ref, mod_ref, oc_ref, ol_ref, *, tm):
    val = _combine_value(x1_ref, y_ref, tw_ref, mod_ref, tm)
    is_ctx = pl.program_id(0) * tm < N_CTX_TOK

    @pl.when(is_ctx)
    def _():
        oc_ref[...] = val

    @pl.when(jnp.logical_not(is_ctx))
    def _():
        ol_ref[...] = val


def _combine(x1, y_tiles, top_w, mod, split):
    tm = 256
    nctx = N_CTX_TOK // tm
    y4 = y_tiles.reshape(TOP_K, N_TOK * TOK_CHUNKS, LANES)
    in_specs = [
        pl.BlockSpec((tm, D_MODEL), lambda i: (i, 0)),
        pl.BlockSpec((TOP_K, tm * TOK_CHUNKS, LANES), lambda i: (0, i, 0)),
        pl.BlockSpec((tm, TOP_K), lambda i: (i, 0)),
        pl.BlockSpec((1, 6, D_MODEL), lambda i: (_cond_of_tile(i, tm), 0, 0)),
    ]
    if not split:
        return pl.pallas_call(
            functools.partial(_combine_kernel, tm=tm),
            grid=(N_TOK // tm,),
            in_specs=in_specs,
            out_specs=pl.BlockSpec((tm, D_MODEL), lambda i: (i, 0)),
            out_shape=jax.ShapeDtypeStruct((N_TOK, D_MODEL), F32),
            name="moe_combine",
        )(x1, y4, top_w, mod)
    return pl.pallas_call(
        functools.partial(_combine_split_kernel, tm=tm),
        grid=(N_TOK // tm,),
        in_specs=in_specs,
        out_specs=[pl.BlockSpec((tm, D_MODEL), lambda i: (jnp.minimum(i, nctx - 1), 0)),
                   pl.BlockSpec((tm, D_MODEL), lambda i: (jnp.maximum(i - nctx, 0), 0))],
        out_shape=[jax.ShapeDtypeStruct((N_CTX_TOK, D_MODEL), F32),
                   jax.ShapeDtypeStruct((N_TOK - N_CTX_TOK, D_MODEL), F32)],
        name="moe_combine_split",
    )(x1, y4, top_w, mod)


def _retention_kernel(lg_ref, q_ref, k_ref, v_ref, gf_ref, gb_ref, *rest, length, has_s0, emit_state):
    rest = list(rest)
    s0_ref = rest.pop(0) if has_s0 else None
    o_ref = rest.pop(0)
    so_ref = rest.pop(0) if emit_state else None
    s_ref, yf_ref = rest
    c = RET_CHUNK
    nc = length // c
    ii = lax.broadcasted_iota(jnp.int32, (c, c), 0).astype(F32)
    jj = lax.broadcasted_iota(jnp.int32, (c, c), 1).astype(F32)
    ci = lax.broadcasted_iota(jnp.int32, (c, 1), 0).astype(F32)

    def scan(direction):
        lg = -jnp.exp(lg_ref[direction, 0])
        lg1 = lg[:, 0:1]
        if direction == 0:
            diff = ii - jj
            q_decay = jnp.exp(lg1 * (ci + 1.0))
            k_decay = jnp.exp(lg1 * (c - 1.0 - ci))
        else:
            diff = jj - ii
            q_decay = jnp.exp(lg1 * (c - ci))
            k_decay = jnp.exp(lg1 * ci)
        inner = jnp.where(diff >= 0, jnp.exp(lg * jnp.maximum(diff, 0.0)), 0.0)
        chunk_decay = jnp.exp(lg1 * float(c))
        if has_s0:
            s_ref[...] = s0_ref[0, direction, 0]
        else:
            s_ref[...] = jnp.zeros_like(s_ref)

        def body(step, carry):
            ch = step if direction == 0 else nc - 1 - step
            rows = pl.ds(pl.multiple_of(ch * c, c), c)
            qc = q_ref[rows, :]
            kc = k_ref[rows, :].astype(F32) * (RET_DK ** -0.5)
            vc = v_ref[rows, :]
            s = s_ref[...]
            att = lax.dot_general(qc, kc.astype(BF16), _DOT_NT, preferred_element_type=F32) * inner
            o = (jnp.dot(att.astype(BF16), vc, preferred_element_type=F32)
                 + jnp.dot(qc, s.astype(BF16), preferred_element_type=F32) * q_decay)
            kd = (kc * k_decay).T.astype(BF16)
            s_ref[...] = s * chunk_decay + jnp.dot(kd, vc, preferred_element_type=F32)
            on = o * lax.rsqrt(jnp.mean(o * o, axis=-1, keepdims=True) + EPS)
            if direction == 0:
                g = gf_ref[rows, :].astype(F32)
                yf_ref[rows, :] = g * _sigmoid(g) * on
            else:
                g = gb_ref[rows, :].astype(F32)
                o_ref[rows, :] = (yf_ref[rows, :] + g * _sigmoid(g) * on).astype(o_ref.dtype)
            return carry

        lax.fori_loop(0, nc, body, 0)
        if emit_state:
            so_ref[0, direction, 0] = s_ref[...]

    scan(0)
    scan(1)


def _retention(u, first_seq, nseq, length, decay_logit, s0, emit_state):
    row0 = first_seq
    lg = jnp.broadcast_to(decay_logit.astype(F32)[:, :, None, None], (2, RET_HEADS, 1, LANES))
    kcol = RET_QK_WIDTH // RET_DK
    vcol = 2 * RET_QK_WIDTH // RET_DV
    gfcol = vcol + RET_HEADS
    gbcol = gfcol + RET_HEADS
    in_specs = [
        pl.BlockSpec((2, 1, 1, LANES), lambda b, h: (0, h, 0, 0)),
        pl.BlockSpec((length, RET_DK), lambda b, h: (row0 + b, h)),
        pl.BlockSpec((length, RET_DK), lambda b, h: (row0 + b, kcol + h)),
        pl.BlockSpec((length, RET_DV), lambda b, h: (row0 + b, vcol + h)),
        pl.BlockSpec((length, RET_DV), lambda b, h: (row0 + b, gfcol + h)),
        pl.BlockSpec((length, RET_DV), lambda b, h: (row0 + b, gbcol + h)),
    ]
    args = [lg, u, u, u, u, u]
    state_spec = pl.BlockSpec((1, 2, 1, RET_DK, RET_DV), lambda b, h: (b, 0, h, 0, 0))
    if s0 is not None:
        in_specs.append(state_spec)
        args.append(s0)
    out_specs = [pl.BlockSpec((length, RET_DV), lambda b, h: (b, h))]
    out_shape = [jax.ShapeDtypeStruct((nseq * length, RET_V_WIDTH), BF16)]
    if emit_state:
        out_specs.append(state_spec)
        out_shape.append(jax.ShapeDtypeStruct((nseq, 2, RET_HEADS, RET_DK, RET_DV), F32))
    return pl.pallas_call(
        functools.partial(_retention_kernel, length=length, has_s0=s0 is not None, emit_state=emit_state),
        grid=(nseq, RET_HEADS),
        in_specs=in_specs,
        out_specs=out_specs,
        out_shape=out_shape,
        scratch_shapes=[pltpu.VMEM((RET_DK, RET_DV), F32), pltpu.VMEM((length, RET_DV), F32)],
        compiler_params=_vmem(48),
        name=f"retention_{length}",
    )(*args)


def kernel(x_prompt, x_sample, cache_k0, cache_v0, state_ret1, c, c_ctx, l0_norm_mix, l0_ada_w, l0_ada_b, l0_w_in, l0_conv_w, l0_conv_b, l0_filt_w1, l0_filt_b1, l0_filt_freq, l0_filt_w2, l0_filt_b2, l0_filt_w3, l0_filt_deltas, l0_hy_skip, l0_q_gain, l0_k_gain, l0_sink, l0_w_out, l0_norm_ffn, l0_router_w, l0_router_b, l0_moe_w1, l0_moe_b1, l0_moe_w2, l0_moe_b2, l1_norm_mix, l1_ada_w, l1_ada_b, l1_w_in, l1_ret_decay_logit, l1_w_out, l1_norm_ffn, l1_router_w, l1_router_b, l1_moe_w1, l1_moe_b1, l1_moe_w2, l1_moe_b2):
    x = jnp.concatenate([x_prompt.reshape(N_CTX_TOK, D_MODEL), x_sample.reshape(N_TOK - N_CTX_TOK, D_MODEL)], axis=0)
    cond = jnp.zeros((SUBLANES, D_MODEL), F32).at[0].set(c_ctx).at[1:1 + N_LAT_SEQ].set(c)
    mod0 = _adaln(cond, l0_ada_w, l0_ada_b)
    mod1 = _adaln(cond, l1_ada_w, l1_ada_b)

    u = _in_proj(x, l0_norm_mix, mod0, l0_w_in, EVEN_IN // 2, F32)
    filt = (l0_filt_w1, l0_filt_b1, l0_filt_freq, l0_filt_w2, l0_filt_b2, l0_filt_w3, l0_filt_deltas)
    hy = []
    for first_block, nseq, length in ((0, N_CTX_SEQ, L_CTX), (N_CTX_TOK // L_LAT // N_LAT_SEQ, N_LAT_SEQ, L_LAT)):
        cmat, smat, stmat = _dft_matrices(length)
        tc, ts = _hyena_filter(length, _filter_features(length), cmat, smat, *filt)
        hy.append(_hyena(u, first_block, nseq, length, l0_conv_w, l0_conv_b, l0_hy_skip, tc, ts, cmat, smat, stmat))
    head = lax.broadcasted_iota(jnp.int32, (ATT_WIDTH, ATT_WIDTH), 0) // HEAD_DIM
    seg = (head == head.T).astype(BF16)
    att_ctx, new_k, new_v = _ctx_attention(u, seg, l0_q_gain, l0_k_gain, l0_sink)
    att_lat = _lat_attention(u, cache_k0, cache_v0, seg, l0_q_gain, l0_k_gain, l0_sink)
    x1, xt, top_idx, top_w = _post_even(x, hy[0], hy[1], att_ctx, att_lat, l0_w_out, l0_norm_ffn, mod0,
                                        l0_router_w, l0_router_b)
    y = _moe_experts(xt, _moe_plan(top_idx), l0_moe_w1, l0_moe_b1, l0_moe_w2, l0_moe_b2)
    x = _combine(x1, y, top_w, mod0, split=False)

    u = _in_proj(x, l1_norm_mix, mod1, l1_w_in, 2048, BF16)
    y_ctx, new_state = _retention(u, 0, N_CTX_SEQ, L_CTX, l1_ret_decay_logit, None, True)
    (y_lat,) = _retention(u, N_CTX_TOK // L_LAT, N_LAT_SEQ, L_LAT, l1_ret_decay_logit, state_ret1, False)
    x1, xt, top_idx, top_w = _post_odd(x, y_ctx, y_lat, l1_w_out, l1_norm_ffn, mod1, l1_router_w, l1_router_b)
    y = _moe_experts(xt, _moe_plan(top_idx), l1_moe_w1, l1_moe_b1, l1_moe_w2, l1_moe_b2)
    y_prompt, y_sample = _combine(x1, y, top_w, mod1, split=True)

    return (y_prompt.reshape(N_CTX_SEQ, L_CTX, D_MODEL), y_sample.reshape(N_LAT_SEQ, L_LAT, D_MODEL),
            new_k.reshape(N_CTX_SEQ, L_CTX, ATT_KV_HEADS, HEAD_DIM),
            new_v.reshape(N_CTX_SEQ, L_CTX, ATT_KV_HEADS, HEAD_DIM), new_state)
```

```python
import functools
import math

import jax
import jax.numpy as jnp
from jax import lax
from jax.experimental import pallas as pl
from jax.experimental.pallas import tpu as pltpu

F32 = jnp.float32
BF16 = jnp.bfloat16
HIGHEST = lax.Precision.HIGHEST

D_MODEL = 1024
N_CTX_SEQ, L_CTX = 16, 256
N_LAT_SEQ, L_LAT = 2, 2048
N_CTX_TOK = N_CTX_SEQ * L_CTX
N_TOK = N_CTX_TOK + N_LAT_SEQ * L_LAT
PAST_LEN = 512
EPS = 1e-6
NEG_BIG = -1e30

HY_WIDTH = 512
HY_BANDS = 16
HY_FILTER_HIDDEN = 64
HY_FEAT_PAD = 64

ATT_HEADS, ATT_KV_HEADS, HEAD_DIM = 8, 2, 64
ATT_GROUP = ATT_HEADS // ATT_KV_HEADS
ATT_WIDTH = ATT_HEADS * HEAD_DIM
KV_WIDTH = ATT_KV_HEADS * HEAD_DIM
ATT_SCALE = HEAD_DIM ** -0.5
WINDOW = 128
ATT_BLOCK = 128
ROPE_THETA = 10000.0
ROPE_FREQS = HEAD_DIM // 4
GRID_W = 64
EVEN_IN = 3 * HY_WIDTH + ATT_WIDTH + 2 * KV_WIDTH

RET_HEADS = 4
RET_DK = 256
RET_DV = 512
RET_CHUNK = 128
RET_QK_WIDTH = RET_HEADS * RET_DK
RET_V_WIDTH = RET_HEADS * RET_DV
ODD_IN = 2 * RET_QK_WIDTH + 3 * RET_V_WIDTH

N_EXPERTS = 32
TOP_K = 4
D_FF = 1024
SWIGLU_ALPHA = 1.702
SWIGLU_LIMIT = 7.0

SUBLANES = 8
LANES = 128
TOK_CHUNKS = D_MODEL // LANES

MOE_TILE = 256
N_PAIRS = N_TOK * TOP_K
MOE_NUM_TILES = N_PAIRS // MOE_TILE + N_EXPERTS
MOE_TRASH_ROW = N_PAIRS
MOE_OUT_ROWS = N_PAIRS + 2 * MOE_TILE

ROW_TILE = 512


def _vmem(mib):
    return pltpu.CompilerParams(vmem_limit_bytes=mib * 1024 * 1024)


def _cond_of_tile(i, tm):
    row = i * tm
    return jnp.where(row < N_CTX_TOK, 0, 1 + (row - N_CTX_TOK) // L_LAT)


def _sigmoid(x):
    return 1.0 / (1.0 + jnp.exp(-x))


def _norm_mod(x, gain, shift, scale):
    ms = jnp.mean(x * x, axis=-1, keepdims=True)
    return (x * lax.rsqrt(ms + EPS) * gain) * (1.0 + scale) + shift


def _adaln_kernel(c_ref, w_ref, b_ref, o_ref):
    c = c_ref[...]
    s = c * _sigmoid(c)
    o_ref[...] = jnp.dot(s, w_ref[...], preferred_element_type=F32, precision=HIGHEST) + b_ref[...]


def _adaln(cond, w, b):
    n = w.shape[1]
    tn = 1024
    out = pl.pallas_call(
        _adaln_kernel,
        grid=(n // tn,),
        in_specs=[
            pl.BlockSpec((SUBLANES, D_MODEL), lambda j: (0, 0)),
            pl.BlockSpec((D_MODEL, tn), lambda j: (0, j)),
            pl.BlockSpec((1, tn), lambda j: (0, j)),
        ],
        out_specs=pl.BlockSpec((SUBLANES, tn), lambda j: (0, j)),
        out_shape=jax.ShapeDtypeStruct((SUBLANES, n), F32),
        name="adaln",
    )(cond, w, b.reshape(1, n))
    return out.reshape(SUBLANES, 6, D_MODEL)


def _in_proj_kernel(x_ref, gain_ref, mod_ref, w_ref, o_ref, wb_ref):
    @pl.when(pl.program_id(1) == 0)
    def _():
        wb_ref[...] = w_ref[...].astype(BF16)

    h = _norm_mod(x_ref[...], gain_ref[...], mod_ref[0, 0:1, :], mod_ref[0, 1:2, :])
    o_ref[...] = jnp.dot(h.astype(BF16), wb_ref[...], preferred_element_type=F32).astype(o_ref.dtype)


def _in_proj(x, gain, mod, w, tn, out_dtype):
    n = w.shape[1]
    tm = ROW_TILE
    return pl.pallas_call(
        _in_proj_kernel,
        grid=(n // tn, N_TOK // tm),
        in_specs=[
            pl.BlockSpec((tm, D_MODEL), lambda j, i: (i, 0)),
            pl.BlockSpec((1, D_MODEL), lambda j, i: (0, 0)),
            pl.BlockSpec((1, 6, D_MODEL), lambda j, i: (_cond_of_tile(i, tm), 0, 0)),
            pl.BlockSpec((D_MODEL, tn), lambda j, i: (0, j)),
        ],
        out_specs=pl.BlockSpec((tm, tn), lambda j, i: (i, j)),
        out_shape=jax.ShapeDtypeStruct((N_TOK, n), out_dtype),
        scratch_shapes=[pltpu.VMEM((D_MODEL, tn), BF16)],
        compiler_params=_vmem(48),
        name="in_proj",
    )(x, gain.reshape(1, D_MODEL), mod, w)


def _dft_matrices(length):
    n = 2 * length
    lo = 16
    s = jnp.arange(length, dtype=jnp.int32)
    k1 = jnp.arange(length // lo, dtype=jnp.int32) * lo
    k0 = jnp.arange(lo, dtype=jnp.int32)
    ang1 = (2.0 * math.pi / n) * ((k1[:, None] * s[None, :]) % n).astype(F32)
    ang0 = (2.0 * math.pi / n) * ((k0[:, None] * s[None, :]) % n).astype(F32)
    c1, s1 = jnp.cos(ang1)[:, None, :], jnp.sin(ang1)[:, None, :]
    c0, s0 = jnp.cos(ang0)[None, :, :], jnp.sin(ang0)[None, :, :]
    cmat = (c1 * c0 - s1 * s0).reshape(length, length)
    smat = (s1 * c0 + c1 * s0).reshape(length, length)
    sign = jnp.where(s % 2 == 0, 1.0, -1.0).astype(F32)
    row = lax.broadcasted_iota(jnp.int32, (length, length), 0)
    col = lax.broadcasted_iota(jnp.int32, (length, length), 1)
    s_nyq = jnp.where(row == 0, sign[None, :], smat)
    st_nyq = jnp.where(col == 0, sign[:, None], smat)
    return cmat.astype(BF16), s_nyq.astype(BF16), st_nyq.astype(BF16)


def _filter_features(length):
    t = jnp.linspace(0.0, 1.0, length, dtype=F32)[:, None]
    w = 2.0 * math.pi * jnp.arange(length, dtype=F32)[:, None] / length
    f = jnp.linspace(1e-4, HY_BANDS - 1, HY_BANDS, dtype=F32)[None, :]
    z = jnp.concatenate([t, jnp.cos(f * w), -jnp.sin(f * w)], axis=-1)
    return jnp.pad(z, ((0, 0), (0, HY_FEAT_PAD - z.shape[1])))


def _filter_kernel(z_ref, w1_ref, b1_ref, fr_ref, w2_ref, b2_ref, w3_ref, dl_ref, c_ref, s_ref,
                   tc_ref, ts_ref, taps_ref, *, length, rb):
    r = pl.program_id(0)

    @pl.when(r == 0)
    def _():
        z = z_ref[...]
        fr = fr_ref[...]
        h = jnp.sin(fr * (jnp.dot(z, w1_ref[...], preferred_element_type=F32, precision=HIGHEST) + b1_ref[...]))
        h = jnp.sin(fr * (jnp.dot(h, w2_ref[...], preferred_element_type=F32, precision=HIGHEST) + b2_ref[...]))
        h = jnp.dot(h, w3_ref[...], preferred_element_type=F32, precision=HIGHEST)
        win = jnp.exp(-z[:, 0:1] * jnp.abs(dl_ref[...]))
        hf = h[:, :HY_WIDTH] * win
        hb = h[:, HY_WIDTH:] * win
        row = lax.broadcasted_iota(jnp.int32, (length, HY_WIDTH), 0)
        hb = jnp.where(row == 0, 0.0, hb)
        l1 = jnp.sum(jnp.abs(hf), axis=0, keepdims=True) + jnp.sum(jnp.abs(hb), axis=0, keepdims=True)
        inv = 1.0 / l1
        taps_ref[:, :HY_WIDTH] = (hf * inv).astype(BF16)
        taps_ref[:, HY_WIDTH:] = (hb * inv).astype(BF16)

    taps = taps_ref[...]
    rc = jnp.dot(c_ref[...], taps, preferred_element_type=F32)
    rs = jnp.dot(s_ref[...], taps, preferred_element_type=F32)
    tc = rc[:, :HY_WIDTH] + rc[:, HY_WIDTH:]
    ts = rs[:, :HY_WIDTH] - rs[:, HY_WIDTH:]
    grow = r * rb + lax.broadcasted_iota(jnp.int32, (rb, HY_WIDTH), 0)
    is0 = grow == 0
    ts = jnp.where(is0, rs[:, :HY_WIDTH] + rs[:, HY_WIDTH:], ts)
    wgt = jnp.where(is0, 1.0 / (2 * length), 2.0 / (2 * length))
    tc_ref[...] = tc * wgt
    ts_ref[...] = ts * wgt


def _hyena_filter(length, feats, cmat, smat, w1, b1, freq, w2, b2, w3, deltas):
    rb = min(length, 512)
    w1p = jnp.pad(w1, ((0, HY_FEAT_PAD - w1.shape[0]), (0, 0)))
    full = lambda shape: pl.BlockSpec(shape, lambda r: (0,) * len(shape))
    hid = HY_FILTER_HIDDEN
    return pl.pallas_call(
        functools.partial(_filter_kernel, length=length, rb=rb),
        grid=(length // rb,),
        in_specs=[
            full((length, HY_FEAT_PAD)), full((HY_FEAT_PAD, hid)), full((1, hid)), full((1, hid)),
            full((hid, hid)), full((1, hid)), full((hid, 2 * HY_WIDTH)), full((1, HY_WIDTH)),
            pl.BlockSpec((rb, length), lambda r: (r, 0)),
            pl.BlockSpec((rb, length), lambda r: (r, 0)),
        ],
        out_specs=[pl.BlockSpec((rb, HY_WIDTH), lambda r: (r, 0)),
                   pl.BlockSpec((rb, HY_WIDTH), lambda r: (r, 0))],
        out_shape=[jax.ShapeDtypeStruct((length, HY_WIDTH), F32)] * 2,
        scratch_shapes=[pltpu.VMEM((length, 2 * HY_WIDTH), BF16)],
        compiler_params=_vmem(48),
        name=f"hyena_filter_{length}",
    )(feats, w1p, b1.reshape(1, hid), freq.reshape(1, hid), w2, b2.reshape(1, hid), w3,
      deltas.reshape(1, HY_WIDTH), cmat, smat)


def _hyena_kernel(x0_ref, x1_ref, v_ref, w0_ref, w1_ref, w2_ref, b0_ref, b1_ref, b2_ref, skip_ref,
                  tc_ref, ts_ref, c_ref, s_ref, ct_ref, st_ref, o_ref,
                  zb_ref, zs_ref, x0c_ref, acc_ref, *, nseq, length, cb, fb):
    f = pl.program_id(1)
    nf = pl.num_programs(1)

    def short_conv(u, w_ref, b_ref):
        row = lax.broadcasted_iota(jnp.int32, u.shape, 0)
        prev = jnp.where(row == 0, 0.0, pltpu.roll(u, 1, 0))
        nxt = jnp.where(row == length - 1, 0.0, pltpu.roll(u, length - 1, 0))
        return prev * w_ref[0:1, :] + u * w_ref[1:2, :] + nxt * w_ref[2:3, :] + b_ref[...]

    @pl.when(f == 0)
    def _():
        for b in range(nseq):
            cols = slice(b * cb, (b + 1) * cb)
            x0c_ref[:, cols] = short_conv(x0_ref[b], w0_ref, b0_ref)
            z = short_conv(x1_ref[b], w1_ref, b1_ref) * short_conv(v_ref[b], w2_ref, b2_ref)
            zb_ref[:, cols] = z.astype(BF16)
            zs_ref[:, cols] = z * skip_ref[...]
        acc_ref[...] = jnp.zeros_like(acc_ref)

    zb = zb_ref[...]
    zc = jnp.dot(c_ref[...], zb, preferred_element_type=F32)
    zsn = jnp.dot(s_ref[...], zb, preferred_element_type=F32)
    tc = jnp.concatenate([tc_ref[...]] * nseq, axis=1)
    ts = jnp.concatenate([ts_ref[...]] * nseq, axis=1)
    grow = f * fb + lax.broadcasted_iota(jnp.int32, zc.shape, 0)
    is0 = grow == 0
    yc = jnp.where(is0, zc * tc, zc * tc - zsn * ts)
    ys = jnp.where(is0, zsn * ts, zc * ts + zsn * tc)
    acc_ref[...] += (jnp.dot(ct_ref[...], yc.astype(BF16), preferred_element_type=F32)
                     + jnp.dot(st_ref[...], ys.astype(BF16), preferred_element_type=F32))

    @pl.when(f == nf - 1)
    def _():
        for b in range(nseq):
            cols = slice(b * cb, (b + 1) * cb)
            o_ref[b] = (x0c_ref[:, cols] * (acc_ref[:, cols] + zs_ref[:, cols])).astype(o_ref.dtype)


def _hyena(u, first_seq_block, nseq, length, conv_w, conv_b, skip, tc, ts, cmat, smat, stmat):
    cb = 128
    fb = min(length, 512)
    u3 = u.reshape(N_TOK // length, length, EVEN_IN)
    ncb = HY_WIDTH // cb
    width = nseq * cb

    def ublock(part):
        return pl.BlockSpec((nseq, length, cb), lambda c, f: (first_seq_block, 0, part * ncb + c))

    def wblock(part, rows):
        return pl.BlockSpec((rows, cb), lambda c, f: (0, part * ncb + c))

    return pl.pallas_call(
        functools.partial(_hyena_kernel, nseq=nseq, length=length, cb=cb, fb=fb),
        grid=(ncb, length // fb),
        in_specs=[
            ublock(0), ublock(1), ublock(2),
            wblock(0, 3), wblock(1, 3), wblock(2, 3),
            wblock(0, 1), wblock(1, 1), wblock(2, 1),
            pl.BlockSpec((1, cb), lambda c, f: (0, c)),
            pl.BlockSpec((fb, cb), lambda c, f: (f, c)),
            pl.BlockSpec((fb, cb), lambda c, f: (f, c)),
            pl.BlockSpec((fb, length), lambda c, f: (f, 0)),
            pl.BlockSpec((fb, length), lambda c, f: (f, 0)),
            pl.BlockSpec((length, fb), lambda c, f: (0, f)),
            pl.BlockSpec((length, fb), lambda c, f: (0, f)),
        ],
        out_specs=pl.BlockSpec((nseq, length, cb), lambda c, f: (0, 0, c)),
        out_shape=jax.ShapeDtypeStruct((nseq, length, HY_WIDTH), BF16),
        scratch_shapes=[pltpu.VMEM((length, width), BF16), pltpu.VMEM((length, width), F32),
                        pltpu.VMEM((length, width), F32), pltpu.VMEM((length, width), F32)],
        compiler_params=_vmem(48),
        name=f"hyena_{length}",
    )(u3, u3, u3, conv_w, conv_w, conv_w, conv_b.reshape(1, -1), conv_b.reshape(1, -1),
      conv_b.reshape(1, -1), skip.reshape(1, HY_WIDTH), tc, ts, cmat, smat, cmat, stmat
      ).reshape(nseq * length, HY_WIDTH)


def _head_rms(x, seg, gain):
    x2 = x * x
    hi = x2.astype(BF16)
    lo = (x2 - hi.astype(F32)).astype(BF16)
    ss = jnp.dot(hi, seg, preferred_element_type=F32) + jnp.dot(lo, seg, preferred_element_type=F32)
    return x * lax.rsqrt(ss * (1.0 / HEAD_DIM) + EPS) * gain


def _rope(x, cos, sin_signed):
    width = x.shape[1]
    lane = lax.broadcasted_iota(jnp.int32, x.shape, 1)
    first = (lane // ROPE_FREQS) % 2 == 0
    partner = jnp.where(first, pltpu.roll(x, width - ROPE_FREQS, 1), pltpu.roll(x, ROPE_FREQS, 1))
    return x * cos + partner * sin_signed


def _stack_heads(x, g):
    return jnp.concatenate(
        [x[:, (g * ATT_GROUP + j) * HEAD_DIM:(g * ATT_GROUP + j + 1) * HEAD_DIM] for j in range(ATT_GROUP)], axis=0)


def _sink_column(sink_ref, g, rows):
    return jnp.concatenate([jnp.full((rows, 1), sink_ref[g * ATT_GROUP + j], F32) for j in range(ATT_GROUP)], axis=0)


def _unstack_heads(outs, rows):
    return jnp.concatenate([outs[g][j * rows:(j + 1) * rows, :]
                            for g in range(ATT_KV_HEADS) for j in range(ATT_GROUP)], axis=1)


_DOT_NT = (((1,), (1,)), ((), ()))


def _ctx_attn_kernel(sink_ref, q_ref, k_ref, v_ref, seg_ref, qg_ref, kg_ref, o_ref, ko_ref, vo_ref):
    seg = seg_ref[...]
    qn = _head_rms(q_ref[...], seg, qg_ref[...]) * ATT_SCALE
    kn = _head_rms(k_ref[...], seg[:KV_WIDTH, :KV_WIDTH], kg_ref[...])
    v = v_ref[...]
    ko_ref[...] = kn
    vo_ref[...] = v
    outs = []
    for g in range(ATT_KV_HEADS):
        cols = slice(g * HEAD_DIM, (g + 1) * HEAD_DIM)
        q = _stack_heads(qn, g).astype(BF16)
        s = lax.dot_general(q, kn[:, cols].astype(BF16), _DOT_NT, preferred_element_type=F32)
        sink = _sink_column(sink_ref, g, L_CTX)
        m = jnp.maximum(jnp.max(s, axis=-1, keepdims=True), sink)
        p = jnp.exp(s - m)
        den = jnp.sum(p, axis=-1, keepdims=True) + jnp.exp(sink - m)
        o = jnp.dot(p.astype(BF16), v[:, cols].astype(BF16), preferred_element_type=F32)
        outs.append(o / den)
    o_ref[...] = _unstack_heads(outs, L_CTX).astype(o_ref.dtype)


def _ctx_attention(u, seg, q_gain, k_gain, sink):
    qcol = 3 * HY_WIDTH // ATT_WIDTH
    kcol = (3 * HY_WIDTH + ATT_WIDTH) // KV_WIDTH
    return pl.pallas_call(
        _ctx_attn_kernel,
        grid_spec=pltpu.PrefetchScalarGridSpec(
            num_scalar_prefetch=1,
            grid=(N_CTX_SEQ,),
            in_specs=[
                pl.BlockSpec((L_CTX, ATT_WIDTH), lambda b, s: (b, qcol)),
                pl.BlockSpec((L_CTX, KV_WIDTH), lambda b, s: (b, kcol)),
                pl.BlockSpec((L_CTX, KV_WIDTH), lambda b, s: (b, kcol + 1)),
                pl.BlockSpec((ATT_WIDTH, ATT_WIDTH), lambda b, s: (0, 0)),
                pl.BlockSpec((1, ATT_WIDTH), lambda b, s: (0, 0)),
                pl.BlockSpec((1, KV_WIDTH), lambda b, s: (0, 0)),
            ],
            out_specs=[
                pl.BlockSpec((L_CTX, ATT_WIDTH), lambda b, s: (b, 0)),
                pl.BlockSpec((L_CTX, KV_WIDTH), lambda b, s: (b, 0)),
                pl.BlockSpec((L_CTX, KV_WIDTH), lambda b, s: (b, 0)),
            ],
        ),
        out_shape=[jax.ShapeDtypeStruct((N_CTX_TOK, ATT_WIDTH), BF16),
                   jax.ShapeDtypeStruct((N_CTX_TOK, KV_WIDTH), F32),
                   jax.ShapeDtypeStruct((N_CTX_TOK, KV_WIDTH), F32)],
        name="ctx_attention",
    )(sink, u, u, u, seg, jnp.tile(q_gain, ATT_HEADS).reshape(1, ATT_WIDTH),
      jnp.tile(k_gain, ATT_KV_HEADS).reshape(1, KV_WIDTH))


def _rope_tables():
    pos = jnp.arange(L_LAT, dtype=jnp.int32)
    row = (pos // GRID_W).astype(F32)
    col = (pos % GRID_W).astype(F32)
    inv = ROPE_THETA ** (-jnp.arange(ROPE_FREQS, dtype=F32) / ROPE_FREQS)
    ar, ac = row[:, None] * inv, col[:, None] * inv
    cos = jnp.concatenate([jnp.cos(ar), jnp.cos(ar), jnp.cos(ac), jnp.cos(ac)], axis=-1)
    sin = jnp.concatenate([-jnp.sin(ar), jnp.sin(ar), -jnp.sin(ac), jnp.sin(ac)], axis=-1)
    return jnp.tile(cos, (1, ATT_HEADS)), jnp.tile(sin, (1, ATT_HEADS))


def _lat_attn_kernel(sink_ref, q_ref, k_ref, v_ref, ck_ref, cv_ref, cosq_ref, sinq_ref, cosk_ref, sink_k_ref,
                     seg_ref, qg_ref, kg_ref, o_ref, kn_ref):
    n = pl.program_id(1)
    seg = seg_ref[...]

    @pl.when(n == 0)
    def _():
        kn = _head_rms(k_ref[...], seg[:KV_WIDTH, :KV_WIDTH], kg_ref[...])
        kn_ref[...] = _rope(kn, cosk_ref[...], sink_k_ref[...]).astype(BF16)

    qn = _head_rms(q_ref[...], seg, qg_ref[...])
    qn = _rope(qn, cosq_ref[...], sinq_ref[...]) * ATT_SCALE
    span = 3 * ATT_BLOCK
    start = pl.multiple_of(jnp.clip((n - 1) * ATT_BLOCK, 0, L_LAT - span), ATT_BLOCK)
    rows = ATT_GROUP * ATT_BLOCK
    q_pos = n * ATT_BLOCK + (lax.broadcasted_iota(jnp.int32, (rows, span), 0) % ATT_BLOCK)
    k_pos = start + lax.broadcasted_iota(jnp.int32, (rows, span), 1)
    valid = jnp.abs(q_pos - k_pos) <= WINDOW
    k_loc = kn_ref[pl.ds(start, span), :]
    v_loc = v_ref[pl.ds(start, span), :].astype(BF16)
    k_ctx = ck_ref[0].astype(BF16)
    v_ctx = cv_ref[0].astype(BF16)
    outs = []
    for g in range(ATT_KV_HEADS):
        cols = slice(g * HEAD_DIM, (g + 1) * HEAD_DIM)
        q = _stack_heads(qn, g).astype(BF16)
        s_loc = lax.dot_general(q, k_loc[:, cols], _DOT_NT, preferred_element_type=F32)
        s_loc = jnp.where(valid, s_loc, NEG_BIG)
        s_ctx = lax.dot_general(q, k_ctx[:, cols], _DOT_NT, preferred_element_type=F32)
        sink = _sink_column(sink_ref, g, ATT_BLOCK)
        m = jnp.maximum(jnp.maximum(jnp.max(s_loc, axis=-1, keepdims=True),
                                    jnp.max(s_ctx, axis=-1, keepdims=True)), sink)
        p_loc = jnp.exp(s_loc - m)
        p_ctx = jnp.exp(s_ctx - m)
        den = (jnp.sum(p_loc, axis=-1, keepdims=True) + jnp.sum(p_ctx, axis=-1, keepdims=True)
               + jnp.exp(sink - m))
        o = (jnp.dot(p_loc.astype(BF16), v_loc[:, cols], preferred_element_type=F32)
             + jnp.dot(p_ctx.astype(BF16), v_ctx[:, cols], preferred_element_type=F32))
        outs.append(o / den)
    o_ref[...] = _unstack_heads(outs, ATT_BLOCK).astype(o_ref.dtype)


def _lat_attention(u, cache_k, cache_v, seg, q_gain, k_gain, sink):
    qcol = 3 * HY_WIDTH // ATT_WIDTH
    kcol = (3 * HY_WIDTH + ATT_WIDTH) // KV_WIDTH
    nblk = L_LAT // ATT_BLOCK
    first_q_block = N_CTX_TOK // ATT_BLOCK
    first_seq = N_CTX_TOK // L_LAT
    cos, sin = _rope_tables()
    return pl.pallas_call(
        _lat_attn_kernel,
        grid_spec=pltpu.PrefetchScalarGridSpec(
            num_scalar_prefetch=1,
            grid=(N_LAT_SEQ, nblk),
            in_specs=[
                pl.BlockSpec((ATT_BLOCK, ATT_WIDTH), lambda b, n, s: (first_q_block + b * nblk + n, qcol)),
                pl.BlockSpec((L_LAT, KV_WIDTH), lambda b, n, s: (first_seq + b, kcol)),
                pl.BlockSpec((L_LAT, KV_WIDTH), lambda b, n, s: (first_seq + b, kcol + 1)),
                pl.BlockSpec((1, PAST_LEN, KV_WIDTH), lambda b, n, s: (b, 0, 0)),
                pl.BlockSpec((1, PAST_LEN, KV_WIDTH), lambda b, n, s: (b, 0, 0)),
                pl.BlockSpec((ATT_BLOCK, ATT_WIDTH), lambda b, n, s: (n, 0)),
                pl.BlockSpec((ATT_BLOCK, ATT_WIDTH), lambda b, n, s: (n, 0)),
                pl.BlockSpec((L_LAT, KV_WIDTH), lambda b, n, s: (0, 0)),
                pl.BlockSpec((L_LAT, KV_WIDTH), lambda b, n, s: (0, 0)),
                pl.BlockSpec((ATT_WIDTH, ATT_WIDTH), lambda b, n, s: (0, 0)),
                pl.BlockSpec((1, ATT_WIDTH), lambda b, n, s: (0, 0)),
                pl.BlockSpec((1, KV_WIDTH), lambda b, n, s: (0, 0)),
            ],
            out_specs=pl.BlockSpec((ATT_BLOCK, ATT_WIDTH), lambda b, n, s: (b * nblk + n, 0)),
            scratch_shapes=[pltpu.VMEM((L_LAT, KV_WIDTH), BF16)],
        ),
        out_shape=jax.ShapeDtypeStruct((N_LAT_SEQ * L_LAT, ATT_WIDTH), BF16),
        name="lat_attention",
    )(sink, u, u, u, cache_k.reshape(N_LAT_SEQ, PAST_LEN, KV_WIDTH), cache_v.reshape(N_LAT_SEQ, PAST_LEN, KV_WIDTH),
      cos, sin, cos[:, :KV_WIDTH], sin[:, :KV_WIDTH], seg,
      jnp.tile(q_gain, ATT_HEADS).reshape(1, ATT_WIDTH), jnp.tile(k_gain, ATT_KV_HEADS).reshape(1, KV_WIDTH))


def _columns_to_lanes(cols):
    rows = cols[0].shape[0]
    lane = lax.broadcasted_iota(jnp.int32, (rows, len(cols)), 1)
    out = jnp.broadcast_to(cols[-1], (rows, len(cols)))
    for k in range(len(cols) - 1):
        out = jnp.where(lane == k, cols[k], out)
    return out


def _post_mixer(x, mix, wb, gain, mod_ref, rw_ref, rb_ref, x1_ref, xt_ref, idx_ref, tw_ref, tm):
    y = jnp.dot(mix, wb, preferred_element_type=F32)
    x1 = x + mod_ref[0, 2:3, :] * y
    x1_ref[...] = x1
    xt = _norm_mod(x1, gain, mod_ref[0, 3:4, :], mod_ref[0, 4:5, :])
    for s in range(TOK_CHUNKS):
        xt_ref[pl.ds(s, tm, stride=TOK_CHUNKS), :] = xt[:, s * LANES:(s + 1) * LANES]
    logits = jnp.dot(xt, rw_ref[...], preferred_element_type=F32, precision=HIGHEST) + rb_ref[...]
    lane = lax.broadcasted_iota(jnp.int32, logits.shape, 1)
    vals, idxs = [], []
    for _ in range(TOP_K):
        m = jnp.max(logits, axis=-1, keepdims=True)
        sel = jnp.min(jnp.where(logits == m, lane, N_EXPERTS), axis=-1, keepdims=True)
        vals.append(m)
        idxs.append(sel)
        logits = jnp.where(lane == sel, -jnp.inf, logits)
    es = [jnp.exp(v - vals[0]) for v in vals]
    den = es[0] + es[1] + es[2] + es[3]
    idx_ref[...] = _columns_to_lanes(idxs)
    tw_ref[...] = _columns_to_lanes([e / den for e in es])


def _post_even_kernel(x_ref, ac_ref, al_ref, tc_ref, tl_ref, w_ref, gain_ref, mod_ref, rw_ref, rb_ref,
                      x1_ref, xt_ref, idx_ref, tw_ref, wb_ref, *, tm):
    i = pl.program_id(0)

    @pl.when(i == 0)
    def _():
        wb_ref[...] = w_ref[...].astype(BF16)

    is_ctx = i * tm < N_CTX_TOK
    mix = jnp.concatenate([jnp.where(is_ctx, ac_ref[...], al_ref[...]),
                           jnp.where(is_ctx, tc_ref[...], tl_ref[...])], axis=1)
    _post_mixer(x_ref[...], mix, wb_ref[...], gain_ref[...], mod_ref, rw_ref, rb_ref,
                x1_ref, xt_ref, idx_ref, tw_ref, tm)


def _post_odd_kernel(x_ref, yc_ref, yl_ref, w_ref, gain_ref, mod_ref, rw_ref, rb_ref,
                     x1_ref, xt_ref, idx_ref, tw_ref, wb_ref, *, tm):
    i = pl.program_id(0)

    @pl.when(i == 0)
    def _():
        wb_ref[...] = w_ref[...].astype(BF16)

    mix = jnp.where(i * tm < N_CTX_TOK, yc_ref[...], yl_ref[...])
    _post_mixer(x_ref[...], mix, wb_ref[...], gain_ref[...], mod_ref, rw_ref, rb_ref,
                x1_ref, xt_ref, idx_ref, tw_ref, tm)


def _post_out_specs(tm):
    specs = [
        pl.BlockSpec((tm, D_MODEL), lambda i: (i, 0)),
        pl.BlockSpec((tm * TOK_CHUNKS, LANES), lambda i: (i, 0)),
        pl.BlockSpec((tm, TOP_K), lambda i: (i, 0)),
        pl.BlockSpec((tm, TOP_K), lambda i: (i, 0)),
    ]
    shapes = [
        jax.ShapeDtypeStruct((N_TOK, D_MODEL), F32),
        jax.ShapeDtypeStruct((N_TOK * TOK_CHUNKS, LANES), F32),
        jax.ShapeDtypeStruct((N_TOK, TOP_K), jnp.int32),
        jax.ShapeDtypeStruct((N_TOK, TOP_K), F32),
    ]
    return specs, shapes


def _post_tail_specs(k_in, tm):
    return [
        pl.BlockSpec((k_in, D_MODEL), lambda i: (0, 0)),
        pl.BlockSpec((1, D_MODEL), lambda i: (0, 0)),
        pl.BlockSpec((1, 6, D_MODEL), lambda i: (_cond_of_tile(i, tm), 0, 0)),
        pl.BlockSpec((D_MODEL, N_EXPERTS), lambda i: (0, 0)),
        pl.BlockSpec((1, N_EXPERTS), lambda i: (0, 0)),
    ]


def _post_even(x, a_ctx, a_lat, t_ctx, t_lat, w_out, gain, mod, router_w, router_b):
    tm = ROW_TILE
    nctx = N_CTX_TOK // tm
    ctx_map = lambda i: (jnp.minimum(i, nctx - 1), 0)
    lat_map = lambda i: (jnp.maximum(i - nctx, 0), 0)
    out_specs, out_shapes = _post_out_specs(tm)
    return pl.pallas_call(
        functools.partial(_post_even_kernel, tm=tm),
        grid=(N_TOK // tm,),
        in_specs=[
            pl.BlockSpec((tm, D_MODEL), lambda i: (i, 0)),
            pl.BlockSpec((tm, HY_WIDTH), ctx_map), pl.BlockSpec((tm, HY_WIDTH), lat_map),
            pl.BlockSpec((tm, ATT_WIDTH), ctx_map), pl.BlockSpec((tm, ATT_WIDTH), lat_map),
        ] + _post_tail_specs(HY_WIDTH + ATT_WIDTH, tm),
        out_specs=out_specs,
        out_shape=out_shapes,
        scratch_shapes=[pltpu.VMEM((HY_WIDTH + ATT_WIDTH, D_MODEL), BF16)],
        compiler_params=_vmem(48),
        name="post_even",
    )(x, a_ctx, a_lat, t_ctx, t_lat, w_out, gain.reshape(1, D_MODEL), mod, router_w,
      router_b.reshape(1, N_EXPERTS))


def _post_odd(x, y_ctx, y_lat, w_out, gain, mod, router_w, router_b):
    tm = ROW_TILE
    nctx = N_CTX_TOK // tm
    out_specs, out_shapes = _post_out_specs(tm)
    return pl.pallas_call(
        functools.partial(_post_odd_kernel, tm=tm),
        grid=(N_TOK // tm,),
        in_specs=[
            pl.BlockSpec((tm, D_MODEL), lambda i: (i, 0)),
            pl.BlockSpec((tm, RET_V_WIDTH), lambda i: (jnp.minimum(i, nctx - 1), 0)),
            pl.BlockSpec((tm, RET_V_WIDTH), lambda i: (jnp.maximum(i - nctx, 0), 0)),
        ] + _post_tail_specs(RET_V_WIDTH, tm),
        out_specs=out_specs,
        out_shape=out_shapes,
        scratch_shapes=[pltpu.VMEM((RET_V_WIDTH, D_MODEL), BF16)],
        compiler_params=_vmem(48),
        name="post_odd",
    )(x, y_ctx, y_lat, w_out, gain.reshape(1, D_MODEL), mod, router_w, router_b.reshape(1, N_EXPERTS))


def _moe_plan(top_idx):
    t = MOE_TILE
    e_flat = top_idx.reshape(N_PAIRS)
    order = jnp.argsort(e_flat, stable=True).astype(jnp.int32)
    dest = (order % TOP_K) * N_TOK + order // TOP_K
    experts = jnp.arange(N_EXPERTS, dtype=jnp.int32)
    counts = jnp.sum((e_flat.reshape(1, -1, LANES) == experts[:, None, None]).astype(jnp.int32), axis=(1, 2))
    offsets = jnp.cumsum(counts) - counts
    ntiles = (counts + t - 1) // t
    tile_end = jnp.cumsum(ntiles)
    tile_base = tile_end - ntiles
    total = tile_end[-1]
    i = jnp.arange(MOE_NUM_TILES, dtype=jnp.int32)
    e_of = jnp.sum((tile_end[None, :] <= jnp.minimum(i, total - 1)[:, None]).astype(jnp.int32), axis=1)
    sel = (e_of[:, None] == experts[None, :]).astype(jnp.int32)
    pick = lambda v: jnp.sum(sel * v[None, :], axis=1)
    j = i - pick(tile_base)
    start = pick(offsets) + j * t
    count = jnp.where(i < total, jnp.clip(pick(counts) - j * t, 0, t), 0)
    base = jnp.clip(start, 0, N_PAIRS - t)
    off = jnp.where(i < total, start - base, 0)
    as_i32 = lambda v: v.astype(jnp.int32)
    return as_i32(e_of), as_i32(base), as_i32(off), as_i32(count), as_i32(dest)


def _moe_kernel(te_ref, tb_ref, toff_ref, tcnt_ref, dest_ref, xt_hbm, w1_ref, b1_ref, w2_ref, b2_ref, y_hbm,
                xbuf, obuf, w1b, w2b, sem_in, sem_out):
    i = pl.program_id(0)
    nt = pl.num_programs(0)
    t = MOE_TILE
    slot = i % 2
    rows = TOK_CHUNKS

    def issue_gather(tile, sl):
        base = tb_ref[tile]
        for r in range(t):
            tok = dest_ref[base + r] & (N_TOK - 1)
            pltpu.make_async_copy(xt_hbm.at[pl.ds(tok * rows, rows), :],
                                  xbuf.at[sl, pl.ds(r * rows, rows), :], sem_in.at[sl]).start()

    def wait_gather(sl):
        pltpu.make_async_copy(xt_hbm.at[pl.ds(0, t * rows), :], xbuf.at[sl], sem_in.at[sl]).wait()

    def issue_scatter(tile, sl):
        base, lo = tb_ref[tile], toff_ref[tile]
        hi = lo + tcnt_ref[tile]
        for r in range(t):
            inside = jnp.logical_and(lo <= r, hi > r)
            d = jnp.where(inside, dest_ref[base + r], MOE_TRASH_ROW + sl * t + r)
            pltpu.make_async_copy(obuf.at[sl, pl.ds(r * rows, rows), :],
                                  y_hbm.at[pl.ds(d * rows, rows), :], sem_out.at[sl]).start()

    def wait_scatter(sl):
        pltpu.make_async_copy(obuf.at[sl], y_hbm.at[pl.ds(0, t * rows), :], sem_out.at[sl]).wait()

    def valid(tile):
        return tcnt_ref[jnp.clip(tile, 0, nt - 1)] > 0

    @pl.when(i == 0)
    def _():
        issue_gather(0, 0)
        obuf[...] = jnp.zeros_like(obuf)
        for sl in range(2):
            trash = y_hbm.at[pl.ds((MOE_TRASH_ROW + sl * t) * rows, t * rows), :]
            pltpu.make_async_copy(obuf.at[sl], trash, sem_out.at[sl]).start()
        for sl in range(2):
            wait_scatter(sl)

    @pl.when(jnp.logical_and(i >= 2, valid(i - 2)))
    def _():
        wait_scatter(slot)

    @pl.when(valid(i))
    def _():
        wait_gather(slot)

        @pl.when(jnp.logical_and(i + 1 < nt, valid(i + 1)))
        def _():
            issue_gather(i + 1, 1 - slot)

        first = jnp.logical_or(i == 0, te_ref[i] != te_ref[jnp.maximum(i - 1, 0)])

        @pl.when(first)
        def _():
            w1b[...] = w1_ref[0].astype(BF16)
            w2b[...] = w2_ref[0].astype(BF16)

        xs = jnp.concatenate([xbuf[slot, pl.ds(s, t, stride=rows), :] for s in range(rows)], axis=1)
        h = jnp.dot(xs.astype(BF16), w1b[...], preferred_element_type=F32) + b1_ref[0]
        glu = jnp.minimum(h[:, :D_FF], SWIGLU_LIMIT)
        lin = jnp.clip(h[:, D_FF:], -SWIGLU_LIMIT, SWIGLU_LIMIT)
        act = glu * _sigmoid(SWIGLU_ALPHA * glu) * (lin + 1.0)
        y = jnp.dot(act.astype(BF16), w2b[...], preferred_element_type=F32) + b2_ref[0]
        for s in range(rows):
            obuf[slot, pl.ds(s, t, stride=rows), :] = y[:, s * LANES:(s + 1) * LANES]
        issue_scatter(i, slot)

    @pl.when(i == nt - 1)
    def _():
        @pl.when(valid(i - 1))
        def _():
            wait_scatter(1 - slot)

        @pl.when(valid(i))
        def _():
            wait_scatter(slot)


def _moe_experts(xt_tiles, plan, w1, b1, w2, b2):
    te, tb, toff, tcnt, dest = plan
    t = MOE_TILE
    return pl.pallas_call(
        _moe_kernel,
        grid_spec=pltpu.PrefetchScalarGridSpec(
            num_scalar_prefetch=5,
            grid=(MOE_NUM_TILES,),
            in_specs=[
                pl.BlockSpec(memory_space=pl.ANY),
                pl.BlockSpec((1, D_MODEL, 2 * D_FF), lambda i, te, *_: (te[i], 0, 0)),
                pl.BlockSpec((1, 1, 2 * D_FF), lambda i, te, *_: (te[i], 0, 0)),
                pl.BlockSpec((1, D_FF, D_MODEL), lambda i, te, *_: (te[i], 0, 0)),
                pl.BlockSpec((1, 1, D_MODEL), lambda i, te, *_: (te[i], 0, 0)),
            ],
            out_specs=pl.BlockSpec(memory_space=pl.ANY),
            scratch_shapes=[
                pltpu.VMEM((2, t * TOK_CHUNKS, LANES), F32),
                pltpu.VMEM((2, t * TOK_CHUNKS, LANES), F32),
                pltpu.VMEM((D_MODEL, 2 * D_FF), BF16),
                pltpu.VMEM((D_FF, D_MODEL), BF16),
                pltpu.SemaphoreType.DMA((2,)),
                pltpu.SemaphoreType.DMA((2,)),
            ],
        ),
        out_shape=jax.ShapeDtypeStruct((MOE_OUT_ROWS * TOK_CHUNKS, LANES), F32),
        compiler_params=_vmem(56),
        name="moe_experts",
    )(te, tb, toff, tcnt, dest, xt_tiles, w1, b1.reshape(N_EXPERTS, 1, 2 * D_FF), w2,
      b2.reshape(N_EXPERTS, 1, D_MODEL))


def _combine_value(x1_ref, y_refs, tw_ref, mod_ref, tm):
    tw = tw_ref[...]
    chunks = []
    for s in range(TOK_CHUNKS):
        acc = tw[:, 0:1] * y_refs[0][pl.ds(s, tm, stride=TOK_CHUNKS), :]
        for k in range(1, TOP_K):
            acc = acc + tw[:, k:k + 1] * y_refs[k][pl.ds(s, tm, stride=TOK_CHUNKS), :]
        chunks.append(acc)
    return x1_ref[...] + mod_ref[0, 5:6, :] * jnp.concatenate(chunks, axis=1)


def _combine_kernel(x1_ref, y0_ref, y1_ref, y2_ref, y3_ref, tw_ref, mod_ref, o_ref, *, tm):
    o_ref[...] = _combine_value(x1_ref, (y0_ref, y1_ref, y2_ref, y3_ref), tw_ref, mod_ref, tm)


def _combine_split_kernel(x1_ref, y0_ref, y1_ref, y2_ref, y3_ref, tw_ref, mod_ref, oc_ref, ol_ref, *, tm):
    val = _combine_value(x1_ref, (y0_ref, y1_ref, y2_ref, y3_ref), tw_ref, mod_ref, tm)
    is_ctx = pl.program_id(0) * tm < N_CTX_TOK

    @pl.when(is_ctx)
    def _():
        oc_ref[...] = val

    @pl.when(jnp.logical_not(is_ctx))
    def _():
        ol_ref[...] = val


def _combine(x1, y_tiles, top_w, mod, split):
    tm = 256
    nctx = N_CTX_TOK // tm
    ntile = N_TOK // tm
    slab = lambda k: pl.BlockSpec((tm * TOK_CHUNKS, LANES), lambda i: (k * ntile + i, 0))
    in_specs = [
        pl.BlockSpec((tm, D_MODEL), lambda i: (i, 0)),
        slab(0), slab(1), slab(2), slab(3),
        pl.BlockSpec((tm, TOP_K), lambda i: (i, 0)),
        pl.BlockSpec((1, 6, D_MODEL), lambda i: (_cond_of_tile(i, tm), 0, 0)),
    ]
    if not split:
        return pl.pallas_call(
            functools.partial(_combine_kernel, tm=tm),
            grid=(N_TOK // tm,),
            in_specs=in_specs,
            out_specs=pl.BlockSpec((tm, D_MODEL), lambda i: (i, 0)),
            out_shape=jax.ShapeDtypeStruct((N_TOK, D_MODEL), F32),
            name="moe_combine",
        )(x1, y_tiles, y_tiles, y_tiles, y_tiles, top_w, mod)
    return pl.pallas_call(
        functools.partial(_combine_split_kernel, tm=tm),
        grid=(N_TOK // tm,),
        in_specs=in_specs,
        out_specs=[pl.BlockSpec((tm, D_MODEL), lambda i: (jnp.minimum(i, nctx - 1), 0)),
                   pl.BlockSpec((tm, D_MODEL), lambda i: (jnp.maximum(i - nctx, 0), 0))],
        out_shape=[jax.ShapeDtypeStruct((N_CTX_TOK, D_MODEL), F32),
                   jax.ShapeDtypeStruct((N_TOK - N_CTX_TOK, D_MODEL), F32)],
        name="moe_combine_split",
    )(x1, y_tiles, y_tiles, y_tiles, y_tiles, top_w, mod)


def _retention_kernel(lg_ref, q_ref, k_ref, v_ref, gf_ref, gb_ref, *rest, length, has_s0, emit_state):
    rest = list(rest)
    s0_ref = rest.pop(0) if has_s0 else None
    o_ref = rest.pop(0)
    so_ref = rest.pop(0) if emit_state else None
    s_ref, yf_ref = rest
    c = RET_CHUNK
    nc = length // c
    ii = lax.broadcasted_iota(jnp.int32, (c, c), 0).astype(F32)
    jj = lax.broadcasted_iota(jnp.int32, (c, c), 1).astype(F32)
    ci = lax.broadcasted_iota(jnp.int32, (c, 1), 0).astype(F32)

    def scan(direction):
        lg = -jnp.exp(lg_ref[direction, 0])
        lg1 = lg[:, 0:1]
        if direction == 0:
            diff = ii - jj
            q_decay = jnp.exp(lg1 * (ci + 1.0))
            k_decay = jnp.exp(lg1 * (c - 1.0 - ci))
        else:
            diff = jj - ii
            q_decay = jnp.exp(lg1 * (c - ci))
            k_decay = jnp.exp(lg1 * ci)
        inner = jnp.where(diff >= 0, jnp.exp(lg * jnp.maximum(diff, 0.0)), 0.0)
        chunk_decay = jnp.exp(lg1 * float(c))
        if has_s0:
            s_ref[...] = s0_ref[0, direction, 0]
        else:
            s_ref[...] = jnp.zeros_like(s_ref)

        def body(step, carry):
            ch = step if direction == 0 else nc - 1 - step
            rows = pl.ds(pl.multiple_of(ch * c, c), c)
            qc = q_ref[rows, :]
            kc = k_ref[rows, :].astype(F32) * (RET_DK ** -0.5)
            vc = v_ref[rows, :]
            s = s_ref[...]
            att = lax.dot_general(qc, kc.astype(BF16), _DOT_NT, preferred_element_type=F32) * inner
            o = (jnp.dot(att.astype(BF16), vc, preferred_element_type=F32)
                 + jnp.dot(qc, s.astype(BF16), preferred_element_type=F32) * q_decay)
            kd = (kc * k_decay).T.astype(BF16)
            s_ref[...] = s * chunk_decay + jnp.dot(kd, vc, preferred_element_type=F32)
            on = o * lax.rsqrt(jnp.mean(o * o, axis=-1, keepdims=True) + EPS)
            if direction == 0:
                g = gf_ref[rows, :].astype(F32)
                yf_ref[rows, :] = g * _sigmoid(g) * on
            else:
                g = gb_ref[rows, :].astype(F32)
                o_ref[rows, :] = (yf_ref[rows, :] + g * _sigmoid(g) * on).astype(o_ref.dtype)
            return carry

        lax.fori_loop(0, nc, body, 0)
        if emit_state:
            so_ref[0, direction, 0] = s_ref[...]

    scan(0)
    scan(1)


def _retention(u, first_seq, nseq, length, decay_logit, s0, emit_state):
    row0 = first_seq
    lg = jnp.broadcast_to(decay_logit.astype(F32)[:, :, None, None], (2, RET_HEADS, 1, LANES))
    kcol = RET_QK_WIDTH // RET_DK
    vcol = 2 * RET_QK_WIDTH // RET_DV
    gfcol = vcol + RET_HEADS
    gbcol = gfcol + RET_HEADS
    in_specs = [
        pl.BlockSpec((2, 1, 1, LANES), lambda b, h: (0, h, 0, 0)),
        pl.BlockSpec((length, RET_DK), lambda b, h: (row0 + b, h)),
        pl.BlockSpec((length, RET_DK), lambda b, h: (row0 + b, kcol + h)),
        pl.BlockSpec((length, RET_DV), lambda b, h: (row0 + b, vcol + h)),
        pl.BlockSpec((length, RET_DV), lambda b, h: (row0 + b, gfcol + h)),
        pl.BlockSpec((length, RET_DV), lambda b, h: (row0 + b, gbcol + h)),
    ]
    args = [lg, u, u, u, u, u]
    state_spec = pl.BlockSpec((1, 2, 1, RET_DK, RET_DV), lambda b, h: (b, 0, h, 0, 0))
    if s0 is not None:
        in_specs.append(state_spec)
        args.append(s0)
    out_specs = [pl.BlockSpec((length, RET_DV), lambda b, h: (b, h))]
    out_shape = [jax.ShapeDtypeStruct((nseq * length, RET_V_WIDTH), BF16)]
    if emit_state:
        out_specs.append(state_spec)
        out_shape.append(jax.ShapeDtypeStruct((nseq, 2, RET_HEADS, RET_DK, RET_DV), F32))
    return pl.pallas_call(
        functools.partial(_retention_kernel, length=length, has_s0=s0 is not None, emit_state=emit_state),
        grid=(nseq, RET_HEADS),
        in_specs=in_specs,
        out_specs=out_specs,
        out_shape=out_shape,
        scratch_shapes=[pltpu.VMEM((RET_DK, RET_DV), F32), pltpu.VMEM((length, RET_DV), F32)],
        compiler_params=_vmem(48),
        name=f"retention_{length}",
    )(*args)


def kernel(x_prompt, x_sample, cache_k0, cache_v0, state_ret1, c, c_ctx, l0_norm_mix, l0_ada_w, l0_ada_b, l0_w_in, l0_conv_w, l0_conv_b, l0_filt_w1, l0_filt_b1, l0_filt_freq, l0_filt_w2, l0_filt_b2, l0_filt_w3, l0_filt_deltas, l0_hy_skip, l0_q_gain, l0_k_gain, l0_sink, l0_w_out, l0_norm_ffn, l0_router_w, l0_router_b, l0_moe_w1, l0_moe_b1, l0_moe_w2, l0_moe_b2, l1_norm_mix, l1_ada_w, l1_ada_b, l1_w_in, l1_ret_decay_logit, l1_w_out, l1_norm_ffn, l1_router_w, l1_router_b, l1_moe_w1, l1_moe_b1, l1_moe_w2, l1_moe_b2):
    x = jnp.concatenate([x_prompt.reshape(N_CTX_TOK, D_MODEL), x_sample.reshape(N_TOK - N_CTX_TOK, D_MODEL)], axis=0)
    cond = jnp.zeros((SUBLANES, D_MODEL), F32).at[0].set(c_ctx).at[1:1 + N_LAT_SEQ].set(c)
    mod0 = _adaln(cond, l0_ada_w, l0_ada_b)
    mod1 = _adaln(cond, l1_ada_w, l1_ada_b)

    u = _in_proj(x, l0_norm_mix, mod0, l0_w_in, EVEN_IN // 2, F32)
    filt = (l0_filt_w1, l0_filt_b1, l0_filt_freq, l0_filt_w2, l0_filt_b2, l0_filt_w3, l0_filt_deltas)
    hy = []
    for first_block, nseq, length in ((0, N_CTX_SEQ, L_CTX), (N_CTX_TOK // L_LAT // N_LAT_SEQ, N_LAT_SEQ, L_LAT)):
        cmat, smat, stmat = _dft_matrices(length)
        tc, ts = _hyena_filter(length, _filter_features(length), cmat, smat, *filt)
        hy.append(_hyena(u, first_block, nseq, length, l0_conv_w, l0_conv_b, l0_hy_skip, tc, ts, cmat, smat, stmat))
    head = lax.broadcasted_iota(jnp.int32, (ATT_WIDTH, ATT_WIDTH), 0) // HEAD_DIM
    seg = (head == head.T).astype(BF16)
    att_ctx, new_k, new_v = _ctx_attention(u, seg, l0_q_gain, l0_k_gain, l0_sink)
    att_lat = _lat_attention(u, cache_k0, cache_v0, seg, l0_q_gain, l0_k_gain, l0_sink)
    x1, xt, top_idx, top_w = _post_even(x, hy[0], hy[1], att_ctx, att_lat, l0_w_out, l0_norm_ffn, mod0,
                                        l0_router_w, l0_router_b)
    y = _moe_experts(xt, _moe_plan(top_idx), l0_moe_w1, l0_moe_b1, l0_moe_w2, l0_moe_b2)
    x = _combine(x1, y, top_w, mod0, split=False)

    u = _in_proj(x, l1_norm_mix, mod1, l1_w_in, 2048, BF16)
    y_ctx, new_state = _retention(u, 0, N_CTX_SEQ, L_CTX, l1_ret_decay_logit, None, True)
    (y_lat,) = _retention(u, N_CTX_TOK // L_LAT, N_LAT_SEQ, L_LAT, l1_ret_decay_logit, state_ret1, False)
    x1, xt, top_idx, top_w = _post_odd(x, y_ctx, y_lat, l1_w_out, l1_norm_ffn, mod1, l1_router_w, l1_router_b)
    y = _moe_experts(xt, _moe_plan(top_idx), l1_moe_w1, l1_moe_b1, l1_moe_w2, l1_moe_b2)
    y_prompt, y_sample = _combine(x1, y, top_w, mod1, split=True)

    return (y_prompt.reshape(N_CTX_SEQ, L_CTX, D_MODEL), y_sample.reshape(N_LAT_SEQ, L_LAT, D_MODEL),
            new_k.reshape(N_CTX_SEQ, L_CTX, ATT_KV_HEADS, HEAD_DIM),
            new_v.reshape(N_CTX_SEQ, L_CTX, ATT_KV_HEADS, HEAD_DIM), new_state)
```

```python
import functools
import math

import jax
import jax.numpy as jnp
from jax import lax
from jax.experimental import pallas as pl
from jax.experimental.pallas import tpu as pltpu

F32 = jnp.float32
BF16 = jnp.bfloat16
HIGHEST = lax.Precision.HIGHEST

D_MODEL = 1024
N_CTX_SEQ, L_CTX = 16, 256
N_LAT_SEQ, L_LAT = 2, 2048
N_CTX_TOK = N_CTX_SEQ * L_CTX
N_TOK = N_CTX_TOK + N_LAT_SEQ * L_LAT
PAST_LEN = 512
EPS = 1e-6
NEG_BIG = -1e30

HY_WIDTH = 512
HY_BANDS = 16
HY_FILTER_HIDDEN = 64
HY_FEAT_PAD = 64

ATT_HEADS, ATT_KV_HEADS, HEAD_DIM = 8, 2, 64
ATT_GROUP = ATT_HEADS // ATT_KV_HEADS
ATT_WIDTH = ATT_HEADS * HEAD_DIM
KV_WIDTH = ATT_KV_HEADS * HEAD_DIM
ATT_SCALE = HEAD_DIM ** -0.5
WINDOW = 128
ATT_BLOCK = 128
ROPE_THETA = 10000.0
ROPE_FREQS = HEAD_DIM // 4
GRID_W = 64
EVEN_IN = 3 * HY_WIDTH + ATT_WIDTH + 2 * KV_WIDTH

RET_HEADS = 4
RET_DK = 256
RET_DV = 512
RET_CHUNK = 128
RET_QK_WIDTH = RET_HEADS * RET_DK
RET_V_WIDTH = RET_HEADS * RET_DV
ODD_IN = 2 * RET_QK_WIDTH + 3 * RET_V_WIDTH

N_EXPERTS = 32
TOP_K = 4
D_FF = 1024
SWIGLU_ALPHA = 1.702
SWIGLU_LIMIT = 7.0

SUBLANES = 8
LANES = 128
TOK_CHUNKS = D_MODEL // LANES

MOE_TILE = 256
N_PAIRS = N_TOK * TOP_K
MOE_NUM_TILES = N_PAIRS // MOE_TILE + N_EXPERTS
MOE_TRASH_ROW = N_PAIRS
MOE_OUT_ROWS = N_PAIRS + 2 * MOE_TILE

ROW_TILE = 512


def _vmem(mib):
    return pltpu.CompilerParams(vmem_limit_bytes=mib * 1024 * 1024)


def _cond_of_tile(i, tm):
    row = i * tm
    return jnp.where(row < N_CTX_TOK, 0, 1 + (row - N_CTX_TOK) // L_LAT)


def _sigmoid(x):
    return 1.0 / (1.0 + jnp.exp(-x))


def _norm_mod(x, gain, shift, scale):
    ms = jnp.mean(x * x, axis=-1, keepdims=True)
    return (x * lax.rsqrt(ms + EPS) * gain) * (1.0 + scale) + shift


def _adaln_kernel(c_ref, w_ref, b_ref, o_ref):
    c = c_ref[...]
    s = c * _sigmoid(c)
    o_ref[...] = jnp.dot(s, w_ref[...], preferred_element_type=F32, precision=HIGHEST) + b_ref[...]


def _adaln(cond, w, b):
    n = w.shape[1]
    tn = 1024
    out = pl.pallas_call(
        _adaln_kernel,
        grid=(n // tn,),
        in_specs=[
            pl.BlockSpec((SUBLANES, D_MODEL), lambda j: (0, 0)),
            pl.BlockSpec((D_MODEL, tn), lambda j: (0, j)),
            pl.BlockSpec((1, tn), lambda j: (0, j)),
        ],
        out_specs=pl.BlockSpec((SUBLANES, tn), lambda j: (0, j)),
        out_shape=jax.ShapeDtypeStruct((SUBLANES, n), F32),
        name="adaln",
    )(cond, w, b.reshape(1, n))
    return out.reshape(SUBLANES, 6, D_MODEL)


def _in_proj_kernel(x_ref, gain_ref, mod_ref, w_ref, o_ref, wb_ref):
    @pl.when(pl.program_id(1) == 0)
    def _():
        wb_ref[...] = w_ref[...].astype(BF16)

    h = _norm_mod(x_ref[...], gain_ref[...], mod_ref[0, 0:1, :], mod_ref[0, 1:2, :])
    o_ref[...] = jnp.dot(h.astype(BF16), wb_ref[...], preferred_element_type=F32).astype(o_ref.dtype)


def _in_proj(x, gain, mod, w, tn, out_dtype):
    n = w.shape[1]
    tm = ROW_TILE
    return pl.pallas_call(
        _in_proj_kernel,
        grid=(n // tn, N_TOK // tm),
        in_specs=[
            pl.BlockSpec((tm, D_MODEL), lambda j, i: (i, 0)),
            pl.BlockSpec((1, D_MODEL), lambda j, i: (0, 0)),
            pl.BlockSpec((1, 6, D_MODEL), lambda j, i: (_cond_of_tile(i, tm), 0, 0)),
            pl.BlockSpec((D_MODEL, tn), lambda j, i: (0, j)),
        ],
        out_specs=pl.BlockSpec((tm, tn), lambda j, i: (i, j)),
        out_shape=jax.ShapeDtypeStruct((N_TOK, n), out_dtype),
        scratch_shapes=[pltpu.VMEM((D_MODEL, tn), BF16)],
        compiler_params=_vmem(48),
        name="in_proj",
    )(x, gain.reshape(1, D_MODEL), mod, w)


def _dft_matrices(length):
    n = 2 * length
    lo = 16
    s = jnp.arange(length, dtype=jnp.int32)
    k1 = jnp.arange(length // lo, dtype=jnp.int32) * lo
    k0 = jnp.arange(lo, dtype=jnp.int32)
    ang1 = (2.0 * math.pi / n) * ((k1[:, None] * s[None, :]) % n).astype(F32)
    ang0 = (2.0 * math.pi / n) * ((k0[:, None] * s[None, :]) % n).astype(F32)
    c1, s1 = jnp.cos(ang1)[:, None, :], jnp.sin(ang1)[:, None, :]
    c0, s0 = jnp.cos(ang0)[None, :, :], jnp.sin(ang0)[None, :, :]
    cmat = (c1 * c0 - s1 * s0).reshape(length, length)
    smat = (s1 * c0 + c1 * s0).reshape(length, length)
    sign = jnp.where(s % 2 == 0, 1.0, -1.0).astype(F32)
    row = lax.broadcasted_iota(jnp.int32, (length, length), 0)
    col = lax.broadcasted_iota(jnp.int32, (length, length), 1)
    s_nyq = jnp.where(row == 0, sign[None, :], smat)
    st_nyq = jnp.where(col == 0, sign[:, None], smat)
    return cmat.astype(BF16), s_nyq.astype(BF16), st_nyq.astype(BF16)


def _filter_features(length):
    t = jnp.linspace(0.0, 1.0, length, dtype=F32)[:, None]
    w = 2.0 * math.pi * jnp.arange(length, dtype=F32)[:, None] / length
    f = jnp.linspace(1e-4, HY_BANDS - 1, HY_BANDS, dtype=F32)[None, :]
    z = jnp.concatenate([t, jnp.cos(f * w), -jnp.sin(f * w)], axis=-1)
    return jnp.pad(z, ((0, 0), (0, HY_FEAT_PAD - z.shape[1])))


def _filter_kernel(z_ref, w1_ref, b1_ref, fr_ref, w2_ref, b2_ref, w3_ref, dl_ref, c_ref, s_ref,
                   tc_ref, ts_ref, taps_ref, *, length, rb):
    r = pl.program_id(0)

    @pl.when(r == 0)
    def _():
        z = z_ref[...]
        fr = fr_ref[...]
        h = jnp.sin(fr * (jnp.dot(z, w1_ref[...], preferred_element_type=F32, precision=HIGHEST) + b1_ref[...]))
        h = jnp.sin(fr * (jnp.dot(h, w2_ref[...], preferred_element_type=F32, precision=HIGHEST) + b2_ref[...]))
        h = jnp.dot(h, w3_ref[...], preferred_element_type=F32, precision=HIGHEST)
        win = jnp.exp(-z[:, 0:1] * jnp.abs(dl_ref[...]))
        hf = h[:, :HY_WIDTH] * win
        hb = h[:, HY_WIDTH:] * win
        row = lax.broadcasted_iota(jnp.int32, (length, HY_WIDTH), 0)
        hb = jnp.where(row == 0, 0.0, hb)
        l1 = jnp.sum(jnp.abs(hf), axis=0, keepdims=True) + jnp.sum(jnp.abs(hb), axis=0, keepdims=True)
        inv = 1.0 / l1
        taps_ref[:, :HY_WIDTH] = (hf * inv).astype(BF16)
        taps_ref[:, HY_WIDTH:] = (hb * inv).astype(BF16)

    taps = taps_ref[...]
    rc = jnp.dot(c_ref[...], taps, preferred_element_type=F32)
    rs = jnp.dot(s_ref[...], taps, preferred_element_type=F32)
    tc = rc[:, :HY_WIDTH] + rc[:, HY_WIDTH:]
    ts = rs[:, :HY_WIDTH] - rs[:, HY_WIDTH:]
    grow = r * rb + lax.broadcasted_iota(jnp.int32, (rb, HY_WIDTH), 0)
    is0 = grow == 0
    ts = jnp.where(is0, rs[:, :HY_WIDTH] + rs[:, HY_WIDTH:], ts)
    wgt = jnp.where(is0, 1.0 / (2 * length), 2.0 / (2 * length))
    tc_ref[...] = tc * wgt
    ts_ref[...] = ts * wgt


def _hyena_filter(length, feats, cmat, smat, w1, b1, freq, w2, b2, w3, deltas):
    rb = min(length, 512)
    w1p = jnp.pad(w1, ((0, HY_FEAT_PAD - w1.shape[0]), (0, 0)))
    full = lambda shape: pl.BlockSpec(shape, lambda r: (0,) * len(shape))
    hid = HY_FILTER_HIDDEN
    return pl.pallas_call(
        functools.partial(_filter_kernel, length=length, rb=rb),
        grid=(length // rb,),
        in_specs=[
            full((length, HY_FEAT_PAD)), full((HY_FEAT_PAD, hid)), full((1, hid)), full((1, hid)),
            full((hid, hid)), full((1, hid)), full((hid, 2 * HY_WIDTH)), full((1, HY_WIDTH)),
            pl.BlockSpec((rb, length), lambda r: (r, 0)),
            pl.BlockSpec((rb, length), lambda r: (r, 0)),
        ],
        out_specs=[pl.BlockSpec((rb, HY_WIDTH), lambda r: (r, 0)),
                   pl.BlockSpec((rb, HY_WIDTH), lambda r: (r, 0))],
        out_shape=[jax.ShapeDtypeStruct((length, HY_WIDTH), F32)] * 2,
        scratch_shapes=[pltpu.VMEM((length, 2 * HY_WIDTH), BF16)],
        compiler_params=_vmem(48),
        name=f"hyena_filter_{length}",
    )(feats, w1p, b1.reshape(1, hid), freq.reshape(1, hid), w2, b2.reshape(1, hid), w3,
      deltas.reshape(1, HY_WIDTH), cmat, smat)


def _hyena_kernel(x0_ref, x1_ref, v_ref, w0_ref, w1_ref, w2_ref, b0_ref, b1_ref, b2_ref, skip_ref,
                  tc_ref, ts_ref, c_ref, s_ref, ct_ref, st_ref, o_ref,
                  zb_ref, zs_ref, x0c_ref, acc_ref, *, nseq, length, cb, fb):
    f = pl.program_id(1)
    nf = pl.num_programs(1)

    def short_conv(u, w_ref, b_ref):
        row = lax.broadcasted_iota(jnp.int32, u.shape, 0)
        prev = jnp.where(row == 0, 0.0, pltpu.roll(u, 1, 0))
        nxt = jnp.where(row == length - 1, 0.0, pltpu.roll(u, length - 1, 0))
        return prev * w_ref[0:1, :] + u * w_ref[1:2, :] + nxt * w_ref[2:3, :] + b_ref[...]

    @pl.when(f == 0)
    def _():
        for b in range(nseq):
            cols = slice(b * cb, (b + 1) * cb)
            x0c_ref[:, cols] = short_conv(x0_ref[b], w0_ref, b0_ref)
            z = short_conv(x1_ref[b], w1_ref, b1_ref) * short_conv(v_ref[b], w2_ref, b2_ref)
            zb_ref[:, cols] = z.astype(BF16)
            zs_ref[:, cols] = z * skip_ref[...]
        acc_ref[...] = jnp.zeros_like(acc_ref)

    zb = zb_ref[...]
    zc = jnp.dot(c_ref[...], zb, preferred_element_type=F32)
    zsn = jnp.dot(s_ref[...], zb, preferred_element_type=F32)
    tc = jnp.concatenate([tc_ref[...]] * nseq, axis=1)
    ts = jnp.concatenate([ts_ref[...]] * nseq, axis=1)
    grow = f * fb + lax.broadcasted_iota(jnp.int32, zc.shape, 0)
    is0 = grow == 0
    yc = jnp.where(is0, zc * tc, zc * tc - zsn * ts)
    ys = jnp.where(is0, zsn * ts, zc * ts + zsn * tc)
    acc_ref[...] += (jnp.dot(ct_ref[...], yc.astype(BF16), preferred_element_type=F32)
                     + jnp.dot(st_ref[...], ys.astype(BF16), preferred_element_type=F32))

    @pl.when(f == nf - 1)
    def _():
        for b in range(nseq):
            cols = slice(b * cb, (b + 1) * cb)
            o_ref[b] = (x0c_ref[:, cols] * (acc_ref[:, cols] + zs_ref[:, cols])).astype(o_ref.dtype)


def _hyena(u, first_seq_block, nseq, length, conv_w, conv_b, skip, tc, ts, cmat, smat, stmat):
    cb = 128
    fb = min(length, 512)
    u3 = u.reshape(N_TOK // length, length, EVEN_IN)
    ncb = HY_WIDTH // cb
    width = nseq * cb

    def ublock(part):
        return pl.BlockSpec((nseq, length, cb), lambda c, f: (first_seq_block, 0, part * ncb + c))

    def wblock(part, rows):
        return pl.BlockSpec((rows, cb), lambda c, f: (0, part * ncb + c))

    return pl.pallas_call(
        functools.partial(_hyena_kernel, nseq=nseq, length=length, cb=cb, fb=fb),
        grid=(ncb, length // fb),
        in_specs=[
            ublock(0), ublock(1), ublock(2),
            wblock(0, 3), wblock(1, 3), wblock(2, 3),
            wblock(0, 1), wblock(1, 1), wblock(2, 1),
            pl.BlockSpec((1, cb), lambda c, f: (0, c)),
            pl.BlockSpec((fb, cb), lambda c, f: (f, c)),
            pl.BlockSpec((fb, cb), lambda c, f: (f, c)),
            pl.BlockSpec((fb, length), lambda c, f: (f, 0)),
            pl.BlockSpec((fb, length), lambda c, f: (f, 0)),
            pl.BlockSpec((length, fb), lambda c, f: (0, f)),
            pl.BlockSpec((length, fb), lambda c, f: (0, f)),
        ],
        out_specs=pl.BlockSpec((nseq, length, cb), lambda c, f: (0, 0, c)),
        out_shape=jax.ShapeDtypeStruct((nseq, length, HY_WIDTH), BF16),
        scratch_shapes=[pltpu.VMEM((length, width), BF16), pltpu.VMEM((length, width), F32),
                        pltpu.VMEM((length, width), F32), pltpu.VMEM((length, width), F32)],
        compiler_params=_vmem(48),
        name=f"hyena_{length}",
    )(u3, u3, u3, conv_w, conv_w, conv_w, conv_b.reshape(1, -1), conv_b.reshape(1, -1),
      conv_b.reshape(1, -1), skip.reshape(1, HY_WIDTH), tc, ts, cmat, smat, cmat, stmat
      ).reshape(nseq * length, HY_WIDTH)


def _head_rms(x, seg, gain):
    x2 = x * x
    hi = x2.astype(BF16)
    lo = (x2 - hi.astype(F32)).astype(BF16)
    ss = jnp.dot(hi, seg, preferred_element_type=F32) + jnp.dot(lo, seg, preferred_element_type=F32)
    return x * lax.rsqrt(ss * (1.0 / HEAD_DIM) + EPS) * gain


def _rope(x, cos, sin_signed):
    width = x.shape[1]
    lane = lax.broadcasted_iota(jnp.int32, x.shape, 1)
    first = (lane // ROPE_FREQS) % 2 == 0
    partner = jnp.where(first, pltpu.roll(x, width - ROPE_FREQS, 1), pltpu.roll(x, ROPE_FREQS, 1))
    return x * cos + partner * sin_signed


def _stack_heads(x, g):
    return jnp.concatenate(
        [x[:, (g * ATT_GROUP + j) * HEAD_DIM:(g * ATT_GROUP + j + 1) * HEAD_DIM] for j in range(ATT_GROUP)], axis=0)


def _sink_column(sink_ref, g, rows):
    return jnp.concatenate([jnp.full((rows, 1), sink_ref[g * ATT_GROUP + j], F32) for j in range(ATT_GROUP)], axis=0)


def _unstack_heads(outs, rows):
    return jnp.concatenate([outs[g][j * rows:(j + 1) * rows, :]
                            for g in range(ATT_KV_HEADS) for j in range(ATT_GROUP)], axis=1)


_DOT_NT = (((1,), (1,)), ((), ()))


def _ctx_attn_kernel(sink_ref, q_ref, k_ref, v_ref, seg_ref, qg_ref, kg_ref, o_ref, ko_ref, vo_ref):
    seg = seg_ref[...]
    qn = _head_rms(q_ref[...], seg, qg_ref[...]) * ATT_SCALE
    kn = _head_rms(k_ref[...], seg[:KV_WIDTH, :KV_WIDTH], kg_ref[...])
    v = v_ref[...]
    ko_ref[...] = kn
    vo_ref[...] = v
    outs = []
    for g in range(ATT_KV_HEADS):
        cols = slice(g * HEAD_DIM, (g + 1) * HEAD_DIM)
        q = _stack_heads(qn, g).astype(BF16)
        s = lax.dot_general(q, kn[:, cols].astype(BF16), _DOT_NT, preferred_element_type=F32)
        sink = _sink_column(sink_ref, g, L_CTX)
        m = jnp.maximum(jnp.max(s, axis=-1, keepdims=True), sink)
        p = jnp.exp(s - m)
        den = jnp.sum(p, axis=-1, keepdims=True) + jnp.exp(sink - m)
        o = jnp.dot(p.astype(BF16), v[:, cols].astype(BF16), preferred_element_type=F32)
        outs.append(o / den)
    o_ref[...] = _unstack_heads(outs, L_CTX).astype(o_ref.dtype)


def _ctx_attention(u, seg, q_gain, k_gain, sink):
    qcol = 3 * HY_WIDTH // ATT_WIDTH
    kcol = (3 * HY_WIDTH + ATT_WIDTH) // KV_WIDTH
    return pl.pallas_call(
        _ctx_attn_kernel,
        grid_spec=pltpu.PrefetchScalarGridSpec(
            num_scalar_prefetch=1,
            grid=(N_CTX_SEQ,),
            in_specs=[
                pl.BlockSpec((L_CTX, ATT_WIDTH), lambda b, s: (b, qcol)),
                pl.BlockSpec((L_CTX, KV_WIDTH), lambda b, s: (b, kcol)),
                pl.BlockSpec((L_CTX, KV_WIDTH), lambda b, s: (b, kcol + 1)),
                pl.BlockSpec((ATT_WIDTH, ATT_WIDTH), lambda b, s: (0, 0)),
                pl.BlockSpec((1, ATT_WIDTH), lambda b, s: (0, 0)),
                pl.BlockSpec((1, KV_WIDTH), lambda b, s: (0, 0)),
            ],
            out_specs=[
                pl.BlockSpec((L_CTX, ATT_WIDTH), lambda b, s: (b, 0)),
                pl.BlockSpec((L_CTX, KV_WIDTH), lambda b, s: (b, 0)),
                pl.BlockSpec((L_CTX, KV_WIDTH), lambda b, s: (b, 0)),
            ],
        ),
        out_shape=[jax.ShapeDtypeStruct((N_CTX_TOK, ATT_WIDTH), BF16),
                   jax.ShapeDtypeStruct((N_CTX_TOK, KV_WIDTH), F32),
                   jax.ShapeDtypeStruct((N_CTX_TOK, KV_WIDTH), F32)],
        name="ctx_attention",
    )(sink, u, u, u, seg, jnp.tile(q_gain, ATT_HEADS).reshape(1, ATT_WIDTH),
      jnp.tile(k_gain, ATT_KV_HEADS).reshape(1, KV_WIDTH))


def _rope_tables():
    pos = jnp.arange(L_LAT, dtype=jnp.int32)
    row = (pos // GRID_W).astype(F32)
    col = (pos % GRID_W).astype(F32)
    inv = ROPE_THETA ** (-jnp.arange(ROPE_FREQS, dtype=F32) / ROPE_FREQS)
    ar, ac = row[:, None] * inv, col[:, None] * inv
    cos = jnp.concatenate([jnp.cos(ar), jnp.cos(ar), jnp.cos(ac), jnp.cos(ac)], axis=-1)
    sin = jnp.concatenate([-jnp.sin(ar), jnp.sin(ar), -jnp.sin(ac), jnp.sin(ac)], axis=-1)
    return jnp.tile(cos, (1, ATT_HEADS)), jnp.tile(sin, (1, ATT_HEADS))


def _lat_attn_kernel(sink_ref, q_ref, k_ref, v_ref, ck_ref, cv_ref, cosq_ref, sinq_ref, cosk_ref, sink_k_ref,
                     seg_ref, qg_ref, kg_ref, o_ref, kn_ref):
    n = pl.program_id(1)
    seg = seg_ref[...]

    @pl.when(n == 0)
    def _():
        kn = _head_rms(k_ref[...], seg[:KV_WIDTH, :KV_WIDTH], kg_ref[...])
        kn_ref[...] = _rope(kn, cosk_ref[...], sink_k_ref[...]).astype(BF16)

    qn = _head_rms(q_ref[...], seg, qg_ref[...])
    qn = _rope(qn, cosq_ref[...], sinq_ref[...]) * ATT_SCALE
    span = 3 * ATT_BLOCK
    start = pl.multiple_of(jnp.clip((n - 1) * ATT_BLOCK, 0, L_LAT - span), ATT_BLOCK)
    rows = ATT_GROUP * ATT_BLOCK
    q_pos = n * ATT_BLOCK + (lax.broadcasted_iota(jnp.int32, (rows, span), 0) % ATT_BLOCK)
    k_pos = start + lax.broadcasted_iota(jnp.int32, (rows, span), 1)
    valid = jnp.abs(q_pos - k_pos) <= WINDOW
    k_loc = kn_ref[pl.ds(start, span), :]
    v_loc = v_ref[pl.ds(start, span), :].astype(BF16)
    k_ctx = ck_ref[0].astype(BF16)
    v_ctx = cv_ref[0].astype(BF16)
    outs = []
    for g in range(ATT_KV_HEADS):
        cols = slice(g * HEAD_DIM, (g + 1) * HEAD_DIM)
        q = _stack_heads(qn, g).astype(BF16)
        s_loc = lax.dot_general(q, k_loc[:, cols], _DOT_NT, preferred_element_type=F32)
        s_loc = jnp.where(valid, s_loc, NEG_BIG)
        s_ctx = lax.dot_general(q, k_ctx[:, cols], _DOT_NT, preferred_element_type=F32)
        sink = _sink_column(sink_ref, g, ATT_BLOCK)
        m = jnp.maximum(jnp.maximum(jnp.max(s_loc, axis=-1, keepdims=True),
                                    jnp.max(s_ctx, axis=-1, keepdims=True)), sink)
        p_loc = jnp.exp(s_loc - m)
        p_ctx = jnp.exp(s_ctx - m)
        den = (jnp.sum(p_loc, axis=-1, keepdims=True) + jnp.sum(p_ctx, axis=-1, keepdims=True)
               + jnp.exp(sink - m))
        o = (jnp.dot(p_loc.astype(BF16), v_loc[:, cols], preferred_element_type=F32)
             + jnp.dot(p_ctx.astype(BF16), v_ctx[:, cols], preferred_element_type=F32))
        outs.append(o / den)
    o_ref[...] = _unstack_heads(outs, ATT_BLOCK).astype(o_ref.dtype)


def _lat_attention(u, cache_k, cache_v, seg, q_gain, k_gain, sink):
    qcol = 3 * HY_WIDTH // ATT_WIDTH
    kcol = (3 * HY_WIDTH + ATT_WIDTH) // KV_WIDTH
    nblk = L_LAT // ATT_BLOCK
    first_q_block = N_CTX_TOK // ATT_BLOCK
    first_seq = N_CTX_TOK // L_LAT
    cos, sin = _rope_tables()
    return pl.pallas_call(
        _lat_attn_kernel,
        grid_spec=pltpu.PrefetchScalarGridSpec(
            num_scalar_prefetch=1,
            grid=(N_LAT_SEQ, nblk),
            in_specs=[
                pl.BlockSpec((ATT_BLOCK, ATT_WIDTH), lambda b, n, s: (first_q_block + b * nblk + n, qcol)),
                pl.BlockSpec((L_LAT, KV_WIDTH), lambda b, n, s: (first_seq + b, kcol)),
                pl.BlockSpec((L_LAT, KV_WIDTH), lambda b, n, s: (first_seq + b, kcol + 1)),
                pl.BlockSpec((1, PAST_LEN, KV_WIDTH), lambda b, n, s: (b, 0, 0)),
                pl.BlockSpec((1, PAST_LEN, KV_WIDTH), lambda b, n, s: (b, 0, 0)),
                pl.BlockSpec((ATT_BLOCK, ATT_WIDTH), lambda b, n, s: (n, 0)),
                pl.BlockSpec((ATT_BLOCK, ATT_WIDTH), lambda b, n, s: (n, 0)),
                pl.BlockSpec((L_LAT, KV_WIDTH), lambda b, n, s: (0, 0)),
                pl.BlockSpec((L_LAT, KV_WIDTH), lambda b, n, s: (0, 0)),
                pl.BlockSpec((ATT_WIDTH, ATT_WIDTH), lambda b, n, s: (0, 0)),
                pl.BlockSpec((1, ATT_WIDTH), lambda b, n, s: (0, 0)),
                pl.BlockSpec((1, KV_WIDTH), lambda b, n, s: (0, 0)),
            ],
            out_specs=pl.BlockSpec((ATT_BLOCK, ATT_WIDTH), lambda b, n, s: (b * nblk + n, 0)),
            scratch_shapes=[pltpu.VMEM((L_LAT, KV_WIDTH), BF16)],
        ),
        out_shape=jax.ShapeDtypeStruct((N_LAT_SEQ * L_LAT, ATT_WIDTH), BF16),
        name="lat_attention",
    )(sink, u, u, u, cache_k.reshape(N_LAT_SEQ, PAST_LEN, KV_WIDTH), cache_v.reshape(N_LAT_SEQ, PAST_LEN, KV_WIDTH),
      cos, sin, cos[:, :KV_WIDTH], sin[:, :KV_WIDTH], seg,
      jnp.tile(q_gain, ATT_HEADS).reshape(1, ATT_WIDTH), jnp.tile(k_gain, ATT_KV_HEADS).reshape(1, KV_WIDTH))


def _columns_to_lanes(cols):
    rows = cols[0].shape[0]
    lane = lax.broadcasted_iota(jnp.int32, (rows, len(cols)), 1)
    out = jnp.broadcast_to(cols[-1], (rows, len(cols)))
    for k in range(len(cols) - 1):
        out = jnp.where(lane == k, cols[k], out)
    return out


def _post_mixer(x, mix, wb, gain, mod_ref, rw_ref, rb_ref, x1_ref, xt_ref, idx_ref, tw_ref, tm):
    y = jnp.dot(mix, wb, preferred_element_type=F32)
    x1 = x + mod_ref[0, 2:3, :] * y
    x1_ref[...] = x1
    xt = _norm_mod(x1, gain, mod_ref[0, 3:4, :], mod_ref[0, 4:5, :])
    for s in range(TOK_CHUNKS):
        xt_ref[pl.ds(s, tm, stride=TOK_CHUNKS), :] = xt[:, s * LANES:(s + 1) * LANES]
    logits = jnp.dot(xt, rw_ref[...], preferred_element_type=F32, precision=HIGHEST) + rb_ref[...]
    lane = lax.broadcasted_iota(jnp.int32, logits.shape, 1)
    vals, idxs = [], []
    for _ in range(TOP_K):
        m = jnp.max(logits, axis=-1, keepdims=True)
        sel = jnp.min(jnp.where(logits == m, lane, N_EXPERTS), axis=-1, keepdims=True)
        vals.append(m)
        idxs.append(sel)
        logits = jnp.where(lane == sel, -jnp.inf, logits)
    es = [jnp.exp(v - vals[0]) for v in vals]
    den = es[0] + es[1] + es[2] + es[3]
    idx_ref[...] = _columns_to_lanes(idxs)
    tw_ref[...] = _columns_to_lanes([e / den for e in es])


def _post_even_kernel(x_ref, ac_ref, al_ref, tc_ref, tl_ref, w_ref, gain_ref, mod_ref, rw_ref, rb_ref,
                      x1_ref, xt_ref, idx_ref, tw_ref, wb_ref, *, tm):
    i = pl.program_id(0)

    @pl.when(i == 0)
    def _():
        wb_ref[...] = w_ref[...].astype(BF16)

    is_ctx = i * tm < N_CTX_TOK
    mix = jnp.concatenate([jnp.where(is_ctx, ac_ref[...], al_ref[...]),
                           jnp.where(is_ctx, tc_ref[...], tl_ref[...])], axis=1)
    _post_mixer(x_ref[...], mix, wb_ref[...], gain_ref[...], mod_ref, rw_ref, rb_ref,
                x1_ref, xt_ref, idx_ref, tw_ref, tm)


def _post_odd_kernel(x_ref, yc_ref, yl_ref, w_ref, gain_ref, mod_ref, rw_ref, rb_ref,
                     x1_ref, xt_ref, idx_ref, tw_ref, wb_ref, *, tm):
    i = pl.program_id(0)

    @pl.when(i == 0)
    def _():
        wb_ref[...] = w_ref[...].astype(BF16)

    mix = jnp.where(i * tm < N_CTX_TOK, yc_ref[...], yl_ref[...])
    _post_mixer(x_ref[...], mix, wb_ref[...], gain_ref[...], mod_ref, rw_ref, rb_ref,
                x1_ref, xt_ref, idx_ref, tw_ref, tm)


def _post_out_specs(tm):
    specs = [
        pl.BlockSpec((tm, D_MODEL), lambda i: (i, 0)),
        pl.BlockSpec((tm * TOK_CHUNKS, LANES), lambda i: (i, 0)),
        pl.BlockSpec((tm, TOP_K), lambda i: (i, 0)),
        pl.BlockSpec((tm, TOP_K), lambda i: (i, 0)),
    ]
    shapes = [
        jax.ShapeDtypeStruct((N_TOK, D_MODEL), F32),
        jax.ShapeDtypeStruct((N_TOK * TOK_CHUNKS, LANES), F32),
        jax.ShapeDtypeStruct((N_TOK, TOP_K), jnp.int32),
        jax.ShapeDtypeStruct((N_TOK, TOP_K), F32),
    ]
    return specs, shapes


def _post_tail_specs(k_in, tm):
    return [
        pl.BlockSpec((k_in, D_MODEL), lambda i: (0, 0)),
        pl.BlockSpec((1, D_MODEL), lambda i: (0, 0)),
        pl.BlockSpec((1, 6, D_MODEL), lambda i: (_cond_of_tile(i, tm), 0, 0)),
        pl.BlockSpec((D_MODEL, N_EXPERTS), lambda i: (0, 0)),
        pl.BlockSpec((1, N_EXPERTS), lambda i: (0, 0)),
    ]


def _post_even(x, a_ctx, a_lat, t_ctx, t_lat, w_out, gain, mod, router_w, router_b):
    tm = ROW_TILE
    nctx = N_CTX_TOK // tm
    ctx_map = lambda i: (jnp.minimum(i, nctx - 1), 0)
    lat_map = lambda i: (jnp.maximum(i - nctx, 0), 0)
    out_specs, out_shapes = _post_out_specs(tm)
    return pl.pallas_call(
        functools.partial(_post_even_kernel, tm=tm),
        grid=(N_TOK // tm,),
        in_specs=[
            pl.BlockSpec((tm, D_MODEL), lambda i: (i, 0)),
            pl.BlockSpec((tm, HY_WIDTH), ctx_map), pl.BlockSpec((tm, HY_WIDTH), lat_map),
            pl.BlockSpec((tm, ATT_WIDTH), ctx_map), pl.BlockSpec((tm, ATT_WIDTH), lat_map),
        ] + _post_tail_specs(HY_WIDTH + ATT_WIDTH, tm),
        out_specs=out_specs,
        out_shape=out_shapes,
        scratch_shapes=[pltpu.VMEM((HY_WIDTH + ATT_WIDTH, D_MODEL), BF16)],
        compiler_params=_vmem(48),
        name="post_even",
    )(x, a_ctx, a_lat, t_ctx, t_lat, w_out, gain.reshape(1, D_MODEL), mod, router_w,
      router_b.reshape(1, N_EXPERTS))


def _post_odd(x, y_ctx, y_lat, w_out, gain, mod, router_w, router_b):
    tm = ROW_TILE
    nctx = N_CTX_TOK // tm
    out_specs, out_shapes = _post_out_specs(tm)
    return pl.pallas_call(
        functools.partial(_post_odd_kernel, tm=tm),
        grid=(N_TOK // tm,),
        in_specs=[
            pl.BlockSpec((tm, D_MODEL), lambda i: (i, 0)),
            pl.BlockSpec((tm, RET_V_WIDTH), lambda i: (jnp.minimum(i, nctx - 1), 0)),
            pl.BlockSpec((tm, RET_V_WIDTH), lambda i: (jnp.maximum(i - nctx, 0), 0)),
        ] + _post_tail_specs(RET_V_WIDTH, tm),
        out_specs=out_specs,
        out_shape=out_shapes,
        scratch_shapes=[pltpu.VMEM((RET_V_WIDTH, D_MODEL), BF16)],
        compiler_params=_vmem(48),
        name="post_odd",
    )(x, y_ctx, y_lat, w_out, gain.reshape(1, D_MODEL), mod, router_w, router_b.reshape(1, N_EXPERTS))


def _moe_plan(top_idx):
    t = MOE_TILE
    e_flat = top_idx.reshape(N_PAIRS)
    experts = jnp.arange(N_EXPERTS, dtype=jnp.int32)
    counts = jnp.sum((e_flat.reshape(1, -1, LANES) == experts[:, None, None]).astype(jnp.int32), axis=(1, 2))
    pad = (-counts) % t
    pad_end = jnp.cumsum(pad)
    q = jnp.arange(MOE_NUM_TILES * t - N_PAIRS, dtype=jnp.int32)
    pad_key = jnp.sum((pad_end[None, :] <= q[:, None]).astype(jnp.int32), axis=1)
    order = jnp.argsort(jnp.concatenate([e_flat, pad_key]), stable=True).astype(jnp.int32)
    is_pad = order >= N_PAIRS
    pos = jnp.arange(-t, MOE_NUM_TILES * t, dtype=jnp.int32)
    trash = MOE_TRASH_ROW + ((pos // t) % 2) * t + pos % t
    tok = jnp.concatenate([jnp.zeros((t,), jnp.int32), jnp.where(is_pad, 0, order // TOP_K)])
    dest = jnp.where(jnp.concatenate([jnp.ones((t,), bool), is_pad]), trash,
                     jnp.concatenate([jnp.zeros((t,), jnp.int32), (order % TOP_K) * N_TOK + order // TOP_K]))
    tile_end = jnp.cumsum((counts + pad) // t)
    total = tile_end[-1]
    i = jnp.arange(MOE_NUM_TILES, dtype=jnp.int32)
    e_of = jnp.sum((tile_end[None, :] <= jnp.minimum(i, total - 1)[:, None]).astype(jnp.int32), axis=1)
    return e_of.astype(jnp.int32), (i < total).astype(jnp.int32), tok.astype(jnp.int32), dest.astype(jnp.int32)


def _moe_kernel(te_ref, tv_ref, tok_ref, dest_ref, xt_hbm, w1_ref, b1_ref, w2_ref, b2_ref, y_hbm,
                xbuf0, xbuf1, obuf0, obuf1, w1b, w2b, sem_in, sem_out):
    i = pl.program_id(0)
    nt = pl.num_programs(0)
    t = MOE_TILE
    rows = TOK_CHUNKS
    xbufs = (xbuf0, xbuf1)
    obufs = (obuf0, obuf1)

    def issue_gather(tile, sl):
        first_row = (tile + 1) * t
        for r in range(t):
            tok = tok_ref[first_row + r]
            pltpu.make_async_copy(xt_hbm.at[pl.ds(tok * rows, rows), :],
                                  xbufs[sl].at[pl.ds(r * rows, rows), :], sem_in.at[sl]).start()

    def wait_gather(sl):
        pltpu.make_async_copy(xt_hbm.at[pl.ds(0, t * rows), :], xbufs[sl], sem_in.at[sl]).wait()

    def issue_scatter(tile, sl):
        first_row = (tile + 1) * t
        for r in range(t):
            d = dest_ref[first_row + r]
            pltpu.make_async_copy(obufs[sl].at[pl.ds(r * rows, rows), :],
                                  y_hbm.at[pl.ds(d * rows, rows), :], sem_out.at[sl]).start()

    def wait_scatter(sl):
        pltpu.make_async_copy(obufs[sl], y_hbm.at[pl.ds(0, t * rows), :], sem_out.at[sl]).wait()

    def valid(tile):
        return tv_ref[jnp.clip(tile, 0, nt - 1)] > 0

    @pl.when(i == 0)
    def _():
        obuf0[...] = jnp.zeros_like(obuf0)
        obuf1[...] = jnp.zeros_like(obuf1)
        issue_gather(0, 0)
        pltpu.make_async_copy(obuf0, y_hbm.at[pl.ds(MOE_TRASH_ROW * rows, t * rows), :], sem_out.at[0]).start()

    def tile_step(sl):
        wait_gather(sl)
        first = jnp.logical_or(i == 0, te_ref[i] != te_ref[jnp.maximum(i - 1, 0)])

        @pl.when(first)
        def _():
            w1b[...] = w1_ref[0].astype(BF16)
            w2b[...] = w2_ref[0].astype(BF16)

        issue_gather(jnp.minimum(i + 1, nt - 1), 1 - sl)
        xs = jnp.concatenate([xbufs[sl][pl.ds(s, t, stride=rows), :] for s in range(rows)], axis=1)
        h = jnp.dot(xs.astype(BF16), w1b[...], preferred_element_type=F32) + b1_ref[0]
        glu = jnp.minimum(h[:, :D_FF], SWIGLU_LIMIT)
        lin = jnp.clip(h[:, D_FF:], -SWIGLU_LIMIT, SWIGLU_LIMIT)
        act = (glu * _sigmoid(SWIGLU_ALPHA * glu) * (lin + 1.0)).astype(BF16)
        wait_scatter(sl)
        issue_scatter(i - 1, 1 - sl)
        y = jnp.dot(act, w2b[...], preferred_element_type=F32) + b2_ref[0]
        for s in range(rows):
            obufs[sl][pl.ds(s, t, stride=rows), :] = y[:, s * LANES:(s + 1) * LANES]

    def drain(last, sl):
        wait_gather(1 - sl)
        wait_scatter(1 - sl)
        issue_scatter(last, sl)
        wait_scatter(sl)

    for sl in range(2):
        @pl.when(jnp.logical_and(valid(i), i % 2 == sl))
        def _():
            tile_step(sl)

    ended_before = jnp.logical_and(jnp.logical_not(valid(i)), jnp.logical_and(i >= 1, valid(i - 1)))
    ends_here = jnp.logical_and(valid(i), i == nt - 1)
    last = jnp.where(ends_here, i, jnp.maximum(i - 1, 0))
    for sl in range(2):
        @pl.when(jnp.logical_and(jnp.logical_or(ended_before, ends_here), last % 2 == sl))
        def _():
            drain(last, sl)


def _moe_experts(xt_tiles, plan, w1, b1, w2, b2):
    te, tv, tok, dest = plan
    t = MOE_TILE
    return pl.pallas_call(
        _moe_kernel,
        grid_spec=pltpu.PrefetchScalarGridSpec(
            num_scalar_prefetch=4,
            grid=(MOE_NUM_TILES,),
            in_specs=[
                pl.BlockSpec(memory_space=pl.ANY),
                pl.BlockSpec((1, D_MODEL, 2 * D_FF), lambda i, te, *_: (te[i], 0, 0)),
                pl.BlockSpec((1, 1, 2 * D_FF), lambda i, te, *_: (te[i], 0, 0)),
                pl.BlockSpec((1, D_FF, D_MODEL), lambda i, te, *_: (te[i], 0, 0)),
                pl.BlockSpec((1, 1, D_MODEL), lambda i, te, *_: (te[i], 0, 0)),
            ],
            out_specs=pl.BlockSpec(memory_space=pl.ANY),
            scratch_shapes=[
                pltpu.VMEM((t * TOK_CHUNKS, LANES), F32),
                pltpu.VMEM((t * TOK_CHUNKS, LANES), F32),
                pltpu.VMEM((t * TOK_CHUNKS, LANES), F32),
                pltpu.VMEM((t * TOK_CHUNKS, LANES), F32),
                pltpu.VMEM((D_MODEL, 2 * D_FF), BF16),
                pltpu.VMEM((D_FF, D_MODEL), BF16),
                pltpu.SemaphoreType.DMA((2,)),
                pltpu.SemaphoreType.DMA((2,)),
            ],
        ),
        out_shape=jax.ShapeDtypeStruct((MOE_OUT_ROWS * TOK_CHUNKS, LANES), F32),
        compiler_params=_vmem(56),
        name="moe_experts",
    )(te, tv, tok, dest, xt_tiles, w1, b1.reshape(N_EXPERTS, 1, 2 * D_FF), w2,
      b2.reshape(N_EXPERTS, 1, D_MODEL))


def _combine_value(x1_ref, y_refs, tw_ref, mod_ref, tm):
    tw = tw_ref[...]
    chunks = []
    for s in range(TOK_CHUNKS):
        acc = tw[:, 0:1] * y_refs[0][pl.ds(s, tm, stride=TOK_CHUNKS), :]
        for k in range(1, TOP_K):
            acc = acc + tw[:, k:k + 1] * y_refs[k][pl.ds(s, tm, stride=TOK_CHUNKS), :]
        chunks.append(acc)
    return x1_ref[...] + mod_ref[0, 5:6, :] * jnp.concatenate(chunks, axis=1)


def _combine_kernel(x1_ref, y0_ref, y1_ref, y2_ref, y3_ref, tw_ref, mod_ref, o_ref, *, tm):
    o_ref[...] = _combine_value(x1_ref, (y0_ref, y1_ref, y2_ref, y3_ref), tw_ref, mod_ref, tm)


def _combine_split_kernel(x1_ref, y0_ref, y1_ref, y2_ref, y3_ref, tw_ref, mod_ref, oc_ref, ol_ref, *, tm):
    val = _combine_value(x1_ref, (y0_ref, y1_ref, y2_ref, y3_ref), tw_ref, mod_ref, tm)
    is_ctx = pl.program_id(0) * tm < N_CTX_TOK

    @pl.when(is_ctx)
    def _():
        oc_ref[...] = val

    @pl.when(jnp.logical_not(is_ctx))
    def _():
        ol_ref[...] = val


def _combine(x1, y_tiles, top_w, mod, split):
    tm = 256
    nctx = N_CTX_TOK // tm
    ntile = N_TOK // tm
    slab = lambda k: pl.BlockSpec((tm * TOK_CHUNKS, LANES), lambda i: (k * ntile + i, 0))
    in_specs = [
        pl.BlockSpec((tm, D_MODEL), lambda i: (i, 0)),
        slab(0), slab(1), slab(2), slab(3),
        pl.BlockSpec((tm, TOP_K), lambda i: (i, 0)),
        pl.BlockSpec((1, 6, D_MODEL), lambda i: (_cond_of_tile(i, tm), 0, 0)),
    ]
    if not split:
        return pl.pallas_call(
            functools.partial(_combine_kernel, tm=tm),
            grid=(N_TOK // tm,),
            in_specs=in_specs,
            out_specs=pl.BlockSpec((tm, D_MODEL), lambda i: (i, 0)),
            out_shape=jax.ShapeDtypeStruct((N_TOK, D_MODEL), F32),
            name="moe_combine",
        )(x1, y_tiles, y_tiles, y_tiles, y_tiles, top_w, mod)
    return pl.pallas_call(
        functools.partial(_combine_split_kernel, tm=tm),
        grid=(N_TOK // tm,),
        in_specs=in_specs,
        out_specs=[pl.BlockSpec((tm, D_MODEL), lambda i: (jnp.minimum(i, nctx - 1), 0)),
                   pl.BlockSpec((tm, D_MODEL), lambda i: (jnp.maximum(i - nctx, 0), 0))],
        out_shape=[jax.ShapeDtypeStruct((N_CTX_TOK, D_MODEL), F32),
                   jax.ShapeDtypeStruct((N_TOK - N_CTX_TOK, D_MODEL), F32)],
        name="moe_combine_split",
    )(x1, y_tiles, y_tiles, y_tiles, y_tiles, top_w, mod)


def _retention_kernel(lg_ref, q_ref, k_ref, v_ref, gf_ref, gb_ref, *rest, length, has_s0, emit_state):
    rest = list(rest)
    s0_ref = rest.pop(0) if has_s0 else None
    o_ref = rest.pop(0)
    so_ref = rest.pop(0) if emit_state else None
    s_ref, yf_ref = rest
    c = RET_CHUNK
    nc = length // c
    ii = lax.broadcasted_iota(jnp.int32, (c, c), 0).astype(F32)
    jj = lax.broadcasted_iota(jnp.int32, (c, c), 1).astype(F32)
    ci = lax.broadcasted_iota(jnp.int32, (c, 1), 0).astype(F32)

    def scan(direction):
        lg = -jnp.exp(lg_ref[direction, 0])
        lg1 = lg[:, 0:1]
        if direction == 0:
            diff = ii - jj
            q_decay = jnp.exp(lg1 * (ci + 1.0))
            k_decay = jnp.exp(lg1 * (c - 1.0 - ci))
        else:
            diff = jj - ii
            q_decay = jnp.exp(lg1 * (c - ci))
            k_decay = jnp.exp(lg1 * ci)
        inner = jnp.where(diff >= 0, jnp.exp(lg * jnp.maximum(diff, 0.0)), 0.0)
        chunk_decay = jnp.exp(lg1 * float(c))
        if has_s0:
            s_ref[...] = s0_ref[0, direction, 0]
        else:
            s_ref[...] = jnp.zeros_like(s_ref)

        def body(step, carry):
            ch = step if direction == 0 else nc - 1 - step
            rows = pl.ds(pl.multiple_of(ch * c, c), c)
            qc = q_ref[rows, :]
            kc = k_ref[rows, :].astype(F32) * (RET_DK ** -0.5)
            vc = v_ref[rows, :]
            s = s_ref[...]
            att = lax.dot_general(qc, kc.astype(BF16), _DOT_NT, preferred_element_type=F32) * inner
            o = (jnp.dot(att.astype(BF16), vc, preferred_element_type=F32)
                 + jnp.dot(qc, s.astype(BF16), preferred_element_type=F32) * q_decay)
            kd = (kc * k_decay).T.astype(BF16)
            s_ref[...] = s * chunk_decay + jnp.dot(kd, vc, preferred_element_type=F32)
            on = o * lax.rsqrt(jnp.mean(o * o, axis=-1, keepdims=True) + EPS)
            if direction == 0:
                g = gf_ref[rows, :].astype(F32)
                yf_ref[rows, :] = g * _sigmoid(g) * on
            else:
                g = gb_ref[rows, :].astype(F32)
                o_ref[rows, :] = (yf_ref[rows, :] + g * _sigmoid(g) * on).astype(o_ref.dtype)
            return carry

        lax.fori_loop(0, nc, body, 0)
        if emit_state:
            so_ref[0, direction, 0] = s_ref[...]

    scan(0)
    scan(1)


def _retention(u, first_seq, nseq, length, decay_logit, s0, emit_state):
    row0 = first_seq
    lg = jnp.broadcast_to(decay_logit.astype(F32)[:, :, None, None], (2, RET_HEADS, 1, LANES))
    kcol = RET_QK_WIDTH // RET_DK
    vcol = 2 * RET_QK_WIDTH // RET_DV
    gfcol = vcol + RET_HEADS
    gbcol = gfcol + RET_HEADS
    in_specs = [
        pl.BlockSpec((2, 1, 1, LANES), lambda b, h: (0, h, 0, 0)),
        pl.BlockSpec((length, RET_DK), lambda b, h: (row0 + b, h)),
        pl.BlockSpec((length, RET_DK), lambda b, h: (row0 + b, kcol + h)),
        pl.BlockSpec((length, RET_DV), lambda b, h: (row0 + b, vcol + h)),
        pl.BlockSpec((length, RET_DV), lambda b, h: (row0 + b, gfcol + h)),
        pl.BlockSpec((length, RET_DV), lambda b, h: (row0 + b, gbcol + h)),
    ]
    args = [lg, u, u, u, u, u]
    state_spec = pl.BlockSpec((1, 2, 1, RET_DK, RET_DV), lambda b, h: (b, 0, h, 0, 0))
    if s0 is not None:
        in_specs.append(state_spec)
        args.append(s0)
    out_specs = [pl.BlockSpec((length, RET_DV), lambda b, h: (b, h))]
    out_shape = [jax.ShapeDtypeStruct((nseq * length, RET_V_WIDTH), BF16)]
    if emit_state:
        out_specs.append(state_spec)
        out_shape.append(jax.ShapeDtypeStruct((nseq, 2, RET_HEADS, RET_DK, RET_DV), F32))
    return pl.pallas_call(
        functools.partial(_retention_kernel, length=length, has_s0=s0 is not None, emit_state=emit_state),
        grid=(nseq, RET_HEADS),
        in_specs=in_specs,
        out_specs=out_specs,
        out_shape=out_shape,
        scratch_shapes=[pltpu.VMEM((RET_DK, RET_DV), F32), pltpu.VMEM((length, RET_DV), F32)],
        compiler_params=_vmem(48),
        name=f"retention_{length}",
    )(*args)


def kernel(x_prompt, x_sample, cache_k0, cache_v0, state_ret1, c, c_ctx, l0_norm_mix, l0_ada_w, l0_ada_b, l0_w_in, l0_conv_w, l0_conv_b, l0_filt_w1, l0_filt_b1, l0_filt_freq, l0_filt_w2, l0_filt_b2, l0_filt_w3, l0_filt_deltas, l0_hy_skip, l0_q_gain, l0_k_gain, l0_sink, l0_w_out, l0_norm_ffn, l0_router_w, l0_router_b, l0_moe_w1, l0_moe_b1, l0_moe_w2, l0_moe_b2, l1_norm_mix, l1_ada_w, l1_ada_b, l1_w_in, l1_ret_decay_logit, l1_w_out, l1_norm_ffn, l1_router_w, l1_router_b, l1_moe_w1, l1_moe_b1, l1_moe_w2, l1_moe_b2):
    x = jnp.concatenate([x_prompt.reshape(N_CTX_TOK, D_MODEL), x_sample.reshape(N_TOK - N_CTX_TOK, D_MODEL)], axis=0)
    cond = jnp.zeros((SUBLANES, D_MODEL), F32).at[0].set(c_ctx).at[1:1 + N_LAT_SEQ].set(c)
    mod0 = _adaln(cond, l0_ada_w, l0_ada_b)
    mod1 = _adaln(cond, l1_ada_w, l1_ada_b)

    u = _in_proj(x, l0_norm_mix, mod0, l0_w_in, EVEN_IN // 2, F32)
    filt = (l0_filt_w1, l0_filt_b1, l0_filt_freq, l0_filt_w2, l0_filt_b2, l0_filt_w3, l0_filt_deltas)
    hy = []
    for first_block, nseq, length in ((0, N_CTX_SEQ, L_CTX), (N_CTX_TOK // L_LAT // N_LAT_SEQ, N_LAT_SEQ, L_LAT)):
        cmat, smat, stmat = _dft_matrices(length)
        tc, ts = _hyena_filter(length, _filter_features(length), cmat, smat, *filt)
        hy.append(_hyena(u, first_block, nseq, length, l0_conv_w, l0_conv_b, l0_hy_skip, tc, ts, cmat, smat, stmat))
    head = lax.broadcasted_iota(jnp.int32, (ATT_WIDTH, ATT_WIDTH), 0) // HEAD_DIM
    seg = (head == head.T).astype(BF16)
    att_ctx, new_k, new_v = _ctx_attention(u, seg, l0_q_gain, l0_k_gain, l0_sink)
    att_lat = _lat_attention(u, cache_k0, cache_v0, seg, l0_q_gain, l0_k_gain, l0_sink)
    x1, xt, top_idx, top_w = _post_even(x, hy[0], hy[1], att_ctx, att_lat, l0_w_out, l0_norm_ffn, mod0,
                                        l0_router_w, l0_router_b)
    y = _moe_experts(xt, _moe_plan(top_idx), l0_moe_w1, l0_moe_b1, l0_moe_w2, l0_moe_b2)
    x = _combine(x1, y, top_w, mod0, split=False)

    u = _in_proj(x, l1_norm_mix, mod1, l1_w_in, 2048, BF16)
    y_ctx, new_state = _retention(u, 0, N_CTX_SEQ, L_CTX, l1_ret_decay_logit, None, True)
    (y_lat,) = _retention(u, N_CTX_TOK // L_LAT, N_LAT_SEQ, L_LAT, l1_ret_decay_logit, state_ret1, False)
    x1, xt, top_idx, top_w = _post_odd(x, y_ctx, y_lat, l1_w_out, l1_norm_ffn, mod1, l1_router_w, l1_router_b)
    y = _moe_experts(xt, _moe_plan(top_idx), l1_moe_w1, l1_moe_b1, l1_moe_w2, l1_moe_b2)
    y_prompt, y_sample = _combine(x1, y, top_w, mod1, split=True)

    return (y_prompt.reshape(N_CTX_SEQ, L_CTX, D_MODEL), y_sample.reshape(N_LAT_SEQ, L_LAT, D_MODEL),
            new_k.reshape(N_CTX_SEQ, L_CTX, ATT_KV_HEADS, HEAD_DIM),
            new_v.reshape(N_CTX_SEQ, L_CTX, ATT_KV_HEADS, HEAD_DIM), new_state)
```

```python
import functools
import math

import jax
import jax.numpy as jnp
from jax import lax
from jax.experimental import pallas as pl
from jax.experimental.pallas import tpu as pltpu

F32 = jnp.float32
BF16 = jnp.bfloat16
HIGHEST = lax.Precision.HIGHEST

D_MODEL = 1024
N_CTX_SEQ, L_CTX = 16, 256
N_LAT_SEQ, L_LAT = 2, 2048
N_CTX_TOK = N_CTX_SEQ * L_CTX
N_TOK = N_CTX_TOK + N_LAT_SEQ * L_LAT
PAST_LEN = 512
EPS = 1e-6
NEG_BIG = -1e30

HY_WIDTH = 512
HY_BANDS = 16
HY_FILTER_HIDDEN = 64
HY_FEAT_PAD = 64

ATT_HEADS, ATT_KV_HEADS, HEAD_DIM = 8, 2, 64
ATT_GROUP = ATT_HEADS // ATT_KV_HEADS
ATT_WIDTH = ATT_HEADS * HEAD_DIM
KV_WIDTH = ATT_KV_HEADS * HEAD_DIM
ATT_SCALE = HEAD_DIM ** -0.5
WINDOW = 128
ATT_BLOCK = 128
ROPE_THETA = 10000.0
ROPE_FREQS = HEAD_DIM // 4
GRID_W = 64
EVEN_IN = 3 * HY_WIDTH + ATT_WIDTH + 2 * KV_WIDTH

RET_HEADS = 4
RET_DK = 256
RET_DV = 512
RET_CHUNK = 128
RET_QK_WIDTH = RET_HEADS * RET_DK
RET_V_WIDTH = RET_HEADS * RET_DV
ODD_IN = 2 * RET_QK_WIDTH + 3 * RET_V_WIDTH

N_EXPERTS = 32
TOP_K = 4
D_FF = 1024
SWIGLU_ALPHA = 1.702
SWIGLU_LIMIT = 7.0

SUBLANES = 8
LANES = 128
TOK_CHUNKS = D_MODEL // LANES

MOE_TILE = 256
N_PAIRS = N_TOK * TOP_K
MOE_NUM_TILES = N_PAIRS // MOE_TILE + N_EXPERTS
MOE_TRASH_ROW = N_PAIRS
MOE_OUT_ROWS = N_PAIRS + 2 * MOE_TILE
MOE_CAPACITY = N_TOK + MOE_TILE

ROW_TILE = 512


def _vmem(mib):
    return pltpu.CompilerParams(vmem_limit_bytes=mib * 1024 * 1024)


def _cond_of_tile(i, tm):
    row = i * tm
    return jnp.where(row < N_CTX_TOK, 0, 1 + (row - N_CTX_TOK) // L_LAT)


def _sigmoid(x):
    return 1.0 / (1.0 + jnp.exp(-x))


def _norm_mod(x, gain, shift, scale):
    ms = jnp.mean(x * x, axis=-1, keepdims=True)
    return (x * lax.rsqrt(ms + EPS) * gain) * (1.0 + scale) + shift


def _adaln_kernel(c_ref, w_ref, b_ref, o_ref):
    c = c_ref[...]
    s = c * _sigmoid(c)
    o_ref[...] = jnp.dot(s, w_ref[...], preferred_element_type=F32, precision=HIGHEST) + b_ref[...]


def _adaln(cond, w, b):
    n = w.shape[1]
    tn = 1024
    out = pl.pallas_call(
        _adaln_kernel,
        grid=(n // tn,),
        in_specs=[
            pl.BlockSpec((SUBLANES, D_MODEL), lambda j: (0, 0)),
            pl.BlockSpec((D_MODEL, tn), lambda j: (0, j)),
            pl.BlockSpec((1, tn), lambda j: (0, j)),
        ],
        out_specs=pl.BlockSpec((SUBLANES, tn), lambda j: (0, j)),
        out_shape=jax.ShapeDtypeStruct((SUBLANES, n), F32),
        name="adaln",
    )(cond, w, b.reshape(1, n))
    return out.reshape(SUBLANES, 6, D_MODEL)


def _in_proj_kernel(x_ref, gain_ref, mod_ref, w_ref, o_ref, wb_ref):
    @pl.when(pl.program_id(1) == 0)
    def _():
        wb_ref[...] = w_ref[...].astype(BF16)

    h = _norm_mod(x_ref[...], gain_ref[...], mod_ref[0, 0:1, :], mod_ref[0, 1:2, :])
    o_ref[...] = jnp.dot(h.astype(BF16), wb_ref[...], preferred_element_type=F32).astype(o_ref.dtype)


def _in_proj(x, gain, mod, w, tn, out_dtype):
    n = w.shape[1]
    tm = ROW_TILE
    return pl.pallas_call(
        _in_proj_kernel,
        grid=(n // tn, N_TOK // tm),
        in_specs=[
            pl.BlockSpec((tm, D_MODEL), lambda j, i: (i, 0)),
            pl.BlockSpec((1, D_MODEL), lambda j, i: (0, 0)),
            pl.BlockSpec((1, 6, D_MODEL), lambda j, i: (_cond_of_tile(i, tm), 0, 0)),
            pl.BlockSpec((D_MODEL, tn), lambda j, i: (0, j)),
        ],
        out_specs=pl.BlockSpec((tm, tn), lambda j, i: (i, j)),
        out_shape=jax.ShapeDtypeStruct((N_TOK, n), out_dtype),
        scratch_shapes=[pltpu.VMEM((D_MODEL, tn), BF16)],
        compiler_params=_vmem(48),
        name="in_proj",
    )(x, gain.reshape(1, D_MODEL), mod, w)


def _dft_matrices(length):
    n = 2 * length
    lo = 16
    s = jnp.arange(length, dtype=jnp.int32)
    k1 = jnp.arange(length // lo, dtype=jnp.int32) * lo
    k0 = jnp.arange(lo, dtype=jnp.int32)
    ang1 = (2.0 * math.pi / n) * ((k1[:, None] * s[None, :]) % n).astype(F32)
    ang0 = (2.0 * math.pi / n) * ((k0[:, None] * s[None, :]) % n).astype(F32)
    c1, s1 = jnp.cos(ang1)[:, None, :], jnp.sin(ang1)[:, None, :]
    c0, s0 = jnp.cos(ang0)[None, :, :], jnp.sin(ang0)[None, :, :]
    cmat = (c1 * c0 - s1 * s0).reshape(length, length)
    smat = (s1 * c0 + c1 * s0).reshape(length, length)
    sign = jnp.where(s % 2 == 0, 1.0, -1.0).astype(F32)
    row = lax.broadcasted_iota(jnp.int32, (length, length), 0)
    col = lax.broadcasted_iota(jnp.int32, (length, length), 1)
    s_nyq = jnp.where(row == 0, sign[None, :], smat)
    st_nyq = jnp.where(col == 0, sign[:, None], smat)
    return cmat.astype(BF16), s_nyq.astype(BF16), st_nyq.astype(BF16)


def _filter_features(length):
    t = jnp.linspace(0.0, 1.0, length, dtype=F32)[:, None]
    w = 2.0 * math.pi * jnp.arange(length, dtype=F32)[:, None] / length
    f = jnp.linspace(1e-4, HY_BANDS - 1, HY_BANDS, dtype=F32)[None, :]
    z = jnp.concatenate([t, jnp.cos(f * w), -jnp.sin(f * w)], axis=-1)
    return jnp.pad(z, ((0, 0), (0, HY_FEAT_PAD - z.shape[1])))


def _filter_kernel(z_ref, w1_ref, b1_ref, fr_ref, w2_ref, b2_ref, w3_ref, dl_ref, c_ref, s_ref,
                   tc_ref, ts_ref, taps_ref, *, length, rb):
    r = pl.program_id(0)

    @pl.when(r == 0)
    def _():
        z = z_ref[...]
        fr = fr_ref[...]
        h = jnp.sin(fr * (jnp.dot(z, w1_ref[...], preferred_element_type=F32, precision=HIGHEST) + b1_ref[...]))
        h = jnp.sin(fr * (jnp.dot(h, w2_ref[...], preferred_element_type=F32, precision=HIGHEST) + b2_ref[...]))
        h = jnp.dot(h, w3_ref[...], preferred_element_type=F32, precision=HIGHEST)
        win = jnp.exp(-z[:, 0:1] * jnp.abs(dl_ref[...]))
        hf = h[:, :HY_WIDTH] * win
        hb = h[:, HY_WIDTH:] * win
        row = lax.broadcasted_iota(jnp.int32, (length, HY_WIDTH), 0)
        hb = jnp.where(row == 0, 0.0, hb)
        l1 = jnp.sum(jnp.abs(hf), axis=0, keepdims=True) + jnp.sum(jnp.abs(hb), axis=0, keepdims=True)
        inv = 1.0 / l1
        taps_ref[:, :HY_WIDTH] = (hf * inv).astype(BF16)
        taps_ref[:, HY_WIDTH:] = (hb * inv).astype(BF16)

    taps = taps_ref[...]
    rc = jnp.dot(c_ref[...], taps, preferred_element_type=F32)
    rs = jnp.dot(s_ref[...], taps, preferred_element_type=F32)
    tc = rc[:, :HY_WIDTH] + rc[:, HY_WIDTH:]
    ts = rs[:, :HY_WIDTH] - rs[:, HY_WIDTH:]
    grow = r * rb + lax.broadcasted_iota(jnp.int32, (rb, HY_WIDTH), 0)
    is0 = grow == 0
    ts = jnp.where(is0, rs[:, :HY_WIDTH] + rs[:, HY_WIDTH:], ts)
    wgt = jnp.where(is0, 1.0 / (2 * length), 2.0 / (2 * length))
    tc_ref[...] = tc * wgt
    ts_ref[...] = ts * wgt


def _hyena_filter(length, feats, cmat, smat, w1, b1, freq, w2, b2, w3, deltas):
    rb = min(length, 512)
    w1p = jnp.pad(w1, ((0, HY_FEAT_PAD - w1.shape[0]), (0, 0)))
    full = lambda shape: pl.BlockSpec(shape, lambda r: (0,) * len(shape))
    hid = HY_FILTER_HIDDEN
    return pl.pallas_call(
        functools.partial(_filter_kernel, length=length, rb=rb),
        grid=(length // rb,),
        in_specs=[
            full((length, HY_FEAT_PAD)), full((HY_FEAT_PAD, hid)), full((1, hid)), full((1, hid)),
            full((hid, hid)), full((1, hid)), full((hid, 2 * HY_WIDTH)), full((1, HY_WIDTH)),
            pl.BlockSpec((rb, length), lambda r: (r, 0)),
            pl.BlockSpec((rb, length), lambda r: (r, 0)),
        ],
        out_specs=[pl.BlockSpec((rb, HY_WIDTH), lambda r: (r, 0)),
                   pl.BlockSpec((rb, HY_WIDTH), lambda r: (r, 0))],
        out_shape=[jax.ShapeDtypeStruct((length, HY_WIDTH), F32)] * 2,
        scratch_shapes=[pltpu.VMEM((length, 2 * HY_WIDTH), BF16)],
        compiler_params=_vmem(48),
        name=f"hyena_filter_{length}",
    )(feats, w1p, b1.reshape(1, hid), freq.reshape(1, hid), w2, b2.reshape(1, hid), w3,
      deltas.reshape(1, HY_WIDTH), cmat, smat)


def _hyena_kernel(x0_ref, x1_ref, v_ref, w0_ref, w1_ref, w2_ref, b0_ref, b1_ref, b2_ref, skip_ref,
                  tc_ref, ts_ref, c_ref, s_ref, ct_ref, st_ref, o_ref,
                  zb_ref, zs_ref, x0c_ref, acc_ref, *, nseq, length, cb, fb):
    f = pl.program_id(1)
    nf = pl.num_programs(1)

    def short_conv(u, w_ref, b_ref):
        row = lax.broadcasted_iota(jnp.int32, u.shape, 0)
        prev = jnp.where(row == 0, 0.0, pltpu.roll(u, 1, 0))
        nxt = jnp.where(row == length - 1, 0.0, pltpu.roll(u, length - 1, 0))
        return prev * w_ref[0:1, :] + u * w_ref[1:2, :] + nxt * w_ref[2:3, :] + b_ref[...]

    @pl.when(f == 0)
    def _():
        for b in range(nseq):
            cols = slice(b * cb, (b + 1) * cb)
            x0c_ref[:, cols] = short_conv(x0_ref[b], w0_ref, b0_ref)
            z = short_conv(x1_ref[b], w1_ref, b1_ref) * short_conv(v_ref[b], w2_ref, b2_ref)
            zb_ref[:, cols] = z.astype(BF16)
            zs_ref[:, cols] = z * skip_ref[...]
        acc_ref[...] = jnp.zeros_like(acc_ref)

    zb = zb_ref[...]
    zc = jnp.dot(c_ref[...], zb, preferred_element_type=F32)
    zsn = jnp.dot(s_ref[...], zb, preferred_element_type=F32)
    tc = jnp.concatenate([tc_ref[...]] * nseq, axis=1)
    ts = jnp.concatenate([ts_ref[...]] * nseq, axis=1)
    grow = f * fb + lax.broadcasted_iota(jnp.int32, zc.shape, 0)
    is0 = grow == 0
    yc = jnp.where(is0, zc * tc, zc * tc - zsn * ts)
    ys = jnp.where(is0, zsn * ts, zc * ts + zsn * tc)
    acc_ref[...] += (jnp.dot(ct_ref[...], yc.astype(BF16), preferred_element_type=F32)
                     + jnp.dot(st_ref[...], ys.astype(BF16), preferred_element_type=F32))

    @pl.when(f == nf - 1)
    def _():
        for b in range(nseq):
            cols = slice(b * cb, (b + 1) * cb)
            o_ref[b] = (x0c_ref[:, cols] * (acc_ref[:, cols] + zs_ref[:, cols])).astype(o_ref.dtype)


def _hyena(u, first_seq_block, nseq, length, conv_w, conv_b, skip, tc, ts, cmat, smat, stmat):
    cb = 128
    fb = min(length, 512)
    u3 = u.reshape(N_TOK // length, length, EVEN_IN)
    ncb = HY_WIDTH // cb
    width = nseq * cb

    def ublock(part):
        return pl.BlockSpec((nseq, length, cb), lambda c, f: (first_seq_block, 0, part * ncb + c))

    def wblock(part, rows):
        return pl.BlockSpec((rows, cb), lambda c, f: (0, part * ncb + c))

    return pl.pallas_call(
        functools.partial(_hyena_kernel, nseq=nseq, length=length, cb=cb, fb=fb),
        grid=(ncb, length // fb),
        in_specs=[
            ublock(0), ublock(1), ublock(2),
            wblock(0, 3), wblock(1, 3), wblock(2, 3),
            wblock(0, 1), wblock(1, 1), wblock(2, 1),
            pl.BlockSpec((1, cb), lambda c, f: (0, c)),
            pl.BlockSpec((fb, cb), lambda c, f: (f, c)),
            pl.BlockSpec((fb, cb), lambda c, f: (f, c)),
            pl.BlockSpec((fb, length), lambda c, f: (f, 0)),
            pl.BlockSpec((fb, length), lambda c, f: (f, 0)),
            pl.BlockSpec((length, fb), lambda c, f: (0, f)),
            pl.BlockSpec((length, fb), lambda c, f: (0, f)),
        ],
        out_specs=pl.BlockSpec((nseq, length, cb), lambda c, f: (0, 0, c)),
        out_shape=jax.ShapeDtypeStruct((nseq, length, HY_WIDTH), BF16),
        scratch_shapes=[pltpu.VMEM((length, width), BF16), pltpu.VMEM((length, width), F32),
                        pltpu.VMEM((length, width), F32), pltpu.VMEM((length, width), F32)],
        compiler_params=_vmem(48),
        name=f"hyena_{length}",
    )(u3, u3, u3, conv_w, conv_w, conv_w, conv_b.reshape(1, -1), conv_b.reshape(1, -1),
      conv_b.reshape(1, -1), skip.reshape(1, HY_WIDTH), tc, ts, cmat, smat, cmat, stmat
      ).reshape(nseq * length, HY_WIDTH)


def _head_rms(x, seg, gain):
    x2 = x * x
    hi = x2.astype(BF16)
    lo = (x2 - hi.astype(F32)).astype(BF16)
    ss = jnp.dot(hi, seg, preferred_element_type=F32) + jnp.dot(lo, seg, preferred_element_type=F32)
    return x * lax.rsqrt(ss * (1.0 / HEAD_DIM) + EPS) * gain


def _rope(x, cos, sin_signed):
    width = x.shape[1]
    lane = lax.broadcasted_iota(jnp.int32, x.shape, 1)
    first = (lane // ROPE_FREQS) % 2 == 0
    partner = jnp.where(first, pltpu.roll(x, width - ROPE_FREQS, 1), pltpu.roll(x, ROPE_FREQS, 1))
    return x * cos + partner * sin_signed


def _stack_heads(x, g):
    return jnp.concatenate(
        [x[:, (g * ATT_GROUP + j) * HEAD_DIM:(g * ATT_GROUP + j + 1) * HEAD_DIM] for j in range(ATT_GROUP)], axis=0)


def _sink_column(sink_ref, g, rows):
    return jnp.concatenate([jnp.full((rows, 1), sink_ref[g * ATT_GROUP + j], F32) for j in range(ATT_GROUP)], axis=0)


def _unstack_heads(outs, rows):
    return jnp.concatenate([outs[g][j * rows:(j + 1) * rows, :]
                            for g in range(ATT_KV_HEADS) for j in range(ATT_GROUP)], axis=1)


_DOT_NT = (((1,), (1,)), ((), ()))


def _ctx_attn_kernel(sink_ref, q_ref, k_ref, v_ref, seg_ref, qg_ref, kg_ref, o_ref, ko_ref, vo_ref):
    seg = seg_ref[...]
    qn = _head_rms(q_ref[...], seg, qg_ref[...]) * ATT_SCALE
    kn = _head_rms(k_ref[...], seg[:KV_WIDTH, :KV_WIDTH], kg_ref[...])
    v = v_ref[...]
    ko_ref[...] = kn
    vo_ref[...] = v
    outs = []
    for g in range(ATT_KV_HEADS):
        cols = slice(g * HEAD_DIM, (g + 1) * HEAD_DIM)
        q = _stack_heads(qn, g).astype(BF16)
        s = lax.dot_general(q, kn[:, cols].astype(BF16), _DOT_NT, preferred_element_type=F32)
        sink = _sink_column(sink_ref, g, L_CTX)
        m = jnp.maximum(jnp.max(s, axis=-1, keepdims=True), sink)
        p = jnp.exp(s - m)
        den = jnp.sum(p, axis=-1, keepdims=True) + jnp.exp(sink - m)
        o = jnp.dot(p.astype(BF16), v[:, cols].astype(BF16), preferred_element_type=F32)
        outs.append(o / den)
    o_ref[...] = _unstack_heads(outs, L_CTX).astype(o_ref.dtype)


def _ctx_attention(u, seg, q_gain, k_gain, sink):
    qcol = 3 * HY_WIDTH // ATT_WIDTH
    kcol = (3 * HY_WIDTH + ATT_WIDTH) // KV_WIDTH
    return pl.pallas_call(
        _ctx_attn_kernel,
        grid_spec=pltpu.PrefetchScalarGridSpec(
            num_scalar_prefetch=1,
            grid=(N_CTX_SEQ,),
            in_specs=[
                pl.BlockSpec((L_CTX, ATT_WIDTH), lambda b, s: (b, qcol)),
                pl.BlockSpec((L_CTX, KV_WIDTH), lambda b, s: (b, kcol)),
                pl.BlockSpec((L_CTX, KV_WIDTH), lambda b, s: (b, kcol + 1)),
                pl.BlockSpec((ATT_WIDTH, ATT_WIDTH), lambda b, s: (0, 0)),
                pl.BlockSpec((1, ATT_WIDTH), lambda b, s: (0, 0)),
                pl.BlockSpec((1, KV_WIDTH), lambda b, s: (0, 0)),
            ],
            out_specs=[
                pl.BlockSpec((L_CTX, ATT_WIDTH), lambda b, s: (b, 0)),
                pl.BlockSpec((L_CTX, KV_WIDTH), lambda b, s: (b, 0)),
                pl.BlockSpec((L_CTX, KV_WIDTH), lambda b, s: (b, 0)),
            ],
        ),
        out_shape=[jax.ShapeDtypeStruct((N_CTX_TOK, ATT_WIDTH), BF16),
                   jax.ShapeDtypeStruct((N_CTX_TOK, KV_WIDTH), F32),
                   jax.ShapeDtypeStruct((N_CTX_TOK, KV_WIDTH), F32)],
        name="ctx_attention",
    )(sink, u, u, u, seg, jnp.tile(q_gain, ATT_HEADS).reshape(1, ATT_WIDTH),
      jnp.tile(k_gain, ATT_KV_HEADS).reshape(1, KV_WIDTH))


def _rope_tables():
    pos = jnp.arange(L_LAT, dtype=jnp.int32)
    row = (pos // GRID_W).astype(F32)
    col = (pos % GRID_W).astype(F32)
    inv = ROPE_THETA ** (-jnp.arange(ROPE_FREQS, dtype=F32) / ROPE_FREQS)
    ar, ac = row[:, None] * inv, col[:, None] * inv
    cos = jnp.concatenate([jnp.cos(ar), jnp.cos(ar), jnp.cos(ac), jnp.cos(ac)], axis=-1)
    sin = jnp.concatenate([-jnp.sin(ar), jnp.sin(ar), -jnp.sin(ac), jnp.sin(ac)], axis=-1)
    return jnp.tile(cos, (1, ATT_HEADS)), jnp.tile(sin, (1, ATT_HEADS))


def _lat_attn_kernel(sink_ref, q_ref, k_ref, v_ref, ck_ref, cv_ref, cosq_ref, sinq_ref, cosk_ref, sink_k_ref,
                     seg_ref, qg_ref, kg_ref, o_ref, kn_ref):
    n = pl.program_id(1)
    seg = seg_ref[...]

    @pl.when(n == 0)
    def _():
        kn = _head_rms(k_ref[...], seg[:KV_WIDTH, :KV_WIDTH], kg_ref[...])
        kn_ref[...] = _rope(kn, cosk_ref[...], sink_k_ref[...]).astype(BF16)

    qn = _head_rms(q_ref[...], seg, qg_ref[...])
    qn = _rope(qn, cosq_ref[...], sinq_ref[...]) * ATT_SCALE
    span = 3 * ATT_BLOCK
    start = pl.multiple_of(jnp.clip((n - 1) * ATT_BLOCK, 0, L_LAT - span), ATT_BLOCK)
    rows = ATT_GROUP * ATT_BLOCK
    q_pos = n * ATT_BLOCK + (lax.broadcasted_iota(jnp.int32, (rows, span), 0) % ATT_BLOCK)
    k_pos = start + lax.broadcasted_iota(jnp.int32, (rows, span), 1)
    valid = jnp.abs(q_pos - k_pos) <= WINDOW
    k_loc = kn_ref[pl.ds(start, span), :]
    v_loc = v_ref[pl.ds(start, span), :].astype(BF16)
    k_ctx = ck_ref[0].astype(BF16)
    v_ctx = cv_ref[0].astype(BF16)
    outs = []
    for g in range(ATT_KV_HEADS):
        cols = slice(g * HEAD_DIM, (g + 1) * HEAD_DIM)
        q = _stack_heads(qn, g).astype(BF16)
        s_loc = lax.dot_general(q, k_loc[:, cols], _DOT_NT, preferred_element_type=F32)
        s_loc = jnp.where(valid, s_loc, NEG_BIG)
        s_ctx = lax.dot_general(q, k_ctx[:, cols], _DOT_NT, preferred_element_type=F32)
        sink = _sink_column(sink_ref, g, ATT_BLOCK)
        m = jnp.maximum(jnp.maximum(jnp.max(s_loc, axis=-1, keepdims=True),
                                    jnp.max(s_ctx, axis=-1, keepdims=True)), sink)
        p_loc = jnp.exp(s_loc - m)
        p_ctx = jnp.exp(s_ctx - m)
        den = (jnp.sum(p_loc, axis=-1, keepdims=True) + jnp.sum(p_ctx, axis=-1, keepdims=True)
               + jnp.exp(sink - m))
        o = (jnp.dot(p_loc.astype(BF16), v_loc[:, cols], preferred_element_type=F32)
             + jnp.dot(p_ctx.astype(BF16), v_ctx[:, cols], preferred_element_type=F32))
        outs.append(o / den)
    o_ref[...] = _unstack_heads(outs, ATT_BLOCK).astype(o_ref.dtype)


def _lat_attention(u, cache_k, cache_v, seg, q_gain, k_gain, sink):
    qcol = 3 * HY_WIDTH // ATT_WIDTH
    kcol = (3 * HY_WIDTH + ATT_WIDTH) // KV_WIDTH
    nblk = L_LAT // ATT_BLOCK
    first_q_block = N_CTX_TOK // ATT_BLOCK
    first_seq = N_CTX_TOK // L_LAT
    cos, sin = _rope_tables()
    return pl.pallas_call(
        _lat_attn_kernel,
        grid_spec=pltpu.PrefetchScalarGridSpec(
            num_scalar_prefetch=1,
            grid=(N_LAT_SEQ, nblk),
            in_specs=[
                pl.BlockSpec((ATT_BLOCK, ATT_WIDTH), lambda b, n, s: (first_q_block + b * nblk + n, qcol)),
                pl.BlockSpec((L_LAT, KV_WIDTH), lambda b, n, s: (first_seq + b, kcol)),
                pl.BlockSpec((L_LAT, KV_WIDTH), lambda b, n, s: (first_seq + b, kcol + 1)),
                pl.BlockSpec((1, PAST_LEN, KV_WIDTH), lambda b, n, s: (b, 0, 0)),
                pl.BlockSpec((1, PAST_LEN, KV_WIDTH), lambda b, n, s: (b, 0, 0)),
                pl.BlockSpec((ATT_BLOCK, ATT_WIDTH), lambda b, n, s: (n, 0)),
                pl.BlockSpec((ATT_BLOCK, ATT_WIDTH), lambda b, n, s: (n, 0)),
                pl.BlockSpec((L_LAT, KV_WIDTH), lambda b, n, s: (0, 0)),
                pl.BlockSpec((L_LAT, KV_WIDTH), lambda b, n, s: (0, 0)),
                pl.BlockSpec((ATT_WIDTH, ATT_WIDTH), lambda b, n, s: (0, 0)),
                pl.BlockSpec((1, ATT_WIDTH), lambda b, n, s: (0, 0)),
                pl.BlockSpec((1, KV_WIDTH), lambda b, n, s: (0, 0)),
            ],
            out_specs=pl.BlockSpec((ATT_BLOCK, ATT_WIDTH), lambda b, n, s: (b * nblk + n, 0)),
            scratch_shapes=[pltpu.VMEM((L_LAT, KV_WIDTH), BF16)],
        ),
        out_shape=jax.ShapeDtypeStruct((N_LAT_SEQ * L_LAT, ATT_WIDTH), BF16),
        name="lat_attention",
    )(sink, u, u, u, cache_k.reshape(N_LAT_SEQ, PAST_LEN, KV_WIDTH), cache_v.reshape(N_LAT_SEQ, PAST_LEN, KV_WIDTH),
      cos, sin, cos[:, :KV_WIDTH], sin[:, :KV_WIDTH], seg,
      jnp.tile(q_gain, ATT_HEADS).reshape(1, ATT_WIDTH), jnp.tile(k_gain, ATT_KV_HEADS).reshape(1, KV_WIDTH))


def _columns_to_lanes(cols):
    rows = cols[0].shape[0]
    lane = lax.broadcasted_iota(jnp.int32, (rows, len(cols)), 1)
    out = jnp.broadcast_to(cols[-1], (rows, len(cols)))
    for k in range(len(cols) - 1):
        out = jnp.where(lane == k, cols[k], out)
    return out


def _post_mixer(x, mix, refs, tm):
    (w_ref, gain_ref, mod_ref, rw_ref, rb_ref, tri_ref, x1_ref, tw_ref, pos_ref, xs_hbm,
     wb_ref, run_ref, xtb, posv, poss, zbuf, sem_sc, sem_pos) = refs
    i = pl.program_id(0)
    nsteps = pl.num_programs(0)
    slot = i % 2
    rows = TOK_CHUNKS

    def wait_scatter(sl):
        n = TOP_K * tm * rows
        pltpu.make_async_copy(xs_hbm.at[pl.ds(0, n), :], xs_hbm.at[pl.ds(0, n), :], sem_sc.at[sl]).wait()

    def to_smem(rows8):
        posv[...] = rows8
        cp = pltpu.make_async_copy(posv, poss, sem_pos.at[0])
        cp.start()
        cp.wait()

    def columns_to_rows(cols):
        lane = lax.broadcasted_iota(jnp.int32, (tm, LANES), 1)
        wide = jnp.zeros((tm, LANES), F32)
        for k, col in enumerate(cols):
            wide = jnp.where(lane == k, col.astype(F32), wide)
        return wide.T[:SUBLANES, :].astype(jnp.int32)

    @pl.when(i == 0)
    def _():
        wb_ref[...] = w_ref[...].astype(BF16)
        run_ref[...] = jnp.zeros_like(run_ref)
        zbuf[...] = jnp.zeros_like(zbuf)

    y = jnp.dot(mix, wb_ref[...], preferred_element_type=F32)
    x1 = x + mod_ref[0, 2:3, :] * y
    x1_ref[...] = x1
    xt = _norm_mod(x1, gain_ref[...], mod_ref[0, 3:4, :], mod_ref[0, 4:5, :])
    logits = jnp.dot(xt, rw_ref[...], preferred_element_type=F32, precision=HIGHEST) + rb_ref[...]
    lane = lax.broadcasted_iota(jnp.int32, logits.shape, 1)
    vals, hits = [], []
    for _ in range(TOP_K):
        m = jnp.max(logits, axis=-1, keepdims=True)
        sel = jnp.min(jnp.where(logits == m, lane, N_EXPERTS), axis=-1, keepdims=True)
        vals.append(m)
        hits.append(lane == sel)
        logits = jnp.where(lane == sel, -jnp.inf, logits)
    es = [jnp.exp(v - vals[0]) for v in vals]
    den = es[0] + es[1] + es[2] + es[3]
    tw_ref[...] = _columns_to_lanes([e / den for e in es])

    routed = sum(h.astype(F32) for h in hits)
    earlier = jnp.dot(tri_ref[...], routed.astype(BF16), preferred_element_type=F32)
    row_of = lane.astype(F32) * float(MOE_CAPACITY) + run_ref[...] + earlier
    pos = [jnp.sum(jnp.where(h, row_of, 0.0), axis=-1, keepdims=True).astype(jnp.int32) for h in hits]
    run_ref[...] += jnp.sum(routed, axis=0, keepdims=True)
    pos_ref[...] = _columns_to_lanes(pos)

    @pl.when(i >= 2)
    def _():
        wait_scatter(slot)

    for s in range(rows):
        xtb[slot, pl.ds(s, tm, stride=rows), :] = xt[:, s * LANES:(s + 1) * LANES]
    to_smem(columns_to_rows(pos))

    unroll = 8

    def issue(g, carry):
        for u in range(unroll):
            t = g * unroll + u
            src = xtb.at[slot, pl.ds(pl.multiple_of(t * rows, rows), rows), :]
            for k in range(TOP_K):
                dst = xs_hbm.at[pl.ds(pl.multiple_of(poss[k, t] * rows, rows), rows), :]
                pltpu.make_async_copy(src, dst, sem_sc.at[slot]).start()
        return carry

    lax.fori_loop(0, tm // unroll, issue, 0)

    @pl.when(i == nsteps - 1)
    def _():
        wait_scatter(1 - slot)
        wait_scatter(slot)
        counts = jnp.concatenate([run_ref[...].astype(jnp.int32), jnp.zeros((1, tm - N_EXPERTS), jnp.int32)], axis=1)
        to_smem(jnp.broadcast_to(counts, (SUBLANES, tm)))
        fills = []
        for e in range(N_EXPERTS):
            first = (e * MOE_CAPACITY + poss[0, e]) * rows
            fills.append(pltpu.make_async_copy(
                zbuf, xs_hbm.at[pl.ds(pl.multiple_of(first, rows), MOE_TILE * rows), :], sem_pos.at[0]))
            fills[-1].start()
        for cp in fills:
            cp.wait()


def _post_even_kernel(x_ref, ac_ref, al_ref, tc_ref, tl_ref, *refs, tm):
    is_ctx = pl.program_id(0) * tm < N_CTX_TOK
    mix = jnp.concatenate([jnp.where(is_ctx, ac_ref[...], al_ref[...]),
                           jnp.where(is_ctx, tc_ref[...], tl_ref[...])], axis=1)
    _post_mixer(x_ref[...], mix, refs, tm)


def _post_odd_kernel(x_ref, yc_ref, yl_ref, *refs, tm):
    mix = jnp.where(pl.program_id(0) * tm < N_CTX_TOK, yc_ref[...], yl_ref[...])
    _post_mixer(x_ref[...], mix, refs, tm)


def _post_call(kernel_fn, name, mixer_specs, mixer_args, k_in, w_out, gain, mod, router_w, router_b):
    tm = ROW_TILE
    row = lax.broadcasted_iota(jnp.int32, (tm, tm), 0)
    col = lax.broadcasted_iota(jnp.int32, (tm, tm), 1)
    tri = (col < row).astype(BF16)
    return pl.pallas_call(
        functools.partial(kernel_fn, tm=tm),
        grid=(N_TOK // tm,),
        in_specs=[pl.BlockSpec((tm, D_MODEL), lambda i: (i, 0))] + mixer_specs + [
            pl.BlockSpec((k_in, D_MODEL), lambda i: (0, 0)),
            pl.BlockSpec((1, D_MODEL), lambda i: (0, 0)),
            pl.BlockSpec((1, 6, D_MODEL), lambda i: (_cond_of_tile(i, tm), 0, 0)),
            pl.BlockSpec((D_MODEL, N_EXPERTS), lambda i: (0, 0)),
            pl.BlockSpec((1, N_EXPERTS), lambda i: (0, 0)),
            pl.BlockSpec((tm, tm), lambda i: (0, 0)),
        ],
        out_specs=[
            pl.BlockSpec((tm, D_MODEL), lambda i: (i, 0)),
            pl.BlockSpec((tm, TOP_K), lambda i: (i, 0)),
            pl.BlockSpec((tm, TOP_K), lambda i: (i, 0)),
            pl.BlockSpec(memory_space=pl.ANY),
        ],
        out_shape=[
            jax.ShapeDtypeStruct((N_TOK, D_MODEL), F32),
            jax.ShapeDtypeStruct((N_TOK, TOP_K), F32),
            jax.ShapeDtypeStruct((N_TOK, TOP_K), jnp.int32),
            jax.ShapeDtypeStruct((N_EXPERTS * MOE_CAPACITY * TOK_CHUNKS, LANES), F32),
        ],
        scratch_shapes=[
            pltpu.VMEM((k_in, D_MODEL), BF16),
            pltpu.VMEM((1, N_EXPERTS), F32),
            pltpu.VMEM((2, tm * TOK_CHUNKS, LANES), F32),
            pltpu.VMEM((SUBLANES, tm), jnp.int32),
            pltpu.SMEM((SUBLANES, tm), jnp.int32),
            pltpu.VMEM((MOE_TILE * TOK_CHUNKS, LANES), F32),
            pltpu.SemaphoreType.DMA((2,)),
            pltpu.SemaphoreType.DMA((1,)),
        ],
        compiler_params=_vmem(48),
        name=name,
    )(*mixer_args, w_out, gain.reshape(1, D_MODEL), mod, router_w, router_b.reshape(1, N_EXPERTS), tri)


def _post_even(x, a_ctx, a_lat, t_ctx, t_lat, w_out, gain, mod, router_w, router_b):
    tm = ROW_TILE
    nctx = N_CTX_TOK // tm
    ctx_map = lambda i: (jnp.minimum(i, nctx - 1), 0)
    lat_map = lambda i: (jnp.maximum(i - nctx, 0), 0)
    specs = [pl.BlockSpec((tm, HY_WIDTH), ctx_map), pl.BlockSpec((tm, HY_WIDTH), lat_map),
             pl.BlockSpec((tm, ATT_WIDTH), ctx_map), pl.BlockSpec((tm, ATT_WIDTH), lat_map)]
    return _post_call(_post_even_kernel, "post_even", specs, (x, a_ctx, a_lat, t_ctx, t_lat),
                      HY_WIDTH + ATT_WIDTH, w_out, gain, mod, router_w, router_b)


def _post_odd(x, y_ctx, y_lat, w_out, gain, mod, router_w, router_b):
    tm = ROW_TILE
    nctx = N_CTX_TOK // tm
    specs = [pl.BlockSpec((tm, RET_V_WIDTH), lambda i: (jnp.minimum(i, nctx - 1), 0)),
             pl.BlockSpec((tm, RET_V_WIDTH), lambda i: (jnp.maximum(i - nctx, 0), 0))]
    return _post_call(_post_odd_kernel, "post_odd", specs, (x, y_ctx, y_lat),
                      RET_V_WIDTH, w_out, gain, mod, router_w, router_b)


def _moe_plan(pos):
    t = MOE_TILE
    e_flat = pos.reshape(N_PAIRS) // MOE_CAPACITY
    experts = jnp.arange(N_EXPERTS, dtype=jnp.int32)
    counts = jnp.sum((e_flat.reshape(1, -1, LANES) == experts[:, None, None]).astype(jnp.int32), axis=(1, 2))
    pad = (-counts) % t
    pad_end = jnp.cumsum(pad)
    q = jnp.arange(MOE_NUM_TILES * t - N_PAIRS, dtype=jnp.int32)
    pad_key = jnp.sum((pad_end[None, :] <= q[:, None]).astype(jnp.int32), axis=1)
    order = jnp.argsort(jnp.concatenate([e_flat, pad_key]), stable=True).astype(jnp.int32)
    is_pad = jnp.concatenate([jnp.ones((t,), bool), order >= N_PAIRS])
    where = jnp.arange(-t, MOE_NUM_TILES * t, dtype=jnp.int32)
    trash = MOE_TRASH_ROW + ((where // t) % 2) * t + where % t
    pair_row = jnp.concatenate([jnp.zeros((t,), jnp.int32), (order % TOP_K) * N_TOK + order // TOP_K])
    dest = jnp.where(is_pad, trash, pair_row)
    ntiles = (counts + pad) // t
    tile_end = jnp.cumsum(ntiles)
    total = tile_end[-1]
    i = jnp.minimum(jnp.arange(MOE_NUM_TILES, dtype=jnp.int32), total - 1)
    e_of = jnp.sum((tile_end[None, :] <= i[:, None]).astype(jnp.int32), axis=1)
    first_tile = jnp.sum(jnp.where(e_of[:, None] == experts[None, :], (tile_end - ntiles)[None, :], 0), axis=1)
    block = e_of * (MOE_CAPACITY // t) + i - first_tile
    valid = jnp.arange(MOE_NUM_TILES, dtype=jnp.int32) < total
    as_i32 = lambda v: v.astype(jnp.int32)
    return as_i32(e_of), as_i32(block), as_i32(valid), as_i32(dest)


def _moe_kernel(te_ref, tblk_ref, tv_ref, dest_ref, x_ref, w1_ref, b1_ref, w2_ref, b2_ref, y_hbm,
                obuf0, obuf1, w1b, w2b, sem_out):
    i = pl.program_id(0)
    nt = pl.num_programs(0)
    t = MOE_TILE
    rows = TOK_CHUNKS
    obufs = (obuf0, obuf1)

    def issue_scatter(tile, sl):
        first_row = (tile + 1) * t
        for r in range(t):
            d = dest_ref[first_row + r]
            pltpu.make_async_copy(obufs[sl].at[pl.ds(r * rows, rows), :],
                                  y_hbm.at[pl.ds(d * rows, rows), :], sem_out.at[sl]).start()

    def wait_scatter(sl):
        pltpu.make_async_copy(obufs[sl], y_hbm.at[pl.ds(0, t * rows), :], sem_out.at[sl]).wait()

    def valid(tile):
        return tv_ref[jnp.clip(tile, 0, nt - 1)] > 0

    @pl.when(i == 0)
    def _():
        obuf0[...] = jnp.zeros_like(obuf0)
        obuf1[...] = jnp.zeros_like(obuf1)
        pltpu.make_async_copy(obuf0, y_hbm.at[pl.ds(MOE_TRASH_ROW * rows, t * rows), :], sem_out.at[0]).start()

    def tile_step(sl):
        first = jnp.logical_or(i == 0, te_ref[i] != te_ref[jnp.maximum(i - 1, 0)])

        @pl.when(first)
        def _():
            w1b[...] = w1_ref[0].astype(BF16)
            w2b[...] = w2_ref[0].astype(BF16)

        issue_scatter(i - 1, 1 - sl)
        xs = jnp.concatenate([x_ref[pl.ds(s, t, stride=rows), :] for s in range(rows)], axis=1)
        h = jnp.dot(xs.astype(BF16), w1b[...], preferred_element_type=F32) + b1_ref[0]
        glu = jnp.minimum(h[:, :D_FF], SWIGLU_LIMIT)
        lin = jnp.clip(h[:, D_FF:], -SWIGLU_LIMIT, SWIGLU_LIMIT)
        act = (glu * _sigmoid(SWIGLU_ALPHA * glu) * (lin + 1.0)).astype(BF16)
        y = jnp.dot(act, w2b[...], preferred_element_type=F32) + b2_ref[0]
        wait_scatter(sl)
        for s in range(rows):
            obufs[sl][pl.ds(s, t, stride=rows), :] = y[:, s * LANES:(s + 1) * LANES]

    def drain(last, sl):
        wait_scatter(1 - sl)
        issue_scatter(last, sl)
        wait_scatter(sl)

    for sl in range(2):
        @pl.when(jnp.logical_and(valid(i), i % 2 == sl))
        def _():
            tile_step(sl)

    ended_before = jnp.logical_and(jnp.logical_not(valid(i)), jnp.logical_and(i >= 1, valid(i - 1)))
    ends_here = jnp.logical_and(valid(i), i == nt - 1)
    last = jnp.where(ends_here, i, jnp.maximum(i - 1, 0))
    for sl in range(2):
        @pl.when(jnp.logical_and(jnp.logical_or(ended_before, ends_here), last % 2 == sl))
        def _():
            drain(last, sl)


def _moe_experts(xs_rows, plan, w1, b1, w2, b2):
    te, tblk, tv, dest = plan
    t = MOE_TILE
    return pl.pallas_call(
        _moe_kernel,
        grid_spec=pltpu.PrefetchScalarGridSpec(
            num_scalar_prefetch=4,
            grid=(MOE_NUM_TILES,),
            in_specs=[
                pl.BlockSpec((t * TOK_CHUNKS, LANES), lambda i, te, tblk, *_: (tblk[i], 0)),
                pl.BlockSpec((1, D_MODEL, 2 * D_FF), lambda i, te, *_: (te[i], 0, 0)),
                pl.BlockSpec((1, 1, 2 * D_FF), lambda i, te, *_: (te[i], 0, 0)),
                pl.BlockSpec((1, D_FF, D_MODEL), lambda i, te, *_: (te[i], 0, 0)),
                pl.BlockSpec((1, 1, D_MODEL), lambda i, te, *_: (te[i], 0, 0)),
            ],
            out_specs=pl.BlockSpec(memory_space=pl.ANY),
            scratch_shapes=[
                pltpu.VMEM((t * TOK_CHUNKS, LANES), F32),
                pltpu.VMEM((t * TOK_CHUNKS, LANES), F32),
                pltpu.VMEM((D_MODEL, 2 * D_FF), BF16),
                pltpu.VMEM((D_FF, D_MODEL), BF16),
                pltpu.SemaphoreType.DMA((2,)),
            ],
        ),
        out_shape=jax.ShapeDtypeStruct((MOE_OUT_ROWS * TOK_CHUNKS, LANES), F32),
        compiler_params=_vmem(56),
        name="moe_experts",
    )(te, tblk, tv, dest, xs_rows, w1, b1.reshape(N_EXPERTS, 1, 2 * D_FF), w2,
      b2.reshape(N_EXPERTS, 1, D_MODEL))


def _combine_value(x1_ref, y_refs, tw_ref, mod_ref, tm):
    tw = tw_ref[...]
    chunks = []
    for s in range(TOK_CHUNKS):
        acc = tw[:, 0:1] * y_refs[0][pl.ds(s, tm, stride=TOK_CHUNKS), :]
        for k in range(1, TOP_K):
            acc = acc + tw[:, k:k + 1] * y_refs[k][pl.ds(s, tm, stride=TOK_CHUNKS), :]
        chunks.append(acc)
    return x1_ref[...] + mod_ref[0, 5:6, :] * jnp.concatenate(chunks, axis=1)


def _combine_kernel(x1_ref, y0_ref, y1_ref, y2_ref, y3_ref, tw_ref, mod_ref, o_ref, *, tm):
    o_ref[...] = _combine_value(x1_ref, (y0_ref, y1_ref, y2_ref, y3_ref), tw_ref, mod_ref, tm)


def _combine_split_kernel(x1_ref, y0_ref, y1_ref, y2_ref, y3_ref, tw_ref, mod_ref, oc_ref, ol_ref, *, tm):
    val = _combine_value(x1_ref, (y0_ref, y1_ref, y2_ref, y3_ref), tw_ref, mod_ref, tm)
    is_ctx = pl.program_id(0) * tm < N_CTX_TOK

    @pl.when(is_ctx)
    def _():
        oc_ref[...] = val

    @pl.when(jnp.logical_not(is_ctx))
    def _():
        ol_ref[...] = val


def _combine(x1, y_tiles, top_w, mod, split):
    tm = 256
    nctx = N_CTX_TOK // tm
    ntile = N_TOK // tm
    slab = lambda k: pl.BlockSpec((tm * TOK_CHUNKS, LANES), lambda i: (k * ntile + i, 0))
    in_specs = [
        pl.BlockSpec((tm, D_MODEL), lambda i: (i, 0)),
        slab(0), slab(1), slab(2), slab(3),
        pl.BlockSpec((tm, TOP_K), lambda i: (i, 0)),
        pl.BlockSpec((1, 6, D_MODEL), lambda i: (_cond_of_tile(i, tm), 0, 0)),
    ]
    if not split:
        return pl.pallas_call(
            functools.partial(_combine_kernel, tm=tm),
            grid=(N_TOK // tm,),
            in_specs=in_specs,
            out_specs=pl.BlockSpec((tm, D_MODEL), lambda i: (i, 0)),
            out_shape=jax.ShapeDtypeStruct((N_TOK, D_MODEL), F32),
            name="moe_combine",
        )(x1, y_tiles, y_tiles, y_tiles, y_tiles, top_w, mod)
    return pl.pallas_call(
        functools.partial(_combine_split_kernel, tm=tm),
        grid=(N_TOK // tm,),
        in_specs=in_specs,
        out_specs=[pl.BlockSpec((tm, D_MODEL), lambda i: (jnp.minimum(i, nctx - 1), 0)),
                   pl.BlockSpec((tm, D_MODEL), lambda i: (jnp.maximum(i - nctx, 0), 0))],
        out_shape=[jax.ShapeDtypeStruct((N_CTX_TOK, D_MODEL), F32),
                   jax.ShapeDtypeStruct((N_TOK - N_CTX_TOK, D_MODEL), F32)],
        name="moe_combine_split",
    )(x1, y_tiles, y_tiles, y_tiles, y_tiles, top_w, mod)


def _retention_kernel(lg_ref, q_ref, k_ref, v_ref, gf_ref, gb_ref, *rest, length, has_s0, emit_state):
    rest = list(rest)
    s0_ref = rest.pop(0) if has_s0 else None
    o_ref = rest.pop(0)
    so_ref = rest.pop(0) if emit_state else None
    s_ref, yf_ref = rest
    c = RET_CHUNK
    nc = length // c
    ii = lax.broadcasted_iota(jnp.int32, (c, c), 0).astype(F32)
    jj = lax.broadcasted_iota(jnp.int32, (c, c), 1).astype(F32)
    ci = lax.broadcasted_iota(jnp.int32, (c, 1), 0).astype(F32)

    def scan(direction):
        lg = -jnp.exp(lg_ref[direction, 0])
        lg1 = lg[:, 0:1]
        if direction == 0:
            diff = ii - jj
            q_decay = jnp.exp(lg1 * (ci + 1.0))
            k_decay = jnp.exp(lg1 * (c - 1.0 - ci))
        else:
            diff = jj - ii
            q_decay = jnp.exp(lg1 * (c - ci))
            k_decay = jnp.exp(lg1 * ci)
        inner = jnp.where(diff >= 0, jnp.exp(lg * jnp.maximum(diff, 0.0)), 0.0)
        chunk_decay = jnp.exp(lg1 * float(c))
        if has_s0:
            s_ref[...] = s0_ref[0, direction, 0]
        else:
            s_ref[...] = jnp.zeros_like(s_ref)

        def body(step, carry):
            ch = step if direction == 0 else nc - 1 - step
            rows = pl.ds(pl.multiple_of(ch * c, c), c)
            qc = q_ref[rows, :]
            kc = k_ref[rows, :].astype(F32) * (RET_DK ** -0.5)
            vc = v_ref[rows, :]
            s = s_ref[...]
            att = lax.dot_general(qc, kc.astype(BF16), _DOT_NT, preferred_element_type=F32) * inner
            o = (jnp.dot(att.astype(BF16), vc, preferred_element_type=F32)
                 + jnp.dot(qc, s.astype(BF16), preferred_element_type=F32) * q_decay)
            kd = (kc * k_decay).T.astype(BF16)
            s_ref[...] = s * chunk_decay + jnp.dot(kd, vc, preferred_element_type=F32)
            on = o * lax.rsqrt(jnp.mean(o * o, axis=-1, keepdims=True) + EPS)
            if direction == 0:
                g = gf_ref[rows, :].astype(F32)
                yf_ref[rows, :] = g * _sigmoid(g) * on
            else:
                g = gb_ref[rows, :].astype(F32)
                o_ref[rows, :] = (yf_ref[rows, :] + g * _sigmoid(g) * on).astype(o_ref.dtype)
            return carry

        lax.fori_loop(0, nc, body, 0)
        if emit_state:
            so_ref[0, direction, 0] = s_ref[...]

    scan(0)
    scan(1)


def _retention(u, first_seq, nseq, length, decay_logit, s0, emit_state):
    row0 = first_seq
    lg = jnp.broadcast_to(decay_logit.astype(F32)[:, :, None, None], (2, RET_HEADS, 1, LANES))
    kcol = RET_QK_WIDTH // RET_DK
    vcol = 2 * RET_QK_WIDTH // RET_DV
    gfcol = vcol + RET_HEADS
    gbcol = gfcol + RET_HEADS
    in_specs = [
        pl.BlockSpec((2, 1, 1, LANES), lambda b, h: (0, h, 0, 0)),
        pl.BlockSpec((length, RET_DK), lambda b, h: (row0 + b, h)),
        pl.BlockSpec((length, RET_DK), lambda b, h: (row0 + b, kcol + h)),
        pl.BlockSpec((length, RET_DV), lambda b, h: (row0 + b, vcol + h)),
        pl.BlockSpec((length, RET_DV), lambda b, h: (row0 + b, gfcol + h)),
        pl.BlockSpec((length, RET_DV), lambda b, h: (row0 + b, gbcol + h)),
    ]
    args = [lg, u, u, u, u, u]
    state_spec = pl.BlockSpec((1, 2, 1, RET_DK, RET_DV), lambda b, h: (b, 0, h, 0, 0))
    if s0 is not None:
        in_specs.append(state_spec)
        args.append(s0)
    out_specs = [pl.BlockSpec((length, RET_DV), lambda b, h: (b, h))]
    out_shape = [jax.ShapeDtypeStruct((nseq * length, RET_V_WIDTH), BF16)]
    if emit_state:
        out_specs.append(state_spec)
        out_shape.append(jax.ShapeDtypeStruct((nseq, 2, RET_HEADS, RET_DK, RET_DV), F32))
    return pl.pallas_call(
        functools.partial(_retention_kernel, length=length, has_s0=s0 is not None, emit_state=emit_state),
        grid=(nseq, RET_HEADS),
        in_specs=in_specs,
        out_specs=out_specs,
        out_shape=out_shape,
        scratch_shapes=[pltpu.VMEM((RET_DK, RET_DV), F32), pltpu.VMEM((length, RET_DV), F32)],
        compiler_params=_vmem(48),
        name=f"retention_{length}",
    )(*args)


def kernel(x_prompt, x_sample, cache_k0, cache_v0, state_ret1, c, c_ctx, l0_norm_mix, l0_ada_w, l0_ada_b, l0_w_in, l0_conv_w, l0_conv_b, l0_filt_w1, l0_filt_b1, l0_filt_freq, l0_filt_w2, l0_filt_b2, l0_filt_w3, l0_filt_deltas, l0_hy_skip, l0_q_gain, l0_k_gain, l0_sink, l0_w_out, l0_norm_ffn, l0_router_w, l0_router_b, l0_moe_w1, l0_moe_b1, l0_moe_w2, l0_moe_b2, l1_norm_mix, l1_ada_w, l1_ada_b, l1_w_in, l1_ret_decay_logit, l1_w_out, l1_norm_ffn, l1_router_w, l1_router_b, l1_moe_w1, l1_moe_b1, l1_moe_w2, l1_moe_b2):
    x = jnp.concatenate([x_prompt.reshape(N_CTX_TOK, D_MODEL), x_sample.reshape(N_TOK - N_CTX_TOK, D_MODEL)], axis=0)
    cond = jnp.zeros((SUBLANES, D_MODEL), F32).at[0].set(c_ctx).at[1:1 + N_LAT_SEQ].set(c)
    mod0 = _adaln(cond, l0_ada_w, l0_ada_b)
    mod1 = _adaln(cond, l1_ada_w, l1_ada_b)

    u = _in_proj(x, l0_norm_mix, mod0, l0_w_in, EVEN_IN // 2, F32)
    filt = (l0_filt_w1, l0_filt_b1, l0_filt_freq, l0_filt_w2, l0_filt_b2, l0_filt_w3, l0_filt_deltas)
    hy = []
    for first_block, nseq, length in ((0, N_CTX_SEQ, L_CTX), (N_CTX_TOK // L_LAT // N_LAT_SEQ, N_LAT_SEQ, L_LAT)):
        cmat, smat, stmat = _dft_matrices(length)
        tc, ts = _hyena_filter(length, _filter_features(length), cmat, smat, *filt)
        hy.append(_hyena(u, first_block, nseq, length, l0_conv_w, l0_conv_b, l0_hy_skip, tc, ts, cmat, smat, stmat))
    head = lax.broadcasted_iota(jnp.int32, (ATT_WIDTH, ATT_WIDTH), 0) // HEAD_DIM
    seg = (head == head.T).astype(BF16)
    att_ctx, new_k, new_v = _ctx_attention(u, seg, l0_q_gain, l0_k_gain, l0_sink)
    att_lat = _lat_attention(u, cache_k0, cache_v0, seg, l0_q_gain, l0_k_gain, l0_sink)
    x1, top_w, pos, xs = _post_even(x, hy[0], hy[1], att_ctx, att_lat, l0_w_out, l0_norm_ffn, mod0,
                                    l0_router_w, l0_router_b)
    y = _moe_experts(xs, _moe_plan(pos), l0_moe_w1, l0_moe_b1, l0_moe_w2, l0_moe_b2)
    x = _combine(x1, y, top_w, mod0, split=False)

    u = _in_proj(x, l1_norm_mix, mod1, l1_w_in, 2048, BF16)
    y_ctx, new_state = _retention(u, 0, N_CTX_SEQ, L_CTX, l1_ret_decay_logit, None, True)
    (y_lat,) = _retention(u, N_CTX_TOK // L_LAT, N_LAT_SEQ, L_LAT, l1_ret_decay_logit, state_ret1, False)
    x1, top_w, pos, xs = _post_odd(x, y_ctx, y_lat, l1_w_out, l1_norm_ffn, mod1, l1_router_w, l1_router_b)
    y = _moe_experts(xs, _moe_plan(pos), l1_moe_w1, l1_moe_b1, l1_moe_w2, l1_moe_b2)
    y_prompt, y_sample = _combine(x1, y, top_w, mod1, split=True)

    return (y_prompt.reshape(N_CTX_SEQ, L_CTX, D_MODEL), y_sample.reshape(N_LAT_SEQ, L_LAT, D_MODEL),
            new_k.reshape(N_CTX_SEQ, L_CTX, ATT_KV_HEADS, HEAD_DIM),
            new_v.reshape(N_CTX_SEQ, L_CTX, ATT_KV_HEADS, HEAD_DIM), new_state)
```

```python
import functools
import math

import jax
import jax.numpy as jnp
from jax import lax
from jax.experimental import pallas as pl
from jax.experimental.pallas import tpu as pltpu

F32 = jnp.float32
BF16 = jnp.bfloat16
HIGHEST = lax.Precision.HIGHEST

D_MODEL = 1024
N_CTX_SEQ, L_CTX = 16, 256
N_LAT_SEQ, L_LAT = 2, 2048
N_CTX_TOK = N_CTX_SEQ * L_CTX
N_TOK = N_CTX_TOK + N_LAT_SEQ * L_LAT
PAST_LEN = 512
EPS = 1e-6
NEG_BIG = -1e30

HY_WIDTH = 512
HY_BANDS = 16
HY_FILTER_HIDDEN = 64
HY_FEAT_PAD = 64

ATT_HEADS, ATT_KV_HEADS, HEAD_DIM = 8, 2, 64
ATT_GROUP = ATT_HEADS // ATT_KV_HEADS
ATT_WIDTH = ATT_HEADS * HEAD_DIM
KV_WIDTH = ATT_KV_HEADS * HEAD_DIM
ATT_SCALE = HEAD_DIM ** -0.5
WINDOW = 128
ATT_BLOCK = 128
ROPE_THETA = 10000.0
ROPE_FREQS = HEAD_DIM // 4
GRID_W = 64
EVEN_IN = 3 * HY_WIDTH + ATT_WIDTH + 2 * KV_WIDTH

RET_HEADS = 4
RET_DK = 256
RET_DV = 512
RET_CHUNK = 128
RET_QK_WIDTH = RET_HEADS * RET_DK
RET_V_WIDTH = RET_HEADS * RET_DV
ODD_IN = 2 * RET_QK_WIDTH + 3 * RET_V_WIDTH

N_EXPERTS = 32
TOP_K = 4
D_FF = 1024
SWIGLU_ALPHA = 1.702
SWIGLU_LIMIT = 7.0

SUBLANES = 8
LANES = 128

MOE_TILE = 256
MOE_GROUP = 16
ROW_TILE = 512
MOE_GROUPED_ROWS = 2560
MOE_NUM_TILES = 192


def _vmem(mib):
    return pltpu.CompilerParams(vmem_limit_bytes=mib * 1024 * 1024)


def _cond_of_tile(i, tm):
    row = i * tm
    return jnp.where(row < N_CTX_TOK, 0, 1 + (row - N_CTX_TOK) // L_LAT)


def _sigmoid(x):
    return 1.0 / (1.0 + jnp.exp(-x))


def _norm_mod(x, gain, shift, scale):
    ms = jnp.mean(x * x, axis=-1, keepdims=True)
    return (x * lax.rsqrt(ms + EPS) * gain) * (1.0 + scale) + shift


def _adaln_kernel(c_ref, w_ref, b_ref, o_ref):
    c = c_ref[...]
    s = c * _sigmoid(c)
    o_ref[...] = jnp.dot(s, w_ref[...], preferred_element_type=F32, precision=HIGHEST) + b_ref[...]


def _adaln(cond, w, b):
    n = w.shape[1]
    tn = 1024
    out = pl.pallas_call(
        _adaln_kernel,
        grid=(n // tn,),
        in_specs=[
            pl.BlockSpec((SUBLANES, D_MODEL), lambda j: (0, 0)),
            pl.BlockSpec((D_MODEL, tn), lambda j: (0, j)),
            pl.BlockSpec((1, tn), lambda j: (0, j)),
        ],
        out_specs=pl.BlockSpec((SUBLANES, tn), lambda j: (0, j)),
        out_shape=jax.ShapeDtypeStruct((SUBLANES, n), F32),
        name="adaln",
    )(cond, w, b.reshape(1, n))
    return out.reshape(SUBLANES, 6, D_MODEL)


def _in_proj_kernel(x_ref, gain_ref, mod_ref, w_ref, o_ref, wb_ref):
    @pl.when(pl.program_id(1) == 0)
    def _():
        wb_ref[...] = w_ref[...].astype(BF16)

    h = _norm_mod(x_ref[...], gain_ref[...], mod_ref[0, 0:1, :], mod_ref[0, 1:2, :])
    o_ref[...] = jnp.dot(h.astype(BF16), wb_ref[...], preferred_element_type=F32).astype(o_ref.dtype)


def _in_proj(x, gain, mod, w, tn, out_dtype):
    n = w.shape[1]
    tm = ROW_TILE
    return pl.pallas_call(
        _in_proj_kernel,
        grid=(n // tn, N_TOK // tm),
        in_specs=[
            pl.BlockSpec((tm, D_MODEL), lambda j, i: (i, 0)),
            pl.BlockSpec((1, D_MODEL), lambda j, i: (0, 0)),
            pl.BlockSpec((1, 6, D_MODEL), lambda j, i: (_cond_of_tile(i, tm), 0, 0)),
            pl.BlockSpec((D_MODEL, tn), lambda j, i: (0, j)),
        ],
        out_specs=pl.BlockSpec((tm, tn), lambda j, i: (i, j)),
        out_shape=jax.ShapeDtypeStruct((N_TOK, n), out_dtype),
        scratch_shapes=[pltpu.VMEM((D_MODEL, tn), BF16)],
        compiler_params=_vmem(48),
        name="in_proj",
    )(x, gain.reshape(1, D_MODEL), mod, w)


def _dft_matrices(length):
    n = 2 * length
    lo = 16
    s = jnp.arange(length, dtype=jnp.int32)
    k1 = jnp.arange(length // lo, dtype=jnp.int32) * lo
    k0 = jnp.arange(lo, dtype=jnp.int32)
    ang1 = (2.0 * math.pi / n) * ((k1[:, None] * s[None, :]) % n).astype(F32)
    ang0 = (2.0 * math.pi / n) * ((k0[:, None] * s[None, :]) % n).astype(F32)
    c1, s1 = jnp.cos(ang1)[:, None, :], jnp.sin(ang1)[:, None, :]
    c0, s0 = jnp.cos(ang0)[None, :, :], jnp.sin(ang0)[None, :, :]
    cmat = (c1 * c0 - s1 * s0).reshape(length, length)
    smat = (s1 * c0 + c1 * s0).reshape(length, length)
    sign = jnp.where(s % 2 == 0, 1.0, -1.0).astype(F32)
    row = lax.broadcasted_iota(jnp.int32, (length, length), 0)
    col = lax.broadcasted_iota(jnp.int32, (length, length), 1)
    s_nyq = jnp.where(row == 0, sign[None, :], smat)
    st_nyq = jnp.where(col == 0, sign[:, None], smat)
    return cmat.astype(BF16), s_nyq.astype(BF16), st_nyq.astype(BF16)


def _filter_features(length):
    t = jnp.linspace(0.0, 1.0, length, dtype=F32)[:, None]
    w = 2.0 * math.pi * jnp.arange(length, dtype=F32)[:, None] / length
    f = jnp.linspace(1e-4, HY_BANDS - 1, HY_BANDS, dtype=F32)[None, :]
    z = jnp.concatenate([t, jnp.cos(f * w), -jnp.sin(f * w)], axis=-1)
    return jnp.pad(z, ((0, 0), (0, HY_FEAT_PAD - z.shape[1])))


def _filter_kernel(z_ref, w1_ref, b1_ref, fr_ref, w2_ref, b2_ref, w3_ref, dl_ref, c_ref, s_ref,
                   tc_ref, ts_ref, taps_ref, *, length, rb):
    r = pl.program_id(0)

    @pl.when(r == 0)
    def _():
        z = z_ref[...]
        fr = fr_ref[...]
        h = jnp.sin(fr * (jnp.dot(z, w1_ref[...], preferred_element_type=F32, precision=HIGHEST) + b1_ref[...]))
        h = jnp.sin(fr * (jnp.dot(h, w2_ref[...], preferred_element_type=F32, precision=HIGHEST) + b2_ref[...]))
        h = jnp.dot(h, w3_ref[...], preferred_element_type=F32, precision=HIGHEST)
        win = jnp.exp(-z[:, 0:1] * jnp.abs(dl_ref[...]))
        hf = h[:, :HY_WIDTH] * win
        hb = h[:, HY_WIDTH:] * win
        row = lax.broadcasted_iota(jnp.int32, (length, HY_WIDTH), 0)
        hb = jnp.where(row == 0, 0.0, hb)
        l1 = jnp.sum(jnp.abs(hf), axis=0, keepdims=True) + jnp.sum(jnp.abs(hb), axis=0, keepdims=True)
        inv = 1.0 / l1
        taps_ref[:, :HY_WIDTH] = (hf * inv).astype(BF16)
        taps_ref[:, HY_WIDTH:] = (hb * inv).astype(BF16)

    taps = taps_ref[...]
    rc = jnp.dot(c_ref[...], taps, preferred_element_type=F32)
    rs = jnp.dot(s_ref[...], taps, preferred_element_type=F32)
    tc = rc[:, :HY_WIDTH] + rc[:, HY_WIDTH:]
    ts = rs[:, :HY_WIDTH] - rs[:, HY_WIDTH:]
    grow = r * rb + lax.broadcasted_iota(jnp.int32, (rb, HY_WIDTH), 0)
    is0 = grow == 0
    ts = jnp.where(is0, rs[:, :HY_WIDTH] + rs[:, HY_WIDTH:], ts)
    wgt = jnp.where(is0, 1.0 / (2 * length), 2.0 / (2 * length))
    tc_ref[...] = tc * wgt
    ts_ref[...] = ts * wgt


def _hyena_filter(length, feats, cmat, smat, w1, b1, freq, w2, b2, w3, deltas):
    rb = min(length, 512)
    w1p = jnp.pad(w1, ((0, HY_FEAT_PAD - w1.shape[0]), (0, 0)))
    full = lambda shape: pl.BlockSpec(shape, lambda r: (0,) * len(shape))
    hid = HY_FILTER_HIDDEN
    return pl.pallas_call(
        functools.partial(_filter_kernel, length=length, rb=rb),
        grid=(length // rb,),
        in_specs=[
            full((length, HY_FEAT_PAD)), full((HY_FEAT_PAD, hid)), full((1, hid)), full((1, hid)),
            full((hid, hid)), full((1, hid)), full((hid, 2 * HY_WIDTH)), full((1, HY_WIDTH)),
            pl.BlockSpec((rb, length), lambda r: (r, 0)),
            pl.BlockSpec((rb, length), lambda r: (r, 0)),
        ],
        out_specs=[pl.BlockSpec((rb, HY_WIDTH), lambda r: (r, 0)),
                   pl.BlockSpec((rb, HY_WIDTH), lambda r: (r, 0))],
        out_shape=[jax.ShapeDtypeStruct((length, HY_WIDTH), F32)] * 2,
        scratch_shapes=[pltpu.VMEM((length, 2 * HY_WIDTH), BF16)],
        compiler_params=_vmem(48),
        name=f"hyena_filter_{length}",
    )(feats, w1p, b1.reshape(1, hid), freq.reshape(1, hid), w2, b2.reshape(1, hid), w3,
      deltas.reshape(1, HY_WIDTH), cmat, smat)


def _hyena_kernel(x0_ref, x1_ref, v_ref, w0_ref, w1_ref, w2_ref, b0_ref, b1_ref, b2_ref, skip_ref,
                  tc_ref, ts_ref, c_ref, s_ref, ct_ref, st_ref, o_ref,
                  zb_ref, zs_ref, x0c_ref, acc_ref, *, nseq, length, cb, fb):
    f = pl.program_id(1)
    nf = pl.num_programs(1)

    def short_conv(u, w_ref, b_ref):
        row = lax.broadcasted_iota(jnp.int32, u.shape, 0)
        prev = jnp.where(row == 0, 0.0, pltpu.roll(u, 1, 0))
        nxt = jnp.where(row == length - 1, 0.0, pltpu.roll(u, length - 1, 0))
        return prev * w_ref[0:1, :] + u * w_ref[1:2, :] + nxt * w_ref[2:3, :] + b_ref[...]

    @pl.when(f == 0)
    def _():
        for b in range(nseq):
            cols = slice(b * cb, (b + 1) * cb)
            x0c_ref[:, cols] = short_conv(x0_ref[b], w0_ref, b0_ref)
            z = short_conv(x1_ref[b], w1_ref, b1_ref) * short_conv(v_ref[b], w2_ref, b2_ref)
            zb_ref[:, cols] = z.astype(BF16)
            zs_ref[:, cols] = z * skip_ref[...]
        acc_ref[...] = jnp.zeros_like(acc_ref)

    zb = zb_ref[...]
    zc = jnp.dot(c_ref[...], zb, preferred_element_type=F32)
    zsn = jnp.dot(s_ref[...], zb, preferred_element_type=F32)
    tc = jnp.concatenate([tc_ref[...]] * nseq, axis=1)
    ts = jnp.concatenate([ts_ref[...]] * nseq, axis=1)
    grow = f * fb + lax.broadcasted_iota(jnp.int32, zc.shape, 0)
    is0 = grow == 0
    yc = jnp.where(is0, zc * tc, zc * tc - zsn * ts)
    ys = jnp.where(is0, zsn * ts, zc * ts + zsn * tc)
    acc_ref[...] += (jnp.dot(ct_ref[...], yc.astype(BF16), preferred_element_type=F32)
                     + jnp.dot(st_ref[...], ys.astype(BF16), preferred_element_type=F32))

    @pl.when(f == nf - 1)
    def _():
        for b in range(nseq):
            cols = slice(b * cb, (b + 1) * cb)
            o_ref[b] = (x0c_ref[:, cols] * (acc_ref[:, cols] + zs_ref[:, cols])).astype(o_ref.dtype)


def _hyena(u, first_seq_block, nseq, length, conv_w, conv_b, skip, tc, ts, cmat, smat, stmat):
    cb = 128
    fb = min(length, 512)
    u3 = u.reshape(N_TOK // length, length, EVEN_IN)
    ncb = HY_WIDTH // cb
    width = nseq * cb

    def ublock(part):
        return pl.BlockSpec((nseq, length, cb), lambda c, f: (first_seq_block, 0, part * ncb + c))

    def wblock(part, rows):
        return pl.BlockSpec((rows, cb), lambda c, f: (0, part * ncb + c))

    return pl.pallas_call(
        functools.partial(_hyena_kernel, nseq=nseq, length=length, cb=cb, fb=fb),
        grid=(ncb, length // fb),
        in_specs=[
            ublock(0), ublock(1), ublock(2),
            wblock(0, 3), wblock(1, 3), wblock(2, 3),
            wblock(0, 1), wblock(1, 1), wblock(2, 1),
            pl.BlockSpec((1, cb), lambda c, f: (0, c)),
            pl.BlockSpec((fb, cb), lambda c, f: (f, c)),
            pl.BlockSpec((fb, cb), lambda c, f: (f, c)),
            pl.BlockSpec((fb, length), lambda c, f: (f, 0)),
            pl.BlockSpec((fb, length), lambda c, f: (f, 0)),
            pl.BlockSpec((length, fb), lambda c, f: (0, f)),
            pl.BlockSpec((length, fb), lambda c, f: (0, f)),
        ],
        out_specs=pl.BlockSpec((nseq, length, cb), lambda c, f: (0, 0, c)),
        out_shape=jax.ShapeDtypeStruct((nseq, length, HY_WIDTH), BF16),
        scratch_shapes=[pltpu.VMEM((length, width), BF16), pltpu.VMEM((length, width), F32),
                        pltpu.VMEM((length, width), F32), pltpu.VMEM((length, width), F32)],
        compiler_params=_vmem(48),
        name=f"hyena_{length}",
    )(u3, u3, u3, conv_w, conv_w, conv_w, conv_b.reshape(1, -1), conv_b.reshape(1, -1),
      conv_b.reshape(1, -1), skip.reshape(1, HY_WIDTH), tc, ts, cmat, smat, cmat, stmat
      ).reshape(nseq * length, HY_WIDTH)


def _head_rms(x, seg, gain):
    x2 = x * x
    hi = x2.astype(BF16)
    lo = (x2 - hi.astype(F32)).astype(BF16)
    ss = jnp.dot(hi, seg, preferred_element_type=F32) + jnp.dot(lo, seg, preferred_element_type=F32)
    return x * lax.rsqrt(ss * (1.0 / HEAD_DIM) + EPS) * gain


def _rope(x, cos, sin_signed):
    width = x.shape[1]
    lane = lax.broadcasted_iota(jnp.int32, x.shape, 1)
    first = (lane // ROPE_FREQS) % 2 == 0
    partner = jnp.where(first, pltpu.roll(x, width - ROPE_FREQS, 1), pltpu.roll(x, ROPE_FREQS, 1))
    return x * cos + partner * sin_signed


def _stack_heads(x, g):
    return jnp.concatenate(
        [x[:, (g * ATT_GROUP + j) * HEAD_DIM:(g * ATT_GROUP + j + 1) * HEAD_DIM] for j in range(ATT_GROUP)], axis=0)


def _sink_column(sink_ref, g, rows):
    return jnp.concatenate([jnp.full((rows, 1), sink_ref[g * ATT_GROUP + j], F32) for j in range(ATT_GROUP)], axis=0)


def _unstack_heads(outs, rows):
    return jnp.concatenate([outs[g][j * rows:(j + 1) * rows, :]
                            for g in range(ATT_KV_HEADS) for j in range(ATT_GROUP)], axis=1)


_DOT_NT = (((1,), (1,)), ((), ()))


def _ctx_attn_kernel(sink_ref, q_ref, k_ref, v_ref, seg_ref, qg_ref, kg_ref, o_ref, ko_ref, vo_ref):
    seg = seg_ref[...]
    qn = _head_rms(q_ref[...], seg, qg_ref[...]) * ATT_SCALE
    kn = _head_rms(k_ref[...], seg[:KV_WIDTH, :KV_WIDTH], kg_ref[...])
    v = v_ref[...]
    ko_ref[...] = kn
    vo_ref[...] = v
    outs = []
    for g in range(ATT_KV_HEADS):
        cols = slice(g * HEAD_DIM, (g + 1) * HEAD_DIM)
        q = _stack_heads(qn, g).astype(BF16)
        s = lax.dot_general(q, kn[:, cols].astype(BF16), _DOT_NT, preferred_element_type=F32)
        sink = _sink_column(sink_ref, g, L_CTX)
        m = jnp.maximum(jnp.max(s, axis=-1, keepdims=True), sink)
        p = jnp.exp(s - m)
        den = jnp.sum(p, axis=-1, keepdims=True) + jnp.exp(sink - m)
        o = jnp.dot(p.astype(BF16), v[:, cols].astype(BF16), preferred_element_type=F32)
        outs.append(o / den)
    o_ref[...] = _unstack_heads(outs, L_CTX).astype(o_ref.dtype)


def _ctx_attention(u, seg, q_gain, k_gain, sink):
    qcol = 3 * HY_WIDTH // ATT_WIDTH
    kcol = (3 * HY_WIDTH + ATT_WIDTH) // KV_WIDTH
    return pl.pallas_call(
        _ctx_attn_kernel,
        grid_spec=pltpu.PrefetchScalarGridSpec(
            num_scalar_prefetch=1,
            grid=(N_CTX_SEQ,),
            in_specs=[
                pl.BlockSpec((L_CTX, ATT_WIDTH), lambda b, s: (b, qcol)),
                pl.BlockSpec((L_CTX, KV_WIDTH), lambda b, s: (b, kcol)),
                pl.BlockSpec((L_CTX, KV_WIDTH), lambda b, s: (b, kcol + 1)),
                pl.BlockSpec((ATT_WIDTH, ATT_WIDTH), lambda b, s: (0, 0)),
                pl.BlockSpec((1, ATT_WIDTH), lambda b, s: (0, 0)),
                pl.BlockSpec((1, KV_WIDTH), lambda b, s: (0, 0)),
            ],
            out_specs=[
                pl.BlockSpec((L_CTX, ATT_WIDTH), lambda b, s: (b, 0)),
                pl.BlockSpec((L_CTX, KV_WIDTH), lambda b, s: (b, 0)),
                pl.BlockSpec((L_CTX, KV_WIDTH), lambda b, s: (b, 0)),
            ],
        ),
        out_shape=[jax.ShapeDtypeStruct((N_CTX_TOK, ATT_WIDTH), BF16),
                   jax.ShapeDtypeStruct((N_CTX_TOK, KV_WIDTH), F32),
                   jax.ShapeDtypeStruct((N_CTX_TOK, KV_WIDTH), F32)],
        name="ctx_attention",
    )(sink, u, u, u, seg, jnp.tile(q_gain, ATT_HEADS).reshape(1, ATT_WIDTH),
      jnp.tile(k_gain, ATT_KV_HEADS).reshape(1, KV_WIDTH))


def _rope_tables():
    pos = jnp.arange(L_LAT, dtype=jnp.int32)
    row = (pos // GRID_W).astype(F32)
    col = (pos % GRID_W).astype(F32)
    inv = ROPE_THETA ** (-jnp.arange(ROPE_FREQS, dtype=F32) / ROPE_FREQS)
    ar, ac = row[:, None] * inv, col[:, None] * inv
    cos = jnp.concatenate([jnp.cos(ar), jnp.cos(ar), jnp.cos(ac), jnp.cos(ac)], axis=-1)
    sin = jnp.concatenate([-jnp.sin(ar), jnp.sin(ar), -jnp.sin(ac), jnp.sin(ac)], axis=-1)
    return jnp.tile(cos, (1, ATT_HEADS)), jnp.tile(sin, (1, ATT_HEADS))


def _lat_attn_kernel(sink_ref, q_ref, k_ref, v_ref, ck_ref, cv_ref, cosq_ref, sinq_ref, cosk_ref, sink_k_ref,
                     seg_ref, qg_ref, kg_ref, o_ref, kn_ref):
    n = pl.program_id(1)
    seg = seg_ref[...]

    @pl.when(n == 0)
    def _():
        kn = _head_rms(k_ref[...], seg[:KV_WIDTH, :KV_WIDTH], kg_ref[...])
        kn_ref[...] = _rope(kn, cosk_ref[...], sink_k_ref[...]).astype(BF16)

    qn = _head_rms(q_ref[...], seg, qg_ref[...])
    qn = _rope(qn, cosq_ref[...], sinq_ref[...]) * ATT_SCALE
    span = 3 * ATT_BLOCK
    start = pl.multiple_of(jnp.clip((n - 1) * ATT_BLOCK, 0, L_LAT - span), ATT_BLOCK)
    rows = ATT_GROUP * ATT_BLOCK
    q_pos = n * ATT_BLOCK + (lax.broadcasted_iota(jnp.int32, (rows, span), 0) % ATT_BLOCK)
    k_pos = start + lax.broadcasted_iota(jnp.int32, (rows, span), 1)
    valid = jnp.abs(q_pos - k_pos) <= WINDOW
    k_loc = kn_ref[pl.ds(start, span), :]
    v_loc = v_ref[pl.ds(start, span), :].astype(BF16)
    k_ctx = ck_ref[0].astype(BF16)
    v_ctx = cv_ref[0].astype(BF16)
    outs = []
    for g in range(ATT_KV_HEADS):
        cols = slice(g * HEAD_DIM, (g + 1) * HEAD_DIM)
        q = _stack_heads(qn, g).astype(BF16)
        s_loc = lax.dot_general(q, k_loc[:, cols], _DOT_NT, preferred_element_type=F32)
        s_loc = jnp.where(valid, s_loc, NEG_BIG)
        s_ctx = lax.dot_general(q, k_ctx[:, cols], _DOT_NT, preferred_element_type=F32)
        sink = _sink_column(sink_ref, g, ATT_BLOCK)
        m = jnp.maximum(jnp.maximum(jnp.max(s_loc, axis=-1, keepdims=True),
                                    jnp.max(s_ctx, axis=-1, keepdims=True)), sink)
        p_loc = jnp.exp(s_loc - m)
        p_ctx = jnp.exp(s_ctx - m)
        den = (jnp.sum(p_loc, axis=-1, keepdims=True) + jnp.sum(p_ctx, axis=-1, keepdims=True)
               + jnp.exp(sink - m))
        o = (jnp.dot(p_loc.astype(BF16), v_loc[:, cols], preferred_element_type=F32)
             + jnp.dot(p_ctx.astype(BF16), v_ctx[:, cols], preferred_element_type=F32))
        outs.append(o / den)
    o_ref[...] = _unstack_heads(outs, ATT_BLOCK).astype(o_ref.dtype)


def _lat_attention(u, cache_k, cache_v, seg, q_gain, k_gain, sink):
    qcol = 3 * HY_WIDTH // ATT_WIDTH
    kcol = (3 * HY_WIDTH + ATT_WIDTH) // KV_WIDTH
    nblk = L_LAT // ATT_BLOCK
    first_q_block = N_CTX_TOK // ATT_BLOCK
    first_seq = N_CTX_TOK // L_LAT
    cos, sin = _rope_tables()
    return pl.pallas_call(
        _lat_attn_kernel,
        grid_spec=pltpu.PrefetchScalarGridSpec(
            num_scalar_prefetch=1,
            grid=(N_LAT_SEQ, nblk),
            in_specs=[
                pl.BlockSpec((ATT_BLOCK, ATT_WIDTH), lambda b, n, s: (first_q_block + b * nblk + n, qcol)),
                pl.BlockSpec((L_LAT, KV_WIDTH), lambda b, n, s: (first_seq + b, kcol)),
                pl.BlockSpec((L_LAT, KV_WIDTH), lambda b, n, s: (first_seq + b, kcol + 1)),
                pl.BlockSpec((1, PAST_LEN, KV_WIDTH), lambda b, n, s: (b, 0, 0)),
                pl.BlockSpec((1, PAST_LEN, KV_WIDTH), lambda b, n, s: (b, 0, 0)),
                pl.BlockSpec((ATT_BLOCK, ATT_WIDTH), lambda b, n, s: (n, 0)),
                pl.BlockSpec((ATT_BLOCK, ATT_WIDTH), lambda b, n, s: (n, 0)),
                pl.BlockSpec((L_LAT, KV_WIDTH), lambda b, n, s: (0, 0)),
                pl.BlockSpec((L_LAT, KV_WIDTH), lambda b, n, s: (0, 0)),
                pl.BlockSpec((ATT_WIDTH, ATT_WIDTH), lambda b, n, s: (0, 0)),
                pl.BlockSpec((1, ATT_WIDTH), lambda b, n, s: (0, 0)),
                pl.BlockSpec((1, KV_WIDTH), lambda b, n, s: (0, 0)),
            ],
            out_specs=pl.BlockSpec((ATT_BLOCK, ATT_WIDTH), lambda b, n, s: (b * nblk + n, 0)),
            scratch_shapes=[pltpu.VMEM((L_LAT, KV_WIDTH), BF16)],
        ),
        out_shape=jax.ShapeDtypeStruct((N_LAT_SEQ * L_LAT, ATT_WIDTH), BF16),
        name="lat_attention",
    )(sink, u, u, u, cache_k.reshape(N_LAT_SEQ, PAST_LEN, KV_WIDTH), cache_v.reshape(N_LAT_SEQ, PAST_LEN, KV_WIDTH),
      cos, sin, cos[:, :KV_WIDTH], sin[:, :KV_WIDTH], seg,
      jnp.tile(q_gain, ATT_HEADS).reshape(1, ATT_WIDTH), jnp.tile(k_gain, ATT_KV_HEADS).reshape(1, KV_WIDTH))


def _ceil_to(v, m):
    return ((v + (m - 1)) // m) * m


def _post_mixer(x, mix, refs, tm):
    (w_ref, gain_ref, mod_ref, rwh_ref, rwm_ref, rb_ref, tri_ref, low_ref,
     x1_ref, sw_ref, cnt_ref, xg_ref, wb_ref) = refs

    @pl.when(pl.program_id(0) == 0)
    def _():
        wb_ref[...] = w_ref[...].astype(BF16)

    y = jnp.dot(mix, wb_ref[...], preferred_element_type=F32)
    x1 = x + mod_ref[0, 2:3, :] * y
    x1_ref[...] = x1
    xt = _norm_mod(x1, gain_ref[...], mod_ref[0, 3:4, :], mod_ref[0, 4:5, :])
    xh = xt.astype(BF16)
    xm = (xt - xh.astype(F32)).astype(BF16)
    logits = (lax.dot_general(rwh_ref[...], xh, _DOT_NT, preferred_element_type=F32)
              + lax.dot_general(rwm_ref[...], xh, _DOT_NT, preferred_element_type=F32)
              + lax.dot_general(rwh_ref[...], xm, _DOT_NT, preferred_element_type=F32)) + rb_ref[...]
    expert = lax.broadcasted_iota(jnp.int32, logits.shape, 0)
    vals, hits = [], []
    for _ in range(TOP_K):
        m = jnp.max(logits, axis=0, keepdims=True)
        sel = jnp.min(jnp.where(logits == m, expert, N_EXPERTS), axis=0, keepdims=True)
        vals.append(m)
        hits.append(expert == sel)
        logits = jnp.where(expert == sel, -jnp.inf, logits)
    es = [jnp.exp(v - vals[0]) for v in vals]
    den = es[0] + es[1] + es[2] + es[3]
    weights = [e / den for e in es]

    routed = sum(h.astype(F32) for h in hits)
    counts = jnp.sum(routed, axis=1, keepdims=True)
    cnt_ref[0] = counts.astype(jnp.int32)
    padded = _ceil_to(counts.astype(jnp.int32), MOE_GROUP).astype(F32)
    g0 = jnp.dot(low_ref[...], jnp.broadcast_to(padded, (N_EXPERTS, LANES)),
                 preferred_element_type=F32, precision=HIGHEST)[:, 0:1]
    earlier = jnp.dot(routed.astype(BF16), tri_ref[...], preferred_element_type=F32)
    row_of = g0 + earlier
    slots = [jnp.sum(jnp.where(h, row_of, 0.0), axis=0, keepdims=True) for h in hits]

    prow = lax.broadcasted_iota(jnp.int32, (LANES, tm), 0)
    packed = jnp.zeros((LANES, tm), F32)
    for k, vec in enumerate(slots + weights):
        packed = jnp.where(prow == k, vec, packed)
    sw_ref[...] = packed.T

    group_row = lax.broadcasted_iota(jnp.int32, (MOE_GROUPED_ROWS, tm), 0).astype(F32)
    pick = group_row == slots[0]
    for k in range(1, TOP_K):
        pick = jnp.logical_or(pick, group_row == slots[k])
    perm = jnp.where(pick, 1.0, 0.0).astype(BF16)
    xg_ref[...] = jnp.dot(perm, xh, preferred_element_type=F32).astype(BF16)


def _post_even_kernel(x_ref, ac_ref, al_ref, tc_ref, tl_ref, *refs, tm):
    is_ctx = pl.program_id(0) * tm < N_CTX_TOK
    mix = jnp.concatenate([jnp.where(is_ctx, ac_ref[...], al_ref[...]),
                           jnp.where(is_ctx, tc_ref[...], tl_ref[...])], axis=1)
    _post_mixer(x_ref[...], mix, refs, tm)


def _post_odd_kernel(x_ref, yc_ref, yl_ref, *refs, tm):
    mix = jnp.where(pl.program_id(0) * tm < N_CTX_TOK, yc_ref[...], yl_ref[...])
    _post_mixer(x_ref[...], mix, refs, tm)


def _post_call(kernel_fn, name, mixer_specs, mixer_args, k_in, w_out, gain, mod, router_w, router_b):
    tm = ROW_TILE
    nsteps = N_TOK // tm
    row = lax.broadcasted_iota(jnp.int32, (tm, tm), 0)
    col = lax.broadcasted_iota(jnp.int32, (tm, tm), 1)
    tri = (row < col).astype(BF16)
    er = lax.broadcasted_iota(jnp.int32, (N_EXPERTS, N_EXPERTS), 0)
    ec = lax.broadcasted_iota(jnp.int32, (N_EXPERTS, N_EXPERTS), 1)
    low = (ec < er).astype(F32)
    rw_t = router_w.T
    rw_hi = rw_t.astype(BF16)
    rw_mid = (rw_t - rw_hi.astype(F32)).astype(BF16)
    const = lambda shape: pl.BlockSpec(shape, lambda i: (0,) * len(shape))
    return pl.pallas_call(
        functools.partial(kernel_fn, tm=tm),
        grid=(nsteps,),
        in_specs=[pl.BlockSpec((tm, D_MODEL), lambda i: (i, 0))] + mixer_specs + [
            const((k_in, D_MODEL)),
            const((1, D_MODEL)),
            pl.BlockSpec((1, 6, D_MODEL), lambda i: (_cond_of_tile(i, tm), 0, 0)),
            const((N_EXPERTS, D_MODEL)), const((N_EXPERTS, D_MODEL)), const((N_EXPERTS, 1)),
            const((tm, tm)), const((N_EXPERTS, N_EXPERTS)),
        ],
        out_specs=[
            pl.BlockSpec((tm, D_MODEL), lambda i: (i, 0)),
            pl.BlockSpec((tm, LANES), lambda i: (i, 0)),
            pl.BlockSpec((1, N_EXPERTS, 1), lambda i: (i, 0, 0)),
            pl.BlockSpec((MOE_GROUPED_ROWS, D_MODEL), lambda i: (i, 0)),
        ],
        out_shape=[
            jax.ShapeDtypeStruct((N_TOK, D_MODEL), F32),
            jax.ShapeDtypeStruct((N_TOK, LANES), F32),
            jax.ShapeDtypeStruct((nsteps, N_EXPERTS, 1), jnp.int32),
            jax.ShapeDtypeStruct((nsteps * MOE_GROUPED_ROWS, D_MODEL), BF16),
        ],
        scratch_shapes=[pltpu.VMEM((k_in, D_MODEL), BF16)],
        compiler_params=_vmem(56),
        name=name,
    )(*mixer_args, w_out, gain.reshape(1, D_MODEL), mod, rw_hi, rw_mid, router_b.reshape(N_EXPERTS, 1), tri, low)


def _post_even(x, a_ctx, a_lat, t_ctx, t_lat, w_out, gain, mod, router_w, router_b):
    tm = ROW_TILE
    nctx = N_CTX_TOK // tm
    ctx_map = lambda i: (jnp.minimum(i, nctx - 1), 0)
    lat_map = lambda i: (jnp.maximum(i - nctx, 0), 0)
    specs = [pl.BlockSpec((tm, HY_WIDTH), ctx_map), pl.BlockSpec((tm, HY_WIDTH), lat_map),
             pl.BlockSpec((tm, ATT_WIDTH), ctx_map), pl.BlockSpec((tm, ATT_WIDTH), lat_map)]
    return _post_call(_post_even_kernel, "post_even", specs, (x, a_ctx, a_lat, t_ctx, t_lat),
                      HY_WIDTH + ATT_WIDTH, w_out, gain, mod, router_w, router_b)


def _post_odd(x, y_ctx, y_lat, w_out, gain, mod, router_w, router_b):
    tm = ROW_TILE
    nctx = N_CTX_TOK // tm
    specs = [pl.BlockSpec((tm, RET_V_WIDTH), lambda i: (jnp.minimum(i, nctx - 1), 0)),
             pl.BlockSpec((tm, RET_V_WIDTH), lambda i: (jnp.maximum(i - nctx, 0), 0))]
    return _post_call(_post_odd_kernel, "post_odd", specs, (x, y_ctx, y_lat),
                      RET_V_WIDTH, w_out, gain, mod, router_w, router_b)


def _moe_plan(counts):
    t, grp, big = MOE_TILE, MOE_GROUP, MOE_GROUPED_ROWS
    n16 = _ceil_to(counts.reshape(-1, N_EXPERTS), grp)
    nsteps = n16.shape[0]
    g0 = jnp.cumsum(n16, axis=1) - n16
    e0 = jnp.cumsum(n16, axis=0) - n16
    rows_e = jnp.sum(n16, axis=0)
    ntiles = (rows_e + t - 1) // t
    tile_end = jnp.cumsum(ntiles)
    tile_first = tile_end - ntiles
    total = tile_end[-1]
    experts = jnp.arange(N_EXPERTS, dtype=jnp.int32)
    tiles = jnp.arange(MOE_NUM_TILES, dtype=jnp.int32)
    valid = tiles < total
    ti = jnp.minimum(tiles, total - 1)
    e_of = jnp.sum((tile_end[None, :] <= ti[:, None]).astype(jnp.int32), axis=1)
    onehot = (e_of[:, None] == experts[None, :]).astype(jnp.int32)
    pick = lambda v: jnp.sum(onehot * v[None, :], axis=1)
    pick2 = lambda m: jnp.sum(onehot[:, None, :] * m[None, :, :], axis=2)
    r = ((ti - pick(tile_first)) * t)[:, None] + grp * jnp.arange(t // grp, dtype=jnp.int32)[None, :]
    ends = pick2(e0 + n16)
    step = jnp.sum((ends[:, None, :] <= r[:, :, None]).astype(jnp.int32), axis=2)
    step = jnp.minimum(step, nsteps - 1)
    sel = (step[:, :, None] == jnp.arange(nsteps, dtype=jnp.int32)[None, None, :]).astype(jnp.int32)
    at_step = lambda m: jnp.sum(sel * pick2(m)[:, None, :], axis=2)
    src = step * big + at_step(g0) + r - at_step(e0)
    live = jnp.logical_and(valid[:, None], r < pick(rows_e)[:, None])
    moe_src = jnp.where(live, src, 0).reshape(-1)
    g = grp * jnp.arange(big // grp, dtype=jnp.int32)
    gend = g0 + n16
    ce = jnp.sum((gend[:, None, :] <= g[None, :, None]).astype(jnp.int32), axis=2)
    used = ce < N_EXPERTS
    ce = jnp.minimum(ce, N_EXPERTS - 1)
    csel = (ce[:, :, None] == experts[None, None, :]).astype(jnp.int32)
    of_e = lambda m: jnp.sum(csel * m[:, None, :], axis=2)
    base = jnp.sum(csel * (tile_first * t)[None, None, :], axis=2)
    csrc = base + of_e(e0) + g[None, :] - of_e(g0)
    comb_src = jnp.where(used, csrc, 0).reshape(-1)
    as_i32 = lambda v: v.astype(jnp.int32)
    return as_i32(e_of), as_i32(valid), as_i32(moe_src), as_i32(comb_src)


def _moe_kernel(te_ref, tv_ref, src_ref, xg_hbm, w1_ref, b1_ref, w2_ref, b2_ref, y_ref,
                xbuf, w1b, w2b, sem_in):
    i = pl.program_id(0)
    nt = pl.num_programs(0)
    t, grp = MOE_TILE, MOE_GROUP
    slot = i % 2

    def issue_gather(tile, sl):
        for c in range(t // grp):
            src = pl.multiple_of(src_ref[tile * (t // grp) + c], grp)
            pltpu.make_async_copy(xg_hbm.at[pl.ds(src, grp), :], xbuf.at[sl, pl.ds(c * grp, grp), :],
                                  sem_in.at[sl]).start()

    def valid(tile):
        return tv_ref[jnp.clip(tile, 0, nt - 1)] > 0

    @pl.when(i == 0)
    def _():
        issue_gather(0, 0)

    @pl.when(valid(i))
    def _():
        pltpu.make_async_copy(xg_hbm.at[pl.ds(0, t), :], xbuf.at[slot], sem_in.at[slot]).wait()

        @pl.when(jnp.logical_and(i + 1 < nt, valid(i + 1)))
        def _():
            issue_gather(i + 1, 1 - slot)

        first = jnp.logical_or(i == 0, te_ref[i] != te_ref[jnp.maximum(i - 1, 0)])

        @pl.when(first)
        def _():
            w1b[...] = w1_ref[0].astype(BF16)
            w2b[...] = w2_ref[0].astype(BF16)

        h = jnp.dot(xbuf[slot], w1b[...], preferred_element_type=F32) + b1_ref[0]
        glu = jnp.minimum(h[:, :D_FF], SWIGLU_LIMIT)
        lin = jnp.clip(h[:, D_FF:], -SWIGLU_LIMIT, SWIGLU_LIMIT)
        act = (glu * _sigmoid(SWIGLU_ALPHA * glu) * (lin + 1.0)).astype(BF16)
        y_ref[...] = (jnp.dot(act, w2b[...], preferred_element_type=F32) + b2_ref[0]).astype(y_ref.dtype)

    @pl.when(jnp.logical_not(valid(i)))
    def _():
        y_ref[...] = jnp.zeros_like(y_ref)


def _moe_experts(xg, plan, w1, b1, w2, b2):
    te, tv, src, _ = plan
    t = MOE_TILE
    return pl.pallas_call(
        _moe_kernel,
        grid_spec=pltpu.PrefetchScalarGridSpec(
            num_scalar_prefetch=3,
            grid=(MOE_NUM_TILES,),
            in_specs=[
                pl.BlockSpec(memory_space=pl.ANY),
                pl.BlockSpec((1, D_MODEL, 2 * D_FF), lambda i, te, *_: (te[i], 0, 0)),
                pl.BlockSpec((1, 1, 2 * D_FF), lambda i, te, *_: (te[i], 0, 0)),
                pl.BlockSpec((1, D_FF, D_MODEL), lambda i, te, *_: (te[i], 0, 0)),
                pl.BlockSpec((1, 1, D_MODEL), lambda i, te, *_: (te[i], 0, 0)),
            ],
            out_specs=pl.BlockSpec((t, D_MODEL), lambda i, *_: (i, 0)),
            scratch_shapes=[
                pltpu.VMEM((2, t, D_MODEL), BF16),
                pltpu.VMEM((D_MODEL, 2 * D_FF), BF16),
                pltpu.VMEM((D_FF, D_MODEL), BF16),
                pltpu.SemaphoreType.DMA((2,)),
            ],
        ),
        out_shape=jax.ShapeDtypeStruct((MOE_NUM_TILES * t, D_MODEL), BF16),
        compiler_params=_vmem(56),
        name="moe_experts",
    )(te, tv, src, xg, w1, b1.reshape(N_EXPERTS, 1, 2 * D_FF), w2, b2.reshape(N_EXPERTS, 1, D_MODEL))


def _combine_value(src_ref, x1_ref, sw_ref, mod_ref, ys_hbm, ybuf, sem, tm):
    s = pl.program_id(0)
    nsteps = pl.num_programs(0)
    slot = s % 2
    grp, big = MOE_GROUP, MOE_GROUPED_ROWS
    nchunk = big // grp

    def issue_gather(step, sl):
        for c in range(nchunk):
            src = pl.multiple_of(src_ref[step * nchunk + c], grp)
            pltpu.make_async_copy(ys_hbm.at[pl.ds(src, grp), :], ybuf.at[sl, pl.ds(c * grp, grp), :],
                                  sem.at[sl]).start()

    @pl.when(s == 0)
    def _():
        issue_gather(0, 0)

    pltpu.make_async_copy(ys_hbm.at[pl.ds(0, big), :], ybuf.at[slot], sem.at[slot]).wait()

    @pl.when(s + 1 < nsteps)
    def _():
        issue_gather(s + 1, 1 - slot)

    sw = sw_ref[...]
    col = lax.broadcasted_iota(jnp.int32, (tm, big), 1).astype(F32)
    wmat = jnp.zeros((tm, big), F32)
    for k in range(TOP_K):
        wmat = jnp.where(col == sw[:, k:k + 1], sw[:, TOP_K + k:TOP_K + k + 1], wmat)
    w_hi = wmat.astype(BF16)
    w_lo = (wmat - w_hi.astype(F32)).astype(BF16)
    yg = ybuf[slot]
    moe = (jnp.dot(w_hi, yg, preferred_element_type=F32) + jnp.dot(w_lo, yg, preferred_element_type=F32))
    return x1_ref[...] + mod_ref[0, 5:6, :] * moe


def _combine_kernel(src_ref, x1_ref, sw_ref, mod_ref, ys_hbm, o_ref, ybuf, sem, *, tm):
    o_ref[...] = _combine_value(src_ref, x1_ref, sw_ref, mod_ref, ys_hbm, ybuf, sem, tm)


def _combine_split_kernel(src_ref, x1_ref, sw_ref, mod_ref, ys_hbm, oc_ref, ol_ref, ybuf, sem, *, tm):
    val = _combine_value(src_ref, x1_ref, sw_ref, mod_ref, ys_hbm, ybuf, sem, tm)
    is_ctx = pl.program_id(0) * tm < N_CTX_TOK

    @pl.when(is_ctx)
    def _():
        oc_ref[...] = val

    @pl.when(jnp.logical_not(is_ctx))
    def _():
        ol_ref[...] = val


def _combine(x1, ys, sw, plan, mod, split):
    tm = ROW_TILE
    nctx = N_CTX_TOK // tm
    in_specs = [
        pl.BlockSpec((tm, D_MODEL), lambda i, *_: (i, 0)),
        pl.BlockSpec((tm, LANES), lambda i, *_: (i, 0)),
        pl.BlockSpec((1, 6, D_MODEL), lambda i, *_: (_cond_of_tile(i, tm), 0, 0)),
        pl.BlockSpec(memory_space=pl.ANY),
    ]
    scratch = [pltpu.VMEM((2, MOE_GROUPED_ROWS, D_MODEL), BF16), pltpu.SemaphoreType.DMA((2,))]
    if not split:
        kernel_fn, name = _combine_kernel, "moe_combine"
        out_specs = pl.BlockSpec((tm, D_MODEL), lambda i, *_: (i, 0))
        out_shape = jax.ShapeDtypeStruct((N_TOK, D_MODEL), F32)
    else:
        kernel_fn, name = _combine_split_kernel, "moe_combine_split"
        out_specs = [pl.BlockSpec((tm, D_MODEL), lambda i, *_: (jnp.minimum(i, nctx - 1), 0)),
                     pl.BlockSpec((tm, D_MODEL), lambda i, *_: (jnp.maximum(i - nctx, 0), 0))]
        out_shape = [jax.ShapeDtypeStruct((N_CTX_TOK, D_MODEL), F32),
                     jax.ShapeDtypeStruct((N_TOK - N_CTX_TOK, D_MODEL), F32)]
    return pl.pallas_call(
        functools.partial(kernel_fn, tm=tm),
        grid_spec=pltpu.PrefetchScalarGridSpec(
            num_scalar_prefetch=1, grid=(N_TOK // tm,), in_specs=in_specs, out_specs=out_specs,
            scratch_shapes=scratch),
        out_shape=out_shape,
        compiler_params=_vmem(56),
        name=name,
    )(plan[3], x1, sw, mod, ys)


def _retention_kernel(lg_ref, q_ref, k_ref, v_ref, gf_ref, gb_ref, *rest, length, has_s0, emit_state):
    rest = list(rest)
    s0_ref = rest.pop(0) if has_s0 else None
    o_ref = rest.pop(0)
    so_ref = rest.pop(0) if emit_state else None
    s_ref, yf_ref = rest
    c = RET_CHUNK
    nc = length // c
    ii = lax.broadcasted_iota(jnp.int32, (c, c), 0).astype(F32)
    jj = lax.broadcasted_iota(jnp.int32, (c, c), 1).astype(F32)
    ci = lax.broadcasted_iota(jnp.int32, (c, 1), 0).astype(F32)

    def scan(direction):
        lg = -jnp.exp(lg_ref[direction, 0])
        lg1 = lg[:, 0:1]
        if direction == 0:
            diff = ii - jj
            q_decay = jnp.exp(lg1 * (ci + 1.0))
            k_decay = jnp.exp(lg1 * (c - 1.0 - ci))
        else:
            diff = jj - ii
            q_decay = jnp.exp(lg1 * (c - ci))
            k_decay = jnp.exp(lg1 * ci)
        inner = jnp.where(diff >= 0, jnp.exp(lg * jnp.maximum(diff, 0.0)), 0.0)
        chunk_decay = jnp.exp(lg1 * float(c))
        if has_s0:
            s_ref[...] = s0_ref[0, direction, 0]
        else:
            s_ref[...] = jnp.zeros_like(s_ref)

        def body(step, carry):
            ch = step if direction == 0 else nc - 1 - step
            rows = pl.ds(pl.multiple_of(ch * c, c), c)
            qc = q_ref[rows, :]
            kc = k_ref[rows, :].astype(F32) * (RET_DK ** -0.5)
            vc = v_ref[rows, :]
            s = s_ref[...]
            att = lax.dot_general(qc, kc.astype(BF16), _DOT_NT, preferred_element_type=F32) * inner
            o = (jnp.dot(att.astype(BF16), vc, preferred_element_type=F32)
                 + jnp.dot(qc, s.astype(BF16), preferred_element_type=F32) * q_decay)
            kd = (kc * k_decay).T.astype(BF16)
            s_ref[...] = s * chunk_decay + jnp.dot(kd, vc, preferred_element_type=F32)
            on = o * lax.rsqrt(jnp.mean(o * o, axis=-1, keepdims=True) + EPS)
            if direction == 0:
                g = gf_ref[rows, :].astype(F32)
                yf_ref[rows, :] = g * _sigmoid(g) * on
            else:
                g = gb_ref[rows, :].astype(F32)
                o_ref[rows, :] = (yf_ref[rows, :] + g * _sigmoid(g) * on).astype(o_ref.dtype)
            return carry

        lax.fori_loop(0, nc, body, 0)
        if emit_state:
            so_ref[0, direction, 0] = s_ref[...]

    scan(0)
    scan(1)


def _retention(u, first_seq, nseq, length, decay_logit, s0, emit_state):
    row0 = first_seq
    lg = jnp.broadcast_to(decay_logit.astype(F32)[:, :, None, None], (2, RET_HEADS, 1, LANES))
    kcol = RET_QK_WIDTH // RET_DK
    vcol = 2 * RET_QK_WIDTH // RET_DV
    gfcol = vcol + RET_HEADS
    gbcol = gfcol + RET_HEADS
    in_specs = [
        pl.BlockSpec((2, 1, 1, LANES), lambda b, h: (0, h, 0, 0)),
        pl.BlockSpec((length, RET_DK), lambda b, h: (row0 + b, h)),
        pl.BlockSpec((length, RET_DK), lambda b, h: (row0 + b, kcol + h)),
        pl.BlockSpec((length, RET_DV), lambda b, h: (row0 + b, vcol + h)),
        pl.BlockSpec((length, RET_DV), lambda b, h: (row0 + b, gfcol + h)),
        pl.BlockSpec((length, RET_DV), lambda b, h: (row0 + b, gbcol + h)),
    ]
    args = [lg, u, u, u, u, u]
    state_spec = pl.BlockSpec((1, 2, 1, RET_DK, RET_DV), lambda b, h: (b, 0, h, 0, 0))
    if s0 is not None:
        in_specs.append(state_spec)
        args.append(s0)
    out_specs = [pl.BlockSpec((length, RET_DV), lambda b, h: (b, h))]
    out_shape = [jax.ShapeDtypeStruct((nseq * length, RET_V_WIDTH), BF16)]
    if emit_state:
        out_specs.append(state_spec)
        out_shape.append(jax.ShapeDtypeStruct((nseq, 2, RET_HEADS, RET_DK, RET_DV), F32))
    return pl.pallas_call(
        functools.partial(_retention_kernel, length=length, has_s0=s0 is not None, emit_state=emit_state),
        grid=(nseq, RET_HEADS),
        in_specs=in_specs,
        out_specs=out_specs,
        out_shape=out_shape,
        scratch_shapes=[pltpu.VMEM((RET_DK, RET_DV), F32), pltpu.VMEM((length, RET_DV), F32)],
        compiler_params=_vmem(48),
        name=f"retention_{length}",
    )(*args)


def kernel(x_prompt, x_sample, cache_k0, cache_v0, state_ret1, c, c_ctx, l0_norm_mix, l0_ada_w, l0_ada_b, l0_w_in, l0_conv_w, l0_conv_b, l0_filt_w1, l0_filt_b1, l0_filt_freq, l0_filt_w2, l0_filt_b2, l0_filt_w3, l0_filt_deltas, l0_hy_skip, l0_q_gain, l0_k_gain, l0_sink, l0_w_out, l0_norm_ffn, l0_router_w, l0_router_b, l0_moe_w1, l0_moe_b1, l0_moe_w2, l0_moe_b2, l1_norm_mix, l1_ada_w, l1_ada_b, l1_w_in, l1_ret_decay_logit, l1_w_out, l1_norm_ffn, l1_router_w, l1_router_b, l1_moe_w1, l1_moe_b1, l1_moe_w2, l1_moe_b2):
    x = jnp.concatenate([x_prompt.reshape(N_CTX_TOK, D_MODEL), x_sample.reshape(N_TOK - N_CTX_TOK, D_MODEL)], axis=0)
    cond = jnp.zeros((SUBLANES, D_MODEL), F32).at[0].set(c_ctx).at[1:1 + N_LAT_SEQ].set(c)
    mod0 = _adaln(cond, l0_ada_w, l0_ada_b)
    mod1 = _adaln(cond, l1_ada_w, l1_ada_b)

    u = _in_proj(x, l0_norm_mix, mod0, l0_w_in, EVEN_IN // 2, F32)
    filt = (l0_filt_w1, l0_filt_b1, l0_filt_freq, l0_filt_w2, l0_filt_b2, l0_filt_w3, l0_filt_deltas)
    hy = []
    for first_block, nseq, length in ((0, N_CTX_SEQ, L_CTX), (N_CTX_TOK // L_LAT // N_LAT_SEQ, N_LAT_SEQ, L_LAT)):
        cmat, smat, stmat = _dft_matrices(length)
        tc, ts = _hyena_filter(length, _filter_features(length), cmat, smat, *filt)
        hy.append(_hyena(u, first_block, nseq, length, l0_conv_w, l0_conv_b, l0_hy_skip, tc, ts, cmat, smat, stmat))
    head = lax.broadcasted_iota(jnp.int32, (ATT_WIDTH, ATT_WIDTH), 0) // HEAD_DIM
    seg = (head == head.T).astype(BF16)
    att_ctx, new_k, new_v = _ctx_attention(u, seg, l0_q_gain, l0_k_gain, l0_sink)
    att_lat = _lat_attention(u, cache_k0, cache_v0, seg, l0_q_gain, l0_k_gain, l0_sink)
    x1, sw, counts, xg = _post_even(x, hy[0], hy[1], att_ctx, att_lat, l0_w_out, l0_norm_ffn, mod0,
                                    l0_router_w, l0_router_b)
    plan = _moe_plan(counts)
    ys = _moe_experts(xg, plan, l0_moe_w1, l0_moe_b1, l0_moe_w2, l0_moe_b2)
    x = _combine(x1, ys, sw, plan, mod0, split=False)

    u = _in_proj(x, l1_norm_mix, mod1, l1_w_in, 2048, BF16)
    y_ctx, new_state = _retention(u, 0, N_CTX_SEQ, L_CTX, l1_ret_decay_logit, None, True)
    (y_lat,) = _retention(u, N_CTX_TOK // L_LAT, N_LAT_SEQ, L_LAT, l1_ret_decay_logit, state_ret1, False)
    x1, sw, counts, xg = _post_odd(x, y_ctx, y_lat, l1_w_out, l1_norm_ffn, mod1, l1_router_w, l1_router_b)
    plan = _moe_plan(counts)
    ys = _moe_experts(xg, plan, l1_moe_w1, l1_moe_b1, l1_moe_w2, l1_moe_b2)
    y_prompt, y_sample = _combine(x1, ys, sw, plan, mod1, split=True)

    return (y_prompt.reshape(N_CTX_SEQ, L_CTX, D_MODEL), y_sample.reshape(N_LAT_SEQ, L_LAT, D_MODEL),
            new_k.reshape(N_CTX_SEQ, L_CTX, ATT_KV_HEADS, HEAD_DIM),
            new_v.reshape(N_CTX_SEQ, L_CTX, ATT_KV_HEADS, HEAD_DIM), new_state)
```

```python
import functools
import math

import jax
import jax.numpy as jnp
from jax import lax
from jax.experimental import pallas as pl
from jax.experimental.pallas import tpu as pltpu

F32 = jnp.float32
BF16 = jnp.bfloat16
HIGHEST = lax.Precision.HIGHEST

D_MODEL = 1024
N_CTX_SEQ, L_CTX = 16, 256
N_LAT_SEQ, L_LAT = 2, 2048
N_CTX_TOK = N_CTX_SEQ * L_CTX
N_TOK = N_CTX_TOK + N_LAT_SEQ * L_LAT
PAST_LEN = 512
EPS = 1e-6
NEG_BIG = -1e30

HY_WIDTH = 512
HY_BANDS = 16
HY_FILTER_HIDDEN = 64
HY_FEAT_PAD = 64

ATT_HEADS, ATT_KV_HEADS, HEAD_DIM = 8, 2, 64
ATT_GROUP = ATT_HEADS // ATT_KV_HEADS
ATT_WIDTH = ATT_HEADS * HEAD_DIM
KV_WIDTH = ATT_KV_HEADS * HEAD_DIM
ATT_SCALE = HEAD_DIM ** -0.5
WINDOW = 128
ATT_BLOCK = 128
ROPE_THETA = 10000.0
ROPE_FREQS = HEAD_DIM // 4
GRID_W = 64
EVEN_IN = 3 * HY_WIDTH + ATT_WIDTH + 2 * KV_WIDTH

RET_HEADS = 4
RET_DK = 256
RET_DV = 512
RET_CHUNK = 128
RET_QK_WIDTH = RET_HEADS * RET_DK
RET_V_WIDTH = RET_HEADS * RET_DV
ODD_IN = 2 * RET_QK_WIDTH + 3 * RET_V_WIDTH

N_EXPERTS = 32
TOP_K = 4
D_FF = 1024
SWIGLU_ALPHA = 1.702
SWIGLU_LIMIT = 7.0

SUBLANES = 8
LANES = 128

MOE_TILE = 256
MOE_GROUP = 16
ROW_TILE = 512
MOE_GROUPED_ROWS = 2560
MOE_NUM_TILES = 192


def _vmem(mib):
    return pltpu.CompilerParams(vmem_limit_bytes=mib * 1024 * 1024)


def _cond_of_tile(i, tm):
    row = i * tm
    return jnp.where(row < N_CTX_TOK, 0, 1 + (row - N_CTX_TOK) // L_LAT)


def _sigmoid(x):
    return 1.0 / (1.0 + jnp.exp(-x))


def _norm_mod(x, gain, shift, scale):
    ms = jnp.mean(x * x, axis=-1, keepdims=True)
    return (x * lax.rsqrt(ms + EPS) * gain) * (1.0 + scale) + shift


def _adaln_kernel(c_ref, w_ref, b_ref, o_ref):
    c = c_ref[...]
    s = c * _sigmoid(c)
    o_ref[...] = jnp.dot(s, w_ref[...], preferred_element_type=F32, precision=HIGHEST) + b_ref[...]


def _adaln(cond, w, b):
    n = w.shape[1]
    tn = 1024
    out = pl.pallas_call(
        _adaln_kernel,
        grid=(n // tn,),
        in_specs=[
            pl.BlockSpec((SUBLANES, D_MODEL), lambda j: (0, 0)),
            pl.BlockSpec((D_MODEL, tn), lambda j: (0, j)),
            pl.BlockSpec((1, tn), lambda j: (0, j)),
        ],
        out_specs=pl.BlockSpec((SUBLANES, tn), lambda j: (0, j)),
        out_shape=jax.ShapeDtypeStruct((SUBLANES, n), F32),
        name="adaln",
    )(cond, w, b.reshape(1, n))
    return out.reshape(SUBLANES, 6, D_MODEL)


def _in_proj_kernel(x_ref, gain_ref, mod_ref, w_ref, o_ref, wb_ref):
    @pl.when(pl.program_id(1) == 0)
    def _():
        wb_ref[...] = w_ref[...].astype(BF16)

    h = _norm_mod(x_ref[...], gain_ref[...], mod_ref[0, 0:1, :], mod_ref[0, 1:2, :])
    o_ref[...] = jnp.dot(h.astype(BF16), wb_ref[...], preferred_element_type=F32).astype(o_ref.dtype)


def _in_proj(x, gain, mod, w, tn, out_dtype):
    n = w.shape[1]
    tm = ROW_TILE
    return pl.pallas_call(
        _in_proj_kernel,
        grid=(n // tn, N_TOK // tm),
        in_specs=[
            pl.BlockSpec((tm, D_MODEL), lambda j, i: (i, 0)),
            pl.BlockSpec((1, D_MODEL), lambda j, i: (0, 0)),
            pl.BlockSpec((1, 6, D_MODEL), lambda j, i: (_cond_of_tile(i, tm), 0, 0)),
            pl.BlockSpec((D_MODEL, tn), lambda j, i: (0, j)),
        ],
        out_specs=pl.BlockSpec((tm, tn), lambda j, i: (i, j)),
        out_shape=jax.ShapeDtypeStruct((N_TOK, n), out_dtype),
        scratch_shapes=[pltpu.VMEM((D_MODEL, tn), BF16)],
        compiler_params=_vmem(48),
        name="in_proj",
    )(x, gain.reshape(1, D_MODEL), mod, w)


def _dft_matrices(length):
    n = 2 * length
    lo = 16
    s = jnp.arange(length, dtype=jnp.int32)
    k1 = jnp.arange(length // lo, dtype=jnp.int32) * lo
    k0 = jnp.arange(lo, dtype=jnp.int32)
    ang1 = (2.0 * math.pi / n) * ((k1[:, None] * s[None, :]) % n).astype(F32)
    ang0 = (2.0 * math.pi / n) * ((k0[:, None] * s[None, :]) % n).astype(F32)
    c1, s1 = jnp.cos(ang1)[:, None, :], jnp.sin(ang1)[:, None, :]
    c0, s0 = jnp.cos(ang0)[None, :, :], jnp.sin(ang0)[None, :, :]
    cmat = (c1 * c0 - s1 * s0).reshape(length, length)
    smat = (s1 * c0 + c1 * s0).reshape(length, length)
    sign = jnp.where(s % 2 == 0, 1.0, -1.0).astype(F32)
    row = lax.broadcasted_iota(jnp.int32, (length, length), 0)
    col = lax.broadcasted_iota(jnp.int32, (length, length), 1)
    s_nyq = jnp.where(row == 0, sign[None, :], smat)
    st_nyq = jnp.where(col == 0, sign[:, None], smat)
    return cmat.astype(BF16), s_nyq.astype(BF16), st_nyq.astype(BF16)


def _filter_features(length):
    t = jnp.linspace(0.0, 1.0, length, dtype=F32)[:, None]
    w = 2.0 * math.pi * jnp.arange(length, dtype=F32)[:, None] / length
    f = jnp.linspace(1e-4, HY_BANDS - 1, HY_BANDS, dtype=F32)[None, :]
    z = jnp.concatenate([t, jnp.cos(f * w), -jnp.sin(f * w)], axis=-1)
    return jnp.pad(z, ((0, 0), (0, HY_FEAT_PAD - z.shape[1])))


def _filter_kernel(z_ref, w1_ref, b1_ref, fr_ref, w2_ref, b2_ref, w3_ref, dl_ref, c_ref, s_ref,
                   tc_ref, ts_ref, taps_ref, *, length, rb):
    r = pl.program_id(0)

    @pl.when(r == 0)
    def _():
        z = z_ref[...]
        fr = fr_ref[...]
        h = jnp.sin(fr * (jnp.dot(z, w1_ref[...], preferred_element_type=F32, precision=HIGHEST) + b1_ref[...]))
        h = jnp.sin(fr * (jnp.dot(h, w2_ref[...], preferred_element_type=F32, precision=HIGHEST) + b2_ref[...]))
        h = jnp.dot(h, w3_ref[...], preferred_element_type=F32, precision=HIGHEST)
        win = jnp.exp(-z[:, 0:1] * jnp.abs(dl_ref[...]))
        hf = h[:, :HY_WIDTH] * win
        hb = h[:, HY_WIDTH:] * win
        row = lax.broadcasted_iota(jnp.int32, (length, HY_WIDTH), 0)
        hb = jnp.where(row == 0, 0.0, hb)
        l1 = jnp.sum(jnp.abs(hf), axis=0, keepdims=True) + jnp.sum(jnp.abs(hb), axis=0, keepdims=True)
        inv = 1.0 / l1
        taps_ref[:, :HY_WIDTH] = (hf * inv).astype(BF16)
        taps_ref[:, HY_WIDTH:] = (hb * inv).astype(BF16)

    taps = taps_ref[...]
    rc = jnp.dot(c_ref[...], taps, preferred_element_type=F32)
    rs = jnp.dot(s_ref[...], taps, preferred_element_type=F32)
    tc = rc[:, :HY_WIDTH] + rc[:, HY_WIDTH:]
    ts = rs[:, :HY_WIDTH] - rs[:, HY_WIDTH:]
    grow = r * rb + lax.broadcasted_iota(jnp.int32, (rb, HY_WIDTH), 0)
    is0 = grow == 0
    ts = jnp.where(is0, rs[:, :HY_WIDTH] + rs[:, HY_WIDTH:], ts)
    wgt = jnp.where(is0, 1.0 / (2 * length), 2.0 / (2 * length))
    tc_ref[...] = tc * wgt
    ts_ref[...] = ts * wgt


def _hyena_filter(length, feats, cmat, smat, w1, b1, freq, w2, b2, w3, deltas):
    rb = min(length, 512)
    w1p = jnp.pad(w1, ((0, HY_FEAT_PAD - w1.shape[0]), (0, 0)))
    full = lambda shape: pl.BlockSpec(shape, lambda r: (0,) * len(shape))
    hid = HY_FILTER_HIDDEN
    return pl.pallas_call(
        functools.partial(_filter_kernel, length=length, rb=rb),
        grid=(length // rb,),
        in_specs=[
            full((length, HY_FEAT_PAD)), full((HY_FEAT_PAD, hid)), full((1, hid)), full((1, hid)),
            full((hid, hid)), full((1, hid)), full((hid, 2 * HY_WIDTH)), full((1, HY_WIDTH)),
            pl.BlockSpec((rb, length), lambda r: (r, 0)),
            pl.BlockSpec((rb, length), lambda r: (r, 0)),
        ],
        out_specs=[pl.BlockSpec((rb, HY_WIDTH), lambda r: (r, 0)),
                   pl.BlockSpec((rb, HY_WIDTH), lambda r: (r, 0))],
        out_shape=[jax.ShapeDtypeStruct((length, HY_WIDTH), F32)] * 2,
        scratch_shapes=[pltpu.VMEM((length, 2 * HY_WIDTH), BF16)],
        compiler_params=_vmem(48),
        name=f"hyena_filter_{length}",
    )(feats, w1p, b1.reshape(1, hid), freq.reshape(1, hid), w2, b2.reshape(1, hid), w3,
      deltas.reshape(1, HY_WIDTH), cmat, smat)


def _hyena_kernel(x0_ref, x1_ref, v_ref, w0_ref, w1_ref, w2_ref, b0_ref, b1_ref, b2_ref, skip_ref,
                  tc_ref, ts_ref, c_ref, s_ref, ct_ref, st_ref, o_ref,
                  zb_ref, zs_ref, x0c_ref, acc_ref, *, nseq, length, cb, fb):
    f = pl.program_id(1)
    nf = pl.num_programs(1)

    def short_conv(u, w_ref, b_ref):
        row = lax.broadcasted_iota(jnp.int32, u.shape, 0)
        prev = jnp.where(row == 0, 0.0, pltpu.roll(u, 1, 0))
        nxt = jnp.where(row == length - 1, 0.0, pltpu.roll(u, length - 1, 0))
        return prev * w_ref[0:1, :] + u * w_ref[1:2, :] + nxt * w_ref[2:3, :] + b_ref[...]

    @pl.when(f == 0)
    def _():
        for b in range(nseq):
            cols = slice(b * cb, (b + 1) * cb)
            x0c_ref[:, cols] = short_conv(x0_ref[b], w0_ref, b0_ref)
            z = short_conv(x1_ref[b], w1_ref, b1_ref) * short_conv(v_ref[b], w2_ref, b2_ref)
            zb_ref[:, cols] = z.astype(BF16)
            zs_ref[:, cols] = z * skip_ref[...]
        acc_ref[...] = jnp.zeros_like(acc_ref)

    zb = zb_ref[...]
    zc = jnp.dot(c_ref[...], zb, preferred_element_type=F32)
    zsn = jnp.dot(s_ref[...], zb, preferred_element_type=F32)
    tc = jnp.concatenate([tc_ref[...]] * nseq, axis=1)
    ts = jnp.concatenate([ts_ref[...]] * nseq, axis=1)
    grow = f * fb + lax.broadcasted_iota(jnp.int32, zc.shape, 0)
    is0 = grow == 0
    yc = jnp.where(is0, zc * tc, zc * tc - zsn * ts)
    ys = jnp.where(is0, zsn * ts, zc * ts + zsn * tc)
    acc_ref[...] += (jnp.dot(ct_ref[...], yc.astype(BF16), preferred_element_type=F32)
                     + jnp.dot(st_ref[...], ys.astype(BF16), preferred_element_type=F32))

    @pl.when(f == nf - 1)
    def _():
        for b in range(nseq):
            cols = slice(b * cb, (b + 1) * cb)
            o_ref[b] = (x0c_ref[:, cols] * (acc_ref[:, cols] + zs_ref[:, cols])).astype(o_ref.dtype)


def _hyena(u, first_seq_block, nseq, length, conv_w, conv_b, skip, tc, ts, cmat, smat, stmat):
    cb = 128
    fb = min(length, 512)
    u3 = u.reshape(N_TOK // length, length, EVEN_IN)
    ncb = HY_WIDTH // cb
    width = nseq * cb

    def ublock(part):
        return pl.BlockSpec((nseq, length, cb), lambda c, f: (first_seq_block, 0, part * ncb + c))

    def wblock(part, rows):
        return pl.BlockSpec((rows, cb), lambda c, f: (0, part * ncb + c))

    return pl.pallas_call(
        functools.partial(_hyena_kernel, nseq=nseq, length=length, cb=cb, fb=fb),
        grid=(ncb, length // fb),
        in_specs=[
            ublock(0), ublock(1), ublock(2),
            wblock(0, 3), wblock(1, 3), wblock(2, 3),
            wblock(0, 1), wblock(1, 1), wblock(2, 1),
            pl.BlockSpec((1, cb), lambda c, f: (0, c)),
            pl.BlockSpec((fb, cb), lambda c, f: (f, c)),
            pl.BlockSpec((fb, cb), lambda c, f: (f, c)),
            pl.BlockSpec((fb, length), lambda c, f: (f, 0)),
            pl.BlockSpec((fb, length), lambda c, f: (f, 0)),
            pl.BlockSpec((length, fb), lambda c, f: (0, f)),
            pl.BlockSpec((length, fb), lambda c, f: (0, f)),
        ],
        out_specs=pl.BlockSpec((nseq, length, cb), lambda c, f: (0, 0, c)),
        out_shape=jax.ShapeDtypeStruct((nseq, length, HY_WIDTH), BF16),
        scratch_shapes=[pltpu.VMEM((length, width), BF16), pltpu.VMEM((length, width), F32),
                        pltpu.VMEM((length, width), F32), pltpu.VMEM((length, width), F32)],
        compiler_params=_vmem(48),
        name=f"hyena_{length}",
    )(u3, u3, u3, conv_w, conv_w, conv_w, conv_b.reshape(1, -1), conv_b.reshape(1, -1),
      conv_b.reshape(1, -1), skip.reshape(1, HY_WIDTH), tc, ts, cmat, smat, cmat, stmat
      ).reshape(nseq * length, HY_WIDTH)


def _head_rms(x, seg, gain):
    x2 = x * x
    hi = x2.astype(BF16)
    lo = (x2 - hi.astype(F32)).astype(BF16)
    ss = jnp.dot(hi, seg, preferred_element_type=F32) + jnp.dot(lo, seg, preferred_element_type=F32)
    return x * lax.rsqrt(ss * (1.0 / HEAD_DIM) + EPS) * gain


def _rope(x, cos, sin_signed):
    width = x.shape[1]
    lane = lax.broadcasted_iota(jnp.int32, x.shape, 1)
    first = (lane // ROPE_FREQS) % 2 == 0
    partner = jnp.where(first, pltpu.roll(x, width - ROPE_FREQS, 1), pltpu.roll(x, ROPE_FREQS, 1))
    return x * cos + partner * sin_signed


def _stack_heads(x, g):
    return jnp.concatenate(
        [x[:, (g * ATT_GROUP + j) * HEAD_DIM:(g * ATT_GROUP + j + 1) * HEAD_DIM] for j in range(ATT_GROUP)], axis=0)


def _sink_column(sink_ref, g, rows):
    return jnp.concatenate([jnp.full((rows, 1), sink_ref[g * ATT_GROUP + j], F32) for j in range(ATT_GROUP)], axis=0)


def _unstack_heads(outs, rows):
    return jnp.concatenate([outs[g][j * rows:(j + 1) * rows, :]
                            for g in range(ATT_KV_HEADS) for j in range(ATT_GROUP)], axis=1)


_DOT_NT = (((1,), (1,)), ((), ()))


def _ctx_attn_kernel(sink_ref, q_ref, k_ref, v_ref, seg_ref, qg_ref, kg_ref, o_ref, ko_ref, vo_ref):
    seg = seg_ref[...]
    qn = _head_rms(q_ref[...], seg, qg_ref[...]) * ATT_SCALE
    kn = _head_rms(k_ref[...], seg[:KV_WIDTH, :KV_WIDTH], kg_ref[...])
    v = v_ref[...]
    ko_ref[...] = kn
    vo_ref[...] = v
    outs = []
    for g in range(ATT_KV_HEADS):
        cols = slice(g * HEAD_DIM, (g + 1) * HEAD_DIM)
        q = _stack_heads(qn, g).astype(BF16)
        s = lax.dot_general(q, kn[:, cols].astype(BF16), _DOT_NT, preferred_element_type=F32)
        sink = _sink_column(sink_ref, g, L_CTX)
        m = jnp.maximum(jnp.max(s, axis=-1, keepdims=True), sink)
        p = jnp.exp(s - m)
        den = jnp.sum(p, axis=-1, keepdims=True) + jnp.exp(sink - m)
        o = jnp.dot(p.astype(BF16), v[:, cols].astype(BF16), preferred_element_type=F32)
        outs.append(o / den)
    o_ref[...] = _unstack_heads(outs, L_CTX).astype(o_ref.dtype)


def _ctx_attention(u, seg, q_gain, k_gain, sink):
    qcol = 3 * HY_WIDTH // ATT_WIDTH
    kcol = (3 * HY_WIDTH + ATT_WIDTH) // KV_WIDTH
    return pl.pallas_call(
        _ctx_attn_kernel,
        grid_spec=pltpu.PrefetchScalarGridSpec(
            num_scalar_prefetch=1,
            grid=(N_CTX_SEQ,),
            in_specs=[
                pl.BlockSpec((L_CTX, ATT_WIDTH), lambda b, s: (b, qcol)),
                pl.BlockSpec((L_CTX, KV_WIDTH), lambda b, s: (b, kcol)),
                pl.BlockSpec((L_CTX, KV_WIDTH), lambda b, s: (b, kcol + 1)),
                pl.BlockSpec((ATT_WIDTH, ATT_WIDTH), lambda b, s: (0, 0)),
                pl.BlockSpec((1, ATT_WIDTH), lambda b, s: (0, 0)),
                pl.BlockSpec((1, KV_WIDTH), lambda b, s: (0, 0)),
            ],
            out_specs=[
                pl.BlockSpec((L_CTX, ATT_WIDTH), lambda b, s: (b, 0)),
                pl.BlockSpec((L_CTX, KV_WIDTH), lambda b, s: (b, 0)),
                pl.BlockSpec((L_CTX, KV_WIDTH), lambda b, s: (b, 0)),
            ],
        ),
        out_shape=[jax.ShapeDtypeStruct((N_CTX_TOK, ATT_WIDTH), BF16),
                   jax.ShapeDtypeStruct((N_CTX_TOK, KV_WIDTH), F32),
                   jax.ShapeDtypeStruct((N_CTX_TOK, KV_WIDTH), F32)],
        name="ctx_attention",
    )(sink, u, u, u, seg, jnp.tile(q_gain, ATT_HEADS).reshape(1, ATT_WIDTH),
      jnp.tile(k_gain, ATT_KV_HEADS).reshape(1, KV_WIDTH))


def _rope_tables():
    pos = jnp.arange(L_LAT, dtype=jnp.int32)
    row = (pos // GRID_W).astype(F32)
    col = (pos % GRID_W).astype(F32)
    inv = ROPE_THETA ** (-jnp.arange(ROPE_FREQS, dtype=F32) / ROPE_FREQS)
    ar, ac = row[:, None] * inv, col[:, None] * inv
    cos = jnp.concatenate([jnp.cos(ar), jnp.cos(ar), jnp.cos(ac), jnp.cos(ac)], axis=-1)
    sin = jnp.concatenate([-jnp.sin(ar), jnp.sin(ar), -jnp.sin(ac), jnp.sin(ac)], axis=-1)
    return jnp.tile(cos, (1, ATT_HEADS)), jnp.tile(sin, (1, ATT_HEADS))


def _lat_attn_kernel(sink_ref, q_ref, k_ref, v_ref, ck_ref, cv_ref, cosq_ref, sinq_ref, cosk_ref, sink_k_ref,
                     seg_ref, qg_ref, kg_ref, o_ref, kn_ref):
    n = pl.program_id(1)
    seg = seg_ref[...]

    @pl.when(n == 0)
    def _():
        kn = _head_rms(k_ref[...], seg[:KV_WIDTH, :KV_WIDTH], kg_ref[...])
        kn_ref[...] = _rope(kn, cosk_ref[...], sink_k_ref[...]).astype(BF16)

    qn = _head_rms(q_ref[...], seg, qg_ref[...])
    qn = _rope(qn, cosq_ref[...], sinq_ref[...]) * ATT_SCALE
    span = 3 * ATT_BLOCK
    start = pl.multiple_of(jnp.clip((n - 1) * ATT_BLOCK, 0, L_LAT - span), ATT_BLOCK)
    rows = ATT_GROUP * ATT_BLOCK
    q_pos = n * ATT_BLOCK + (lax.broadcasted_iota(jnp.int32, (rows, span), 0) % ATT_BLOCK)
    k_pos = start + lax.broadcasted_iota(jnp.int32, (rows, span), 1)
    valid = jnp.abs(q_pos - k_pos) <= WINDOW
    k_loc = kn_ref[pl.ds(start, span), :]
    v_loc = v_ref[pl.ds(start, span), :].astype(BF16)
    k_ctx = ck_ref[0].astype(BF16)
    v_ctx = cv_ref[0].astype(BF16)
    outs = []
    for g in range(ATT_KV_HEADS):
        cols = slice(g * HEAD_DIM, (g + 1) * HEAD_DIM)
        q = _stack_heads(qn, g).astype(BF16)
        s_loc = lax.dot_general(q, k_loc[:, cols], _DOT_NT, preferred_element_type=F32)
        s_loc = jnp.where(valid, s_loc, NEG_BIG)
        s_ctx = lax.dot_general(q, k_ctx[:, cols], _DOT_NT, preferred_element_type=F32)
        sink = _sink_column(sink_ref, g, ATT_BLOCK)
        m = jnp.maximum(jnp.maximum(jnp.max(s_loc, axis=-1, keepdims=True),
                                    jnp.max(s_ctx, axis=-1, keepdims=True)), sink)
        p_loc = jnp.exp(s_loc - m)
        p_ctx = jnp.exp(s_ctx - m)
        den = (jnp.sum(p_loc, axis=-1, keepdims=True) + jnp.sum(p_ctx, axis=-1, keepdims=True)
               + jnp.exp(sink - m))
        o = (jnp.dot(p_loc.astype(BF16), v_loc[:, cols], preferred_element_type=F32)
             + jnp.dot(p_ctx.astype(BF16), v_ctx[:, cols], preferred_element_type=F32))
        outs.append(o / den)
    o_ref[...] = _unstack_heads(outs, ATT_BLOCK).astype(o_ref.dtype)


def _lat_attention(u, cache_k, cache_v, seg, q_gain, k_gain, sink):
    qcol = 3 * HY_WIDTH // ATT_WIDTH
    kcol = (3 * HY_WIDTH + ATT_WIDTH) // KV_WIDTH
    nblk = L_LAT // ATT_BLOCK
    first_q_block = N_CTX_TOK // ATT_BLOCK
    first_seq = N_CTX_TOK // L_LAT
    cos, sin = _rope_tables()
    return pl.pallas_call(
        _lat_attn_kernel,
        grid_spec=pltpu.PrefetchScalarGridSpec(
            num_scalar_prefetch=1,
            grid=(N_LAT_SEQ, nblk),
            in_specs=[
                pl.BlockSpec((ATT_BLOCK, ATT_WIDTH), lambda b, n, s: (first_q_block + b * nblk + n, qcol)),
                pl.BlockSpec((L_LAT, KV_WIDTH), lambda b, n, s: (first_seq + b, kcol)),
                pl.BlockSpec((L_LAT, KV_WIDTH), lambda b, n, s: (first_seq + b, kcol + 1)),
                pl.BlockSpec((1, PAST_LEN, KV_WIDTH), lambda b, n, s: (b, 0, 0)),
                pl.BlockSpec((1, PAST_LEN, KV_WIDTH), lambda b, n, s: (b, 0, 0)),
                pl.BlockSpec((ATT_BLOCK, ATT_WIDTH), lambda b, n, s: (n, 0)),
                pl.BlockSpec((ATT_BLOCK, ATT_WIDTH), lambda b, n, s: (n, 0)),
                pl.BlockSpec((L_LAT, KV_WIDTH), lambda b, n, s: (0, 0)),
                pl.BlockSpec((L_LAT, KV_WIDTH), lambda b, n, s: (0, 0)),
                pl.BlockSpec((ATT_WIDTH, ATT_WIDTH), lambda b, n, s: (0, 0)),
                pl.BlockSpec((1, ATT_WIDTH), lambda b, n, s: (0, 0)),
                pl.BlockSpec((1, KV_WIDTH), lambda b, n, s: (0, 0)),
            ],
            out_specs=pl.BlockSpec((ATT_BLOCK, ATT_WIDTH), lambda b, n, s: (b * nblk + n, 0)),
            scratch_shapes=[pltpu.VMEM((L_LAT, KV_WIDTH), BF16)],
        ),
        out_shape=jax.ShapeDtypeStruct((N_LAT_SEQ * L_LAT, ATT_WIDTH), BF16),
        name="lat_attention",
    )(sink, u, u, u, cache_k.reshape(N_LAT_SEQ, PAST_LEN, KV_WIDTH), cache_v.reshape(N_LAT_SEQ, PAST_LEN, KV_WIDTH),
      cos, sin, cos[:, :KV_WIDTH], sin[:, :KV_WIDTH], seg,
      jnp.tile(q_gain, ATT_HEADS).reshape(1, ATT_WIDTH), jnp.tile(k_gain, ATT_KV_HEADS).reshape(1, KV_WIDTH))


def _ceil_to(v, m):
    return ((v + (m - 1)) // m) * m


def _post_mixer(x, mix, refs, tm):
    (w_ref, gain_ref, mod_ref, rwh_ref, rwm_ref, rb_ref, tri_ref, low_ref,
     x1_ref, sw_ref, cnt_ref, xg_ref, wb_ref) = refs

    @pl.when(pl.program_id(0) == 0)
    def _():
        wb_ref[...] = w_ref[...].astype(BF16)

    y = jnp.dot(mix, wb_ref[...], preferred_element_type=F32)
    x1 = x + mod_ref[0, 2:3, :] * y
    x1_ref[...] = x1
    xt = _norm_mod(x1, gain_ref[...], mod_ref[0, 3:4, :], mod_ref[0, 4:5, :])
    xh = xt.astype(BF16)
    xm = (xt - xh.astype(F32)).astype(BF16)
    logits = (lax.dot_general(rwh_ref[...], xh, _DOT_NT, preferred_element_type=F32)
              + lax.dot_general(rwm_ref[...], xh, _DOT_NT, preferred_element_type=F32)
              + lax.dot_general(rwh_ref[...], xm, _DOT_NT, preferred_element_type=F32)) + rb_ref[...]
    expert = lax.broadcasted_iota(jnp.int32, logits.shape, 0)
    vals, hits = [], []
    for _ in range(TOP_K):
        m = jnp.max(logits, axis=0, keepdims=True)
        sel = jnp.min(jnp.where(logits == m, expert, N_EXPERTS), axis=0, keepdims=True)
        vals.append(m)
        hits.append(expert == sel)
        logits = jnp.where(expert == sel, -jnp.inf, logits)
    es = [jnp.exp(v - vals[0]) for v in vals]
    den = es[0] + es[1] + es[2] + es[3]
    weights = [e / den for e in es]

    routed = sum(h.astype(F32) for h in hits)
    counts = jnp.sum(routed, axis=1, keepdims=True)
    cnt_ref[0] = counts.astype(jnp.int32)
    padded = _ceil_to(counts.astype(jnp.int32), MOE_GROUP).astype(F32)
    g0 = jnp.dot(low_ref[...], jnp.broadcast_to(padded, (N_EXPERTS, LANES)),
                 preferred_element_type=F32, precision=HIGHEST)[:, 0:1]
    earlier = jnp.dot(routed.astype(BF16), tri_ref[...], preferred_element_type=F32)
    row_of = g0 + earlier
    slots = [jnp.sum(jnp.where(h, row_of, 0.0), axis=0, keepdims=True) for h in hits]

    prow = lax.broadcasted_iota(jnp.int32, (LANES, tm), 0)
    packed = jnp.zeros((LANES, tm), F32)
    for k, vec in enumerate(slots + weights):
        packed = jnp.where(prow == k, vec, packed)
    sw_ref[...] = packed.T

    group_row = lax.broadcasted_iota(jnp.int32, (MOE_GROUPED_ROWS, tm), 0).astype(F32)
    pick = group_row == slots[0]
    for k in range(1, TOP_K):
        pick = jnp.logical_or(pick, group_row == slots[k])
    perm = jnp.where(pick, 1.0, 0.0).astype(BF16)
    xg_ref[...] = jnp.dot(perm, xh, preferred_element_type=F32).astype(BF16)


def _post_even_kernel(x_ref, ac_ref, al_ref, tc_ref, tl_ref, *refs, tm):
    is_ctx = pl.program_id(0) * tm < N_CTX_TOK
    mix = jnp.concatenate([jnp.where(is_ctx, ac_ref[...], al_ref[...]),
                           jnp.where(is_ctx, tc_ref[...], tl_ref[...])], axis=1)
    _post_mixer(x_ref[...], mix, refs, tm)


def _post_odd_kernel(x_ref, yc_ref, yl_ref, *refs, tm):
    mix = jnp.where(pl.program_id(0) * tm < N_CTX_TOK, yc_ref[...], yl_ref[...])
    _post_mixer(x_ref[...], mix, refs, tm)


def _post_call(kernel_fn, name, mixer_specs, mixer_args, k_in, w_out, gain, mod, router_w, router_b):
    tm = ROW_TILE
    nsteps = N_TOK // tm
    row = lax.broadcasted_iota(jnp.int32, (tm, tm), 0)
    col = lax.broadcasted_iota(jnp.int32, (tm, tm), 1)
    tri = (row < col).astype(BF16)
    er = lax.broadcasted_iota(jnp.int32, (N_EXPERTS, N_EXPERTS), 0)
    ec = lax.broadcasted_iota(jnp.int32, (N_EXPERTS, N_EXPERTS), 1)
    low = (ec < er).astype(F32)
    rw_t = router_w.T
    rw_hi = rw_t.astype(BF16)
    rw_mid = (rw_t - rw_hi.astype(F32)).astype(BF16)
    const = lambda shape: pl.BlockSpec(shape, lambda i: (0,) * len(shape))
    return pl.pallas_call(
        functools.partial(kernel_fn, tm=tm),
        grid=(nsteps,),
        in_specs=[pl.BlockSpec((tm, D_MODEL), lambda i: (i, 0))] + mixer_specs + [
            const((k_in, D_MODEL)),
            const((1, D_MODEL)),
            pl.BlockSpec((1, 6, D_MODEL), lambda i: (_cond_of_tile(i, tm), 0, 0)),
            const((N_EXPERTS, D_MODEL)), const((N_EXPERTS, D_MODEL)), const((N_EXPERTS, 1)),
            const((tm, tm)), const((N_EXPERTS, N_EXPERTS)),
        ],
        out_specs=[
            pl.BlockSpec((tm, D_MODEL), lambda i: (i, 0)),
            pl.BlockSpec((tm, LANES), lambda i: (i, 0)),
            pl.BlockSpec((1, N_EXPERTS, 1), lambda i: (i, 0, 0)),
            pl.BlockSpec((MOE_GROUPED_ROWS, D_MODEL), lambda i: (i, 0)),
        ],
        out_shape=[
            jax.ShapeDtypeStruct((N_TOK, D_MODEL), F32),
            jax.ShapeDtypeStruct((N_TOK, LANES), F32),
            jax.ShapeDtypeStruct((nsteps, N_EXPERTS, 1), jnp.int32),
            jax.ShapeDtypeStruct((nsteps * MOE_GROUPED_ROWS, D_MODEL), BF16),
        ],
        scratch_shapes=[pltpu.VMEM((k_in, D_MODEL), BF16)],
        compiler_params=_vmem(56),
        name=name,
    )(*mixer_args, w_out, gain.reshape(1, D_MODEL), mod, rw_hi, rw_mid, router_b.reshape(N_EXPERTS, 1), tri, low)


def _post_even(x, a_ctx, a_lat, t_ctx, t_lat, w_out, gain, mod, router_w, router_b):
    tm = ROW_TILE
    nctx = N_CTX_TOK // tm
    ctx_map = lambda i: (jnp.minimum(i, nctx - 1), 0)
    lat_map = lambda i: (jnp.maximum(i - nctx, 0), 0)
    specs = [pl.BlockSpec((tm, HY_WIDTH), ctx_map), pl.BlockSpec((tm, HY_WIDTH), lat_map),
             pl.BlockSpec((tm, ATT_WIDTH), ctx_map), pl.BlockSpec((tm, ATT_WIDTH), lat_map)]
    return _post_call(_post_even_kernel, "post_even", specs, (x, a_ctx, a_lat, t_ctx, t_lat),
                      HY_WIDTH + ATT_WIDTH, w_out, gain, mod, router_w, router_b)


def _post_odd(x, y_ctx, y_lat, w_out, gain, mod, router_w, router_b):
    tm = ROW_TILE
    nctx = N_CTX_TOK // tm
    specs = [pl.BlockSpec((tm, RET_V_WIDTH), lambda i: (jnp.minimum(i, nctx - 1), 0)),
             pl.BlockSpec((tm, RET_V_WIDTH), lambda i: (jnp.maximum(i - nctx, 0), 0))]
    return _post_call(_post_odd_kernel, "post_odd", specs, (x, y_ctx, y_lat),
                      RET_V_WIDTH, w_out, gain, mod, router_w, router_b)


def _moe_plan(counts):
    t, grp, big = MOE_TILE, MOE_GROUP, MOE_GROUPED_ROWS
    n16 = _ceil_to(counts.reshape(-1, N_EXPERTS), grp)
    nsteps = n16.shape[0]
    g0 = jnp.cumsum(n16, axis=1) - n16
    e0 = jnp.cumsum(n16, axis=0) - n16
    rows_e = jnp.sum(n16, axis=0)
    ntiles = (rows_e + t - 1) // t
    tile_end = jnp.cumsum(ntiles)
    tile_first = tile_end - ntiles
    total = tile_end[-1]
    experts = jnp.arange(N_EXPERTS, dtype=jnp.int32)
    tiles = jnp.arange(MOE_NUM_TILES, dtype=jnp.int32)
    valid = tiles < total
    ti = jnp.minimum(tiles, total - 1)
    e_of = jnp.sum((tile_end[None, :] <= ti[:, None]).astype(jnp.int32), axis=1)
    onehot = (e_of[:, None] == experts[None, :]).astype(jnp.int32)
    pick = lambda v: jnp.sum(onehot * v[None, :], axis=1)
    pick2 = lambda m: jnp.sum(onehot[:, None, :] * m[None, :, :], axis=2)
    r = ((ti - pick(tile_first)) * t)[:, None] + grp * jnp.arange(t // grp, dtype=jnp.int32)[None, :]
    ends = pick2(e0 + n16)
    step = jnp.sum((ends[:, None, :] <= r[:, :, None]).astype(jnp.int32), axis=2)
    step = jnp.minimum(step, nsteps - 1)
    sel = (step[:, :, None] == jnp.arange(nsteps, dtype=jnp.int32)[None, None, :]).astype(jnp.int32)
    at_step = lambda m: jnp.sum(sel * pick2(m)[:, None, :], axis=2)
    src = step * big + at_step(g0) + r - at_step(e0)
    live = jnp.logical_and(valid[:, None], r < pick(rows_e)[:, None])
    moe_src = jnp.where(live, src, 0).reshape(-1)
    g = grp * jnp.arange(big // grp, dtype=jnp.int32)
    gend = g0 + n16
    ce = jnp.sum((gend[:, None, :] <= g[None, :, None]).astype(jnp.int32), axis=2)
    used = ce < N_EXPERTS
    ce = jnp.minimum(ce, N_EXPERTS - 1)
    csel = (ce[:, :, None] == experts[None, None, :]).astype(jnp.int32)
    of_e = lambda m: jnp.sum(csel * m[:, None, :], axis=2)
    base = jnp.sum(csel * (tile_first * t)[None, None, :], axis=2)
    csrc = base + of_e(e0) + g[None, :] - of_e(g0)
    comb_src = jnp.where(used, csrc, 0).reshape(-1)
    used_e = ntiles > 0
    rank = jnp.cumsum(used_e.astype(jnp.int32)) - 1
    later = jnp.logical_and(used_e[None, :], experts[None, :] > experts[:, None])
    next_e = jnp.min(jnp.where(later, experts[None, :], N_EXPERTS), axis=1)
    next_e = jnp.where(next_e == N_EXPERTS, -1, next_e)
    wbuf = jnp.stack([pick(rank) % 2, pick(next_e)], axis=1).reshape(-1)
    as_i32 = lambda v: v.astype(jnp.int32)
    return as_i32(e_of), as_i32(valid), as_i32(wbuf), as_i32(moe_src), as_i32(comb_src)


def _moe_kernel(te_ref, tv_ref, nx_ref, src_ref, xg_hbm, w1_hbm, b1_ref, w2_hbm, b2_ref, y_ref,
                xbuf, w1f, w2f, w1b, w2b, sem_in, sem_w):
    i = pl.program_id(0)
    nt = pl.num_programs(0)
    t, grp = MOE_TILE, MOE_GROUP
    slot = i % 2

    def issue_gather(tile, sl):
        for c in range(t // grp):
            src = pl.multiple_of(src_ref[tile * (t // grp) + c], grp)
            pltpu.make_async_copy(xg_hbm.at[pl.ds(src, grp), :], xbuf.at[sl, pl.ds(c * grp, grp), :],
                                  sem_in.at[sl]).start()

    def weight_copies(e, ws):
        return (pltpu.make_async_copy(w1_hbm.at[e], w1f.at[ws], sem_w.at[ws]),
                pltpu.make_async_copy(w2_hbm.at[e], w2f.at[ws], sem_w.at[ws]))

    def valid(tile):
        return tv_ref[jnp.clip(tile, 0, nt - 1)] > 0

    @pl.when(i == 0)
    def _():
        issue_gather(0, 0)
        for cp in weight_copies(te_ref[0], 0):
            cp.start()

    @pl.when(valid(i))
    def _():
        pltpu.make_async_copy(xg_hbm.at[pl.ds(0, t), :], xbuf.at[slot], sem_in.at[slot]).wait()

        @pl.when(jnp.logical_and(i + 1 < nt, valid(i + 1)))
        def _():
            issue_gather(i + 1, 1 - slot)

        e = te_ref[i]
        first = jnp.logical_or(i == 0, e != te_ref[jnp.maximum(i - 1, 0)])
        ws = nx_ref[2 * i]
        nxt = nx_ref[2 * i + 1]

        @pl.when(first)
        def _():
            for cp in weight_copies(e, ws):
                cp.wait()
            w1b[...] = w1f[ws].astype(BF16)
            w2b[...] = w2f[ws].astype(BF16)

            @pl.when(nxt >= 0)
            def _():
                for cp in weight_copies(nxt, 1 - ws):
                    cp.start()

        h = jnp.dot(xbuf[slot], w1b[...], preferred_element_type=F32) + b1_ref[0]
        glu = jnp.minimum(h[:, :D_FF], SWIGLU_LIMIT)
        lin = jnp.clip(h[:, D_FF:], -SWIGLU_LIMIT, SWIGLU_LIMIT)
        act = (glu * _sigmoid(SWIGLU_ALPHA * glu) * (lin + 1.0)).astype(BF16)
        y_ref[...] = (jnp.dot(act, w2b[...], preferred_element_type=F32) + b2_ref[0]).astype(y_ref.dtype)

    @pl.when(jnp.logical_not(valid(i)))
    def _():
        y_ref[...] = jnp.zeros_like(y_ref)


def _moe_experts(xg, plan, w1, b1, w2, b2):
    te, tv, nx, src, _ = plan
    t = MOE_TILE
    return pl.pallas_call(
        _moe_kernel,
        grid_spec=pltpu.PrefetchScalarGridSpec(
            num_scalar_prefetch=4,
            grid=(MOE_NUM_TILES,),
            in_specs=[
                pl.BlockSpec(memory_space=pl.ANY),
                pl.BlockSpec(memory_space=pl.ANY),
                pl.BlockSpec((1, 1, 2 * D_FF), lambda i, te, *_: (te[i], 0, 0)),
                pl.BlockSpec(memory_space=pl.ANY),
                pl.BlockSpec((1, 1, D_MODEL), lambda i, te, *_: (te[i], 0, 0)),
            ],
            out_specs=pl.BlockSpec((t, D_MODEL), lambda i, *_: (i, 0)),
            scratch_shapes=[
                pltpu.VMEM((2, t, D_MODEL), BF16),
                pltpu.VMEM((2, D_MODEL, 2 * D_FF), F32),
                pltpu.VMEM((2, D_FF, D_MODEL), F32),
                pltpu.VMEM((D_MODEL, 2 * D_FF), BF16),
                pltpu.VMEM((D_FF, D_MODEL), BF16),
                pltpu.SemaphoreType.DMA((2,)),
                pltpu.SemaphoreType.DMA((2,)),
            ],
        ),
        out_shape=jax.ShapeDtypeStruct((MOE_NUM_TILES * t, D_MODEL), BF16),
        compiler_params=_vmem(56),
        name="moe_experts",
    )(te, tv, nx, src, xg, w1, b1.reshape(N_EXPERTS, 1, 2 * D_FF), w2, b2.reshape(N_EXPERTS, 1, D_MODEL))


def _combine_value(src_ref, x1_ref, sw_ref, mod_ref, ys_hbm, ybuf, sem, tm):
    s = pl.program_id(0)
    nsteps = pl.num_programs(0)
    slot = s % 2
    grp, big = MOE_GROUP, MOE_GROUPED_ROWS
    nchunk = big // grp

    def issue_gather(step, sl):
        for c in range(nchunk):
            src = pl.multiple_of(src_ref[step * nchunk + c], grp)
            pltpu.make_async_copy(ys_hbm.at[pl.ds(src, grp), :], ybuf.at[sl, pl.ds(c * grp, grp), :],
                                  sem.at[sl]).start()

    @pl.when(s == 0)
    def _():
        issue_gather(0, 0)

    pltpu.make_async_copy(ys_hbm.at[pl.ds(0, big), :], ybuf.at[slot], sem.at[slot]).wait()

    @pl.when(s + 1 < nsteps)
    def _():
        issue_gather(s + 1, 1 - slot)

    sw = sw_ref[...]
    col = lax.broadcasted_iota(jnp.int32, (tm, big), 1).astype(F32)
    wmat = jnp.zeros((tm, big), F32)
    for k in range(TOP_K):
        wmat = jnp.where(col == sw[:, k:k + 1], sw[:, TOP_K + k:TOP_K + k + 1], wmat)
    moe = jnp.dot(wmat.astype(BF16), ybuf[slot], preferred_element_type=F32)
    return x1_ref[...] + mod_ref[0, 5:6, :] * moe


def _combine_kernel(src_ref, x1_ref, sw_ref, mod_ref, ys_hbm, o_ref, ybuf, sem, *, tm):
    o_ref[...] = _combine_value(src_ref, x1_ref, sw_ref, mod_ref, ys_hbm, ybuf, sem, tm)


def _combine_split_kernel(src_ref, x1_ref, sw_ref, mod_ref, ys_hbm, oc_ref, ol_ref, ybuf, sem, *, tm):
    val = _combine_value(src_ref, x1_ref, sw_ref, mod_ref, ys_hbm, ybuf, sem, tm)
    is_ctx = pl.program_id(0) * tm < N_CTX_TOK

    @pl.when(is_ctx)
    def _():
        oc_ref[...] = val

    @pl.when(jnp.logical_not(is_ctx))
    def _():
        ol_ref[...] = val


def _combine(x1, ys, sw, plan, mod, split):
    tm = ROW_TILE
    nctx = N_CTX_TOK // tm
    in_specs = [
        pl.BlockSpec((tm, D_MODEL), lambda i, *_: (i, 0)),
        pl.BlockSpec((tm, LANES), lambda i, *_: (i, 0)),
        pl.BlockSpec((1, 6, D_MODEL), lambda i, *_: (_cond_of_tile(i, tm), 0, 0)),
        pl.BlockSpec(memory_space=pl.ANY),
    ]
    scratch = [pltpu.VMEM((2, MOE_GROUPED_ROWS, D_MODEL), BF16), pltpu.SemaphoreType.DMA((2,))]
    if not split:
        kernel_fn, name = _combine_kernel, "moe_combine"
        out_specs = pl.BlockSpec((tm, D_MODEL), lambda i, *_: (i, 0))
        out_shape = jax.ShapeDtypeStruct((N_TOK, D_MODEL), F32)
    else:
        kernel_fn, name = _combine_split_kernel, "moe_combine_split"
        out_specs = [pl.BlockSpec((tm, D_MODEL), lambda i, *_: (jnp.minimum(i, nctx - 1), 0)),
                     pl.BlockSpec((tm, D_MODEL), lambda i, *_: (jnp.maximum(i - nctx, 0), 0))]
        out_shape = [jax.ShapeDtypeStruct((N_CTX_TOK, D_MODEL), F32),
                     jax.ShapeDtypeStruct((N_TOK - N_CTX_TOK, D_MODEL), F32)]
    return pl.pallas_call(
        functools.partial(kernel_fn, tm=tm),
        grid_spec=pltpu.PrefetchScalarGridSpec(
            num_scalar_prefetch=1, grid=(N_TOK // tm,), in_specs=in_specs, out_specs=out_specs,
            scratch_shapes=scratch),
        out_shape=out_shape,
        compiler_params=_vmem(56),
        name=name,
    )(plan[4], x1, sw, mod, ys)


def _retention_kernel(lg_ref, q_ref, k_ref, v_ref, gf_ref, gb_ref, *rest, length, has_s0, emit_state):
    rest = list(rest)
    s0_ref = rest.pop(0) if has_s0 else None
    o_ref = rest.pop(0)
    so_ref = rest.pop(0) if emit_state else None
    s_ref, yf_ref, yb_ref = rest
    c = RET_CHUNK
    nc = length // c
    ii = lax.broadcasted_iota(jnp.int32, (c, c), 0).astype(F32)
    jj = lax.broadcasted_iota(jnp.int32, (c, c), 1).astype(F32)
    ci = lax.broadcasted_iota(jnp.int32, (c, 1), 0).astype(F32)

    def decays(direction):
        lg = -jnp.exp(lg_ref[direction, 0])
        lg1 = lg[:, 0:1]
        if direction == 0:
            diff = ii - jj
            q_decay = jnp.exp(lg1 * (ci + 1.0))
            k_decay = jnp.exp(lg1 * (c - 1.0 - ci))
        else:
            diff = jj - ii
            q_decay = jnp.exp(lg1 * (c - ci))
            k_decay = jnp.exp(lg1 * ci)
        inner = jnp.where(diff >= 0, jnp.exp(lg * jnp.maximum(diff, 0.0)), 0.0)
        return inner, q_decay, k_decay, jnp.exp(lg1 * float(c))

    def chunk(direction, ch, consts, g_ref, y_ref):
        inner, q_decay, k_decay, chunk_decay = consts
        rows = pl.ds(pl.multiple_of(ch * c, c), c)
        qc = q_ref[rows, :]
        kc = k_ref[rows, :].astype(F32) * (RET_DK ** -0.5)
        vc = v_ref[rows, :]
        s = s_ref[direction]
        att = lax.dot_general(qc, kc.astype(BF16), _DOT_NT, preferred_element_type=F32) * inner
        o = (jnp.dot(att.astype(BF16), vc, preferred_element_type=F32)
             + jnp.dot(qc, s.astype(BF16), preferred_element_type=F32) * q_decay)
        kd = (kc * k_decay).T.astype(BF16)
        s_ref[direction] = s * chunk_decay + jnp.dot(kd, vc, preferred_element_type=F32)
        on = o * lax.rsqrt(jnp.mean(o * o, axis=-1, keepdims=True) + EPS)
        g = g_ref[rows, :].astype(F32)
        y_ref[rows, :] = g * _sigmoid(g) * on

    for direction in range(2):
        if has_s0:
            s_ref[direction] = s0_ref[0, direction, 0]
        else:
            s_ref[direction] = jnp.zeros((RET_DK, RET_DV), F32)
    forward, backward = decays(0), decays(1)

    def body(step, carry):
        chunk(0, step, forward, gf_ref, yf_ref)
        chunk(1, nc - 1 - step, backward, gb_ref, yb_ref)
        return carry

    lax.fori_loop(0, nc, body, 0)
    o_ref[...] = (yf_ref[...] + yb_ref[...]).astype(o_ref.dtype)
    if emit_state:
        for direction in range(2):
            so_ref[0, direction, 0] = s_ref[direction]


def _retention(u, first_seq, nseq, length, decay_logit, s0, emit_state):
    row0 = first_seq
    lg = jnp.broadcast_to(decay_logit.astype(F32)[:, :, None, None], (2, RET_HEADS, 1, LANES))
    kcol = RET_QK_WIDTH // RET_DK
    vcol = 2 * RET_QK_WIDTH // RET_DV
    gfcol = vcol + RET_HEADS
    gbcol = gfcol + RET_HEADS
    in_specs = [
        pl.BlockSpec((2, 1, 1, LANES), lambda b, h: (0, h, 0, 0)),
        pl.BlockSpec((length, RET_DK), lambda b, h: (row0 + b, h)),
        pl.BlockSpec((length, RET_DK), lambda b, h: (row0 + b, kcol + h)),
        pl.BlockSpec((length, RET_DV), lambda b, h: (row0 + b, vcol + h)),
        pl.BlockSpec((length, RET_DV), lambda b, h: (row0 + b, gfcol + h)),
        pl.BlockSpec((length, RET_DV), lambda b, h: (row0 + b, gbcol + h)),
    ]
    args = [lg, u, u, u, u, u]
    state_spec = pl.BlockSpec((1, 2, 1, RET_DK, RET_DV), lambda b, h: (b, 0, h, 0, 0))
    if s0 is not None:
        in_specs.append(state_spec)
        args.append(s0)
    out_specs = [pl.BlockSpec((length, RET_DV), lambda b, h: (b, h))]
    out_shape = [jax.ShapeDtypeStruct((nseq * length, RET_V_WIDTH), BF16)]
    if emit_state:
        out_specs.append(state_spec)
        out_shape.append(jax.ShapeDtypeStruct((nseq, 2, RET_HEADS, RET_DK, RET_DV), F32))
    return pl.pallas_call(
        functools.partial(_retention_kernel, length=length, has_s0=s0 is not None, emit_state=emit_state),
        grid=(nseq, RET_HEADS),
        in_specs=in_specs,
        out_specs=out_specs,
        out_shape=out_shape,
        scratch_shapes=[pltpu.VMEM((2, RET_DK, RET_DV), F32), pltpu.VMEM((length, RET_DV), F32),
                        pltpu.VMEM((length, RET_DV), F32)],
        compiler_params=_vmem(48),
        name=f"retention_{length}",
    )(*args)


def kernel(x_prompt, x_sample, cache_k0, cache_v0, state_ret1, c, c_ctx, l0_norm_mix, l0_ada_w, l0_ada_b, l0_w_in, l0_conv_w, l0_conv_b, l0_filt_w1, l0_filt_b1, l0_filt_freq, l0_filt_w2, l0_filt_b2, l0_filt_w3, l0_filt_deltas, l0_hy_skip, l0_q_gain, l0_k_gain, l0_sink, l0_w_out, l0_norm_ffn, l0_router_w, l0_router_b, l0_moe_w1, l0_moe_b1, l0_moe_w2, l0_moe_b2, l1_norm_mix, l1_ada_w, l1_ada_b, l1_w_in, l1_ret_decay_logit, l1_w_out, l1_norm_ffn, l1_router_w, l1_router_b, l1_moe_w1, l1_moe_b1, l1_moe_w2, l1_moe_b2):
    x = jnp.concatenate([x_prompt.reshape(N_CTX_TOK, D_MODEL), x_sample.reshape(N_TOK - N_CTX_TOK, D_MODEL)], axis=0)
    cond = jnp.zeros((SUBLANES, D_MODEL), F32).at[0].set(c_ctx).at[1:1 + N_LAT_SEQ].set(c)
    mod0 = _adaln(cond, l0_ada_w, l0_ada_b)
    mod1 = _adaln(cond, l1_ada_w, l1_ada_b)

    u = _in_proj(x, l0_norm_mix, mod0, l0_w_in, EVEN_IN // 2, F32)
    filt = (l0_filt_w1, l0_filt_b1, l0_filt_freq, l0_filt_w2, l0_filt_b2, l0_filt_w3, l0_filt_deltas)
    hy = []
    for first_block, nseq, length in ((0, N_CTX_SEQ, L_CTX), (N_CTX_TOK // L_LAT // N_LAT_SEQ, N_LAT_SEQ, L_LAT)):
        cmat, smat, stmat = _dft_matrices(length)
        tc, ts = _hyena_filter(length, _filter_features(length), cmat, smat, *filt)
        hy.append(_hyena(u, first_block, nseq, length, l0_conv_w, l0_conv_b, l0_hy_skip, tc, ts, cmat, smat, stmat))
    head = lax.broadcasted_iota(jnp.int32, (ATT_WIDTH, ATT_WIDTH), 0) // HEAD_DIM
    seg = (head == head.T).astype(BF16)
    att_ctx, new_k, new_v = _ctx_attention(u, seg, l0_q_gain, l0_k_gain, l0_sink)
    att_lat = _lat_attention(u, cache_k0, cache_v0, seg, l0_q_gain, l0_k_gain, l0_sink)
    x1, sw, counts, xg = _post_even(x, hy[0], hy[1], att_ctx, att_lat, l0_w_out, l0_norm_ffn, mod0,
                                    l0_router_w, l0_router_b)
    plan = _moe_plan(counts)
    ys = _moe_experts(xg, plan, l0_moe_w1, l0_moe_b1, l0_moe_w2, l0_moe_b2)
    x = _combine(x1, ys, sw, plan, mod0, split=False)

    u = _in_proj(x, l1_norm_mix, mod1, l1_w_in, 2048, BF16)
    y_ctx, new_state = _retention(u, 0, N_CTX_SEQ, L_CTX, l1_ret_decay_logit, None, True)
    (y_lat,) = _retention(u, N_CTX_TOK // L_LAT, N_LAT_SEQ, L_LAT, l1_ret_decay_logit, state_ret1, False)
    x1, sw, counts, xg = _post_odd(x, y_ctx, y_lat, l1_w_out, l1_norm_ffn, mod1, l1_router_w, l1_router_b)
    plan = _moe_plan(counts)
    ys = _moe_experts(xg, plan, l1_moe_w1, l1_moe_b1, l1_moe_w2, l1_moe_b2)
    y_prompt, y_sample = _combine(x1, ys, sw, plan, mod1, split=True)

    return (y_prompt.reshape(N_CTX_SEQ, L_CTX, D_MODEL), y_sample.reshape(N_LAT_SEQ, L_LAT, D_MODEL),
            new_k.reshape(N_CTX_SEQ, L_CTX, ATT_KV_HEADS, HEAD_DIM),
            new_v.reshape(N_CTX_SEQ, L_CTX, ATT_KV_HEADS, HEAD_DIM), new_state)
```

```python
import functools
import math

import jax
import jax.numpy as jnp
from jax import lax
from jax.experimental import pallas as pl
from jax.experimental.pallas import tpu as pltpu

F32 = jnp.float32
BF16 = jnp.bfloat16
HIGHEST = lax.Precision.HIGHEST

D_MODEL = 1024
N_CTX_SEQ, L_CTX = 16, 256
N_LAT_SEQ, L_LAT = 2, 2048
N_CTX_TOK = N_CTX_SEQ * L_CTX
N_TOK = N_CTX_TOK + N_LAT_SEQ * L_LAT
PAST_LEN = 512
EPS = 1e-6
NEG_BIG = -1e30

HY_WIDTH = 512
HY_BANDS = 16
HY_FILTER_HIDDEN = 64
HY_FEAT_PAD = 64

ATT_HEADS, ATT_KV_HEADS, HEAD_DIM = 8, 2, 64
ATT_GROUP = ATT_HEADS // ATT_KV_HEADS
ATT_WIDTH = ATT_HEADS * HEAD_DIM
KV_WIDTH = ATT_KV_HEADS * HEAD_DIM
ATT_SCALE = HEAD_DIM ** -0.5
WINDOW = 128
ATT_BLOCK = 128
ROPE_THETA = 10000.0
ROPE_FREQS = HEAD_DIM // 4
GRID_W = 64
EVEN_IN = 3 * HY_WIDTH + ATT_WIDTH + 2 * KV_WIDTH

RET_HEADS = 4
RET_DK = 256
RET_DV = 512
RET_CHUNK = 128
RET_QK_WIDTH = RET_HEADS * RET_DK
RET_V_WIDTH = RET_HEADS * RET_DV
ODD_IN = 2 * RET_QK_WIDTH + 3 * RET_V_WIDTH

N_EXPERTS = 32
TOP_K = 4
D_FF = 1024
SWIGLU_ALPHA = 1.702
SWIGLU_LIMIT = 7.0

SUBLANES = 8
LANES = 128

MOE_TILE = 512
MOE_GROUP = 16
ROW_TILE = 512
MOE_GROUPED_ROWS = 2560
MOE_NUM_TILES = 112


def _vmem(mib):
    return pltpu.CompilerParams(vmem_limit_bytes=mib * 1024 * 1024)


def _cond_of_tile(i, tm):
    row = i * tm
    return jnp.where(row < N_CTX_TOK, 0, 1 + (row - N_CTX_TOK) // L_LAT)


def _sigmoid(x):
    return 1.0 / (1.0 + jnp.exp(-x))


def _norm_mod(x, gain, shift, scale):
    ms = jnp.mean(x * x, axis=-1, keepdims=True)
    return (x * lax.rsqrt(ms + EPS) * gain) * (1.0 + scale) + shift


def _adaln_kernel(c_ref, w_ref, b_ref, o_ref):
    c = c_ref[...]
    s = c * _sigmoid(c)
    o_ref[...] = jnp.dot(s, w_ref[...], preferred_element_type=F32, precision=HIGHEST) + b_ref[...]


def _adaln(cond, w, b):
    n = w.shape[1]
    tn = 1024
    out = pl.pallas_call(
        _adaln_kernel,
        grid=(n // tn,),
        in_specs=[
            pl.BlockSpec((SUBLANES, D_MODEL), lambda j: (0, 0)),
            pl.BlockSpec((D_MODEL, tn), lambda j: (0, j)),
            pl.BlockSpec((1, tn), lambda j: (0, j)),
        ],
        out_specs=pl.BlockSpec((SUBLANES, tn), lambda j: (0, j)),
        out_shape=jax.ShapeDtypeStruct((SUBLANES, n), F32),
        name="adaln",
    )(cond, w, b.reshape(1, n))
    return out.reshape(SUBLANES, 6, D_MODEL)


def _in_proj_kernel(*refs, tm):
    *x_refs, gain_ref, mod_ref, w_ref, o_ref, wb_ref = refs
    i = pl.program_id(1)

    @pl.when(i == 0)
    def _():
        wb_ref[...] = w_ref[...].astype(BF16)

    if len(x_refs) == 2:
        x = jnp.where(i * tm < N_CTX_TOK, x_refs[0][...], x_refs[1][...])
    else:
        x = x_refs[0][...]
    h = _norm_mod(x, gain_ref[...], mod_ref[0, 0:1, :], mod_ref[0, 1:2, :])
    o_ref[...] = jnp.dot(h.astype(BF16), wb_ref[...], preferred_element_type=F32).astype(o_ref.dtype)


def _in_proj(xs, gain, mod, w, tn, out_dtype):
    n = w.shape[1]
    tm = ROW_TILE
    nctx = N_CTX_TOK // tm
    if len(xs) == 2:
        x_specs = [pl.BlockSpec((tm, D_MODEL), lambda j, i: (jnp.minimum(i, nctx - 1), 0)),
                   pl.BlockSpec((tm, D_MODEL), lambda j, i: (jnp.maximum(i - nctx, 0), 0))]
    else:
        x_specs = [pl.BlockSpec((tm, D_MODEL), lambda j, i: (i, 0))]
    return pl.pallas_call(
        functools.partial(_in_proj_kernel, tm=tm),
        grid=(n // tn, N_TOK // tm),
        in_specs=x_specs + [
            pl.BlockSpec((1, D_MODEL), lambda j, i: (0, 0)),
            pl.BlockSpec((1, 6, D_MODEL), lambda j, i: (_cond_of_tile(i, tm), 0, 0)),
            pl.BlockSpec((D_MODEL, tn), lambda j, i: (0, j)),
        ],
        out_specs=pl.BlockSpec((tm, tn), lambda j, i: (i, j)),
        out_shape=jax.ShapeDtypeStruct((N_TOK, n), out_dtype),
        scratch_shapes=[pltpu.VMEM((D_MODEL, tn), BF16)],
        compiler_params=_vmem(48),
        name="in_proj",
    )(*xs, gain.reshape(1, D_MODEL), mod, w)


def _dft_matrices(length):
    n = 2 * length
    lo = 16
    s = jnp.arange(length, dtype=jnp.int32)
    k1 = jnp.arange(length // lo, dtype=jnp.int32) * lo
    k0 = jnp.arange(lo, dtype=jnp.int32)
    ang1 = (2.0 * math.pi / n) * ((k1[:, None] * s[None, :]) % n).astype(F32)
    ang0 = (2.0 * math.pi / n) * ((k0[:, None] * s[None, :]) % n).astype(F32)
    c1, s1 = jnp.cos(ang1)[:, None, :], jnp.sin(ang1)[:, None, :]
    c0, s0 = jnp.cos(ang0)[None, :, :], jnp.sin(ang0)[None, :, :]
    cmat = (c1 * c0 - s1 * s0).reshape(length, length)
    smat = (s1 * c0 + c1 * s0).reshape(length, length)
    sign = jnp.where(s % 2 == 0, 1.0, -1.0).astype(F32)
    row = lax.broadcasted_iota(jnp.int32, (length, length), 0)
    col = lax.broadcasted_iota(jnp.int32, (length, length), 1)
    s_nyq = jnp.where(row == 0, sign[None, :], smat)
    st_nyq = jnp.where(col == 0, sign[:, None], smat)
    return cmat.astype(BF16), s_nyq.astype(BF16), st_nyq.astype(BF16)


def _filter_features(length):
    t = jnp.linspace(0.0, 1.0, length, dtype=F32)[:, None]
    w = 2.0 * math.pi * jnp.arange(length, dtype=F32)[:, None] / length
    f = jnp.linspace(1e-4, HY_BANDS - 1, HY_BANDS, dtype=F32)[None, :]
    z = jnp.concatenate([t, jnp.cos(f * w), -jnp.sin(f * w)], axis=-1)
    return jnp.pad(z, ((0, 0), (0, HY_FEAT_PAD - z.shape[1])))


def _filter_kernel(z_ref, w1_ref, b1_ref, fr_ref, w2_ref, b2_ref, w3_ref, dl_ref, c_ref, s_ref,
                   tc_ref, ts_ref, taps_ref, *, length, rb):
    r = pl.program_id(0)

    @pl.when(r == 0)
    def _():
        z = z_ref[...]
        fr = fr_ref[...]
        h = jnp.sin(fr * (jnp.dot(z, w1_ref[...], preferred_element_type=F32, precision=HIGHEST) + b1_ref[...]))
        h = jnp.sin(fr * (jnp.dot(h, w2_ref[...], preferred_element_type=F32, precision=HIGHEST) + b2_ref[...]))
        h = jnp.dot(h, w3_ref[...], preferred_element_type=F32, precision=HIGHEST)
        win = jnp.exp(-z[:, 0:1] * jnp.abs(dl_ref[...]))
        hf = h[:, :HY_WIDTH] * win
        hb = h[:, HY_WIDTH:] * win
        row = lax.broadcasted_iota(jnp.int32, (length, HY_WIDTH), 0)
        hb = jnp.where(row == 0, 0.0, hb)
        l1 = jnp.sum(jnp.abs(hf), axis=0, keepdims=True) + jnp.sum(jnp.abs(hb), axis=0, keepdims=True)
        inv = 1.0 / l1
        taps_ref[:, :HY_WIDTH] = (hf * inv).astype(BF16)
        taps_ref[:, HY_WIDTH:] = (hb * inv).astype(BF16)

    taps = taps_ref[...]
    rc = jnp.dot(c_ref[...], taps, preferred_element_type=F32)
    rs = jnp.dot(s_ref[...], taps, preferred_element_type=F32)
    tc = rc[:, :HY_WIDTH] + rc[:, HY_WIDTH:]
    ts = rs[:, :HY_WIDTH] - rs[:, HY_WIDTH:]
    grow = r * rb + lax.broadcasted_iota(jnp.int32, (rb, HY_WIDTH), 0)
    is0 = grow == 0
    ts = jnp.where(is0, rs[:, :HY_WIDTH] + rs[:, HY_WIDTH:], ts)
    wgt = jnp.where(is0, 1.0 / (2 * length), 2.0 / (2 * length))
    tc_ref[...] = tc * wgt
    ts_ref[...] = ts * wgt


def _hyena_filter(length, feats, cmat, smat, w1, b1, freq, w2, b2, w3, deltas):
    rb = min(length, 512)
    w1p = jnp.pad(w1, ((0, HY_FEAT_PAD - w1.shape[0]), (0, 0)))
    full = lambda shape: pl.BlockSpec(shape, lambda r: (0,) * len(shape))
    hid = HY_FILTER_HIDDEN
    return pl.pallas_call(
        functools.partial(_filter_kernel, length=length, rb=rb),
        grid=(length // rb,),
        in_specs=[
            full((length, HY_FEAT_PAD)), full((HY_FEAT_PAD, hid)), full((1, hid)), full((1, hid)),
            full((hid, hid)), full((1, hid)), full((hid, 2 * HY_WIDTH)), full((1, HY_WIDTH)),
            pl.BlockSpec((rb, length), lambda r: (r, 0)),
            pl.BlockSpec((rb, length), lambda r: (r, 0)),
        ],
        out_specs=[pl.BlockSpec((rb, HY_WIDTH), lambda r: (r, 0)),
                   pl.BlockSpec((rb, HY_WIDTH), lambda r: (r, 0))],
        out_shape=[jax.ShapeDtypeStruct((length, HY_WIDTH), F32)] * 2,
        scratch_shapes=[pltpu.VMEM((length, 2 * HY_WIDTH), BF16)],
        compiler_params=_vmem(48),
        name=f"hyena_filter_{length}",
    )(feats, w1p, b1.reshape(1, hid), freq.reshape(1, hid), w2, b2.reshape(1, hid), w3,
      deltas.reshape(1, HY_WIDTH), cmat, smat)


def _hyena_kernel(x0_ref, x1_ref, v_ref, w0_ref, w1_ref, w2_ref, b0_ref, b1_ref, b2_ref, skip_ref,
                  tc_ref, ts_ref, c_ref, s_ref, ct_ref, st_ref, o_ref,
                  zb_ref, zs_ref, x0c_ref, acc_ref, *, nseq, length, cb, fb):
    f = pl.program_id(1)
    nf = pl.num_programs(1)

    def short_conv(u, w_ref, b_ref):
        row = lax.broadcasted_iota(jnp.int32, u.shape, 0)
        prev = jnp.where(row == 0, 0.0, pltpu.roll(u, 1, 0))
        nxt = jnp.where(row == length - 1, 0.0, pltpu.roll(u, length - 1, 0))
        return prev * w_ref[0:1, :] + u * w_ref[1:2, :] + nxt * w_ref[2:3, :] + b_ref[...]

    @pl.when(f == 0)
    def _():
        for b in range(nseq):
            cols = slice(b * cb, (b + 1) * cb)
            x0c_ref[:, cols] = short_conv(x0_ref[b], w0_ref, b0_ref)
            z = short_conv(x1_ref[b], w1_ref, b1_ref) * short_conv(v_ref[b], w2_ref, b2_ref)
            zb_ref[:, cols] = z.astype(BF16)
            zs_ref[:, cols] = z * skip_ref[...]
        acc_ref[...] = jnp.zeros_like(acc_ref)

    zb = zb_ref[...]
    zc = jnp.dot(c_ref[...], zb, preferred_element_type=F32)
    zsn = jnp.dot(s_ref[...], zb, preferred_element_type=F32)
    tc = jnp.concatenate([tc_ref[...]] * nseq, axis=1)
    ts = jnp.concatenate([ts_ref[...]] * nseq, axis=1)
    grow = f * fb + lax.broadcasted_iota(jnp.int32, zc.shape, 0)
    is0 = grow == 0
    yc = jnp.where(is0, zc * tc, zc * tc - zsn * ts)
    ys = jnp.where(is0, zsn * ts, zc * ts + zsn * tc)
    acc_ref[...] += (jnp.dot(ct_ref[...], yc.astype(BF16), preferred_element_type=F32)
                     + jnp.dot(st_ref[...], ys.astype(BF16), preferred_element_type=F32))

    @pl.when(f == nf - 1)
    def _():
        for b in range(nseq):
            cols = slice(b * cb, (b + 1) * cb)
            o_ref[b] = (x0c_ref[:, cols] * (acc_ref[:, cols] + zs_ref[:, cols])).astype(o_ref.dtype)


def _hyena(u, first_seq_block, nseq, length, conv_w, conv_b, skip, tc, ts, cmat, smat, stmat):
    cb = 128
    fb = min(length, 512)
    u3 = u.reshape(N_TOK // length, length, EVEN_IN)
    ncb = HY_WIDTH // cb
    width = nseq * cb

    def ublock(part):
        return pl.BlockSpec((nseq, length, cb), lambda c, f: (first_seq_block, 0, part * ncb + c))

    def wblock(part, rows):
        return pl.BlockSpec((rows, cb), lambda c, f: (0, part * ncb + c))

    return pl.pallas_call(
        functools.partial(_hyena_kernel, nseq=nseq, length=length, cb=cb, fb=fb),
        grid=(ncb, length // fb),
        in_specs=[
            ublock(0), ublock(1), ublock(2),
            wblock(0, 3), wblock(1, 3), wblock(2, 3),
            wblock(0, 1), wblock(1, 1), wblock(2, 1),
            pl.BlockSpec((1, cb), lambda c, f: (0, c)),
            pl.BlockSpec((fb, cb), lambda c, f: (f, c)),
            pl.BlockSpec((fb, cb), lambda c, f: (f, c)),
            pl.BlockSpec((fb, length), lambda c, f: (f, 0)),
            pl.BlockSpec((fb, length), lambda c, f: (f, 0)),
            pl.BlockSpec((length, fb), lambda c, f: (0, f)),
            pl.BlockSpec((length, fb), lambda c, f: (0, f)),
        ],
        out_specs=pl.BlockSpec((nseq, length, cb), lambda c, f: (0, 0, c)),
        out_shape=jax.ShapeDtypeStruct((nseq, length, HY_WIDTH), BF16),
        scratch_shapes=[pltpu.VMEM((length, width), BF16), pltpu.VMEM((length, width), F32),
                        pltpu.VMEM((length, width), F32), pltpu.VMEM((length, width), F32)],
        compiler_params=_vmem(48),
        name=f"hyena_{length}",
    )(u3, u3, u3, conv_w, conv_w, conv_w, conv_b.reshape(1, -1), conv_b.reshape(1, -1),
      conv_b.reshape(1, -1), skip.reshape(1, HY_WIDTH), tc, ts, cmat, smat, cmat, stmat
      ).reshape(nseq * length, HY_WIDTH)


def _head_rms(x, seg, gain):
    x2 = x * x
    hi = x2.astype(BF16)
    lo = (x2 - hi.astype(F32)).astype(BF16)
    ss = jnp.dot(hi, seg, preferred_element_type=F32) + jnp.dot(lo, seg, preferred_element_type=F32)
    return x * lax.rsqrt(ss * (1.0 / HEAD_DIM) + EPS) * gain


def _rope(x, cos, sin_signed):
    width = x.shape[1]
    lane = lax.broadcasted_iota(jnp.int32, x.shape, 1)
    first = (lane // ROPE_FREQS) % 2 == 0
    partner = jnp.where(first, pltpu.roll(x, width - ROPE_FREQS, 1), pltpu.roll(x, ROPE_FREQS, 1))
    return x * cos + partner * sin_signed


def _stack_heads(x, g):
    return jnp.concatenate(
        [x[:, (g * ATT_GROUP + j) * HEAD_DIM:(g * ATT_GROUP + j + 1) * HEAD_DIM] for j in range(ATT_GROUP)], axis=0)


def _sink_column(sink_ref, g, rows):
    return jnp.concatenate([jnp.full((rows, 1), sink_ref[g * ATT_GROUP + j], F32) for j in range(ATT_GROUP)], axis=0)


def _unstack_heads(outs, rows):
    return jnp.concatenate([outs[g][j * rows:(j + 1) * rows, :]
                            for g in range(ATT_KV_HEADS) for j in range(ATT_GROUP)], axis=1)


_DOT_NT = (((1,), (1,)), ((), ()))


def _ctx_attn_kernel(sink_ref, q_ref, k_ref, v_ref, seg_ref, qg_ref, kg_ref, o_ref, ko_ref, vo_ref, *, nseq):
    seg = seg_ref[...]
    for b in range(nseq):
        rows = slice(b * L_CTX, (b + 1) * L_CTX)
        qn = _head_rms(q_ref[rows, :], seg, qg_ref[...]) * ATT_SCALE
        kn = _head_rms(k_ref[rows, :], seg[:KV_WIDTH, :KV_WIDTH], kg_ref[...])
        v = v_ref[rows, :]
        ko_ref[rows, :] = kn
        vo_ref[rows, :] = v
        outs = []
        for g in range(ATT_KV_HEADS):
            cols = slice(g * HEAD_DIM, (g + 1) * HEAD_DIM)
            q = _stack_heads(qn, g).astype(BF16)
            s = lax.dot_general(q, kn[:, cols].astype(BF16), _DOT_NT, preferred_element_type=F32)
            sink = _sink_column(sink_ref, g, L_CTX)
            m = jnp.maximum(jnp.max(s, axis=-1, keepdims=True), sink)
            p = jnp.exp(s - m)
            den = jnp.sum(p, axis=-1, keepdims=True) + jnp.exp(sink - m)
            o = jnp.dot(p.astype(BF16), v[:, cols].astype(BF16), preferred_element_type=F32)
            outs.append(o / den)
        o_ref[rows, :] = _unstack_heads(outs, L_CTX).astype(o_ref.dtype)


def _ctx_attention(u, seg, q_gain, k_gain, sink):
    qcol = 3 * HY_WIDTH // ATT_WIDTH
    kcol = (3 * HY_WIDTH + ATT_WIDTH) // KV_WIDTH
    nseq = 2
    rows = nseq * L_CTX
    return pl.pallas_call(
        functools.partial(_ctx_attn_kernel, nseq=nseq),
        grid_spec=pltpu.PrefetchScalarGridSpec(
            num_scalar_prefetch=1,
            grid=(N_CTX_SEQ // nseq,),
            in_specs=[
                pl.BlockSpec((rows, ATT_WIDTH), lambda b, s: (b, qcol)),
                pl.BlockSpec((rows, KV_WIDTH), lambda b, s: (b, kcol)),
                pl.BlockSpec((rows, KV_WIDTH), lambda b, s: (b, kcol + 1)),
                pl.BlockSpec((ATT_WIDTH, ATT_WIDTH), lambda b, s: (0, 0)),
                pl.BlockSpec((1, ATT_WIDTH), lambda b, s: (0, 0)),
                pl.BlockSpec((1, KV_WIDTH), lambda b, s: (0, 0)),
            ],
            out_specs=[
                pl.BlockSpec((rows, ATT_WIDTH), lambda b, s: (b, 0)),
                pl.BlockSpec((rows, KV_WIDTH), lambda b, s: (b, 0)),
                pl.BlockSpec((rows, KV_WIDTH), lambda b, s: (b, 0)),
            ],
        ),
        out_shape=[jax.ShapeDtypeStruct((N_CTX_TOK, ATT_WIDTH), BF16),
                   jax.ShapeDtypeStruct((N_CTX_TOK, KV_WIDTH), F32),
                   jax.ShapeDtypeStruct((N_CTX_TOK, KV_WIDTH), F32)],
        name="ctx_attention",
    )(sink, u, u, u, seg, jnp.tile(q_gain, ATT_HEADS).reshape(1, ATT_WIDTH),
      jnp.tile(k_gain, ATT_KV_HEADS).reshape(1, KV_WIDTH))


def _rope_tables():
    pos = jnp.arange(L_LAT, dtype=jnp.int32)
    row = (pos // GRID_W).astype(F32)
    col = (pos % GRID_W).astype(F32)
    inv = ROPE_THETA ** (-jnp.arange(ROPE_FREQS, dtype=F32) / ROPE_FREQS)
    ar, ac = row[:, None] * inv, col[:, None] * inv
    cos = jnp.concatenate([jnp.cos(ar), jnp.cos(ar), jnp.cos(ac), jnp.cos(ac)], axis=-1)
    sin = jnp.concatenate([-jnp.sin(ar), jnp.sin(ar), -jnp.sin(ac), jnp.sin(ac)], axis=-1)
    return jnp.tile(cos, (1, ATT_HEADS)), jnp.tile(sin, (1, ATT_HEADS))


def _lat_attn_kernel(sink_ref, q_ref, k_ref, v_ref, ck_ref, cv_ref, cosq_ref, sinq_ref, cosk_ref, sink_k_ref,
                     seg_ref, qg_ref, kg_ref, o_ref, kn_ref, *, nq):
    n = pl.program_id(1)
    seg = seg_ref[...]

    @pl.when(n == 0)
    def _():
        kn = _head_rms(k_ref[...], seg[:KV_WIDTH, :KV_WIDTH], kg_ref[...])
        kn_ref[...] = _rope(kn, cosk_ref[...], sink_k_ref[...]).astype(BF16)

    k_ctx = ck_ref[0].astype(BF16)
    v_ctx = cv_ref[0].astype(BF16)
    span = 3 * ATT_BLOCK
    rows = ATT_GROUP * ATT_BLOCK
    for j in range(nq):
        blk = n * nq + j
        qrows = slice(j * ATT_BLOCK, (j + 1) * ATT_BLOCK)
        qn = _head_rms(q_ref[qrows, :], seg, qg_ref[...])
        qn = _rope(qn, cosq_ref[qrows, :], sinq_ref[qrows, :]) * ATT_SCALE
        start = pl.multiple_of(jnp.clip((blk - 1) * ATT_BLOCK, 0, L_LAT - span), ATT_BLOCK)
        q_pos = blk * ATT_BLOCK + (lax.broadcasted_iota(jnp.int32, (rows, span), 0) % ATT_BLOCK)
        k_pos = start + lax.broadcasted_iota(jnp.int32, (rows, span), 1)
        valid = jnp.abs(q_pos - k_pos) <= WINDOW
        k_loc = kn_ref[pl.ds(start, span), :]
        v_loc = v_ref[pl.ds(start, span), :].astype(BF16)
        outs = []
        for g in range(ATT_KV_HEADS):
            cols = slice(g * HEAD_DIM, (g + 1) * HEAD_DIM)
            q = _stack_heads(qn, g).astype(BF16)
            s_loc = lax.dot_general(q, k_loc[:, cols], _DOT_NT, preferred_element_type=F32)
            s_loc = jnp.where(valid, s_loc, NEG_BIG)
            s_ctx = lax.dot_general(q, k_ctx[:, cols], _DOT_NT, preferred_element_type=F32)
            sink = _sink_column(sink_ref, g, ATT_BLOCK)
            m = jnp.maximum(jnp.maximum(jnp.max(s_loc, axis=-1, keepdims=True),
                                        jnp.max(s_ctx, axis=-1, keepdims=True)), sink)
            p_loc = jnp.exp(s_loc - m)
            p_ctx = jnp.exp(s_ctx - m)
            den = (jnp.sum(p_loc, axis=-1, keepdims=True) + jnp.sum(p_ctx, axis=-1, keepdims=True)
                   + jnp.exp(sink - m))
            o = (jnp.dot(p_loc.astype(BF16), v_loc[:, cols], preferred_element_type=F32)
                 + jnp.dot(p_ctx.astype(BF16), v_ctx[:, cols], preferred_element_type=F32))
            outs.append(o / den)
        o_ref[qrows, :] = _unstack_heads(outs, ATT_BLOCK).astype(o_ref.dtype)


def _lat_attention(u, cache_k, cache_v, seg, q_gain, k_gain, sink):
    qcol = 3 * HY_WIDTH // ATT_WIDTH
    kcol = (3 * HY_WIDTH + ATT_WIDTH) // KV_WIDTH
    nq = 2
    qrows = nq * ATT_BLOCK
    nblk = L_LAT // qrows
    first_q_block = N_CTX_TOK // qrows
    first_seq = N_CTX_TOK // L_LAT
    cos, sin = _rope_tables()
    return pl.pallas_call(
        functools.partial(_lat_attn_kernel, nq=nq),
        grid_spec=pltpu.PrefetchScalarGridSpec(
            num_scalar_prefetch=1,
            grid=(N_LAT_SEQ, nblk),
            in_specs=[
                pl.BlockSpec((qrows, ATT_WIDTH), lambda b, n, s: (first_q_block + b * nblk + n, qcol)),
                pl.BlockSpec((L_LAT, KV_WIDTH), lambda b, n, s: (first_seq + b, kcol)),
                pl.BlockSpec((L_LAT, KV_WIDTH), lambda b, n, s: (first_seq + b, kcol + 1)),
                pl.BlockSpec((1, PAST_LEN, KV_WIDTH), lambda b, n, s: (b, 0, 0)),
                pl.BlockSpec((1, PAST_LEN, KV_WIDTH), lambda b, n, s: (b, 0, 0)),
                pl.BlockSpec((qrows, ATT_WIDTH), lambda b, n, s: (n, 0)),
                pl.BlockSpec((qrows, ATT_WIDTH), lambda b, n, s: (n, 0)),
                pl.BlockSpec((L_LAT, KV_WIDTH), lambda b, n, s: (0, 0)),
                pl.BlockSpec((L_LAT, KV_WIDTH), lambda b, n, s: (0, 0)),
                pl.BlockSpec((ATT_WIDTH, ATT_WIDTH), lambda b, n, s: (0, 0)),
                pl.BlockSpec((1, ATT_WIDTH), lambda b, n, s: (0, 0)),
                pl.BlockSpec((1, KV_WIDTH), lambda b, n, s: (0, 0)),
            ],
            out_specs=pl.BlockSpec((qrows, ATT_WIDTH), lambda b, n, s: (b * nblk + n, 0)),
            scratch_shapes=[pltpu.VMEM((L_LAT, KV_WIDTH), BF16)],
        ),
        out_shape=jax.ShapeDtypeStruct((N_LAT_SEQ * L_LAT, ATT_WIDTH), BF16),
        name="lat_attention",
    )(sink, u, u, u, cache_k.reshape(N_LAT_SEQ, PAST_LEN, KV_WIDTH), cache_v.reshape(N_LAT_SEQ, PAST_LEN, KV_WIDTH),
      cos, sin, cos[:, :KV_WIDTH], sin[:, :KV_WIDTH], seg,
      jnp.tile(q_gain, ATT_HEADS).reshape(1, ATT_WIDTH), jnp.tile(k_gain, ATT_KV_HEADS).reshape(1, KV_WIDTH))


def _ceil_to(v, m):
    return ((v + (m - 1)) // m) * m


def _post_mixer(x, mix, refs, tm):
    (w_ref, gain_ref, mod_ref, rwh_ref, rwm_ref, rb_ref, tri_ref, low_ref,
     x1_ref, sw_ref, cnt_ref, xg_ref, wb_ref) = refs

    @pl.when(pl.program_id(0) == 0)
    def _():
        wb_ref[...] = w_ref[...].astype(BF16)

    y = jnp.dot(mix, wb_ref[...], preferred_element_type=F32)
    x1 = x + mod_ref[0, 2:3, :] * y
    x1_ref[...] = x1
    xt = _norm_mod(x1, gain_ref[...], mod_ref[0, 3:4, :], mod_ref[0, 4:5, :])
    xh = xt.astype(BF16)
    xm = (xt - xh.astype(F32)).astype(BF16)
    logits = (lax.dot_general(rwh_ref[...], xh, _DOT_NT, preferred_element_type=F32)
              + lax.dot_general(rwm_ref[...], xh, _DOT_NT, preferred_element_type=F32)
              + lax.dot_general(rwh_ref[...], xm, _DOT_NT, preferred_element_type=F32)) + rb_ref[...]
    expert = lax.broadcasted_iota(jnp.int32, logits.shape, 0)
    vals, hits = [], []
    for _ in range(TOP_K):
        m = jnp.max(logits, axis=0, keepdims=True)
        sel = jnp.min(jnp.where(logits == m, expert, N_EXPERTS), axis=0, keepdims=True)
        vals.append(m)
        hits.append(expert == sel)
        logits = jnp.where(expert == sel, -jnp.inf, logits)
    es = [jnp.exp(v - vals[0]) for v in vals]
    den = es[0] + es[1] + es[2] + es[3]
    weights = [e / den for e in es]

    routed = sum(h.astype(F32) for h in hits)
    counts = jnp.sum(routed, axis=1, keepdims=True)
    cnt_ref[0] = counts.astype(jnp.int32)
    padded = _ceil_to(counts.astype(jnp.int32), MOE_GROUP).astype(F32)
    g0 = jnp.dot(low_ref[...], jnp.broadcast_to(padded, (N_EXPERTS, LANES)),
                 preferred_element_type=F32, precision=HIGHEST)[:, 0:1]
    earlier = jnp.dot(routed.astype(BF16), tri_ref[...], preferred_element_type=F32)
    row_of = g0 + earlier
    slots = [jnp.sum(jnp.where(h, row_of, 0.0), axis=0, keepdims=True) for h in hits]

    prow = lax.broadcasted_iota(jnp.int32, (LANES, tm), 0)
    packed = jnp.zeros((LANES, tm), F32)
    for k, vec in enumerate(slots + weights):
        packed = jnp.where(prow == k, vec, packed)
    sw_ref[...] = packed.T

    group_row = lax.broadcasted_iota(jnp.int32, (MOE_GROUPED_ROWS, tm), 0).astype(F32)
    pick = group_row == slots[0]
    for k in range(1, TOP_K):
        pick = jnp.logical_or(pick, group_row == slots[k])
    perm = jnp.where(pick, 1.0, 0.0).astype(BF16)
    xg_ref[...] = jnp.dot(perm, xh, preferred_element_type=F32).astype(BF16)


def _post_even_kernel(xc_ref, xl_ref, ac_ref, al_ref, tc_ref, tl_ref, *refs, tm):
    is_ctx = pl.program_id(0) * tm < N_CTX_TOK
    mix = jnp.concatenate([jnp.where(is_ctx, ac_ref[...], al_ref[...]),
                           jnp.where(is_ctx, tc_ref[...], tl_ref[...])], axis=1)
    _post_mixer(jnp.where(is_ctx, xc_ref[...], xl_ref[...]), mix, refs, tm)


def _post_odd_kernel(x_ref, yc_ref, yl_ref, *refs, tm):
    mix = jnp.where(pl.program_id(0) * tm < N_CTX_TOK, yc_ref[...], yl_ref[...])
    _post_mixer(x_ref[...], mix, refs, tm)


def _post_call(kernel_fn, name, mixer_specs, mixer_args, k_in, w_out, gain, mod, router_w, router_b):
    tm = ROW_TILE
    nsteps = N_TOK // tm
    row = lax.broadcasted_iota(jnp.int32, (tm, tm), 0)
    col = lax.broadcasted_iota(jnp.int32, (tm, tm), 1)
    tri = (row < col).astype(BF16)
    er = lax.broadcasted_iota(jnp.int32, (N_EXPERTS, N_EXPERTS), 0)
    ec = lax.broadcasted_iota(jnp.int32, (N_EXPERTS, N_EXPERTS), 1)
    low = (ec < er).astype(F32)
    rw_t = router_w.T
    rw_hi = rw_t.astype(BF16)
    rw_mid = (rw_t - rw_hi.astype(F32)).astype(BF16)
    const = lambda shape: pl.BlockSpec(shape, lambda i: (0,) * len(shape))
    return pl.pallas_call(
        functools.partial(kernel_fn, tm=tm),
        grid=(nsteps,),
        in_specs=mixer_specs + [
            const((k_in, D_MODEL)),
            const((1, D_MODEL)),
            pl.BlockSpec((1, 6, D_MODEL), lambda i: (_cond_of_tile(i, tm), 0, 0)),
            const((N_EXPERTS, D_MODEL)), const((N_EXPERTS, D_MODEL)), const((N_EXPERTS, 1)),
            const((tm, tm)), const((N_EXPERTS, N_EXPERTS)),
        ],
        out_specs=[
            pl.BlockSpec((tm, D_MODEL), lambda i: (i, 0)),
            pl.BlockSpec((tm, LANES), lambda i: (i, 0)),
            pl.BlockSpec((1, N_EXPERTS, 1), lambda i: (i, 0, 0)),
            pl.BlockSpec((MOE_GROUPED_ROWS, D_MODEL), lambda i: (i, 0)),
        ],
        out_shape=[
            jax.ShapeDtypeStruct((N_TOK, D_MODEL), F32),
            jax.ShapeDtypeStruct((N_TOK, LANES), F32),
            jax.ShapeDtypeStruct((nsteps, N_EXPERTS, 1), jnp.int32),
            jax.ShapeDtypeStruct((nsteps * MOE_GROUPED_ROWS, D_MODEL), BF16),
        ],
        scratch_shapes=[pltpu.VMEM((k_in, D_MODEL), BF16)],
        compiler_params=_vmem(56),
        name=name,
    )(*mixer_args, w_out, gain.reshape(1, D_MODEL), mod, rw_hi, rw_mid, router_b.reshape(N_EXPERTS, 1), tri, low)


def _post_even(x_ctx, x_lat, a_ctx, a_lat, t_ctx, t_lat, w_out, gain, mod, router_w, router_b):
    tm = ROW_TILE
    nctx = N_CTX_TOK // tm
    ctx_map = lambda i: (jnp.minimum(i, nctx - 1), 0)
    lat_map = lambda i: (jnp.maximum(i - nctx, 0), 0)
    specs = [pl.BlockSpec((tm, D_MODEL), ctx_map), pl.BlockSpec((tm, D_MODEL), lat_map),
             pl.BlockSpec((tm, HY_WIDTH), ctx_map), pl.BlockSpec((tm, HY_WIDTH), lat_map),
             pl.BlockSpec((tm, ATT_WIDTH), ctx_map), pl.BlockSpec((tm, ATT_WIDTH), lat_map)]
    return _post_call(_post_even_kernel, "post_even", specs, (x_ctx, x_lat, a_ctx, a_lat, t_ctx, t_lat),
                      HY_WIDTH + ATT_WIDTH, w_out, gain, mod, router_w, router_b)


def _post_odd(x, y_ctx, y_lat, w_out, gain, mod, router_w, router_b):
    tm = ROW_TILE
    nctx = N_CTX_TOK // tm
    specs = [pl.BlockSpec((tm, D_MODEL), lambda i: (i, 0)),
             pl.BlockSpec((tm, RET_V_WIDTH), lambda i: (jnp.minimum(i, nctx - 1), 0)),
             pl.BlockSpec((tm, RET_V_WIDTH), lambda i: (jnp.maximum(i - nctx, 0), 0))]
    return _post_call(_post_odd_kernel, "post_odd", specs, (x, y_ctx, y_lat),
                      RET_V_WIDTH, w_out, gain, mod, router_w, router_b)


def _moe_plan(counts):
    t, grp, big = MOE_TILE, MOE_GROUP, MOE_GROUPED_ROWS
    n16 = _ceil_to(counts.reshape(-1, N_EXPERTS), grp)
    nsteps = n16.shape[0]
    g0 = jnp.cumsum(n16, axis=1) - n16
    e0 = jnp.cumsum(n16, axis=0) - n16
    rows_e = jnp.sum(n16, axis=0)
    ntiles = (rows_e + t - 1) // t
    tile_end = jnp.cumsum(ntiles)
    tile_first = tile_end - ntiles
    total = tile_end[-1]
    experts = jnp.arange(N_EXPERTS, dtype=jnp.int32)
    tiles = jnp.arange(MOE_NUM_TILES, dtype=jnp.int32)
    valid = tiles < total
    ti = jnp.minimum(tiles, total - 1)
    e_of = jnp.sum((tile_end[None, :] <= ti[:, None]).astype(jnp.int32), axis=1)
    onehot = (e_of[:, None] == experts[None, :]).astype(jnp.int32)
    pick = lambda v: jnp.sum(onehot * v[None, :], axis=1)
    pick2 = lambda m: jnp.sum(onehot[:, None, :] * m[None, :, :], axis=2)
    r = ((ti - pick(tile_first)) * t)[:, None] + grp * jnp.arange(t // grp, dtype=jnp.int32)[None, :]
    ends = pick2(e0 + n16)
    step = jnp.sum((ends[:, None, :] <= r[:, :, None]).astype(jnp.int32), axis=2)
    step = jnp.minimum(step, nsteps - 1)
    sel = (step[:, :, None] == jnp.arange(nsteps, dtype=jnp.int32)[None, None, :]).astype(jnp.int32)
    at_step = lambda m: jnp.sum(sel * pick2(m)[:, None, :], axis=2)
    src = step * big + at_step(g0) + r - at_step(e0)
    live = jnp.logical_and(valid[:, None], r < pick(rows_e)[:, None])
    moe_src = jnp.where(live, src, 0).reshape(-1)
    g = grp * jnp.arange(big // grp, dtype=jnp.int32)
    gend = g0 + n16
    ce = jnp.sum((gend[:, None, :] <= g[None, :, None]).astype(jnp.int32), axis=2)
    used = ce < N_EXPERTS
    ce = jnp.minimum(ce, N_EXPERTS - 1)
    csel = (ce[:, :, None] == experts[None, None, :]).astype(jnp.int32)
    of_e = lambda m: jnp.sum(csel * m[:, None, :], axis=2)
    base = jnp.sum(csel * (tile_first * t)[None, None, :], axis=2)
    csrc = base + of_e(e0) + g[None, :] - of_e(g0)
    comb_src = jnp.where(used, csrc, 0).reshape(-1)
    used_e = ntiles > 0
    rank = jnp.cumsum(used_e.astype(jnp.int32)) - 1
    later = jnp.logical_and(used_e[None, :], experts[None, :] > experts[:, None])
    next_e = jnp.min(jnp.where(later, experts[None, :], N_EXPERTS), axis=1)
    next_e = jnp.where(next_e == N_EXPERTS, -1, next_e)
    wbuf = jnp.stack([pick(rank) % 2, pick(next_e)], axis=1).reshape(-1)
    as_i32 = lambda v: v.astype(jnp.int32)
    return as_i32(e_of), as_i32(valid), as_i32(wbuf), as_i32(moe_src), as_i32(comb_src)


def _moe_kernel(te_ref, tv_ref, nx_ref, src_ref, xg_hbm, w1_hbm, b1_ref, w2_hbm, b2_ref, y_ref,
                xbuf, w1f, w2f, w1b, w2b, sem_in, sem_w):
    i = pl.program_id(0)
    nt = pl.num_programs(0)
    t, grp = MOE_TILE, MOE_GROUP
    slot = i % 2

    def issue_gather(tile, sl):
        for c in range(t // grp):
            src = pl.multiple_of(src_ref[tile * (t // grp) + c], grp)
            pltpu.make_async_copy(xg_hbm.at[pl.ds(src, grp), :], xbuf.at[sl, pl.ds(c * grp, grp), :],
                                  sem_in.at[sl]).start()

    def weight_copies(e, ws):
        return (pltpu.make_async_copy(w1_hbm.at[e], w1f.at[ws], sem_w.at[ws]),
                pltpu.make_async_copy(w2_hbm.at[e], w2f.at[ws], sem_w.at[ws]))

    def valid(tile):
        return tv_ref[jnp.clip(tile, 0, nt - 1)] > 0

    @pl.when(i == 0)
    def _():
        issue_gather(0, 0)
        for cp in weight_copies(te_ref[0], 0):
            cp.start()

    @pl.when(valid(i))
    def _():
        pltpu.make_async_copy(xg_hbm.at[pl.ds(0, t), :], xbuf.at[slot], sem_in.at[slot]).wait()

        @pl.when(jnp.logical_and(i + 1 < nt, valid(i + 1)))
        def _():
            issue_gather(i + 1, 1 - slot)

        e = te_ref[i]
        first = jnp.logical_or(i == 0, e != te_ref[jnp.maximum(i - 1, 0)])
        ws = nx_ref[2 * i]
        nxt = nx_ref[2 * i + 1]

        @pl.when(first)
        def _():
            for cp in weight_copies(e, ws):
                cp.wait()
            w1b[...] = w1f[ws].astype(BF16)
            w2b[...] = w2f[ws].astype(BF16)

            @pl.when(nxt >= 0)
            def _():
                for cp in weight_copies(nxt, 1 - ws):
                    cp.start()

        h = jnp.dot(xbuf[slot], w1b[...], preferred_element_type=F32) + b1_ref[0]
        glu = jnp.minimum(h[:, :D_FF], SWIGLU_LIMIT)
        lin = jnp.clip(h[:, D_FF:], -SWIGLU_LIMIT, SWIGLU_LIMIT)
        act = (glu * _sigmoid(SWIGLU_ALPHA * glu) * (lin + 1.0)).astype(BF16)
        y_ref[...] = (jnp.dot(act, w2b[...], preferred_element_type=F32) + b2_ref[0]).astype(y_ref.dtype)

    @pl.when(jnp.logical_not(valid(i)))
    def _():
        y_ref[...] = jnp.zeros_like(y_ref)


def _moe_experts(xg, plan, w1, b1, w2, b2):
    te, tv, nx, src, _ = plan
    t = MOE_TILE
    return pl.pallas_call(
        _moe_kernel,
        grid_spec=pltpu.PrefetchScalarGridSpec(
            num_scalar_prefetch=4,
            grid=(MOE_NUM_TILES,),
            in_specs=[
                pl.BlockSpec(memory_space=pl.ANY),
                pl.BlockSpec(memory_space=pl.ANY),
                pl.BlockSpec((1, 1, 2 * D_FF), lambda i, te, *_: (te[i], 0, 0)),
                pl.BlockSpec(memory_space=pl.ANY),
                pl.BlockSpec((1, 1, D_MODEL), lambda i, te, *_: (te[i], 0, 0)),
            ],
            out_specs=pl.BlockSpec((t, D_MODEL), lambda i, *_: (i, 0)),
            scratch_shapes=[
                pltpu.VMEM((2, t, D_MODEL), BF16),
                pltpu.VMEM((2, D_MODEL, 2 * D_FF), F32),
                pltpu.VMEM((2, D_FF, D_MODEL), F32),
                pltpu.VMEM((D_MODEL, 2 * D_FF), BF16),
                pltpu.VMEM((D_FF, D_MODEL), BF16),
                pltpu.SemaphoreType.DMA((2,)),
                pltpu.SemaphoreType.DMA((2,)),
            ],
        ),
        out_shape=jax.ShapeDtypeStruct((MOE_NUM_TILES * t, D_MODEL), BF16),
        compiler_params=_vmem(56),
        name="moe_experts",
    )(te, tv, nx, src, xg, w1, b1.reshape(N_EXPERTS, 1, 2 * D_FF), w2, b2.reshape(N_EXPERTS, 1, D_MODEL))


def _combine_value(src_ref, x1_ref, sw_ref, mod_ref, ys_hbm, ybuf, sem, tm):
    s = pl.program_id(0)
    nsteps = pl.num_programs(0)
    slot = s % 2
    grp, big = MOE_GROUP, MOE_GROUPED_ROWS
    nchunk = big // grp

    def issue_gather(step, sl):
        for c in range(nchunk):
            src = pl.multiple_of(src_ref[step * nchunk + c], grp)
            pltpu.make_async_copy(ys_hbm.at[pl.ds(src, grp), :], ybuf.at[sl, pl.ds(c * grp, grp), :],
                                  sem.at[sl]).start()

    @pl.when(s == 0)
    def _():
        issue_gather(0, 0)

    pltpu.make_async_copy(ys_hbm.at[pl.ds(0, big), :], ybuf.at[slot], sem.at[slot]).wait()

    @pl.when(s + 1 < nsteps)
    def _():
        issue_gather(s + 1, 1 - slot)

    sw = sw_ref[...]
    col = lax.broadcasted_iota(jnp.int32, (tm, big), 1).astype(F32)
    wmat = jnp.zeros((tm, big), F32)
    for k in range(TOP_K):
        wmat = jnp.where(col == sw[:, k:k + 1], sw[:, TOP_K + k:TOP_K + k + 1], wmat)
    moe = jnp.dot(wmat.astype(BF16), ybuf[slot], preferred_element_type=F32)
    return x1_ref[...] + mod_ref[0, 5:6, :] * moe


def _combine_kernel(src_ref, x1_ref, sw_ref, mod_ref, ys_hbm, o_ref, ybuf, sem, *, tm):
    o_ref[...] = _combine_value(src_ref, x1_ref, sw_ref, mod_ref, ys_hbm, ybuf, sem, tm)


def _combine_split_kernel(src_ref, x1_ref, sw_ref, mod_ref, ys_hbm, oc_ref, ol_ref, ybuf, sem, *, tm):
    val = _combine_value(src_ref, x1_ref, sw_ref, mod_ref, ys_hbm, ybuf, sem, tm)
    is_ctx = pl.program_id(0) * tm < N_CTX_TOK

    @pl.when(is_ctx)
    def _():
        oc_ref[...] = val

    @pl.when(jnp.logical_not(is_ctx))
    def _():
        ol_ref[...] = val


def _combine(x1, ys, sw, plan, mod, split):
    tm = ROW_TILE
    nctx = N_CTX_TOK // tm
    in_specs = [
        pl.BlockSpec((tm, D_MODEL), lambda i, *_: (i, 0)),
        pl.BlockSpec((tm, LANES), lambda i, *_: (i, 0)),
        pl.BlockSpec((1, 6, D_MODEL), lambda i, *_: (_cond_of_tile(i, tm), 0, 0)),
        pl.BlockSpec(memory_space=pl.ANY),
    ]
    scratch = [pltpu.VMEM((2, MOE_GROUPED_ROWS, D_MODEL), BF16), pltpu.SemaphoreType.DMA((2,))]
    if not split:
        kernel_fn, name = _combine_kernel, "moe_combine"
        out_specs = pl.BlockSpec((tm, D_MODEL), lambda i, *_: (i, 0))
        out_shape = jax.ShapeDtypeStruct((N_TOK, D_MODEL), F32)
    else:
        kernel_fn, name = _combine_split_kernel, "moe_combine_split"
        out_specs = [pl.BlockSpec((tm, D_MODEL), lambda i, *_: (jnp.minimum(i, nctx - 1), 0)),
                     pl.BlockSpec((tm, D_MODEL), lambda i, *_: (jnp.maximum(i - nctx, 0), 0))]
        out_shape = [jax.ShapeDtypeStruct((N_CTX_TOK, D_MODEL), F32),
                     jax.ShapeDtypeStruct((N_TOK - N_CTX_TOK, D_MODEL), F32)]
    return pl.pallas_call(
        functools.partial(kernel_fn, tm=tm),
        grid_spec=pltpu.PrefetchScalarGridSpec(
            num_scalar_prefetch=1, grid=(N_TOK // tm,), in_specs=in_specs, out_specs=out_specs,
            scratch_shapes=scratch),
        out_shape=out_shape,
        compiler_params=_vmem(56),
        name=name,
    )(plan[4], x1, sw, mod, ys)


def _retention_kernel(lg_ref, q_ref, k_ref, v_ref, gf_ref, gb_ref, *rest, length, has_s0, emit_state):
    rest = list(rest)
    s0_ref = rest.pop(0) if has_s0 else None
    o_ref = rest.pop(0)
    so_ref = rest.pop(0) if emit_state else None
    s_ref, yf_ref, yb_ref = rest
    c = RET_CHUNK
    nc = length // c
    ii = lax.broadcasted_iota(jnp.int32, (c, c), 0).astype(F32)
    jj = lax.broadcasted_iota(jnp.int32, (c, c), 1).astype(F32)
    ci = lax.broadcasted_iota(jnp.int32, (c, 1), 0).astype(F32)

    def decays(direction):
        lg = -jnp.exp(lg_ref[direction, 0])
        lg1 = lg[:, 0:1]
        if direction == 0:
            diff = ii - jj
            q_decay = jnp.exp(lg1 * (ci + 1.0))
            k_decay = jnp.exp(lg1 * (c - 1.0 - ci))
        else:
            diff = jj - ii
            q_decay = jnp.exp(lg1 * (c - ci))
            k_decay = jnp.exp(lg1 * ci)
        scale = RET_DK ** -0.5
        inner = jnp.where(diff >= 0, jnp.exp(lg * jnp.maximum(diff, 0.0)), 0.0) * scale
        return inner, q_decay, k_decay * scale, jnp.exp(lg1 * float(c))

    def chunk(direction, ch, consts, g_ref, y_ref):
        inner, q_decay, k_decay, chunk_decay = consts
        rows = pl.ds(pl.multiple_of(ch * c, c), c)
        qc = q_ref[rows, :]
        kc = k_ref[rows, :]
        vc = v_ref[rows, :]
        s = s_ref[direction]
        att = lax.dot_general(qc, kc, _DOT_NT, preferred_element_type=F32) * inner
        o = (jnp.dot(att.astype(BF16), vc, preferred_element_type=F32)
             + jnp.dot(qc, s.astype(BF16), preferred_element_type=F32) * q_decay)
        kd = (kc.astype(F32) * k_decay).T.astype(BF16)
        s_ref[direction] = s * chunk_decay + jnp.dot(kd, vc, preferred_element_type=F32)
        on = o * lax.rsqrt(jnp.mean(o * o, axis=-1, keepdims=True) + EPS)
        g = g_ref[rows, :].astype(F32)
        y_ref[rows, :] = g * _sigmoid(g) * on

    for direction in range(2):
        if has_s0:
            s_ref[direction] = s0_ref[0, direction, 0]
        else:
            s_ref[direction] = jnp.zeros((RET_DK, RET_DV), F32)
    forward, backward = decays(0), decays(1)

    def body(step, carry):
        chunk(0, step, forward, gf_ref, yf_ref)
        chunk(1, nc - 1 - step, backward, gb_ref, yb_ref)
        return carry

    lax.fori_loop(0, nc, body, 0)
    o_ref[...] = (yf_ref[...] + yb_ref[...]).astype(o_ref.dtype)
    if emit_state:
        for direction in range(2):
            so_ref[0, direction, 0] = s_ref[direction]


def _retention(u, first_seq, nseq, length, decay_logit, s0, emit_state):
    row0 = first_seq
    lg = jnp.broadcast_to(decay_logit.astype(F32)[:, :, None, None], (2, RET_HEADS, 1, LANES))
    kcol = RET_QK_WIDTH // RET_DK
    vcol = 2 * RET_QK_WIDTH // RET_DV
    gfcol = vcol + RET_HEADS
    gbcol = gfcol + RET_HEADS
    in_specs = [
        pl.BlockSpec((2, 1, 1, LANES), lambda b, h: (0, h, 0, 0)),
        pl.BlockSpec((length, RET_DK), lambda b, h: (row0 + b, h)),
        pl.BlockSpec((length, RET_DK), lambda b, h: (row0 + b, kcol + h)),
        pl.BlockSpec((length, RET_DV), lambda b, h: (row0 + b, vcol + h)),
        pl.BlockSpec((length, RET_DV), lambda b, h: (row0 + b, gfcol + h)),
        pl.BlockSpec((length, RET_DV), lambda b, h: (row0 + b, gbcol + h)),
    ]
    args = [lg, u, u, u, u, u]
    state_spec = pl.BlockSpec((1, 2, 1, RET_DK, RET_DV), lambda b, h: (b, 0, h, 0, 0))
    if s0 is not None:
        in_specs.append(state_spec)
        args.append(s0)
    out_specs = [pl.BlockSpec((length, RET_DV), lambda b, h: (b, h))]
    out_shape = [jax.ShapeDtypeStruct((nseq * length, RET_V_WIDTH), BF16)]
    if emit_state:
        out_specs.append(state_spec)
        out_shape.append(jax.ShapeDtypeStruct((nseq, 2, RET_HEADS, RET_DK, RET_DV), F32))
    return pl.pallas_call(
        functools.partial(_retention_kernel, length=length, has_s0=s0 is not None, emit_state=emit_state),
        grid=(nseq, RET_HEADS),
        in_specs=in_specs,
        out_specs=out_specs,
        out_shape=out_shape,
        scratch_shapes=[pltpu.VMEM((2, RET_DK, RET_DV), F32), pltpu.VMEM((length, RET_DV), F32),
                        pltpu.VMEM((length, RET_DV), F32)],
        compiler_params=_vmem(48),
        name=f"retention_{length}",
    )(*args)


def kernel(x_prompt, x_sample, cache_k0, cache_v0, state_ret1, c, c_ctx, l0_norm_mix, l0_ada_w, l0_ada_b, l0_w_in, l0_conv_w, l0_conv_b, l0_filt_w1, l0_filt_b1, l0_filt_freq, l0_filt_w2, l0_filt_b2, l0_filt_w3, l0_filt_deltas, l0_hy_skip, l0_q_gain, l0_k_gain, l0_sink, l0_w_out, l0_norm_ffn, l0_router_w, l0_router_b, l0_moe_w1, l0_moe_b1, l0_moe_w2, l0_moe_b2, l1_norm_mix, l1_ada_w, l1_ada_b, l1_w_in, l1_ret_decay_logit, l1_w_out, l1_norm_ffn, l1_router_w, l1_router_b, l1_moe_w1, l1_moe_b1, l1_moe_w2, l1_moe_b2):
    x_ctx = x_prompt.reshape(N_CTX_TOK, D_MODEL)
    x_lat = x_sample.reshape(N_TOK - N_CTX_TOK, D_MODEL)
    cond = jnp.zeros((SUBLANES, D_MODEL), F32).at[0].set(c_ctx).at[1:1 + N_LAT_SEQ].set(c)
    mod0 = _adaln(cond, l0_ada_w, l0_ada_b)
    mod1 = _adaln(cond, l1_ada_w, l1_ada_b)

    u = _in_proj((x_ctx, x_lat), l0_norm_mix, mod0, l0_w_in, EVEN_IN // 2, F32)
    filt = (l0_filt_w1, l0_filt_b1, l0_filt_freq, l0_filt_w2, l0_filt_b2, l0_filt_w3, l0_filt_deltas)
    hy = []
    for first_block, nseq, length in ((0, N_CTX_SEQ, L_CTX), (N_CTX_TOK // L_LAT // N_LAT_SEQ, N_LAT_SEQ, L_LAT)):
        cmat, smat, stmat = _dft_matrices(length)
        tc, ts = _hyena_filter(length, _filter_features(length), cmat, smat, *filt)
        hy.append(_hyena(u, first_block, nseq, length, l0_conv_w, l0_conv_b, l0_hy_skip, tc, ts, cmat, smat, stmat))
    head = lax.broadcasted_iota(jnp.int32, (ATT_WIDTH, ATT_WIDTH), 0) // HEAD_DIM
    seg = (head == head.T).astype(BF16)
    att_ctx, new_k, new_v = _ctx_attention(u, seg, l0_q_gain, l0_k_gain, l0_sink)
    att_lat = _lat_attention(u, cache_k0, cache_v0, seg, l0_q_gain, l0_k_gain, l0_sink)
    x1, sw, counts, xg = _post_even(x_ctx, x_lat, hy[0], hy[1], att_ctx, att_lat, l0_w_out, l0_norm_ffn, mod0,
                                    l0_router_w, l0_router_b)
    plan = _moe_plan(counts)
    ys = _moe_experts(xg, plan, l0_moe_w1, l0_moe_b1, l0_moe_w2, l0_moe_b2)
    x = _combine(x1, ys, sw, plan, mod0, split=False)

    u = _in_proj((x,), l1_norm_mix, mod1, l1_w_in, 2048, BF16)
    y_ctx, new_state = _retention(u, 0, N_CTX_SEQ, L_CTX, l1_ret_decay_logit, None, True)
    (y_lat,) = _retention(u, N_CTX_TOK // L_LAT, N_LAT_SEQ, L_LAT, l1_ret_decay_logit, state_ret1, False)
    x1, sw, counts, xg = _post_odd(x, y_ctx, y_lat, l1_w_out, l1_norm_ffn, mod1, l1_router_w, l1_router_b)
    plan = _moe_plan(counts)
    ys = _moe_experts(xg, plan, l1_moe_w1, l1_moe_b1, l1_moe_w2, l1_moe_b2)
    y_prompt, y_sample = _combine(x1, ys, sw, plan, mod1, split=True)

    return (y_prompt.reshape(N_CTX_SEQ, L_CTX, D_MODEL), y_sample.reshape(N_LAT_SEQ, L_LAT, D_MODEL),
            new_k.reshape(N_CTX_SEQ, L_CTX, ATT_KV_HEADS, HEAD_DIM),
            new_v.reshape(N_CTX_SEQ, L_CTX, ATT_KV_HEADS, HEAD_DIM), new_state)
```

```python
import functools
import math

import jax
import jax.numpy as jnp
from jax import lax
from jax.experimental import pallas as pl
from jax.experimental.pallas import tpu as pltpu

F32 = jnp.float32
BF16 = jnp.bfloat16
HIGHEST = lax.Precision.HIGHEST

D_MODEL = 1024
N_CTX_SEQ, L_CTX = 16, 256
N_LAT_SEQ, L_LAT = 2, 2048
N_CTX_TOK = N_CTX_SEQ * L_CTX
N_TOK = N_CTX_TOK + N_LAT_SEQ * L_LAT
PAST_LEN = 512
EPS = 1e-6
NEG_BIG = -1e30

HY_WIDTH = 512
HY_BANDS = 16
HY_FILTER_HIDDEN = 64
HY_FEAT_PAD = 64

ATT_HEADS, ATT_KV_HEADS, HEAD_DIM = 8, 2, 64
ATT_GROUP = ATT_HEADS // ATT_KV_HEADS
ATT_WIDTH = ATT_HEADS * HEAD_DIM
KV_WIDTH = ATT_KV_HEADS * HEAD_DIM
ATT_SCALE = HEAD_DIM ** -0.5
WINDOW = 128
ATT_BLOCK = 128
ROPE_THETA = 10000.0
ROPE_FREQS = HEAD_DIM // 4
GRID_W = 64
EVEN_IN = 3 * HY_WIDTH + ATT_WIDTH + 2 * KV_WIDTH

RET_HEADS = 4
RET_DK = 256
RET_DV = 512
RET_CHUNK = 128
RET_QK_WIDTH = RET_HEADS * RET_DK
RET_V_WIDTH = RET_HEADS * RET_DV
ODD_IN = 2 * RET_QK_WIDTH + 3 * RET_V_WIDTH

N_EXPERTS = 32
TOP_K = 4
D_FF = 1024
SWIGLU_ALPHA = 1.702
SWIGLU_LIMIT = 7.0

SUBLANES = 8
LANES = 128

MOE_TILE = 512
MOE_FF_CHUNK = 256
MOE_GROUP = 16
ROW_TILE = 512
MOE_GROUPED_ROWS = 2560
MOE_NUM_TILES = 112


def _vmem(mib):
    return pltpu.CompilerParams(vmem_limit_bytes=mib * 1024 * 1024)


def _cond_of_tile(i, tm):
    row = i * tm
    return jnp.where(row < N_CTX_TOK, 0, 1 + (row - N_CTX_TOK) // L_LAT)


def _sigmoid(x):
    return 1.0 / (1.0 + jnp.exp(-x))


def _norm_mod(x, gain, shift, scale):
    ms = jnp.mean(x * x, axis=-1, keepdims=True)
    return (x * lax.rsqrt(ms + EPS) * gain) * (1.0 + scale) + shift


def _adaln_kernel(c_ref, w_ref, b_ref, o_ref):
    c = c_ref[...]
    s = c * _sigmoid(c)
    o_ref[...] = jnp.dot(s, w_ref[...], preferred_element_type=F32, precision=HIGHEST) + b_ref[...]


def _adaln(cond, w, b):
    n = w.shape[1]
    tn = 1024
    out = pl.pallas_call(
        _adaln_kernel,
        grid=(n // tn,),
        in_specs=[
            pl.BlockSpec((SUBLANES, D_MODEL), lambda j: (0, 0)),
            pl.BlockSpec((D_MODEL, tn), lambda j: (0, j)),
            pl.BlockSpec((1, tn), lambda j: (0, j)),
        ],
        out_specs=pl.BlockSpec((SUBLANES, tn), lambda j: (0, j)),
        out_shape=jax.ShapeDtypeStruct((SUBLANES, n), F32),
        name="adaln",
    )(cond, w, b.reshape(1, n))
    return out.reshape(SUBLANES, 6, D_MODEL)


def _in_proj_kernel(*refs, tm):
    *x_refs, gain_ref, mod_ref, w_ref, o_ref, wb_ref, h_ref = refs
    j, i = pl.program_id(0), pl.program_id(1)

    @pl.when(i == 0)
    def _():
        wb_ref[...] = w_ref[...].astype(BF16)

    @pl.when(j == 0)
    def _():
        if len(x_refs) == 2:
            x = jnp.where(i * tm < N_CTX_TOK, x_refs[0][...], x_refs[1][...])
        else:
            x = x_refs[0][...]
        h_ref[i] = _norm_mod(x, gain_ref[...], mod_ref[0, 0:1, :], mod_ref[0, 1:2, :]).astype(BF16)

    o_ref[...] = jnp.dot(h_ref[i], wb_ref[...], preferred_element_type=F32).astype(o_ref.dtype)


def _in_proj(xs, gain, mod, w, tn, out_dtype):
    n = w.shape[1]
    tm = ROW_TILE
    nctx = N_CTX_TOK // tm
    first = lambda j, i: jnp.where(j == 0, i, 0)
    if len(xs) == 2:
        x_specs = [pl.BlockSpec((tm, D_MODEL), lambda j, i: (jnp.minimum(first(j, i), nctx - 1), 0)),
                   pl.BlockSpec((tm, D_MODEL), lambda j, i: (jnp.maximum(first(j, i) - nctx, 0), 0))]
    else:
        x_specs = [pl.BlockSpec((tm, D_MODEL), lambda j, i: (first(j, i), 0))]
    return pl.pallas_call(
        functools.partial(_in_proj_kernel, tm=tm),
        grid=(n // tn, N_TOK // tm),
        in_specs=x_specs + [
            pl.BlockSpec((1, D_MODEL), lambda j, i: (0, 0)),
            pl.BlockSpec((1, 6, D_MODEL), lambda j, i: (_cond_of_tile(i, tm), 0, 0)),
            pl.BlockSpec((D_MODEL, tn), lambda j, i: (0, j)),
        ],
        out_specs=pl.BlockSpec((tm, tn), lambda j, i: (i, j)),
        out_shape=jax.ShapeDtypeStruct((N_TOK, n), out_dtype),
        scratch_shapes=[pltpu.VMEM((D_MODEL, tn), BF16), pltpu.VMEM((N_TOK // tm, tm, D_MODEL), BF16)],
        compiler_params=_vmem(56),
        name="in_proj",
    )(*xs, gain.reshape(1, D_MODEL), mod, w)


def _dft_matrices(length):
    n = 2 * length
    lo = 16
    s = jnp.arange(length, dtype=jnp.int32)
    k1 = jnp.arange(length // lo, dtype=jnp.int32) * lo
    k0 = jnp.arange(lo, dtype=jnp.int32)
    ang1 = (2.0 * math.pi / n) * ((k1[:, None] * s[None, :]) % n).astype(F32)
    ang0 = (2.0 * math.pi / n) * ((k0[:, None] * s[None, :]) % n).astype(F32)
    c1, s1 = jnp.cos(ang1)[:, None, :], jnp.sin(ang1)[:, None, :]
    c0, s0 = jnp.cos(ang0)[None, :, :], jnp.sin(ang0)[None, :, :]
    cmat = (c1 * c0 - s1 * s0).reshape(length, length)
    smat = (s1 * c0 + c1 * s0).reshape(length, length)
    sign = jnp.where(s % 2 == 0, 1.0, -1.0).astype(F32)
    row = lax.broadcasted_iota(jnp.int32, (length, length), 0)
    col = lax.broadcasted_iota(jnp.int32, (length, length), 1)
    s_nyq = jnp.where(row == 0, sign[None, :], smat)
    st_nyq = jnp.where(col == 0, sign[:, None], smat)
    return cmat.astype(BF16), s_nyq.astype(BF16), st_nyq.astype(BF16)


def _filter_features(length):
    t = jnp.linspace(0.0, 1.0, length, dtype=F32)[:, None]
    w = 2.0 * math.pi * jnp.arange(length, dtype=F32)[:, None] / length
    f = jnp.linspace(1e-4, HY_BANDS - 1, HY_BANDS, dtype=F32)[None, :]
    z = jnp.concatenate([t, jnp.cos(f * w), -jnp.sin(f * w)], axis=-1)
    return jnp.pad(z, ((0, 0), (0, HY_FEAT_PAD - z.shape[1])))


def _filter_kernel(z_ref, w1_ref, b1_ref, fr_ref, w2_ref, b2_ref, w3_ref, dl_ref, c_ref, s_ref,
                   tc_ref, ts_ref, taps_ref, *, length, rb):
    r = pl.program_id(0)

    @pl.when(r == 0)
    def _():
        z = z_ref[...]
        fr = fr_ref[...]
        h = jnp.sin(fr * (jnp.dot(z, w1_ref[...], preferred_element_type=F32, precision=HIGHEST) + b1_ref[...]))
        h = jnp.sin(fr * (jnp.dot(h, w2_ref[...], preferred_element_type=F32, precision=HIGHEST) + b2_ref[...]))
        h = jnp.dot(h, w3_ref[...], preferred_element_type=F32, precision=HIGHEST)
        win = jnp.exp(-z[:, 0:1] * jnp.abs(dl_ref[...]))
        hf = h[:, :HY_WIDTH] * win
        hb = h[:, HY_WIDTH:] * win
        row = lax.broadcasted_iota(jnp.int32, (length, HY_WIDTH), 0)
        hb = jnp.where(row == 0, 0.0, hb)
        l1 = jnp.sum(jnp.abs(hf), axis=0, keepdims=True) + jnp.sum(jnp.abs(hb), axis=0, keepdims=True)
        inv = 1.0 / l1
        taps_ref[:, :HY_WIDTH] = (hf * inv).astype(BF16)
        taps_ref[:, HY_WIDTH:] = (hb * inv).astype(BF16)

    taps = taps_ref[...]
    rc = jnp.dot(c_ref[...], taps, preferred_element_type=F32)
    rs = jnp.dot(s_ref[...], taps, preferred_element_type=F32)
    tc = rc[:, :HY_WIDTH] + rc[:, HY_WIDTH:]
    ts = rs[:, :HY_WIDTH] - rs[:, HY_WIDTH:]
    grow = r * rb + lax.broadcasted_iota(jnp.int32, (rb, HY_WIDTH), 0)
    is0 = grow == 0
    ts = jnp.where(is0, rs[:, :HY_WIDTH] + rs[:, HY_WIDTH:], ts)
    wgt = jnp.where(is0, 1.0 / (2 * length), 2.0 / (2 * length))
    tc_ref[...] = tc * wgt
    ts_ref[...] = ts * wgt


def _hyena_filter(length, feats, cmat, smat, w1, b1, freq, w2, b2, w3, deltas):
    rb = min(length, 512)
    w1p = jnp.pad(w1, ((0, HY_FEAT_PAD - w1.shape[0]), (0, 0)))
    full = lambda shape: pl.BlockSpec(shape, lambda r: (0,) * len(shape))
    hid = HY_FILTER_HIDDEN
    return pl.pallas_call(
        functools.partial(_filter_kernel, length=length, rb=rb),
        grid=(length // rb,),
        in_specs=[
            full((length, HY_FEAT_PAD)), full((HY_FEAT_PAD, hid)), full((1, hid)), full((1, hid)),
            full((hid, hid)), full((1, hid)), full((hid, 2 * HY_WIDTH)), full((1, HY_WIDTH)),
            pl.BlockSpec((rb, length), lambda r: (r, 0)),
            pl.BlockSpec((rb, length), lambda r: (r, 0)),
        ],
        out_specs=[pl.BlockSpec((rb, HY_WIDTH), lambda r: (r, 0)),
                   pl.BlockSpec((rb, HY_WIDTH), lambda r: (r, 0))],
        out_shape=[jax.ShapeDtypeStruct((length, HY_WIDTH), F32)] * 2,
        scratch_shapes=[pltpu.VMEM((length, 2 * HY_WIDTH), BF16)],
        compiler_params=_vmem(48),
        name=f"hyena_filter_{length}",
    )(feats, w1p, b1.reshape(1, hid), freq.reshape(1, hid), w2, b2.reshape(1, hid), w3,
      deltas.reshape(1, HY_WIDTH), cmat, smat)


def _hyena_kernel(x0_ref, x1_ref, v_ref, w0_ref, w1_ref, w2_ref, b0_ref, b1_ref, b2_ref, skip_ref,
                  tc_ref, ts_ref, c_ref, s_ref, ct_ref, st_ref, o_ref,
                  zb_ref, zs_ref, x0c_ref, acc_ref, *, nseq, length, cb, fb):
    f = pl.program_id(1)
    nf = pl.num_programs(1)

    def short_conv(u, w_ref, b_ref):
        row = lax.broadcasted_iota(jnp.int32, u.shape, 0)
        prev = jnp.where(row == 0, 0.0, pltpu.roll(u, 1, 0))
        nxt = jnp.where(row == length - 1, 0.0, pltpu.roll(u, length - 1, 0))
        return prev * w_ref[0:1, :] + u * w_ref[1:2, :] + nxt * w_ref[2:3, :] + b_ref[...]

    @pl.when(f == 0)
    def _():
        for b in range(nseq):
            cols = slice(b * cb, (b + 1) * cb)
            x0c_ref[:, cols] = short_conv(x0_ref[b], w0_ref, b0_ref)
            z = short_conv(x1_ref[b], w1_ref, b1_ref) * short_conv(v_ref[b], w2_ref, b2_ref)
            zb_ref[:, cols] = z.astype(BF16)
            zs_ref[:, cols] = z * skip_ref[...]
        acc_ref[...] = jnp.zeros_like(acc_ref)

    zb = zb_ref[...]
    zc = jnp.dot(c_ref[...], zb, preferred_element_type=F32)
    zsn = jnp.dot(s_ref[...], zb, preferred_element_type=F32)
    tc = jnp.concatenate([tc_ref[...]] * nseq, axis=1)
    ts = jnp.concatenate([ts_ref[...]] * nseq, axis=1)
    grow = f * fb + lax.broadcasted_iota(jnp.int32, zc.shape, 0)
    is0 = grow == 0
    yc = jnp.where(is0, zc * tc, zc * tc - zsn * ts)
    ys = jnp.where(is0, zsn * ts, zc * ts + zsn * tc)
    acc_ref[...] += (jnp.dot(ct_ref[...], yc.astype(BF16), preferred_element_type=F32)
                     + jnp.dot(st_ref[...], ys.astype(BF16), preferred_element_type=F32))

    @pl.when(f == nf - 1)
    def _():
        for b in range(nseq):
            cols = slice(b * cb, (b + 1) * cb)
            o_ref[b] = (x0c_ref[:, cols] * (acc_ref[:, cols] + zs_ref[:, cols])).astype(o_ref.dtype)


def _hyena(u, first_seq_block, nseq, length, conv_w, conv_b, skip, tc, ts, cmat, smat, stmat):
    cb = 128
    fb = min(length, 512)
    u3 = u.reshape(N_TOK // length, length, EVEN_IN)
    ncb = HY_WIDTH // cb
    width = nseq * cb

    def ublock(part):
        return pl.BlockSpec((nseq, length, cb), lambda c, f: (first_seq_block, 0, part * ncb + c))

    def wblock(part, rows):
        return pl.BlockSpec((rows, cb), lambda c, f: (0, part * ncb + c))

    return pl.pallas_call(
        functools.partial(_hyena_kernel, nseq=nseq, length=length, cb=cb, fb=fb),
        grid=(ncb, length // fb),
        in_specs=[
            ublock(0), ublock(1), ublock(2),
            wblock(0, 3), wblock(1, 3), wblock(2, 3),
            wblock(0, 1), wblock(1, 1), wblock(2, 1),
            pl.BlockSpec((1, cb), lambda c, f: (0, c)),
            pl.BlockSpec((fb, cb), lambda c, f: (f, c)),
            pl.BlockSpec((fb, cb), lambda c, f: (f, c)),
            pl.BlockSpec((fb, length), lambda c, f: (f, 0)),
            pl.BlockSpec((fb, length), lambda c, f: (f, 0)),
            pl.BlockSpec((length, fb), lambda c, f: (0, f)),
            pl.BlockSpec((length, fb), lambda c, f: (0, f)),
        ],
        out_specs=pl.BlockSpec((nseq, length, cb), lambda c, f: (0, 0, c)),
        out_shape=jax.ShapeDtypeStruct((nseq, length, HY_WIDTH), BF16),
        scratch_shapes=[pltpu.VMEM((length, width), BF16), pltpu.VMEM((length, width), F32),
                        pltpu.VMEM((length, width), F32), pltpu.VMEM((length, width), F32)],
        compiler_params=_vmem(48),
        name=f"hyena_{length}",
    )(u3, u3, u3, conv_w, conv_w, conv_w, conv_b.reshape(1, -1), conv_b.reshape(1, -1),
      conv_b.reshape(1, -1), skip.reshape(1, HY_WIDTH), tc, ts, cmat, smat, cmat, stmat
      ).reshape(nseq * length, HY_WIDTH)


def _head_rms(x, seg, gain):
    x2 = x * x
    hi = x2.astype(BF16)
    lo = (x2 - hi.astype(F32)).astype(BF16)
    ss = jnp.dot(hi, seg, preferred_element_type=F32) + jnp.dot(lo, seg, preferred_element_type=F32)
    return x * lax.rsqrt(ss * (1.0 / HEAD_DIM) + EPS) * gain


def _rope(x, cos, sin_signed):
    width = x.shape[1]
    lane = lax.broadcasted_iota(jnp.int32, x.shape, 1)
    first = (lane // ROPE_FREQS) % 2 == 0
    partner = jnp.where(first, pltpu.roll(x, width - ROPE_FREQS, 1), pltpu.roll(x, ROPE_FREQS, 1))
    return x * cos + partner * sin_signed


def _stack_heads(x, g):
    return jnp.concatenate(
        [x[:, (g * ATT_GROUP + j) * HEAD_DIM:(g * ATT_GROUP + j + 1) * HEAD_DIM] for j in range(ATT_GROUP)], axis=0)


def _sink_column(sink_ref, g, rows):
    return jnp.concatenate([jnp.full((rows, 1), sink_ref[g * ATT_GROUP + j], F32) for j in range(ATT_GROUP)], axis=0)


def _unstack_heads(outs, rows):
    return jnp.concatenate([outs[g][j * rows:(j + 1) * rows, :]
                            for g in range(ATT_KV_HEADS) for j in range(ATT_GROUP)], axis=1)


_DOT_NT = (((1,), (1,)), ((), ()))


def _ctx_attn_kernel(sink_ref, q_ref, k_ref, v_ref, seg_ref, qg_ref, kg_ref, o_ref, ko_ref, vo_ref, *, nseq):
    seg = seg_ref[...]
    for b in range(nseq):
        rows = slice(b * L_CTX, (b + 1) * L_CTX)
        qn = _head_rms(q_ref[rows, :], seg, qg_ref[...]) * ATT_SCALE
        kn = _head_rms(k_ref[rows, :], seg[:KV_WIDTH, :KV_WIDTH], kg_ref[...])
        v = v_ref[rows, :]
        ko_ref[rows, :] = kn
        vo_ref[rows, :] = v
        outs = []
        for g in range(ATT_KV_HEADS):
            cols = slice(g * HEAD_DIM, (g + 1) * HEAD_DIM)
            q = _stack_heads(qn, g).astype(BF16)
            s = lax.dot_general(q, kn[:, cols].astype(BF16), _DOT_NT, preferred_element_type=F32)
            sink = _sink_column(sink_ref, g, L_CTX)
            m = jnp.maximum(jnp.max(s, axis=-1, keepdims=True), sink)
            p = jnp.exp(s - m)
            den = jnp.sum(p, axis=-1, keepdims=True) + jnp.exp(sink - m)
            o = jnp.dot(p.astype(BF16), v[:, cols].astype(BF16), preferred_element_type=F32)
            outs.append(o / den)
        o_ref[rows, :] = _unstack_heads(outs, L_CTX).astype(o_ref.dtype)


def _ctx_attention(u, seg, q_gain, k_gain, sink):
    qcol = 3 * HY_WIDTH // ATT_WIDTH
    kcol = (3 * HY_WIDTH + ATT_WIDTH) // KV_WIDTH
    nseq = 2
    rows = nseq * L_CTX
    return pl.pallas_call(
        functools.partial(_ctx_attn_kernel, nseq=nseq),
        grid_spec=pltpu.PrefetchScalarGridSpec(
            num_scalar_prefetch=1,
            grid=(N_CTX_SEQ // nseq,),
            in_specs=[
                pl.BlockSpec((rows, ATT_WIDTH), lambda b, s: (b, qcol)),
                pl.BlockSpec((rows, KV_WIDTH), lambda b, s: (b, kcol)),
                pl.BlockSpec((rows, KV_WIDTH), lambda b, s: (b, kcol + 1)),
                pl.BlockSpec((ATT_WIDTH, ATT_WIDTH), lambda b, s: (0, 0)),
                pl.BlockSpec((1, ATT_WIDTH), lambda b, s: (0, 0)),
                pl.BlockSpec((1, KV_WIDTH), lambda b, s: (0, 0)),
            ],
            out_specs=[
                pl.BlockSpec((rows, ATT_WIDTH), lambda b, s: (b, 0)),
                pl.BlockSpec((rows, KV_WIDTH), lambda b, s: (b, 0)),
                pl.BlockSpec((rows, KV_WIDTH), lambda b, s: (b, 0)),
            ],
        ),
        out_shape=[jax.ShapeDtypeStruct((N_CTX_TOK, ATT_WIDTH), BF16),
                   jax.ShapeDtypeStruct((N_CTX_TOK, KV_WIDTH), F32),
                   jax.ShapeDtypeStruct((N_CTX_TOK, KV_WIDTH), F32)],
        name="ctx_attention",
    )(sink, u, u, u, seg, jnp.tile(q_gain, ATT_HEADS).reshape(1, ATT_WIDTH),
      jnp.tile(k_gain, ATT_KV_HEADS).reshape(1, KV_WIDTH))


def _rope_tables():
    pos = jnp.arange(L_LAT, dtype=jnp.int32)
    row = (pos // GRID_W).astype(F32)
    col = (pos % GRID_W).astype(F32)
    inv = ROPE_THETA ** (-jnp.arange(ROPE_FREQS, dtype=F32) / ROPE_FREQS)
    ar, ac = row[:, None] * inv, col[:, None] * inv
    cos = jnp.concatenate([jnp.cos(ar), jnp.cos(ar), jnp.cos(ac), jnp.cos(ac)], axis=-1)
    sin = jnp.concatenate([-jnp.sin(ar), jnp.sin(ar), -jnp.sin(ac), jnp.sin(ac)], axis=-1)
    return jnp.tile(cos, (1, ATT_HEADS)), jnp.tile(sin, (1, ATT_HEADS))


def _lat_attn_kernel(sink_ref, q_ref, k_ref, v_ref, ck_ref, cv_ref, cosq_ref, sinq_ref, cosk_ref, sink_k_ref,
                     seg_ref, qg_ref, kg_ref, o_ref, kn_ref, *, nq):
    n = pl.program_id(1)
    seg = seg_ref[...]

    @pl.when(n == 0)
    def _():
        kn = _head_rms(k_ref[...], seg[:KV_WIDTH, :KV_WIDTH], kg_ref[...])
        kn_ref[...] = _rope(kn, cosk_ref[...], sink_k_ref[...]).astype(BF16)

    k_ctx = ck_ref[0].astype(BF16)
    v_ctx = cv_ref[0].astype(BF16)
    span = 3 * ATT_BLOCK
    rows = ATT_GROUP * ATT_BLOCK
    for j in range(nq):
        blk = n * nq + j
        qrows = slice(j * ATT_BLOCK, (j + 1) * ATT_BLOCK)
        qn = _head_rms(q_ref[qrows, :], seg, qg_ref[...])
        qn = _rope(qn, cosq_ref[qrows, :], sinq_ref[qrows, :]) * ATT_SCALE
        start = pl.multiple_of(jnp.clip((blk - 1) * ATT_BLOCK, 0, L_LAT - span), ATT_BLOCK)
        q_pos = blk * ATT_BLOCK + (lax.broadcasted_iota(jnp.int32, (rows, span), 0) % ATT_BLOCK)
        k_pos = start + lax.broadcasted_iota(jnp.int32, (rows, span), 1)
        valid = jnp.abs(q_pos - k_pos) <= WINDOW
        k_loc = kn_ref[pl.ds(start, span), :]
        v_loc = v_ref[pl.ds(start, span), :].astype(BF16)
        outs = []
        for g in range(ATT_KV_HEADS):
            cols = slice(g * HEAD_DIM, (g + 1) * HEAD_DIM)
            q = _stack_heads(qn, g).astype(BF16)
            s_loc = lax.dot_general(q, k_loc[:, cols], _DOT_NT, preferred_element_type=F32)
            s_loc = jnp.where(valid, s_loc, NEG_BIG)
            s_ctx = lax.dot_general(q, k_ctx[:, cols], _DOT_NT, preferred_element_type=F32)
            sink = _sink_column(sink_ref, g, ATT_BLOCK)
            m = jnp.maximum(jnp.maximum(jnp.max(s_loc, axis=-1, keepdims=True),
                                        jnp.max(s_ctx, axis=-1, keepdims=True)), sink)
            p_loc = jnp.exp(s_loc - m)
            p_ctx = jnp.exp(s_ctx - m)
            den = (jnp.sum(p_loc, axis=-1, keepdims=True) + jnp.sum(p_ctx, axis=-1, keepdims=True)
                   + jnp.exp(sink - m))
            o = (jnp.dot(p_loc.astype(BF16), v_loc[:, cols], preferred_element_type=F32)
                 + jnp.dot(p_ctx.astype(BF16), v_ctx[:, cols], preferred_element_type=F32))
            outs.append(o / den)
        o_ref[qrows, :] = _unstack_heads(outs, ATT_BLOCK).astype(o_ref.dtype)


def _lat_attention(u, cache_k, cache_v, seg, q_gain, k_gain, sink):
    qcol = 3 * HY_WIDTH // ATT_WIDTH
    kcol = (3 * HY_WIDTH + ATT_WIDTH) // KV_WIDTH
    nq = 2
    qrows = nq * ATT_BLOCK
    nblk = L_LAT // qrows
    first_q_block = N_CTX_TOK // qrows
    first_seq = N_CTX_TOK // L_LAT
    cos, sin = _rope_tables()
    return pl.pallas_call(
        functools.partial(_lat_attn_kernel, nq=nq),
        grid_spec=pltpu.PrefetchScalarGridSpec(
            num_scalar_prefetch=1,
            grid=(N_LAT_SEQ, nblk),
            in_specs=[
                pl.BlockSpec((qrows, ATT_WIDTH), lambda b, n, s: (first_q_block + b * nblk + n, qcol)),
                pl.BlockSpec((L_LAT, KV_WIDTH), lambda b, n, s: (first_seq + b, kcol)),
                pl.BlockSpec((L_LAT, KV_WIDTH), lambda b, n, s: (first_seq + b, kcol + 1)),
                pl.BlockSpec((1, PAST_LEN, KV_WIDTH), lambda b, n, s: (b, 0, 0)),
                pl.BlockSpec((1, PAST_LEN, KV_WIDTH), lambda b, n, s: (b, 0, 0)),
                pl.BlockSpec((qrows, ATT_WIDTH), lambda b, n, s: (n, 0)),
                pl.BlockSpec((qrows, ATT_WIDTH), lambda b, n, s: (n, 0)),
                pl.BlockSpec((L_LAT, KV_WIDTH), lambda b, n, s: (0, 0)),
                pl.BlockSpec((L_LAT, KV_WIDTH), lambda b, n, s: (0, 0)),
                pl.BlockSpec((ATT_WIDTH, ATT_WIDTH), lambda b, n, s: (0, 0)),
                pl.BlockSpec((1, ATT_WIDTH), lambda b, n, s: (0, 0)),
                pl.BlockSpec((1, KV_WIDTH), lambda b, n, s: (0, 0)),
            ],
            out_specs=pl.BlockSpec((qrows, ATT_WIDTH), lambda b, n, s: (b * nblk + n, 0)),
            scratch_shapes=[pltpu.VMEM((L_LAT, KV_WIDTH), BF16)],
        ),
        out_shape=jax.ShapeDtypeStruct((N_LAT_SEQ * L_LAT, ATT_WIDTH), BF16),
        name="lat_attention",
    )(sink, u, u, u, cache_k.reshape(N_LAT_SEQ, PAST_LEN, KV_WIDTH), cache_v.reshape(N_LAT_SEQ, PAST_LEN, KV_WIDTH),
      cos, sin, cos[:, :KV_WIDTH], sin[:, :KV_WIDTH], seg,
      jnp.tile(q_gain, ATT_HEADS).reshape(1, ATT_WIDTH), jnp.tile(k_gain, ATT_KV_HEADS).reshape(1, KV_WIDTH))


def _ceil_to(v, m):
    return ((v + (m - 1)) // m) * m


def _post_mixer(x, mix, refs, tm):
    (w_ref, gain_ref, mod_ref, rwh_ref, rwm_ref, rb_ref, tri_ref, low_ref,
     x1_ref, sw_ref, cnt_ref, xg_ref, wb_ref) = refs

    @pl.when(pl.program_id(0) == 0)
    def _():
        wb_ref[...] = w_ref[...].astype(BF16)

    y = jnp.dot(mix, wb_ref[...], preferred_element_type=F32)
    x1 = x + mod_ref[0, 2:3, :] * y
    x1_ref[...] = x1
    xt = _norm_mod(x1, gain_ref[...], mod_ref[0, 3:4, :], mod_ref[0, 4:5, :])
    xh = xt.astype(BF16)
    xm = (xt - xh.astype(F32)).astype(BF16)
    logits = (lax.dot_general(rwh_ref[...], xh, _DOT_NT, preferred_element_type=F32)
              + lax.dot_general(rwm_ref[...], xh, _DOT_NT, preferred_element_type=F32)
              + lax.dot_general(rwh_ref[...], xm, _DOT_NT, preferred_element_type=F32)) + rb_ref[...]
    expert = lax.broadcasted_iota(jnp.int32, logits.shape, 0)
    vals, hits = [], []
    for _ in range(TOP_K):
        m = jnp.max(logits, axis=0, keepdims=True)
        sel = jnp.min(jnp.where(logits == m, expert, N_EXPERTS), axis=0, keepdims=True)
        vals.append(m)
        hits.append(expert == sel)
        logits = jnp.where(expert == sel, -jnp.inf, logits)
    es = [jnp.exp(v - vals[0]) for v in vals]
    den = es[0] + es[1] + es[2] + es[3]
    weights = [e / den for e in es]

    routed = sum(h.astype(F32) for h in hits)
    counts = jnp.sum(routed, axis=1, keepdims=True)
    cnt_ref[0] = counts.astype(jnp.int32)
    padded = _ceil_to(counts.astype(jnp.int32), MOE_GROUP).astype(F32)
    g0 = jnp.dot(low_ref[...], jnp.broadcast_to(padded, (N_EXPERTS, LANES)),
                 preferred_element_type=F32, precision=HIGHEST)[:, 0:1]
    earlier = jnp.dot(routed.astype(BF16), tri_ref[...], preferred_element_type=F32)
    row_of = g0 + earlier
    slots = [jnp.sum(jnp.where(h, row_of, 0.0), axis=0, keepdims=True) for h in hits]

    prow = lax.broadcasted_iota(jnp.int32, (LANES, tm), 0)
    packed = jnp.zeros((LANES, tm), F32)
    for k, vec in enumerate(slots + weights):
        packed = jnp.where(prow == k, vec, packed)
    sw_ref[...] = packed.T

    group_row = lax.broadcasted_iota(jnp.int32, (MOE_GROUPED_ROWS, tm), 0).astype(F32)
    perm = jnp.zeros((MOE_GROUPED_ROWS, tm), F32)
    for k in range(TOP_K):
        perm = jnp.where(group_row == slots[k], 1.0, perm)
    perm = perm.astype(BF16)
    xg_ref[...] = jnp.dot(perm, xh, preferred_element_type=F32).astype(BF16)


def _post_even_kernel(xc_ref, xl_ref, ac_ref, al_ref, tc_ref, tl_ref, *refs, tm):
    is_ctx = pl.program_id(0) * tm < N_CTX_TOK
    mix = jnp.concatenate([jnp.where(is_ctx, ac_ref[...], al_ref[...]),
                           jnp.where(is_ctx, tc_ref[...], tl_ref[...])], axis=1)
    _post_mixer(jnp.where(is_ctx, xc_ref[...], xl_ref[...]), mix, refs, tm)


def _post_odd_kernel(x_ref, yc_ref, yl_ref, *refs, tm):
    mix = jnp.where(pl.program_id(0) * tm < N_CTX_TOK, yc_ref[...], yl_ref[...])
    _post_mixer(x_ref[...], mix, refs, tm)


def _post_call(kernel_fn, name, mixer_specs, mixer_args, k_in, w_out, gain, mod, router_w, router_b):
    tm = ROW_TILE
    nsteps = N_TOK // tm
    row = lax.broadcasted_iota(jnp.int32, (tm, tm), 0)
    col = lax.broadcasted_iota(jnp.int32, (tm, tm), 1)
    tri = (row < col).astype(BF16)
    er = lax.broadcasted_iota(jnp.int32, (N_EXPERTS, N_EXPERTS), 0)
    ec = lax.broadcasted_iota(jnp.int32, (N_EXPERTS, N_EXPERTS), 1)
    low = (ec < er).astype(F32)
    rw_t = router_w.T
    rw_hi = rw_t.astype(BF16)
    rw_mid = (rw_t - rw_hi.astype(F32)).astype(BF16)
    const = lambda shape: pl.BlockSpec(shape, lambda i: (0,) * len(shape))
    return pl.pallas_call(
        functools.partial(kernel_fn, tm=tm),
        grid=(nsteps,),
        in_specs=mixer_specs + [
            const((k_in, D_MODEL)),
            const((1, D_MODEL)),
            pl.BlockSpec((1, 6, D_MODEL), lambda i: (_cond_of_tile(i, tm), 0, 0)),
            const((N_EXPERTS, D_MODEL)), const((N_EXPERTS, D_MODEL)), const((N_EXPERTS, 1)),
            const((tm, tm)), const((N_EXPERTS, N_EXPERTS)),
        ],
        out_specs=[
            pl.BlockSpec((tm, D_MODEL), lambda i: (i, 0)),
            pl.BlockSpec((tm, LANES), lambda i: (i, 0)),
            pl.BlockSpec((1, N_EXPERTS, 1), lambda i: (i, 0, 0)),
            pl.BlockSpec((MOE_GROUPED_ROWS, D_MODEL), lambda i: (i, 0)),
        ],
        out_shape=[
            jax.ShapeDtypeStruct((N_TOK, D_MODEL), F32),
            jax.ShapeDtypeStruct((N_TOK, LANES), F32),
            jax.ShapeDtypeStruct((nsteps, N_EXPERTS, 1), jnp.int32),
            jax.ShapeDtypeStruct((nsteps * MOE_GROUPED_ROWS, D_MODEL), BF16),
        ],
        scratch_shapes=[pltpu.VMEM((k_in, D_MODEL), BF16)],
        compiler_params=_vmem(56),
        name=name,
    )(*mixer_args, w_out, gain.reshape(1, D_MODEL), mod, rw_hi, rw_mid, router_b.reshape(N_EXPERTS, 1), tri, low)


def _post_even(x_ctx, x_lat, a_ctx, a_lat, t_ctx, t_lat, w_out, gain, mod, router_w, router_b):
    tm = ROW_TILE
    nctx = N_CTX_TOK // tm
    ctx_map = lambda i: (jnp.minimum(i, nctx - 1), 0)
    lat_map = lambda i: (jnp.maximum(i - nctx, 0), 0)
    specs = [pl.BlockSpec((tm, D_MODEL), ctx_map), pl.BlockSpec((tm, D_MODEL), lat_map),
             pl.BlockSpec((tm, HY_WIDTH), ctx_map), pl.BlockSpec((tm, HY_WIDTH), lat_map),
             pl.BlockSpec((tm, ATT_WIDTH), ctx_map), pl.BlockSpec((tm, ATT_WIDTH), lat_map)]
    return _post_call(_post_even_kernel, "post_even", specs, (x_ctx, x_lat, a_ctx, a_lat, t_ctx, t_lat),
                      HY_WIDTH + ATT_WIDTH, w_out, gain, mod, router_w, router_b)


def _post_odd(x, y_ctx, y_lat, w_out, gain, mod, router_w, router_b):
    tm = ROW_TILE
    nctx = N_CTX_TOK // tm
    specs = [pl.BlockSpec((tm, D_MODEL), lambda i: (i, 0)),
             pl.BlockSpec((tm, RET_V_WIDTH), lambda i: (jnp.minimum(i, nctx - 1), 0)),
             pl.BlockSpec((tm, RET_V_WIDTH), lambda i: (jnp.maximum(i - nctx, 0), 0))]
    return _post_call(_post_odd_kernel, "post_odd", specs, (x, y_ctx, y_lat),
                      RET_V_WIDTH, w_out, gain, mod, router_w, router_b)


def _moe_plan(counts):
    t, grp, big = MOE_TILE, MOE_GROUP, MOE_GROUPED_ROWS
    n16 = _ceil_to(counts.reshape(-1, N_EXPERTS), grp)
    nsteps = n16.shape[0]
    g0 = jnp.cumsum(n16, axis=1) - n16
    e0 = jnp.cumsum(n16, axis=0) - n16
    rows_e = jnp.sum(n16, axis=0)
    ntiles = (rows_e + t - 1) // t
    tile_end = jnp.cumsum(ntiles)
    tile_first = tile_end - ntiles
    total = tile_end[-1]
    experts = jnp.arange(N_EXPERTS, dtype=jnp.int32)
    tiles = jnp.arange(MOE_NUM_TILES, dtype=jnp.int32)
    valid = tiles < total
    ti = jnp.minimum(tiles, total - 1)
    e_of = jnp.sum((tile_end[None, :] <= ti[:, None]).astype(jnp.int32), axis=1)
    onehot = (e_of[:, None] == experts[None, :]).astype(jnp.int32)
    pick = lambda v: jnp.sum(onehot * v[None, :], axis=1)
    pick2 = lambda m: jnp.sum(onehot[:, None, :] * m[None, :, :], axis=2)
    r = ((ti - pick(tile_first)) * t)[:, None] + grp * jnp.arange(t // grp, dtype=jnp.int32)[None, :]
    ends = pick2(e0 + n16)
    step = jnp.sum((ends[:, None, :] <= r[:, :, None]).astype(jnp.int32), axis=2)
    step = jnp.minimum(step, nsteps - 1)
    sel = (step[:, :, None] == jnp.arange(nsteps, dtype=jnp.int32)[None, None, :]).astype(jnp.int32)
    at_step = lambda m: jnp.sum(sel * pick2(m)[:, None, :], axis=2)
    src = step * big + at_step(g0) + r - at_step(e0)
    live = jnp.logical_and(valid[:, None], r < pick(rows_e)[:, None])
    moe_src = jnp.where(live, src, 0).reshape(-1)
    g = grp * jnp.arange(big // grp, dtype=jnp.int32)
    gend = g0 + n16
    ce = jnp.sum((gend[:, None, :] <= g[None, :, None]).astype(jnp.int32), axis=2)
    used = ce < N_EXPERTS
    ce = jnp.minimum(ce, N_EXPERTS - 1)
    csel = (ce[:, :, None] == experts[None, None, :]).astype(jnp.int32)
    of_e = lambda m: jnp.sum(csel * m[:, None, :], axis=2)
    base = jnp.sum(csel * (tile_first * t)[None, None, :], axis=2)
    csrc = base + of_e(e0) + g[None, :] - of_e(g0)
    comb_src = jnp.where(used, csrc, 0).reshape(-1)
    used_e = ntiles > 0
    rank = jnp.cumsum(used_e.astype(jnp.int32)) - 1
    later = jnp.logical_and(used_e[None, :], experts[None, :] > experts[:, None])
    next_e = jnp.min(jnp.where(later, experts[None, :], N_EXPERTS), axis=1)
    next_e = jnp.where(next_e == N_EXPERTS, -1, next_e)
    wbuf = jnp.stack([pick(rank) % 2, pick(next_e)], axis=1).reshape(-1)
    as_i32 = lambda v: v.astype(jnp.int32)
    return as_i32(e_of), as_i32(valid), as_i32(wbuf), as_i32(moe_src), as_i32(comb_src)


def _moe_kernel(te_ref, tv_ref, nx_ref, src_ref, xg_hbm, w1_hbm, b1_ref, w2_hbm, b2_ref, y_ref,
                xbuf, w1f, w2f, w1b, w2b, sem_in, sem_w):
    i = pl.program_id(0)
    nt = pl.num_programs(0)
    t, grp = MOE_TILE, MOE_GROUP
    slot = i % 2

    def issue_gather(tile, sl):
        for c in range(t // grp):
            src = pl.multiple_of(src_ref[tile * (t // grp) + c], grp)
            pltpu.make_async_copy(xg_hbm.at[pl.ds(src, grp), :], xbuf.at[sl, pl.ds(c * grp, grp), :],
                                  sem_in.at[sl]).start()

    def weight_copies(e, ws):
        return (pltpu.make_async_copy(w1_hbm.at[e], w1f.at[ws], sem_w.at[ws]),
                pltpu.make_async_copy(w2_hbm.at[e], w2f.at[ws], sem_w.at[ws]))

    def valid(tile):
        return tv_ref[jnp.clip(tile, 0, nt - 1)] > 0

    @pl.when(i == 0)
    def _():
        issue_gather(0, 0)
        for cp in weight_copies(te_ref[0], 0):
            cp.start()

    @pl.when(valid(i))
    def _():
        pltpu.make_async_copy(xg_hbm.at[pl.ds(0, t), :], xbuf.at[slot], sem_in.at[slot]).wait()

        @pl.when(jnp.logical_and(i + 1 < nt, valid(i + 1)))
        def _():
            issue_gather(i + 1, 1 - slot)

        e = te_ref[i]
        first = jnp.logical_or(i == 0, e != te_ref[jnp.maximum(i - 1, 0)])
        ws = nx_ref[2 * i]
        nxt = nx_ref[2 * i + 1]

        @pl.when(first)
        def _():
            for cp in weight_copies(e, ws):
                cp.wait()
            w1b[...] = w1f[ws].astype(BF16)
            w2b[...] = w2f[ws].astype(BF16)

            @pl.when(nxt >= 0)
            def _():
                for cp in weight_copies(nxt, 1 - ws):
                    cp.start()

        x = xbuf[slot]
        b1 = b1_ref[0]
        y = b2_ref[0]
        for j in range(D_FF // MOE_FF_CHUNK):
            gcols = slice(j * MOE_FF_CHUNK, (j + 1) * MOE_FF_CHUNK)
            lcols = slice(D_FF + j * MOE_FF_CHUNK, D_FF + (j + 1) * MOE_FF_CHUNK)
            glu = jnp.dot(x, w1b[:, gcols], preferred_element_type=F32) + b1[:, gcols]
            lin = jnp.dot(x, w1b[:, lcols], preferred_element_type=F32) + b1[:, lcols]
            glu = jnp.minimum(glu, SWIGLU_LIMIT)
            lin = jnp.clip(lin, -SWIGLU_LIMIT, SWIGLU_LIMIT)
            act = (glu * _sigmoid(SWIGLU_ALPHA * glu) * (lin + 1.0)).astype(BF16)
            y = y + jnp.dot(act, w2b[gcols, :], preferred_element_type=F32)
        y_ref[...] = y.astype(y_ref.dtype)

    @pl.when(jnp.logical_not(valid(i)))
    def _():
        y_ref[...] = jnp.zeros_like(y_ref)


def _moe_experts(xg, plan, w1, b1, w2, b2):
    te, tv, nx, src, _ = plan
    t = MOE_TILE
    return pl.pallas_call(
        _moe_kernel,
        grid_spec=pltpu.PrefetchScalarGridSpec(
            num_scalar_prefetch=4,
            grid=(MOE_NUM_TILES,),
            in_specs=[
                pl.BlockSpec(memory_space=pl.ANY),
                pl.BlockSpec(memory_space=pl.ANY),
                pl.BlockSpec((1, 1, 2 * D_FF), lambda i, te, *_: (te[i], 0, 0)),
                pl.BlockSpec(memory_space=pl.ANY),
                pl.BlockSpec((1, 1, D_MODEL), lambda i, te, *_: (te[i], 0, 0)),
            ],
            out_specs=pl.BlockSpec((t, D_MODEL), lambda i, *_: (i, 0)),
            scratch_shapes=[
                pltpu.VMEM((2, t, D_MODEL), BF16),
                pltpu.VMEM((2, D_MODEL, 2 * D_FF), F32),
                pltpu.VMEM((2, D_FF, D_MODEL), F32),
                pltpu.VMEM((D_MODEL, 2 * D_FF), BF16),
                pltpu.VMEM((D_FF, D_MODEL), BF16),
                pltpu.SemaphoreType.DMA((2,)),
                pltpu.SemaphoreType.DMA((2,)),
            ],
        ),
        out_shape=jax.ShapeDtypeStruct((MOE_NUM_TILES * t, D_MODEL), BF16),
        compiler_params=_vmem(56),
        name="moe_experts",
    )(te, tv, nx, src, xg, w1, b1.reshape(N_EXPERTS, 1, 2 * D_FF), w2, b2.reshape(N_EXPERTS, 1, D_MODEL))


def _combine_value(src_ref, x1_ref, sw_ref, mod_ref, ys_hbm, ybuf, sem, tm):
    s = pl.program_id(0)
    nsteps = pl.num_programs(0)
    slot = s % 2
    grp, big = MOE_GROUP, MOE_GROUPED_ROWS
    nchunk = big // grp

    def issue_gather(step, sl):
        for c in range(nchunk):
            src = pl.multiple_of(src_ref[step * nchunk + c], grp)
            pltpu.make_async_copy(ys_hbm.at[pl.ds(src, grp), :], ybuf.at[sl, pl.ds(c * grp, grp), :],
                                  sem.at[sl]).start()

    @pl.when(s == 0)
    def _():
        issue_gather(0, 0)

    pltpu.make_async_copy(ys_hbm.at[pl.ds(0, big), :], ybuf.at[slot], sem.at[slot]).wait()

    @pl.when(s + 1 < nsteps)
    def _():
        issue_gather(s + 1, 1 - slot)

    sw = sw_ref[...]
    col = lax.broadcasted_iota(jnp.int32, (tm, big), 1).astype(F32)
    wmat = jnp.zeros((tm, big), F32)
    for k in range(TOP_K):
        wmat = jnp.where(col == sw[:, k:k + 1], sw[:, TOP_K + k:TOP_K + k + 1], wmat)
    moe = jnp.dot(wmat.astype(BF16), ybuf[slot], preferred_element_type=F32)
    return x1_ref[...] + mod_ref[0, 5:6, :] * moe


def _combine_kernel(src_ref, x1_ref, sw_ref, mod_ref, ys_hbm, o_ref, ybuf, sem, *, tm):
    o_ref[...] = _combine_value(src_ref, x1_ref, sw_ref, mod_ref, ys_hbm, ybuf, sem, tm)


def _combine_split_kernel(src_ref, x1_ref, sw_ref, mod_ref, ys_hbm, oc_ref, ol_ref, ybuf, sem, *, tm):
    val = _combine_value(src_ref, x1_ref, sw_ref, mod_ref, ys_hbm, ybuf, sem, tm)
    is_ctx = pl.program_id(0) * tm < N_CTX_TOK

    @pl.when(is_ctx)
    def _():
        oc_ref[...] = val

    @pl.when(jnp.logical_not(is_ctx))
    def _():
        ol_ref[...] = val


def _combine(x1, ys, sw, plan, mod, split):
    tm = ROW_TILE
    nctx = N_CTX_TOK // tm
    in_specs = [
        pl.BlockSpec((tm, D_MODEL), lambda i, *_: (i, 0)),
        pl.BlockSpec((tm, LANES), lambda i, *_: (i, 0)),
        pl.BlockSpec((1, 6, D_MODEL), lambda i, *_: (_cond_of_tile(i, tm), 0, 0)),
        pl.BlockSpec(memory_space=pl.ANY),
    ]
    scratch = [pltpu.VMEM((2, MOE_GROUPED_ROWS, D_MODEL), BF16), pltpu.SemaphoreType.DMA((2,))]
    if not split:
        kernel_fn, name = _combine_kernel, "moe_combine"
        out_specs = pl.BlockSpec((tm, D_MODEL), lambda i, *_: (i, 0))
        out_shape = jax.ShapeDtypeStruct((N_TOK, D_MODEL), F32)
    else:
        kernel_fn, name = _combine_split_kernel, "moe_combine_split"
        out_specs = [pl.BlockSpec((tm, D_MODEL), lambda i, *_: (jnp.minimum(i, nctx - 1), 0)),
                     pl.BlockSpec((tm, D_MODEL), lambda i, *_: (jnp.maximum(i - nctx, 0), 0))]
        out_shape = [jax.ShapeDtypeStruct((N_CTX_TOK, D_MODEL), F32),
                     jax.ShapeDtypeStruct((N_TOK - N_CTX_TOK, D_MODEL), F32)]
    return pl.pallas_call(
        functools.partial(kernel_fn, tm=tm),
        grid_spec=pltpu.PrefetchScalarGridSpec(
            num_scalar_prefetch=1, grid=(N_TOK // tm,), in_specs=in_specs, out_specs=out_specs,
            scratch_shapes=scratch),
        out_shape=out_shape,
        compiler_params=_vmem(56),
        name=name,
    )(plan[4], x1, sw, mod, ys)


def _retention_kernel(lg_ref, q_ref, k_ref, v_ref, gf_ref, gb_ref, *rest, length, has_s0, emit_state):
    rest = list(rest)
    s0_ref = rest.pop(0) if has_s0 else None
    o_ref = rest.pop(0)
    so_ref = rest.pop(0) if emit_state else None
    s_ref, yf_ref, yb_ref = rest
    c = RET_CHUNK
    nc = length // c
    ii = lax.broadcasted_iota(jnp.int32, (c, c), 0).astype(F32)
    jj = lax.broadcasted_iota(jnp.int32, (c, c), 1).astype(F32)
    ci = lax.broadcasted_iota(jnp.int32, (c, 1), 0).astype(F32)

    def decays(direction):
        lg = -jnp.exp(lg_ref[direction, 0])
        lg1 = lg[:, 0:1]
        if direction == 0:
            diff = ii - jj
            q_decay = jnp.exp(lg1 * (ci + 1.0))
            k_decay = jnp.exp(lg1 * (c - 1.0 - ci))
        else:
            diff = jj - ii
            q_decay = jnp.exp(lg1 * (c - ci))
            k_decay = jnp.exp(lg1 * ci)
        scale = RET_DK ** -0.5
        inner = jnp.where(diff >= 0, jnp.exp(lg * jnp.maximum(diff, 0.0)), 0.0) * scale
        return inner, q_decay, k_decay * scale, jnp.exp(lg1 * float(c))

    def chunk(direction, ch, consts, g_ref, y_ref):
        inner, q_decay, k_decay, chunk_decay = consts
        rows = pl.ds(pl.multiple_of(ch * c, c), c)
        qc = q_ref[rows, :]
        kc = k_ref[rows, :]
        vc = v_ref[rows, :]
        s = s_ref[direction]
        att = lax.dot_general(qc, kc, _DOT_NT, preferred_element_type=F32) * inner
        o = (jnp.dot(att.astype(BF16), vc, preferred_element_type=F32)
             + jnp.dot(qc, s.astype(BF16), preferred_element_type=F32) * q_decay)
        kd = (kc.astype(F32) * k_decay).T.astype(BF16)
        s_ref[direction] = s * chunk_decay + jnp.dot(kd, vc, preferred_element_type=F32)
        on = o * lax.rsqrt(jnp.mean(o * o, axis=-1, keepdims=True) + EPS)
        g = g_ref[rows, :].astype(F32)
        y_ref[rows, :] = g * _sigmoid(g) * on

    for direction in range(2):
        if has_s0:
            s_ref[direction] = s0_ref[0, direction, 0]
        else:
            s_ref[direction] = jnp.zeros((RET_DK, RET_DV), F32)
    forward, backward = decays(0), decays(1)

    def body(step, carry):
        chunk(0, step, forward, gf_ref, yf_ref)
        chunk(1, nc - 1 - step, backward, gb_ref, yb_ref)
        return carry

    lax.fori_loop(0, nc, body, 0)
    o_ref[...] = (yf_ref[...] + yb_ref[...]).astype(o_ref.dtype)
    if emit_state:
        for direction in range(2):
            so_ref[0, direction, 0] = s_ref[direction]


def _retention(u, first_seq, nseq, length, decay_logit, s0, emit_state):
    row0 = first_seq
    lg = jnp.broadcast_to(decay_logit.astype(F32)[:, :, None, None], (2, RET_HEADS, 1, LANES))
    kcol = RET_QK_WIDTH // RET_DK
    vcol = 2 * RET_QK_WIDTH // RET_DV
    gfcol = vcol + RET_HEADS
    gbcol = gfcol + RET_HEADS
    in_specs = [
        pl.BlockSpec((2, 1, 1, LANES), lambda b, h: (0, h, 0, 0)),
        pl.BlockSpec((length, RET_DK), lambda b, h: (row0 + b, h)),
        pl.BlockSpec((length, RET_DK), lambda b, h: (row0 + b, kcol + h)),
        pl.BlockSpec((length, RET_DV), lambda b, h: (row0 + b, vcol + h)),
        pl.BlockSpec((length, RET_DV), lambda b, h: (row0 + b, gfcol + h)),
        pl.BlockSpec((length, RET_DV), lambda b, h: (row0 + b, gbcol + h)),
    ]
    args = [lg, u, u, u, u, u]
    state_spec = pl.BlockSpec((1, 2, 1, RET_DK, RET_DV), lambda b, h: (b, 0, h, 0, 0))
    if s0 is not None:
        in_specs.append(state_spec)
        args.append(s0)
    out_specs = [pl.BlockSpec((length, RET_DV), lambda b, h: (b, h))]
    out_shape = [jax.ShapeDtypeStruct((nseq * length, RET_V_WIDTH), BF16)]
    if emit_state:
        out_specs.append(state_spec)
        out_shape.append(jax.ShapeDtypeStruct((nseq, 2, RET_HEADS, RET_DK, RET_DV), F32))
    return pl.pallas_call(
        functools.partial(_retention_kernel, length=length, has_s0=s0 is not None, emit_state=emit_state),
        grid=(nseq, RET_HEADS),
        in_specs=in_specs,
        out_specs=out_specs,
        out_shape=out_shape,
        scratch_shapes=[pltpu.VMEM((2, RET_DK, RET_DV), F32), pltpu.VMEM((length, RET_DV), F32),
                        pltpu.VMEM((length, RET_DV), F32)],
        compiler_params=_vmem(48),
        name=f"retention_{length}",
    )(*args)


def kernel(x_prompt, x_sample, cache_k0, cache_v0, state_ret1, c, c_ctx, l0_norm_mix, l0_ada_w, l0_ada_b, l0_w_in, l0_conv_w, l0_conv_b, l0_filt_w1, l0_filt_b1, l0_filt_freq, l0_filt_w2, l0_filt_b2, l0_filt_w3, l0_filt_deltas, l0_hy_skip, l0_q_gain, l0_k_gain, l0_sink, l0_w_out, l0_norm_ffn, l0_router_w, l0_router_b, l0_moe_w1, l0_moe_b1, l0_moe_w2, l0_moe_b2, l1_norm_mix, l1_ada_w, l1_ada_b, l1_w_in, l1_ret_decay_logit, l1_w_out, l1_norm_ffn, l1_router_w, l1_router_b, l1_moe_w1, l1_moe_b1, l1_moe_w2, l1_moe_b2):
    x_ctx = x_prompt.reshape(N_CTX_TOK, D_MODEL)
    x_lat = x_sample.reshape(N_TOK - N_CTX_TOK, D_MODEL)
    cond = jnp.zeros((SUBLANES, D_MODEL), F32).at[0].set(c_ctx).at[1:1 + N_LAT_SEQ].set(c)
    mod0 = _adaln(cond, l0_ada_w, l0_ada_b)
    mod1 = _adaln(cond, l1_ada_w, l1_ada_b)

    u = _in_proj((x_ctx, x_lat), l0_norm_mix, mod0, l0_w_in, EVEN_IN // 2, F32)
    filt = (l0_filt_w1, l0_filt_b1, l0_filt_freq, l0_filt_w2, l0_filt_b2, l0_filt_w3, l0_filt_deltas)
    hy = []
    for first_block, nseq, length in ((0, N_CTX_SEQ, L_CTX), (N_CTX_TOK // L_LAT // N_LAT_SEQ, N_LAT_SEQ, L_LAT)):
        cmat, smat, stmat = _dft_matrices(length)
        tc, ts = _hyena_filter(length, _filter_features(length), cmat, smat, *filt)
        hy.append(_hyena(u, first_block, nseq, length, l0_conv_w, l0_conv_b, l0_hy_skip, tc, ts, cmat, smat, stmat))
    head = lax.broadcasted_iota(jnp.int32, (ATT_WIDTH, ATT_WIDTH), 0) // HEAD_DIM
    seg = (head == head.T).astype(BF16)
    att_ctx, new_k, new_v = _ctx_attention(u, seg, l0_q_gain, l0_k_gain, l0_sink)
    att_lat = _lat_attention(u, cache_k0, cache_v0, seg, l0_q_gain, l0_k_gain, l0_sink)
    x1, sw, counts, xg = _post_even(x_ctx, x_lat, hy[0], hy[1], att_ctx, att_lat, l0_w_out, l0_norm_ffn, mod0,
                                    l0_router_w, l0_router_b)
    plan = _moe_plan(counts)
    ys = _moe_experts(xg, plan, l0_moe_w1, l0_moe_b1, l0_moe_w2, l0_moe_b2)
    x = _combine(x1, ys, sw, plan, mod0, split=False)

    u = _in_proj((x,), l1_norm_mix, mod1, l1_w_in, 2048, BF16)
    y_ctx, new_state = _retention(u, 0, N_CTX_SEQ, L_CTX, l1_ret_decay_logit, None, True)
    (y_lat,) = _retention(u, N_CTX_TOK // L_LAT, N_LAT_SEQ, L_LAT, l1_ret_decay_logit, state_ret1, False)
    x1, sw, counts, xg = _post_odd(x, y_ctx, y_lat, l1_w_out, l1_norm_ffn, mod1, l1_router_w, l1_router_b)
    plan = _moe_plan(counts)
    ys = _moe_experts(xg, plan, l1_moe_w1, l1_moe_b1, l1_moe_w2, l1_moe_b2)
    y_prompt, y_sample = _combine(x1, ys, sw, plan, mod1, split=True)

    return (y_prompt.reshape(N_CTX_SEQ, L_CTX, D_MODEL), y_sample.reshape(N_LAT_SEQ, L_LAT, D_MODEL),
            new_k.reshape(N_CTX_SEQ, L_CTX, ATT_KV_HEADS, HEAD_DIM),
            new_v.reshape(N_CTX_SEQ, L_CTX, ATT_KV_HEADS, HEAD_DIM), new_state)
```

```python
import functools
import math

import jax
import jax.numpy as jnp
from jax import lax
from jax.experimental import pallas as pl
from jax.experimental.pallas import tpu as pltpu

F32 = jnp.float32
BF16 = jnp.bfloat16
HIGHEST = lax.Precision.HIGHEST

D_MODEL = 1024
N_CTX_SEQ, L_CTX = 16, 256
N_LAT_SEQ, L_LAT = 2, 2048
N_CTX_TOK = N_CTX_SEQ * L_CTX
N_TOK = N_CTX_TOK + N_LAT_SEQ * L_LAT
PAST_LEN = 512
EPS = 1e-6
NEG_BIG = -1e30

HY_WIDTH = 512
HY_BANDS = 16
HY_FILTER_HIDDEN = 64
HY_FEAT_PAD = 64

ATT_HEADS, ATT_KV_HEADS, HEAD_DIM = 8, 2, 64
ATT_GROUP = ATT_HEADS // ATT_KV_HEADS
ATT_WIDTH = ATT_HEADS * HEAD_DIM
KV_WIDTH = ATT_KV_HEADS * HEAD_DIM
ATT_SCALE = HEAD_DIM ** -0.5
WINDOW = 128
ATT_BLOCK = 128
ROPE_THETA = 10000.0
ROPE_FREQS = HEAD_DIM // 4
GRID_W = 64
EVEN_IN = 3 * HY_WIDTH + ATT_WIDTH + 2 * KV_WIDTH

RET_HEADS = 4
RET_DK = 256
RET_DV = 512
RET_CHUNK = 128
RET_QK_WIDTH = RET_HEADS * RET_DK
RET_V_WIDTH = RET_HEADS * RET_DV
ODD_IN = 2 * RET_QK_WIDTH + 3 * RET_V_WIDTH

N_EXPERTS = 32
TOP_K = 4
D_FF = 1024
SWIGLU_ALPHA = 1.702
SWIGLU_LIMIT = 7.0

SUBLANES = 8
LANES = 128

MOE_TILE = 512
MOE_GROUP = 16
ROW_TILE = 512
MOE_GROUPED_ROWS = 2560
MOE_NUM_TILES = 112


def _vmem(mib):
    return pltpu.CompilerParams(vmem_limit_bytes=mib * 1024 * 1024)


def _cond_of_tile(i, tm):
    row = i * tm
    return jnp.where(row < N_CTX_TOK, 0, 1 + (row - N_CTX_TOK) // L_LAT)


def _sigmoid(x):
    return 1.0 / (1.0 + jnp.exp(-x))


def _norm_mod(x, gain, shift, scale):
    ms = jnp.mean(x * x, axis=-1, keepdims=True)
    return (x * lax.rsqrt(ms + EPS) * gain) * (1.0 + scale) + shift


def _adaln_kernel(c_ref, w_ref, b_ref, o_ref):
    c = c_ref[...]
    s = c * _sigmoid(c)
    o_ref[...] = jnp.dot(s, w_ref[...], preferred_element_type=F32, precision=HIGHEST) + b_ref[...]


def _adaln(cond, w, b):
    n = w.shape[1]
    tn = 1024
    out = pl.pallas_call(
        _adaln_kernel,
        grid=(n // tn,),
        in_specs=[
            pl.BlockSpec((SUBLANES, D_MODEL), lambda j: (0, 0)),
            pl.BlockSpec((D_MODEL, tn), lambda j: (0, j)),
            pl.BlockSpec((1, tn), lambda j: (0, j)),
        ],
        out_specs=pl.BlockSpec((SUBLANES, tn), lambda j: (0, j)),
        out_shape=jax.ShapeDtypeStruct((SUBLANES, n), F32),
        name="adaln",
    )(cond, w, b.reshape(1, n))
    return out.reshape(SUBLANES, 6, D_MODEL)


def _in_proj_kernel(*refs, tm):
    *x_refs, gain_ref, mod_ref, w_ref, o_ref, wb_ref = refs
    i = pl.program_id(1)

    @pl.when(i == 0)
    def _():
        wb_ref[...] = w_ref[...].astype(BF16)

    if len(x_refs) == 2:
        x = jnp.where(i * tm < N_CTX_TOK, x_refs[0][...], x_refs[1][...])
    else:
        x = x_refs[0][...]
    h = _norm_mod(x, gain_ref[...], mod_ref[0, 0:1, :], mod_ref[0, 1:2, :])
    o_ref[...] = jnp.dot(h.astype(BF16), wb_ref[...], preferred_element_type=F32).astype(o_ref.dtype)


def _in_proj(xs, gain, mod, w, tn, out_dtype):
    n = w.shape[1]
    tm = ROW_TILE
    nctx = N_CTX_TOK // tm
    if len(xs) == 2:
        x_specs = [pl.BlockSpec((tm, D_MODEL), lambda j, i: (jnp.minimum(i, nctx - 1), 0)),
                   pl.BlockSpec((tm, D_MODEL), lambda j, i: (jnp.maximum(i - nctx, 0), 0))]
    else:
        x_specs = [pl.BlockSpec((tm, D_MODEL), lambda j, i: (i, 0))]
    return pl.pallas_call(
        functools.partial(_in_proj_kernel, tm=tm),
        grid=(n // tn, N_TOK // tm),
        in_specs=x_specs + [
            pl.BlockSpec((1, D_MODEL), lambda j, i: (0, 0)),
            pl.BlockSpec((1, 6, D_MODEL), lambda j, i: (_cond_of_tile(i, tm), 0, 0)),
            pl.BlockSpec((D_MODEL, tn), lambda j, i: (0, j)),
        ],
        out_specs=pl.BlockSpec((tm, tn), lambda j, i: (i, j)),
        out_shape=jax.ShapeDtypeStruct((N_TOK, n), out_dtype),
        scratch_shapes=[pltpu.VMEM((D_MODEL, tn), BF16)],
        compiler_params=_vmem(48),
        name="in_proj",
    )(*xs, gain.reshape(1, D_MODEL), mod, w)


def _dft_matrices(length):
    n = 2 * length
    lo = 16
    s = jnp.arange(length, dtype=jnp.int32)
    k1 = jnp.arange(length // lo, dtype=jnp.int32) * lo
    k0 = jnp.arange(lo, dtype=jnp.int32)
    ang1 = (2.0 * math.pi / n) * ((k1[:, None] * s[None, :]) % n).astype(F32)
    ang0 = (2.0 * math.pi / n) * ((k0[:, None] * s[None, :]) % n).astype(F32)
    c1, s1 = jnp.cos(ang1)[:, None, :], jnp.sin(ang1)[:, None, :]
    c0, s0 = jnp.cos(ang0)[None, :, :], jnp.sin(ang0)[None, :, :]
    cmat = (c1 * c0 - s1 * s0).reshape(length, length)
    smat = (s1 * c0 + c1 * s0).reshape(length, length)
    sign = jnp.where(s % 2 == 0, 1.0, -1.0).astype(F32)
    row = lax.broadcasted_iota(jnp.int32, (length, length), 0)
    col = lax.broadcasted_iota(jnp.int32, (length, length), 1)
    s_nyq = jnp.where(row == 0, sign[None, :], smat)
    st_nyq = jnp.where(col == 0, sign[:, None], smat)
    return cmat.astype(BF16), s_nyq.astype(BF16), st_nyq.astype(BF16)


def _filter_features(length):
    t = jnp.linspace(0.0, 1.0, length, dtype=F32)[:, None]
    w = 2.0 * math.pi * jnp.arange(length, dtype=F32)[:, None] / length
    f = jnp.linspace(1e-4, HY_BANDS - 1, HY_BANDS, dtype=F32)[None, :]
    z = jnp.concatenate([t, jnp.cos(f * w), -jnp.sin(f * w)], axis=-1)
    return jnp.pad(z, ((0, 0), (0, HY_FEAT_PAD - z.shape[1])))


def _filter_kernel(z_ref, w1_ref, b1_ref, fr_ref, w2_ref, b2_ref, w3_ref, dl_ref, c_ref, s_ref,
                   tc_ref, ts_ref, taps_ref, *, length, rb):
    r = pl.program_id(0)

    @pl.when(r == 0)
    def _():
        z = z_ref[...]
        fr = fr_ref[...]
        h = jnp.sin(fr * (jnp.dot(z, w1_ref[...], preferred_element_type=F32, precision=HIGHEST) + b1_ref[...]))
        h = jnp.sin(fr * (jnp.dot(h, w2_ref[...], preferred_element_type=F32, precision=HIGHEST) + b2_ref[...]))
        h = jnp.dot(h, w3_ref[...], preferred_element_type=F32, precision=HIGHEST)
        win = jnp.exp(-z[:, 0:1] * jnp.abs(dl_ref[...]))
        hf = h[:, :HY_WIDTH] * win
        hb = h[:, HY_WIDTH:] * win
        row = lax.broadcasted_iota(jnp.int32, (length, HY_WIDTH), 0)
        hb = jnp.where(row == 0, 0.0, hb)
        l1 = jnp.sum(jnp.abs(hf), axis=0, keepdims=True) + jnp.sum(jnp.abs(hb), axis=0, keepdims=True)
        inv = 1.0 / l1
        taps_ref[:, :HY_WIDTH] = (hf * inv).astype(BF16)
        taps_ref[:, HY_WIDTH:] = (hb * inv).astype(BF16)

    taps = taps_ref[...]
    rc = jnp.dot(c_ref[...], taps, preferred_element_type=F32)
    rs = jnp.dot(s_ref[...], taps, preferred_element_type=F32)
    tc = rc[:, :HY_WIDTH] + rc[:, HY_WIDTH:]
    ts = rs[:, :HY_WIDTH] - rs[:, HY_WIDTH:]
    grow = r * rb + lax.broadcasted_iota(jnp.int32, (rb, HY_WIDTH), 0)
    is0 = grow == 0
    ts = jnp.where(is0, rs[:, :HY_WIDTH] + rs[:, HY_WIDTH:], ts)
    wgt = jnp.where(is0, 1.0 / (2 * length), 2.0 / (2 * length))
    tc_ref[...] = tc * wgt
    ts_ref[...] = ts * wgt


def _hyena_filter(length, feats, cmat, smat, w1, b1, freq, w2, b2, w3, deltas):
    rb = min(length, 512)
    w1p = jnp.pad(w1, ((0, HY_FEAT_PAD - w1.shape[0]), (0, 0)))
    full = lambda shape: pl.BlockSpec(shape, lambda r: (0,) * len(shape))
    hid = HY_FILTER_HIDDEN
    return pl.pallas_call(
        functools.partial(_filter_kernel, length=length, rb=rb),
        grid=(length // rb,),
        in_specs=[
            full((length, HY_FEAT_PAD)), full((HY_FEAT_PAD, hid)), full((1, hid)), full((1, hid)),
            full((hid, hid)), full((1, hid)), full((hid, 2 * HY_WIDTH)), full((1, HY_WIDTH)),
            pl.BlockSpec((rb, length), lambda r: (r, 0)),
            pl.BlockSpec((rb, length), lambda r: (r, 0)),
        ],
        out_specs=[pl.BlockSpec((rb, HY_WIDTH), lambda r: (r, 0)),
                   pl.BlockSpec((rb, HY_WIDTH), lambda r: (r, 0))],
        out_shape=[jax.ShapeDtypeStruct((length, HY_WIDTH), F32)] * 2,
        scratch_shapes=[pltpu.VMEM((length, 2 * HY_WIDTH), BF16)],
        compiler_params=_vmem(48),
        name=f"hyena_filter_{length}",
    )(feats, w1p, b1.reshape(1, hid), freq.reshape(1, hid), w2, b2.reshape(1, hid), w3,
      deltas.reshape(1, HY_WIDTH), cmat, smat)


def _hyena_kernel(x0_ref, x1_ref, v_ref, w0_ref, w1_ref, w2_ref, b0_ref, b1_ref, b2_ref, skip_ref,
                  tc_ref, ts_ref, c_ref, s_ref, ct_ref, st_ref, o_ref,
                  zb_ref, zs_ref, x0c_ref, acc_ref, *, nseq, length, cb, fb):
    f = pl.program_id(1)
    nf = pl.num_programs(1)

    def short_conv(u, w_ref, b_ref):
        row = lax.broadcasted_iota(jnp.int32, u.shape, 0)
        prev = jnp.where(row == 0, 0.0, pltpu.roll(u, 1, 0))
        nxt = jnp.where(row == length - 1, 0.0, pltpu.roll(u, length - 1, 0))
        return prev * w_ref[0:1, :] + u * w_ref[1:2, :] + nxt * w_ref[2:3, :] + b_ref[...]

    @pl.when(f == 0)
    def _():
        for b in range(nseq):
            cols = slice(b * cb, (b + 1) * cb)
            x0c_ref[:, cols] = short_conv(x0_ref[b], w0_ref, b0_ref)
            z = short_conv(x1_ref[b], w1_ref, b1_ref) * short_conv(v_ref[b], w2_ref, b2_ref)
            zb_ref[:, cols] = z.astype(BF16)
            zs_ref[:, cols] = z * skip_ref[...]
        acc_ref[...] = jnp.zeros_like(acc_ref)

    zb = zb_ref[...]
    zc = jnp.dot(c_ref[...], zb, preferred_element_type=F32)
    zsn = jnp.dot(s_ref[...], zb, preferred_element_type=F32)
    tc = jnp.concatenate([tc_ref[...]] * nseq, axis=1)
    ts = jnp.concatenate([ts_ref[...]] * nseq, axis=1)
    grow = f * fb + lax.broadcasted_iota(jnp.int32, zc.shape, 0)
    is0 = grow == 0
    yc = jnp.where(is0, zc * tc, zc * tc - zsn * ts)
    ys = jnp.where(is0, zsn * ts, zc * ts + zsn * tc)
    acc_ref[...] += (jnp.dot(ct_ref[...], yc.astype(BF16), preferred_element_type=F32)
                     + jnp.dot(st_ref[...], ys.astype(BF16), preferred_element_type=F32))

    @pl.when(f == nf - 1)
    def _():
        for b in range(nseq):
            cols = slice(b * cb, (b + 1) * cb)
            o_ref[b] = (x0c_ref[:, cols] * (acc_ref[:, cols] + zs_ref[:, cols])).astype(o_ref.dtype)


def _hyena(u, first_seq_block, nseq, length, conv_w, conv_b, skip, tc, ts, cmat, smat, stmat):
    cb = 128
    fb = min(length, 512)
    u3 = u.reshape(N_TOK // length, length, EVEN_IN)
    ncb = HY_WIDTH // cb
    width = nseq * cb

    def ublock(part):
        return pl.BlockSpec((nseq, length, cb), lambda c, f: (first_seq_block, 0, part * ncb + c))

    def wblock(part, rows):
        return pl.BlockSpec((rows, cb), lambda c, f: (0, part * ncb + c))

    return pl.pallas_call(
        functools.partial(_hyena_kernel, nseq=nseq, length=length, cb=cb, fb=fb),
        grid=(ncb, length // fb),
        in_specs=[
            ublock(0), ublock(1), ublock(2),
            wblock(0, 3), wblock(1, 3), wblock(2, 3),
            wblock(0, 1), wblock(1, 1), wblock(2, 1),
            pl.BlockSpec((1, cb), lambda c, f: (0, c)),
            pl.BlockSpec((fb, cb), lambda c, f: (f, c)),
            pl.BlockSpec((fb, cb), lambda c, f: (f, c)),
            pl.BlockSpec((fb, length), lambda c, f: (f, 0)),
            pl.BlockSpec((fb, length), lambda c, f: (f, 0)),
            pl.BlockSpec((length, fb), lambda c, f: (0, f)),
            pl.BlockSpec((length, fb), lambda c, f: (0, f)),
        ],
        out_specs=pl.BlockSpec((nseq, length, cb), lambda c, f: (0, 0, c)),
        out_shape=jax.ShapeDtypeStruct((nseq, length, HY_WIDTH), BF16),
        scratch_shapes=[pltpu.VMEM((length, width), BF16), pltpu.VMEM((length, width), F32),
                        pltpu.VMEM((length, width), F32), pltpu.VMEM((length, width), F32)],
        compiler_params=_vmem(48),
        name=f"hyena_{length}",
    )(u3, u3, u3, conv_w, conv_w, conv_w, conv_b.reshape(1, -1), conv_b.reshape(1, -1),
      conv_b.reshape(1, -1), skip.reshape(1, HY_WIDTH), tc, ts, cmat, smat, cmat, stmat
      ).reshape(nseq * length, HY_WIDTH)


def _head_rms(x, seg, gain):
    x2 = x * x
    hi = x2.astype(BF16)
    lo = (x2 - hi.astype(F32)).astype(BF16)
    ss = jnp.dot(hi, seg, preferred_element_type=F32) + jnp.dot(lo, seg, preferred_element_type=F32)
    return x * lax.rsqrt(ss * (1.0 / HEAD_DIM) + EPS) * gain


def _rope(x, cos, sin_signed):
    width = x.shape[1]
    lane = lax.broadcasted_iota(jnp.int32, x.shape, 1)
    first = (lane // ROPE_FREQS) % 2 == 0
    partner = jnp.where(first, pltpu.roll(x, width - ROPE_FREQS, 1), pltpu.roll(x, ROPE_FREQS, 1))
    return x * cos + partner * sin_signed


def _stack_heads(x, g):
    return jnp.concatenate(
        [x[:, (g * ATT_GROUP + j) * HEAD_DIM:(g * ATT_GROUP + j + 1) * HEAD_DIM] for j in range(ATT_GROUP)], axis=0)


def _sink_column(sink_ref, g, rows):
    return jnp.concatenate([jnp.full((rows, 1), sink_ref[g * ATT_GROUP + j], F32) for j in range(ATT_GROUP)], axis=0)


def _unstack_heads(outs, rows):
    return jnp.concatenate([outs[g][j * rows:(j + 1) * rows, :]
                            for g in range(ATT_KV_HEADS) for j in range(ATT_GROUP)], axis=1)


_DOT_NT = (((1,), (1,)), ((), ()))


def _ctx_attn_kernel(sink_ref, q_ref, k_ref, v_ref, seg_ref, qg_ref, kg_ref, o_ref, ko_ref, vo_ref, *, nseq):
    seg = seg_ref[...]
    for b in range(nseq):
        rows = slice(b * L_CTX, (b + 1) * L_CTX)
        qn = _head_rms(q_ref[rows, :], seg, qg_ref[...]) * ATT_SCALE
        kn = _head_rms(k_ref[rows, :], seg[:KV_WIDTH, :KV_WIDTH], kg_ref[...])
        v = v_ref[rows, :]
        ko_ref[rows, :] = kn
        vo_ref[rows, :] = v
        outs = []
        for g in range(ATT_KV_HEADS):
            cols = slice(g * HEAD_DIM, (g + 1) * HEAD_DIM)
            q = _stack_heads(qn, g).astype(BF16)
            s = lax.dot_general(q, kn[:, cols].astype(BF16), _DOT_NT, preferred_element_type=F32)
            sink = _sink_column(sink_ref, g, L_CTX)
            m = jnp.maximum(jnp.max(s, axis=-1, keepdims=True), sink)
            p = jnp.exp(s - m)
            den = jnp.sum(p, axis=-1, keepdims=True) + jnp.exp(sink - m)
            o = jnp.dot(p.astype(BF16), v[:, cols].astype(BF16), preferred_element_type=F32)
            outs.append(o / den)
        o_ref[rows, :] = _unstack_heads(outs, L_CTX).astype(o_ref.dtype)


def _ctx_attention(u, seg, q_gain, k_gain, sink):
    qcol = 3 * HY_WIDTH // ATT_WIDTH
    kcol = (3 * HY_WIDTH + ATT_WIDTH) // KV_WIDTH
    nseq = 2
    rows = nseq * L_CTX
    return pl.pallas_call(
        functools.partial(_ctx_attn_kernel, nseq=nseq),
        grid_spec=pltpu.PrefetchScalarGridSpec(
            num_scalar_prefetch=1,
            grid=(N_CTX_SEQ // nseq,),
            in_specs=[
                pl.BlockSpec((rows, ATT_WIDTH), lambda b, s: (b, qcol)),
                pl.BlockSpec((rows, KV_WIDTH), lambda b, s: (b, kcol)),
                pl.BlockSpec((rows, KV_WIDTH), lambda b, s: (b, kcol + 1)),
                pl.BlockSpec((ATT_WIDTH, ATT_WIDTH), lambda b, s: (0, 0)),
                pl.BlockSpec((1, ATT_WIDTH), lambda b, s: (0, 0)),
                pl.BlockSpec((1, KV_WIDTH), lambda b, s: (0, 0)),
            ],
            out_specs=[
                pl.BlockSpec((rows, ATT_WIDTH), lambda b, s: (b, 0)),
                pl.BlockSpec((rows, KV_WIDTH), lambda b, s: (b, 0)),
                pl.BlockSpec((rows, KV_WIDTH), lambda b, s: (b, 0)),
            ],
        ),
        out_shape=[jax.ShapeDtypeStruct((N_CTX_TOK, ATT_WIDTH), BF16),
                   jax.ShapeDtypeStruct((N_CTX_TOK, KV_WIDTH), F32),
                   jax.ShapeDtypeStruct((N_CTX_TOK, KV_WIDTH), F32)],
        name="ctx_attention",
    )(sink, u, u, u, seg, jnp.tile(q_gain, ATT_HEADS).reshape(1, ATT_WIDTH),
      jnp.tile(k_gain, ATT_KV_HEADS).reshape(1, KV_WIDTH))


def _rope_tables():
    pos = jnp.arange(L_LAT, dtype=jnp.int32)
    row = (pos // GRID_W).astype(F32)
    col = (pos % GRID_W).astype(F32)
    inv = ROPE_THETA ** (-jnp.arange(ROPE_FREQS, dtype=F32) / ROPE_FREQS)
    ar, ac = row[:, None] * inv, col[:, None] * inv
    cos = jnp.concatenate([jnp.cos(ar), jnp.cos(ar), jnp.cos(ac), jnp.cos(ac)], axis=-1)
    sin = jnp.concatenate([-jnp.sin(ar), jnp.sin(ar), -jnp.sin(ac), jnp.sin(ac)], axis=-1)
    return jnp.tile(cos, (1, ATT_HEADS)), jnp.tile(sin, (1, ATT_HEADS))


def _lat_attn_kernel(sink_ref, q_ref, k_ref, v_ref, ck_ref, cv_ref, cosq_ref, sinq_ref, cosk_ref, sink_k_ref,
                     seg_ref, qg_ref, kg_ref, o_ref, kn_ref, *, nq):
    n = pl.program_id(1)
    seg = seg_ref[...]

    @pl.when(n == 0)
    def _():
        kn = _head_rms(k_ref[...], seg[:KV_WIDTH, :KV_WIDTH], kg_ref[...])
        kn_ref[...] = _rope(kn, cosk_ref[...], sink_k_ref[...]).astype(BF16)

    k_ctx = ck_ref[0].astype(BF16)
    v_ctx = cv_ref[0].astype(BF16)
    span = 3 * ATT_BLOCK
    rows = ATT_GROUP * ATT_BLOCK
    for j in range(nq):
        blk = n * nq + j
        qrows = slice(j * ATT_BLOCK, (j + 1) * ATT_BLOCK)
        qn = _head_rms(q_ref[qrows, :], seg, qg_ref[...])
        qn = _rope(qn, cosq_ref[qrows, :], sinq_ref[qrows, :]) * ATT_SCALE
        start = pl.multiple_of(jnp.clip((blk - 1) * ATT_BLOCK, 0, L_LAT - span), ATT_BLOCK)
        q_pos = blk * ATT_BLOCK + (lax.broadcasted_iota(jnp.int32, (rows, span), 0) % ATT_BLOCK)
        k_pos = start + lax.broadcasted_iota(jnp.int32, (rows, span), 1)
        valid = jnp.abs(q_pos - k_pos) <= WINDOW
        k_loc = kn_ref[pl.ds(start, span), :]
        v_loc = v_ref[pl.ds(start, span), :].astype(BF16)
        outs = []
        for g in range(ATT_KV_HEADS):
            cols = slice(g * HEAD_DIM, (g + 1) * HEAD_DIM)
            q = _stack_heads(qn, g).astype(BF16)
            s_loc = lax.dot_general(q, k_loc[:, cols], _DOT_NT, preferred_element_type=F32)
            s_loc = jnp.where(valid, s_loc, NEG_BIG)
            s_ctx = lax.dot_general(q, k_ctx[:, cols], _DOT_NT, preferred_element_type=F32)
            sink = _sink_column(sink_ref, g, ATT_BLOCK)
            m = jnp.maximum(jnp.maximum(jnp.max(s_loc, axis=-1, keepdims=True),
                                        jnp.max(s_ctx, axis=-1, keepdims=True)), sink)
            p_loc = jnp.exp(s_loc - m)
            p_ctx = jnp.exp(s_ctx - m)
            den = (jnp.sum(p_loc, axis=-1, keepdims=True) + jnp.sum(p_ctx, axis=-1, keepdims=True)
                   + jnp.exp(sink - m))
            o = (jnp.dot(p_loc.astype(BF16), v_loc[:, cols], preferred_element_type=F32)
                 + jnp.dot(p_ctx.astype(BF16), v_ctx[:, cols], preferred_element_type=F32))
            outs.append(o / den)
        o_ref[qrows, :] = _unstack_heads(outs, ATT_BLOCK).astype(o_ref.dtype)


def _lat_attention(u, cache_k, cache_v, seg, q_gain, k_gain, sink):
    qcol = 3 * HY_WIDTH // ATT_WIDTH
    kcol = (3 * HY_WIDTH + ATT_WIDTH) // KV_WIDTH
    nq = 2
    qrows = nq * ATT_BLOCK
    nblk = L_LAT // qrows
    first_q_block = N_CTX_TOK // qrows
    first_seq = N_CTX_TOK // L_LAT
    cos, sin = _rope_tables()
    return pl.pallas_call(
        functools.partial(_lat_attn_kernel, nq=nq),
        grid_spec=pltpu.PrefetchScalarGridSpec(
            num_scalar_prefetch=1,
            grid=(N_LAT_SEQ, nblk),
            in_specs=[
                pl.BlockSpec((qrows, ATT_WIDTH), lambda b, n, s: (first_q_block + b * nblk + n, qcol)),
                pl.BlockSpec((L_LAT, KV_WIDTH), lambda b, n, s: (first_seq + b, kcol)),
                pl.BlockSpec((L_LAT, KV_WIDTH), lambda b, n, s: (first_seq + b, kcol + 1)),
                pl.BlockSpec((1, PAST_LEN, KV_WIDTH), lambda b, n, s: (b, 0, 0)),
                pl.BlockSpec((1, PAST_LEN, KV_WIDTH), lambda b, n, s: (b, 0, 0)),
                pl.BlockSpec((qrows, ATT_WIDTH), lambda b, n, s: (n, 0)),
                pl.BlockSpec((qrows, ATT_WIDTH), lambda b, n, s: (n, 0)),
                pl.BlockSpec((L_LAT, KV_WIDTH), lambda b, n, s: (0, 0)),
                pl.BlockSpec((L_LAT, KV_WIDTH), lambda b, n, s: (0, 0)),
                pl.BlockSpec((ATT_WIDTH, ATT_WIDTH), lambda b, n, s: (0, 0)),
                pl.BlockSpec((1, ATT_WIDTH), lambda b, n, s: (0, 0)),
                pl.BlockSpec((1, KV_WIDTH), lambda b, n, s: (0, 0)),
            ],
            out_specs=pl.BlockSpec((qrows, ATT_WIDTH), lambda b, n, s: (b * nblk + n, 0)),
            scratch_shapes=[pltpu.VMEM((L_LAT, KV_WIDTH), BF16)],
        ),
        out_shape=jax.ShapeDtypeStruct((N_LAT_SEQ * L_LAT, ATT_WIDTH), BF16),
        name="lat_attention",
    )(sink, u, u, u, cache_k.reshape(N_LAT_SEQ, PAST_LEN, KV_WIDTH), cache_v.reshape(N_LAT_SEQ, PAST_LEN, KV_WIDTH),
      cos, sin, cos[:, :KV_WIDTH], sin[:, :KV_WIDTH], seg,
      jnp.tile(q_gain, ATT_HEADS).reshape(1, ATT_WIDTH), jnp.tile(k_gain, ATT_KV_HEADS).reshape(1, KV_WIDTH))


def _ceil_to(v, m):
    return ((v + (m - 1)) // m) * m


def _post_mixer(x, mix, refs, tm):
    (w_ref, gain_ref, mod_ref, rwh_ref, rwm_ref, rb_ref, tri_ref, low_ref,
     x1_ref, sw_ref, cnt_ref, xg_ref, wb_ref) = refs

    @pl.when(pl.program_id(0) == 0)
    def _():
        wb_ref[...] = w_ref[...].astype(BF16)

    y = jnp.dot(mix, wb_ref[...], preferred_element_type=F32)
    x1 = x + mod_ref[0, 2:3, :] * y
    x1_ref[...] = x1
    xt = _norm_mod(x1, gain_ref[...], mod_ref[0, 3:4, :], mod_ref[0, 4:5, :])
    xh = xt.astype(BF16)
    xm = (xt - xh.astype(F32)).astype(BF16)
    logits = (lax.dot_general(rwh_ref[...], xh, _DOT_NT, preferred_element_type=F32)
              + lax.dot_general(rwm_ref[...], xh, _DOT_NT, preferred_element_type=F32)
              + lax.dot_general(rwh_ref[...], xm, _DOT_NT, preferred_element_type=F32)) + rb_ref[...]
    expert = lax.broadcasted_iota(jnp.int32, logits.shape, 0)
    vals, hits = [], []
    for _ in range(TOP_K):
        m = jnp.max(logits, axis=0, keepdims=True)
        sel = jnp.min(jnp.where(logits == m, expert, N_EXPERTS), axis=0, keepdims=True)
        vals.append(m)
        hits.append(expert == sel)
        logits = jnp.where(expert == sel, -jnp.inf, logits)
    es = [jnp.exp(v - vals[0]) for v in vals]
    den = es[0] + es[1] + es[2] + es[3]
    weights = [e / den for e in es]

    routed = sum(h.astype(F32) for h in hits)
    counts = jnp.sum(routed, axis=1, keepdims=True)
    cnt_ref[0] = counts.astype(jnp.int32)
    padded = _ceil_to(counts.astype(jnp.int32), MOE_GROUP).astype(F32)
    g0 = jnp.dot(low_ref[...], jnp.broadcast_to(padded, (N_EXPERTS, LANES)),
                 preferred_element_type=F32, precision=HIGHEST)[:, 0:1]
    earlier = jnp.dot(routed.astype(BF16), tri_ref[...], preferred_element_type=F32)
    row_of = g0 + earlier
    slots = [jnp.sum(jnp.where(h, row_of, 0.0), axis=0, keepdims=True) for h in hits]

    prow = lax.broadcasted_iota(jnp.int32, (LANES, tm), 0)
    packed = jnp.zeros((LANES, tm), F32)
    for k, vec in enumerate(slots + weights):
        packed = jnp.where(prow == k, vec, packed)
    sw_ref[...] = packed.T

    group_row = lax.broadcasted_iota(jnp.int32, (MOE_GROUPED_ROWS, tm), 0).astype(F32)
    perm = jnp.zeros((MOE_GROUPED_ROWS, tm), F32)
    for k in range(TOP_K):
        perm = jnp.where(group_row == slots[k], 1.0, perm)
    perm = perm.astype(BF16)
    xg_ref[...] = jnp.dot(perm, xh, preferred_element_type=F32).astype(BF16)


def _post_even_kernel(xc_ref, xl_ref, ac_ref, al_ref, tc_ref, tl_ref, *refs, tm):
    is_ctx = pl.program_id(0) * tm < N_CTX_TOK
    mix = jnp.concatenate([jnp.where(is_ctx, ac_ref[...], al_ref[...]),
                           jnp.where(is_ctx, tc_ref[...], tl_ref[...])], axis=1)
    _post_mixer(jnp.where(is_ctx, xc_ref[...], xl_ref[...]), mix, refs, tm)


def _post_odd_kernel(x_ref, yc_ref, yl_ref, *refs, tm):
    mix = jnp.where(pl.program_id(0) * tm < N_CTX_TOK, yc_ref[...], yl_ref[...])
    _post_mixer(x_ref[...], mix, refs, tm)


def _post_call(kernel_fn, name, mixer_specs, mixer_args, k_in, w_out, gain, mod, router_w, router_b):
    tm = ROW_TILE
    nsteps = N_TOK // tm
    row = lax.broadcasted_iota(jnp.int32, (tm, tm), 0)
    col = lax.broadcasted_iota(jnp.int32, (tm, tm), 1)
    tri = (row < col).astype(BF16)
    er = lax.broadcasted_iota(jnp.int32, (N_EXPERTS, N_EXPERTS), 0)
    ec = lax.broadcasted_iota(jnp.int32, (N_EXPERTS, N_EXPERTS), 1)
    low = (ec < er).astype(F32)
    rw_t = router_w.T
    rw_hi = rw_t.astype(BF16)
    rw_mid = (rw_t - rw_hi.astype(F32)).astype(BF16)
    const = lambda shape: pl.BlockSpec(shape, lambda i: (0,) * len(shape))
    return pl.pallas_call(
        functools.partial(kernel_fn, tm=tm),
        grid=(nsteps,),
        in_specs=mixer_specs + [
            const((k_in, D_MODEL)),
            const((1, D_MODEL)),
            pl.BlockSpec((1, 6, D_MODEL), lambda i: (_cond_of_tile(i, tm), 0, 0)),
            const((N_EXPERTS, D_MODEL)), const((N_EXPERTS, D_MODEL)), const((N_EXPERTS, 1)),
            const((tm, tm)), const((N_EXPERTS, N_EXPERTS)),
        ],
        out_specs=[
            pl.BlockSpec((tm, D_MODEL), lambda i: (i, 0)),
            pl.BlockSpec((tm, LANES), lambda i: (i, 0)),
            pl.BlockSpec((1, N_EXPERTS, 1), lambda i: (i, 0, 0)),
            pl.BlockSpec((MOE_GROUPED_ROWS, D_MODEL), lambda i: (i, 0)),
        ],
        out_shape=[
            jax.ShapeDtypeStruct((N_TOK, D_MODEL), F32),
            jax.ShapeDtypeStruct((N_TOK, LANES), F32),
            jax.ShapeDtypeStruct((nsteps, N_EXPERTS, 1), jnp.int32),
            jax.ShapeDtypeStruct((nsteps * MOE_GROUPED_ROWS, D_MODEL), BF16),
        ],
        scratch_shapes=[pltpu.VMEM((k_in, D_MODEL), BF16)],
        compiler_params=_vmem(56),
        name=name,
    )(*mixer_args, w_out, gain.reshape(1, D_MODEL), mod, rw_hi, rw_mid, router_b.reshape(N_EXPERTS, 1), tri, low)


def _post_even(x_ctx, x_lat, a_ctx, a_lat, t_ctx, t_lat, w_out, gain, mod, router_w, router_b):
    tm = ROW_TILE
    nctx = N_CTX_TOK // tm
    ctx_map = lambda i: (jnp.minimum(i, nctx - 1), 0)
    lat_map = lambda i: (jnp.maximum(i - nctx, 0), 0)
    specs = [pl.BlockSpec((tm, D_MODEL), ctx_map), pl.BlockSpec((tm, D_MODEL), lat_map),
             pl.BlockSpec((tm, HY_WIDTH), ctx_map), pl.BlockSpec((tm, HY_WIDTH), lat_map),
             pl.BlockSpec((tm, ATT_WIDTH), ctx_map), pl.BlockSpec((tm, ATT_WIDTH), lat_map)]
    return _post_call(_post_even_kernel, "post_even", specs, (x_ctx, x_lat, a_ctx, a_lat, t_ctx, t_lat),
                      HY_WIDTH + ATT_WIDTH, w_out, gain, mod, router_w, router_b)


def _post_odd(x, y_ctx, y_lat, w_out, gain, mod, router_w, router_b):
    tm = ROW_TILE
    nctx = N_CTX_TOK // tm
    specs = [pl.BlockSpec((tm, D_MODEL), lambda i: (i, 0)),
             pl.BlockSpec((tm, RET_V_WIDTH), lambda i: (jnp.minimum(i, nctx - 1), 0)),
             pl.BlockSpec((tm, RET_V_WIDTH), lambda i: (jnp.maximum(i - nctx, 0), 0))]
    return _post_call(_post_odd_kernel, "post_odd", specs, (x, y_ctx, y_lat),
                      RET_V_WIDTH, w_out, gain, mod, router_w, router_b)


def _moe_plan(counts):
    t, grp, big = MOE_TILE, MOE_GROUP, MOE_GROUPED_ROWS
    n16 = _ceil_to(counts.reshape(-1, N_EXPERTS), grp)
    nsteps = n16.shape[0]
    g0 = jnp.cumsum(n16, axis=1) - n16
    e0 = jnp.cumsum(n16, axis=0) - n16
    rows_e = jnp.sum(n16, axis=0)
    ntiles = (rows_e + t - 1) // t
    tile_end = jnp.cumsum(ntiles)
    tile_first = tile_end - ntiles
    total = tile_end[-1]
    experts = jnp.arange(N_EXPERTS, dtype=jnp.int32)
    tiles = jnp.arange(MOE_NUM_TILES, dtype=jnp.int32)
    valid = tiles < total
    ti = jnp.minimum(tiles, total - 1)
    e_of = jnp.sum((tile_end[None, :] <= ti[:, None]).astype(jnp.int32), axis=1)
    onehot = (e_of[:, None] == experts[None, :]).astype(jnp.int32)
    pick = lambda v: jnp.sum(onehot * v[None, :], axis=1)
    pick2 = lambda m: jnp.sum(onehot[:, None, :] * m[None, :, :], axis=2)
    r = ((ti - pick(tile_first)) * t)[:, None] + grp * jnp.arange(t // grp, dtype=jnp.int32)[None, :]
    ends = pick2(e0 + n16)
    step = jnp.sum((ends[:, None, :] <= r[:, :, None]).astype(jnp.int32), axis=2)
    step = jnp.minimum(step, nsteps - 1)
    sel = (step[:, :, None] == jnp.arange(nsteps, dtype=jnp.int32)[None, None, :]).astype(jnp.int32)
    at_step = lambda m: jnp.sum(sel * pick2(m)[:, None, :], axis=2)
    src = step * big + at_step(g0) + r - at_step(e0)
    live = jnp.logical_and(valid[:, None], r < pick(rows_e)[:, None])
    moe_src = jnp.where(live, src, 0).reshape(-1)
    g = grp * jnp.arange(big // grp, dtype=jnp.int32)
    gend = g0 + n16
    ce = jnp.sum((gend[:, None, :] <= g[None, :, None]).astype(jnp.int32), axis=2)
    used = ce < N_EXPERTS
    ce = jnp.minimum(ce, N_EXPERTS - 1)
    csel = (ce[:, :, None] == experts[None, None, :]).astype(jnp.int32)
    of_e = lambda m: jnp.sum(csel * m[:, None, :], axis=2)
    base = jnp.sum(csel * (tile_first * t)[None, None, :], axis=2)
    csrc = base + of_e(e0) + g[None, :] - of_e(g0)
    comb_src = jnp.where(used, csrc, 0).reshape(-1)
    used_e = ntiles > 0
    rank = jnp.cumsum(used_e.astype(jnp.int32)) - 1
    later = jnp.logical_and(used_e[None, :], experts[None, :] > experts[:, None])
    next_e = jnp.min(jnp.where(later, experts[None, :], N_EXPERTS), axis=1)
    next_e = jnp.where(next_e == N_EXPERTS, -1, next_e)
    half = jnp.logical_and(valid, pick(rows_e) - (ti - pick(tile_first)) * t <= t // 2)
    wbuf = jnp.stack([pick(rank) % 2, pick(next_e), half.astype(jnp.int32)], axis=1).reshape(-1)
    as_i32 = lambda v: v.astype(jnp.int32)
    return as_i32(e_of), as_i32(valid), as_i32(wbuf), as_i32(moe_src), as_i32(comb_src)


def _moe_kernel(te_ref, tv_ref, nx_ref, src_ref, xg_hbm, w1_hbm, b1_ref, w2_hbm, b2_ref, y_ref,
                xbuf, w1f, w2f, w1b, w2b, sem_in, sem_w):
    i = pl.program_id(0)
    nt = pl.num_programs(0)
    t, grp = MOE_TILE, MOE_GROUP
    slot = i % 2

    def issue_gather(tile, sl):
        for c in range(t // grp):
            src = pl.multiple_of(src_ref[tile * (t // grp) + c], grp)
            pltpu.make_async_copy(xg_hbm.at[pl.ds(src, grp), :], xbuf.at[sl, pl.ds(c * grp, grp), :],
                                  sem_in.at[sl]).start()

    def weight_copies(e, ws):
        return (pltpu.make_async_copy(w1_hbm.at[e], w1f.at[ws], sem_w.at[ws]),
                pltpu.make_async_copy(w2_hbm.at[e], w2f.at[ws], sem_w.at[ws]))

    def valid(tile):
        return tv_ref[jnp.clip(tile, 0, nt - 1)] > 0

    @pl.when(i == 0)
    def _():
        issue_gather(0, 0)
        for cp in weight_copies(te_ref[0], 0):
            cp.start()

    @pl.when(valid(i))
    def _():
        pltpu.make_async_copy(xg_hbm.at[pl.ds(0, t), :], xbuf.at[slot], sem_in.at[slot]).wait()

        @pl.when(jnp.logical_and(i + 1 < nt, valid(i + 1)))
        def _():
            issue_gather(i + 1, 1 - slot)

        e = te_ref[i]
        first = jnp.logical_or(i == 0, e != te_ref[jnp.maximum(i - 1, 0)])
        ws = nx_ref[3 * i]
        nxt = nx_ref[3 * i + 1]
        half = nx_ref[3 * i + 2] > 0

        @pl.when(first)
        def _():
            for cp in weight_copies(e, ws):
                cp.wait()
            w1b[...] = w1f[ws].astype(BF16)
            w2b[...] = w2f[ws].astype(BF16)

            @pl.when(nxt >= 0)
            def _():
                for cp in weight_copies(nxt, 1 - ws):
                    cp.start()

        def expert_mlp(rows):
            h = jnp.dot(xbuf[slot, :rows], w1b[...], preferred_element_type=F32) + b1_ref[0]
            glu = jnp.minimum(h[:, :D_FF], SWIGLU_LIMIT)
            lin = jnp.clip(h[:, D_FF:], -SWIGLU_LIMIT, SWIGLU_LIMIT)
            act = (glu * _sigmoid(SWIGLU_ALPHA * glu) * (lin + 1.0)).astype(BF16)
            y = jnp.dot(act, w2b[...], preferred_element_type=F32) + b2_ref[0]
            y_ref[:rows, :] = y.astype(y_ref.dtype)

        @pl.when(jnp.logical_not(half))
        def _():
            expert_mlp(t)

        @pl.when(half)
        def _():
            expert_mlp(t // 2)
            y_ref[t // 2:, :] = jnp.zeros((t // 2, D_MODEL), y_ref.dtype)

    @pl.when(jnp.logical_not(valid(i)))
    def _():
        y_ref[...] = jnp.zeros_like(y_ref)


def _moe_experts(xg, plan, w1, b1, w2, b2):
    te, tv, nx, src, _ = plan
    t = MOE_TILE
    return pl.pallas_call(
        _moe_kernel,
        grid_spec=pltpu.PrefetchScalarGridSpec(
            num_scalar_prefetch=4,
            grid=(MOE_NUM_TILES,),
            in_specs=[
                pl.BlockSpec(memory_space=pl.ANY),
                pl.BlockSpec(memory_space=pl.ANY),
                pl.BlockSpec((1, 1, 2 * D_FF), lambda i, te, *_: (te[i], 0, 0)),
                pl.BlockSpec(memory_space=pl.ANY),
                pl.BlockSpec((1, 1, D_MODEL), lambda i, te, *_: (te[i], 0, 0)),
            ],
            out_specs=pl.BlockSpec((t, D_MODEL), lambda i, *_: (i, 0)),
            scratch_shapes=[
                pltpu.VMEM((2, t, D_MODEL), BF16),
                pltpu.VMEM((2, D_MODEL, 2 * D_FF), F32),
                pltpu.VMEM((2, D_FF, D_MODEL), F32),
                pltpu.VMEM((D_MODEL, 2 * D_FF), BF16),
                pltpu.VMEM((D_FF, D_MODEL), BF16),
                pltpu.SemaphoreType.DMA((2,)),
                pltpu.SemaphoreType.DMA((2,)),
            ],
        ),
        out_shape=jax.ShapeDtypeStruct((MOE_NUM_TILES * t, D_MODEL), BF16),
        compiler_params=_vmem(56),
        name="moe_experts",
    )(te, tv, nx, src, xg, w1, b1.reshape(N_EXPERTS, 1, 2 * D_FF), w2, b2.reshape(N_EXPERTS, 1, D_MODEL))


def _combine_value(src_ref, x1_ref, sw_ref, mod_ref, ys_hbm, ybuf, sem, tm):
    s = pl.program_id(0)
    nsteps = pl.num_programs(0)
    slot = s % 2
    grp, big = MOE_GROUP, MOE_GROUPED_ROWS
    nchunk = big // grp

    def issue_gather(step, sl):
        for c in range(nchunk):
            src = pl.multiple_of(src_ref[step * nchunk + c], grp)
            pltpu.make_async_copy(ys_hbm.at[pl.ds(src, grp), :], ybuf.at[sl, pl.ds(c * grp, grp), :],
                                  sem.at[sl]).start()

    @pl.when(s == 0)
    def _():
        issue_gather(0, 0)

    pltpu.make_async_copy(ys_hbm.at[pl.ds(0, big), :], ybuf.at[slot], sem.at[slot]).wait()

    @pl.when(s + 1 < nsteps)
    def _():
        issue_gather(s + 1, 1 - slot)

    sw = sw_ref[...]
    col = lax.broadcasted_iota(jnp.int32, (tm, big), 1).astype(F32)
    wmat = jnp.zeros((tm, big), F32)
    for k in range(TOP_K):
        wmat = jnp.where(col == sw[:, k:k + 1], sw[:, TOP_K + k:TOP_K + k + 1], wmat)
    moe = jnp.dot(wmat.astype(BF16), ybuf[slot], preferred_element_type=F32)
    return x1_ref[...] + mod_ref[0, 5:6, :] * moe


def _combine_kernel(src_ref, x1_ref, sw_ref, mod_ref, ys_hbm, o_ref, ybuf, sem, *, tm):
    o_ref[...] = _combine_value(src_ref, x1_ref, sw_ref, mod_ref, ys_hbm, ybuf, sem, tm)


def _combine_split_kernel(src_ref, x1_ref, sw_ref, mod_ref, ys_hbm, oc_ref, ol_ref, ybuf, sem, *, tm):
    val = _combine_value(src_ref, x1_ref, sw_ref, mod_ref, ys_hbm, ybuf, sem, tm)
    is_ctx = pl.program_id(0) * tm < N_CTX_TOK

    @pl.when(is_ctx)
    def _():
        oc_ref[...] = val

    @pl.when(jnp.logical_not(is_ctx))
    def _():
        ol_ref[...] = val


def _combine(x1, ys, sw, plan, mod, split):
    tm = ROW_TILE
    nctx = N_CTX_TOK // tm
    in_specs = [
        pl.BlockSpec((tm, D_MODEL), lambda i, *_: (i, 0)),
        pl.BlockSpec((tm, LANES), lambda i, *_: (i, 0)),
        pl.BlockSpec((1, 6, D_MODEL), lambda i, *_: (_cond_of_tile(i, tm), 0, 0)),
        pl.BlockSpec(memory_space=pl.ANY),
    ]
    scratch = [pltpu.VMEM((2, MOE_GROUPED_ROWS, D_MODEL), BF16), pltpu.SemaphoreType.DMA((2,))]
    if not split:
        kernel_fn, name = _combine_kernel, "moe_combine"
        out_specs = pl.BlockSpec((tm, D_MODEL), lambda i, *_: (i, 0))
        out_shape = jax.ShapeDtypeStruct((N_TOK, D_MODEL), F32)
    else:
        kernel_fn, name = _combine_split_kernel, "moe_combine_split"
        out_specs = [pl.BlockSpec((tm, D_MODEL), lambda i, *_: (jnp.minimum(i, nctx - 1), 0)),
                     pl.BlockSpec((tm, D_MODEL), lambda i, *_: (jnp.maximum(i - nctx, 0), 0))]
        out_shape = [jax.ShapeDtypeStruct((N_CTX_TOK, D_MODEL), F32),
                     jax.ShapeDtypeStruct((N_TOK - N_CTX_TOK, D_MODEL), F32)]
    return pl.pallas_call(
        functools.partial(kernel_fn, tm=tm),
        grid_spec=pltpu.PrefetchScalarGridSpec(
            num_scalar_prefetch=1, grid=(N_TOK // tm,), in_specs=in_specs, out_specs=out_specs,
            scratch_shapes=scratch),
        out_shape=out_shape,
        compiler_params=_vmem(56),
        name=name,
    )(plan[4], x1, sw, mod, ys)


def _retention_kernel(lg_ref, q_ref, k_ref, v_ref, gf_ref, gb_ref, *rest, length, has_s0, emit_state):
    rest = list(rest)
    s0_ref = rest.pop(0) if has_s0 else None
    o_ref = rest.pop(0)
    so_ref = rest.pop(0) if emit_state else None
    s_ref, yf_ref, yb_ref = rest
    c = RET_CHUNK
    nc = length // c
    ii = lax.broadcasted_iota(jnp.int32, (c, c), 0).astype(F32)
    jj = lax.broadcasted_iota(jnp.int32, (c, c), 1).astype(F32)
    ci = lax.broadcasted_iota(jnp.int32, (c, 1), 0).astype(F32)

    def decays(direction):
        lg = -jnp.exp(lg_ref[direction, 0])
        lg1 = lg[:, 0:1]
        if direction == 0:
            diff = ii - jj
            q_decay = jnp.exp(lg1 * (ci + 1.0))
            k_decay = jnp.exp(lg1 * (c - 1.0 - ci))
        else:
            diff = jj - ii
            q_decay = jnp.exp(lg1 * (c - ci))
            k_decay = jnp.exp(lg1 * ci)
        scale = RET_DK ** -0.5
        inner = jnp.where(diff >= 0, jnp.exp(lg * jnp.maximum(diff, 0.0)), 0.0) * scale
        return inner, q_decay, k_decay * scale, jnp.exp(lg1 * float(c))

    def chunk(direction, ch, consts, g_ref, y_ref):
        inner, q_decay, k_decay, chunk_decay = consts
        rows = pl.ds(pl.multiple_of(ch * c, c), c)
        qc = q_ref[rows, :]
        kc = k_ref[rows, :]
        vc = v_ref[rows, :]
        s = s_ref[direction]
        att = lax.dot_general(qc, kc, _DOT_NT, preferred_element_type=F32) * inner
        o = (jnp.dot(att.astype(BF16), vc, preferred_element_type=F32)
             + jnp.dot(qc, s.astype(BF16), preferred_element_type=F32) * q_decay)
        kd = (kc.astype(F32) * k_decay).T.astype(BF16)
        s_ref[direction] = s * chunk_decay + jnp.dot(kd, vc, preferred_element_type=F32)
        on = o * lax.rsqrt(jnp.mean(o * o, axis=-1, keepdims=True) + EPS)
        g = g_ref[rows, :].astype(F32)
        y_ref[rows, :] = g * _sigmoid(g) * on

    for direction in range(2):
        if has_s0:
            s_ref[direction] = s0_ref[0, direction, 0]
        else:
            s_ref[direction] = jnp.zeros((RET_DK, RET_DV), F32)
    forward, backward = decays(0), decays(1)

    def body(step, carry):
        chunk(0, step, forward, gf_ref, yf_ref)
        chunk(1, nc - 1 - step, backward, gb_ref, yb_ref)
        return carry

    lax.fori_loop(0, nc, body, 0)
    o_ref[...] = (yf_ref[...] + yb_ref[...]).astype(o_ref.dtype)
    if emit_state:
        for direction in range(2):
            so_ref[0, direction, 0] = s_ref[direction]


def _retention(u, first_seq, nseq, length, decay_logit, s0, emit_state):
    row0 = first_seq
    lg = jnp.broadcast_to(decay_logit.astype(F32)[:, :, None, None], (2, RET_HEADS, 1, LANES))
    kcol = RET_QK_WIDTH // RET_DK
    vcol = 2 * RET_QK_WIDTH // RET_DV
    gfcol = vcol + RET_HEADS
    gbcol = gfcol + RET_HEADS
    in_specs = [
        pl.BlockSpec((2, 1, 1, LANES), lambda b, h: (0, h, 0, 0)),
        pl.BlockSpec((length, RET_DK), lambda b, h: (row0 + b, h)),
        pl.BlockSpec((length, RET_DK), lambda b, h: (row0 + b, kcol + h)),
        pl.BlockSpec((length, RET_DV), lambda b, h: (row0 + b, vcol + h)),
        pl.BlockSpec((length, RET_DV), lambda b, h: (row0 + b, gfcol + h)),
        pl.BlockSpec((length, RET_DV), lambda b, h: (row0 + b, gbcol + h)),
    ]
    args = [lg, u, u, u, u, u]
    state_spec = pl.BlockSpec((1, 2, 1, RET_DK, RET_DV), lambda b, h: (b, 0, h, 0, 0))
    if s0 is not None:
        in_specs.append(state_spec)
        args.append(s0)
    out_specs = [pl.BlockSpec((length, RET_DV), lambda b, h: (b, h))]
    out_shape = [jax.ShapeDtypeStruct((nseq * length, RET_V_WIDTH), BF16)]
    if emit_state:
        out_specs.append(state_spec)
        out_shape.append(jax.ShapeDtypeStruct((nseq, 2, RET_HEADS, RET_DK, RET_DV), F32))
    return pl.pallas_call(
        functools.partial(_retention_kernel, length=length, has_s0=s0 is not None, emit_state=emit_state),
        grid=(nseq, RET_HEADS),
        in_specs=in_specs,
        out_specs=out_specs,
        out_shape=out_shape,
        scratch_shapes=[pltpu.VMEM((2, RET_DK, RET_DV), F32), pltpu.VMEM((length, RET_DV), F32),
                        pltpu.VMEM((length, RET_DV), F32)],
        compiler_params=_vmem(48),
        name=f"retention_{length}",
    )(*args)


def kernel(x_prompt, x_sample, cache_k0, cache_v0, state_ret1, c, c_ctx, l0_norm_mix, l0_ada_w, l0_ada_b, l0_w_in, l0_conv_w, l0_conv_b, l0_filt_w1, l0_filt_b1, l0_filt_freq, l0_filt_w2, l0_filt_b2, l0_filt_w3, l0_filt_deltas, l0_hy_skip, l0_q_gain, l0_k_gain, l0_sink, l0_w_out, l0_norm_ffn, l0_router_w, l0_router_b, l0_moe_w1, l0_moe_b1, l0_moe_w2, l0_moe_b2, l1_norm_mix, l1_ada_w, l1_ada_b, l1_w_in, l1_ret_decay_logit, l1_w_out, l1_norm_ffn, l1_router_w, l1_router_b, l1_moe_w1, l1_moe_b1, l1_moe_w2, l1_moe_b2):
    x_ctx = x_prompt.reshape(N_CTX_TOK, D_MODEL)
    x_lat = x_sample.reshape(N_TOK - N_CTX_TOK, D_MODEL)
    cond = jnp.zeros((SUBLANES, D_MODEL), F32).at[0].set(c_ctx).at[1:1 + N_LAT_SEQ].set(c)
    mod0 = _adaln(cond, l0_ada_w, l0_ada_b)
    mod1 = _adaln(cond, l1_ada_w, l1_ada_b)

    u = _in_proj((x_ctx, x_lat), l0_norm_mix, mod0, l0_w_in, EVEN_IN // 2, F32)
    filt = (l0_filt_w1, l0_filt_b1, l0_filt_freq, l0_filt_w2, l0_filt_b2, l0_filt_w3, l0_filt_deltas)
    hy = []
    for first_block, nseq, length in ((0, N_CTX_SEQ, L_CTX), (N_CTX_TOK // L_LAT // N_LAT_SEQ, N_LAT_SEQ, L_LAT)):
        cmat, smat, stmat = _dft_matrices(length)
        tc, ts = _hyena_filter(length, _filter_features(length), cmat, smat, *filt)
        hy.append(_hyena(u, first_block, nseq, length, l0_conv_w, l0_conv_b, l0_hy_skip, tc, ts, cmat, smat, stmat))
    head = lax.broadcasted_iota(jnp.int32, (ATT_WIDTH, ATT_WIDTH), 0) // HEAD_DIM
    seg = (head == head.T).astype(BF16)
    att_ctx, new_k, new_v = _ctx_attention(u, seg, l0_q_gain, l0_k_gain, l0_sink)
    att_lat = _lat_attention(u, cache_k0, cache_v0, seg, l0_q_gain, l0_k_gain, l0_sink)
    x1, sw, counts, xg = _post_even(x_ctx, x_lat, hy[0], hy[1], att_ctx, att_lat, l0_w_out, l0_norm_ffn, mod0,
                                    l0_router_w, l0_router_b)
    plan = _moe_plan(counts)
    ys = _moe_experts(xg, plan, l0_moe_w1, l0_moe_b1, l0_moe_w2, l0_moe_b2)
    x = _combine(x1, ys, sw, plan, mod0, split=False)

    u = _in_proj((x,), l1_norm_mix, mod1, l1_w_in, 2048, BF16)
    y_ctx, new_state = _retention(u, 0, N_CTX_SEQ, L_CTX, l1_ret_decay_logit, None, True)
    (y_lat,) = _retention(u, N_CTX_TOK // L_LAT, N_LAT_SEQ, L_LAT, l1_ret_decay_logit, state_ret1, False)
    x1, sw, counts, xg = _post_odd(x, y_ctx, y_lat, l1_w_out, l1_norm_ffn, mod1, l1_router_w, l1_router_b)
    plan = _moe_plan(counts)
    ys = _moe_experts(xg, plan, l1_moe_w1, l1_moe_b1, l1_moe_w2, l1_moe_b2)
    y_prompt, y_sample = _combine(x1, ys, sw, plan, mod1, split=True)

    return (y_prompt.reshape(N_CTX_SEQ, L_CTX, D_MODEL), y_sample.reshape(N_LAT_SEQ, L_LAT, D_MODEL),
            new_k.reshape(N_CTX_SEQ, L_CTX, ATT_KV_HEADS, HEAD_DIM),
            new_v.reshape(N_CTX_SEQ, L_CTX, ATT_KV_HEADS, HEAD_DIM), new_state)
```

```python
import functools
import math

import jax
import jax.numpy as jnp
from jax import lax
from jax.experimental import pallas as pl
from jax.experimental.pallas import tpu as pltpu

F32 = jnp.float32
BF16 = jnp.bfloat16
HIGHEST = lax.Precision.HIGHEST

D_MODEL = 1024
N_CTX_SEQ, L_CTX = 16, 256
N_LAT_SEQ, L_LAT = 2, 2048
N_CTX_TOK = N_CTX_SEQ * L_CTX
N_TOK = N_CTX_TOK + N_LAT_SEQ * L_LAT
PAST_LEN = 512
EPS = 1e-6
NEG_BIG = -1e30

HY_WIDTH = 512
HY_BANDS = 16
HY_FILTER_HIDDEN = 64
HY_FEAT_PAD = 64

ATT_HEADS, ATT_KV_HEADS, HEAD_DIM = 8, 2, 64
ATT_GROUP = ATT_HEADS // ATT_KV_HEADS
ATT_WIDTH = ATT_HEADS * HEAD_DIM
KV_WIDTH = ATT_KV_HEADS * HEAD_DIM
ATT_SCALE = HEAD_DIM ** -0.5
WINDOW = 128
ATT_BLOCK = 128
ROPE_THETA = 10000.0
ROPE_FREQS = HEAD_DIM // 4
GRID_W = 64
EVEN_IN = 3 * HY_WIDTH + ATT_WIDTH + 2 * KV_WIDTH

RET_HEADS = 4
RET_DK = 256
RET_DV = 512
RET_CHUNK = 128
RET_QK_WIDTH = RET_HEADS * RET_DK
RET_V_WIDTH = RET_HEADS * RET_DV
ODD_IN = 2 * RET_QK_WIDTH + 3 * RET_V_WIDTH

N_EXPERTS = 32
TOP_K = 4
D_FF = 1024
SWIGLU_ALPHA = 1.702
SWIGLU_LIMIT = 7.0

SUBLANES = 8
LANES = 128

MOE_TILE = 512
MOE_GROUP = 16
ROW_TILE = 512
MOE_GROUPED_ROWS = 2560
MOE_NUM_TILES = 112


def _vmem(mib):
    return pltpu.CompilerParams(vmem_limit_bytes=mib * 1024 * 1024)


def _cond_of_tile(i, tm):
    row = i * tm
    return jnp.where(row < N_CTX_TOK, 0, 1 + (row - N_CTX_TOK) // L_LAT)


def _sigmoid(x):
    return 1.0 / (1.0 + jnp.exp(-x))


def _norm_mod(x, gain, shift, scale):
    ms = jnp.mean(x * x, axis=-1, keepdims=True)
    return (x * lax.rsqrt(ms + EPS) * gain) * (1.0 + scale) + shift


def _adaln_kernel(c_ref, w_ref, b_ref, o_ref):
    c = c_ref[...]
    s = c * _sigmoid(c)
    o_ref[...] = jnp.dot(s, w_ref[...], preferred_element_type=F32, precision=HIGHEST) + b_ref[...]


def _adaln(cond, w, b):
    n = w.shape[1]
    tn = 1024
    out = pl.pallas_call(
        _adaln_kernel,
        grid=(n // tn,),
        in_specs=[
            pl.BlockSpec((SUBLANES, D_MODEL), lambda j: (0, 0)),
            pl.BlockSpec((D_MODEL, tn), lambda j: (0, j)),
            pl.BlockSpec((1, tn), lambda j: (0, j)),
        ],
        out_specs=pl.BlockSpec((SUBLANES, tn), lambda j: (0, j)),
        out_shape=jax.ShapeDtypeStruct((SUBLANES, n), F32),
        name="adaln",
    )(cond, w, b.reshape(1, n))
    return out.reshape(SUBLANES, 6, D_MODEL)


def _in_proj_kernel(*refs, tm):
    *x_refs, gain_ref, mod_ref, w_ref, o_ref, wb_ref = refs
    i = pl.program_id(1)

    @pl.when(i == 0)
    def _():
        wb_ref[...] = w_ref[...].astype(BF16)

    if len(x_refs) == 2:
        x = jnp.where(i * tm < N_CTX_TOK, x_refs[0][...], x_refs[1][...])
    else:
        x = x_refs[0][...]
    h = _norm_mod(x, gain_ref[...], mod_ref[0, 0:1, :], mod_ref[0, 1:2, :])
    o_ref[...] = jnp.dot(h.astype(BF16), wb_ref[...], preferred_element_type=F32).astype(o_ref.dtype)


def _in_proj(xs, gain, mod, w, tn, out_dtype):
    n = w.shape[1]
    tm = ROW_TILE
    nctx = N_CTX_TOK // tm
    if len(xs) == 2:
        x_specs = [pl.BlockSpec((tm, D_MODEL), lambda j, i: (jnp.minimum(i, nctx - 1), 0)),
                   pl.BlockSpec((tm, D_MODEL), lambda j, i: (jnp.maximum(i - nctx, 0), 0))]
    else:
        x_specs = [pl.BlockSpec((tm, D_MODEL), lambda j, i: (i, 0))]
    return pl.pallas_call(
        functools.partial(_in_proj_kernel, tm=tm),
        grid=(n // tn, N_TOK // tm),
        in_specs=x_specs + [
            pl.BlockSpec((1, D_MODEL), lambda j, i: (0, 0)),
            pl.BlockSpec((1, 6, D_MODEL), lambda j, i: (_cond_of_tile(i, tm), 0, 0)),
            pl.BlockSpec((D_MODEL, tn), lambda j, i: (0, j)),
        ],
        out_specs=pl.BlockSpec((tm, tn), lambda j, i: (i, j)),
        out_shape=jax.ShapeDtypeStruct((N_TOK, n), out_dtype),
        scratch_shapes=[pltpu.VMEM((D_MODEL, tn), BF16)],
        compiler_params=_vmem(48),
        name="in_proj",
    )(*xs, gain.reshape(1, D_MODEL), mod, w)


def _dft_matrices(length):
    n = 2 * length
    lo = 16
    s = jnp.arange(length, dtype=jnp.int32)
    k1 = jnp.arange(length // lo, dtype=jnp.int32) * lo
    k0 = jnp.arange(lo, dtype=jnp.int32)
    ang1 = (2.0 * math.pi / n) * ((k1[:, None] * s[None, :]) % n).astype(F32)
    ang0 = (2.0 * math.pi / n) * ((k0[:, None] * s[None, :]) % n).astype(F32)
    c1, s1 = jnp.cos(ang1)[:, None, :], jnp.sin(ang1)[:, None, :]
    c0, s0 = jnp.cos(ang0)[None, :, :], jnp.sin(ang0)[None, :, :]
    cmat = (c1 * c0 - s1 * s0).reshape(length, length)
    smat = (s1 * c0 + c1 * s0).reshape(length, length)
    sign = jnp.where(s % 2 == 0, 1.0, -1.0).astype(F32)
    row = lax.broadcasted_iota(jnp.int32, (length, length), 0)
    col = lax.broadcasted_iota(jnp.int32, (length, length), 1)
    s_nyq = jnp.where(row == 0, sign[None, :], smat)
    st_nyq = jnp.where(col == 0, sign[:, None], smat)
    return cmat.astype(BF16), s_nyq.astype(BF16), st_nyq.astype(BF16)


def _filter_features(length):
    t = jnp.linspace(0.0, 1.0, length, dtype=F32)[:, None]
    w = 2.0 * math.pi * jnp.arange(length, dtype=F32)[:, None] / length
    f = jnp.linspace(1e-4, HY_BANDS - 1, HY_BANDS, dtype=F32)[None, :]
    z = jnp.concatenate([t, jnp.cos(f * w), -jnp.sin(f * w)], axis=-1)
    return jnp.pad(z, ((0, 0), (0, HY_FEAT_PAD - z.shape[1])))


def _filter_kernel(z_ref, w1_ref, b1_ref, fr_ref, w2_ref, b2_ref, w3_ref, dl_ref, c_ref, s_ref,
                   tc_ref, ts_ref, taps_ref, *, length, rb):
    r = pl.program_id(0)

    @pl.when(r == 0)
    def _():
        z = z_ref[...]
        fr = fr_ref[...]
        h = jnp.sin(fr * (jnp.dot(z, w1_ref[...], preferred_element_type=F32, precision=HIGHEST) + b1_ref[...]))
        h = jnp.sin(fr * (jnp.dot(h, w2_ref[...], preferred_element_type=F32, precision=HIGHEST) + b2_ref[...]))
        h = jnp.dot(h, w3_ref[...], preferred_element_type=F32, precision=HIGHEST)
        win = jnp.exp(-z[:, 0:1] * jnp.abs(dl_ref[...]))
        hf = h[:, :HY_WIDTH] * win
        hb = h[:, HY_WIDTH:] * win
        row = lax.broadcasted_iota(jnp.int32, (length, HY_WIDTH), 0)
        hb = jnp.where(row == 0, 0.0, hb)
        l1 = jnp.sum(jnp.abs(hf), axis=0, keepdims=True) + jnp.sum(jnp.abs(hb), axis=0, keepdims=True)
        inv = 1.0 / l1
        taps_ref[:, :HY_WIDTH] = (hf * inv).astype(BF16)
        taps_ref[:, HY_WIDTH:] = (hb * inv).astype(BF16)

    taps = taps_ref[...]
    rc = jnp.dot(c_ref[...], taps, preferred_element_type=F32)
    rs = jnp.dot(s_ref[...], taps, preferred_element_type=F32)
    tc = rc[:, :HY_WIDTH] + rc[:, HY_WIDTH:]
    ts = rs[:, :HY_WIDTH] - rs[:, HY_WIDTH:]
    grow = r * rb + lax.broadcasted_iota(jnp.int32, (rb, HY_WIDTH), 0)
    is0 = grow == 0
    ts = jnp.where(is0, rs[:, :HY_WIDTH] + rs[:, HY_WIDTH:], ts)
    wgt = jnp.where(is0, 1.0 / (2 * length), 2.0 / (2 * length))
    tc_ref[...] = tc * wgt
    ts_ref[...] = ts * wgt


def _hyena_filter(length, feats, cmat, smat, w1, b1, freq, w2, b2, w3, deltas):
    rb = min(length, 512)
    w1p = jnp.pad(w1, ((0, HY_FEAT_PAD - w1.shape[0]), (0, 0)))
    full = lambda shape: pl.BlockSpec(shape, lambda r: (0,) * len(shape))
    hid = HY_FILTER_HIDDEN
    return pl.pallas_call(
        functools.partial(_filter_kernel, length=length, rb=rb),
        grid=(length // rb,),
        in_specs=[
            full((length, HY_FEAT_PAD)), full((HY_FEAT_PAD, hid)), full((1, hid)), full((1, hid)),
            full((hid, hid)), full((1, hid)), full((hid, 2 * HY_WIDTH)), full((1, HY_WIDTH)),
            pl.BlockSpec((rb, length), lambda r: (r, 0)),
            pl.BlockSpec((rb, length), lambda r: (r, 0)),
        ],
        out_specs=[pl.BlockSpec((rb, HY_WIDTH), lambda r: (r, 0)),
                   pl.BlockSpec((rb, HY_WIDTH), lambda r: (r, 0))],
        out_shape=[jax.ShapeDtypeStruct((length, HY_WIDTH), F32)] * 2,
        scratch_shapes=[pltpu.VMEM((length, 2 * HY_WIDTH), BF16)],
        compiler_params=_vmem(48),
        name=f"hyena_filter_{length}",
    )(feats, w1p, b1.reshape(1, hid), freq.reshape(1, hid), w2, b2.reshape(1, hid), w3,
      deltas.reshape(1, HY_WIDTH), cmat, smat)


def _hyena_kernel(x0_ref, x1_ref, v_ref, w0_ref, w1_ref, w2_ref, b0_ref, b1_ref, b2_ref, skip_ref,
                  tc_ref, ts_ref, c_ref, s_ref, ct_ref, st_ref, o_ref,
                  zb_ref, zs_ref, x0c_ref, acc_ref, *, nseq, length, cb, fb):
    f = pl.program_id(1)
    nf = pl.num_programs(1)

    def short_conv(u, w_ref, b_ref):
        row = lax.broadcasted_iota(jnp.int32, u.shape, 0)
        prev = jnp.where(row == 0, 0.0, pltpu.roll(u, 1, 0))
        nxt = jnp.where(row == length - 1, 0.0, pltpu.roll(u, length - 1, 0))
        return prev * w_ref[0:1, :] + u * w_ref[1:2, :] + nxt * w_ref[2:3, :] + b_ref[...]

    @pl.when(f == 0)
    def _():
        for b in range(nseq):
            cols = slice(b * cb, (b + 1) * cb)
            x0c_ref[:, cols] = short_conv(x0_ref[b], w0_ref, b0_ref)
            z = short_conv(x1_ref[b], w1_ref, b1_ref) * short_conv(v_ref[b], w2_ref, b2_ref)
            zb_ref[:, cols] = z.astype(BF16)
            zs_ref[:, cols] = z * skip_ref[...]
        acc_ref[...] = jnp.zeros_like(acc_ref)

    zb = zb_ref[...]
    zc = jnp.dot(c_ref[...], zb, preferred_element_type=F32)
    zsn = jnp.dot(s_ref[...], zb, preferred_element_type=F32)
    tc = jnp.concatenate([tc_ref[...]] * nseq, axis=1)
    ts = jnp.concatenate([ts_ref[...]] * nseq, axis=1)
    grow = f * fb + lax.broadcasted_iota(jnp.int32, zc.shape, 0)
    is0 = grow == 0
    yc = jnp.where(is0, zc * tc, zc * tc - zsn * ts)
    ys = jnp.where(is0, zsn * ts, zc * ts + zsn * tc)
    acc_ref[...] += (jnp.dot(ct_ref[...], yc.astype(BF16), preferred_element_type=F32)
                     + jnp.dot(st_ref[...], ys.astype(BF16), preferred_element_type=F32))

    @pl.when(f == nf - 1)
    def _():
        for b in range(nseq):
            cols = slice(b * cb, (b + 1) * cb)
            o_ref[b] = (x0c_ref[:, cols] * (acc_ref[:, cols] + zs_ref[:, cols])).astype(o_ref.dtype)


def _hyena(u, first_seq_block, nseq, length, conv_w, conv_b, skip, tc, ts, cmat, smat, stmat):
    cb = 128
    fb = min(length, 512)
    u3 = u.reshape(N_TOK // length, length, EVEN_IN)
    ncb = HY_WIDTH // cb
    width = nseq * cb

    def ublock(part):
        return pl.BlockSpec((nseq, length, cb), lambda c, f: (first_seq_block, 0, part * ncb + c))

    def wblock(part, rows):
        return pl.BlockSpec((rows, cb), lambda c, f: (0, part * ncb + c))

    return pl.pallas_call(
        functools.partial(_hyena_kernel, nseq=nseq, length=length, cb=cb, fb=fb),
        grid=(ncb, length // fb),
        in_specs=[
            ublock(0), ublock(1), ublock(2),
            wblock(0, 3), wblock(1, 3), wblock(2, 3),
            wblock(0, 1), wblock(1, 1), wblock(2, 1),
            pl.BlockSpec((1, cb), lambda c, f: (0, c)),
            pl.BlockSpec((fb, cb), lambda c, f: (f, c)),
            pl.BlockSpec((fb, cb), lambda c, f: (f, c)),
            pl.BlockSpec((fb, length), lambda c, f: (f, 0)),
            pl.BlockSpec((fb, length), lambda c, f: (f, 0)),
            pl.BlockSpec((length, fb), lambda c, f: (0, f)),
            pl.BlockSpec((length, fb), lambda c, f: (0, f)),
        ],
        out_specs=pl.BlockSpec((nseq, length, cb), lambda c, f: (0, 0, c)),
        out_shape=jax.ShapeDtypeStruct((nseq, length, HY_WIDTH), BF16),
        scratch_shapes=[pltpu.VMEM((length, width), BF16), pltpu.VMEM((length, width), F32),
                        pltpu.VMEM((length, width), F32), pltpu.VMEM((length, width), F32)],
        compiler_params=_vmem(48),
        name=f"hyena_{length}",
    )(u3, u3, u3, conv_w, conv_w, conv_w, conv_b.reshape(1, -1), conv_b.reshape(1, -1),
      conv_b.reshape(1, -1), skip.reshape(1, HY_WIDTH), tc, ts, cmat, smat, cmat, stmat
      ).reshape(nseq * length, HY_WIDTH)


def _head_rms(x, seg, gain):
    x2 = x * x
    hi = x2.astype(BF16)
    lo = (x2 - hi.astype(F32)).astype(BF16)
    ss = jnp.dot(hi, seg, preferred_element_type=F32) + jnp.dot(lo, seg, preferred_element_type=F32)
    return x * lax.rsqrt(ss * (1.0 / HEAD_DIM) + EPS) * gain


def _rope(x, cos, sin_signed):
    width = x.shape[1]
    lane = lax.broadcasted_iota(jnp.int32, x.shape, 1)
    first = (lane // ROPE_FREQS) % 2 == 0
    partner = jnp.where(first, pltpu.roll(x, width - ROPE_FREQS, 1), pltpu.roll(x, ROPE_FREQS, 1))
    return x * cos + partner * sin_signed


_DOT_NT = (((1,), (1,)), ((), ()))


def _softmax_over_keys(s, sink_row):
    m = jnp.maximum(jnp.max(s, axis=0, keepdims=True), sink_row)
    p = jnp.exp(s - m)
    return p, jnp.sum(p, axis=0, keepdims=True) + jnp.exp(sink_row - m)


def _ctx_attn_kernel(sink_ref, q_ref, k_ref, v_ref, seg_ref, qg_ref, kg_ref, o_ref, ko_ref, vo_ref, *, nseq):
    seg = seg_ref[...]
    for b in range(nseq):
        rows = slice(b * L_CTX, (b + 1) * L_CTX)
        qt = (_head_rms(q_ref[rows, :], seg, qg_ref[...]) * ATT_SCALE).T.astype(BF16)
        kn = _head_rms(k_ref[rows, :], seg[:KV_WIDTH, :KV_WIDTH], kg_ref[...])
        v = v_ref[rows, :]
        ko_ref[rows, :] = kn
        vo_ref[rows, :] = v
        kb = kn.astype(BF16)
        vt = v.T.astype(BF16)
        blocks = []
        for j in range(ATT_GROUP):
            outs = []
            for g in range(ATT_KV_HEADS):
                h = g * ATT_GROUP + j
                cols = slice(g * HEAD_DIM, (g + 1) * HEAD_DIM)
                s = jnp.dot(kb[:, cols], qt[h * HEAD_DIM:(h + 1) * HEAD_DIM, :], preferred_element_type=F32)
                p, den = _softmax_over_keys(s, jnp.full((1, L_CTX), sink_ref[h], F32))
                outs.append(jnp.dot(vt[cols, :], p.astype(BF16), preferred_element_type=F32) / den)
            blocks.append(jnp.concatenate(outs, axis=0).T)
        o_ref[rows, :] = jnp.concatenate(blocks, axis=1).astype(o_ref.dtype)


def _ctx_attention(u, seg, q_gain, k_gain, sink):
    qcol = 3 * HY_WIDTH // ATT_WIDTH
    kcol = (3 * HY_WIDTH + ATT_WIDTH) // KV_WIDTH
    nseq = 2
    rows = nseq * L_CTX
    return pl.pallas_call(
        functools.partial(_ctx_attn_kernel, nseq=nseq),
        grid_spec=pltpu.PrefetchScalarGridSpec(
            num_scalar_prefetch=1,
            grid=(N_CTX_SEQ // nseq,),
            in_specs=[
                pl.BlockSpec((rows, ATT_WIDTH), lambda b, s: (b, qcol)),
                pl.BlockSpec((rows, KV_WIDTH), lambda b, s: (b, kcol)),
                pl.BlockSpec((rows, KV_WIDTH), lambda b, s: (b, kcol + 1)),
                pl.BlockSpec((ATT_WIDTH, ATT_WIDTH), lambda b, s: (0, 0)),
                pl.BlockSpec((1, ATT_WIDTH), lambda b, s: (0, 0)),
                pl.BlockSpec((1, KV_WIDTH), lambda b, s: (0, 0)),
            ],
            out_specs=[
                pl.BlockSpec((rows, ATT_WIDTH), lambda b, s: (b, 0)),
                pl.BlockSpec((rows, KV_WIDTH), lambda b, s: (b, 0)),
                pl.BlockSpec((rows, KV_WIDTH), lambda b, s: (b, 0)),
            ],
        ),
        out_shape=[jax.ShapeDtypeStruct((N_CTX_TOK, ATT_WIDTH), BF16),
                   jax.ShapeDtypeStruct((N_CTX_TOK, KV_WIDTH), F32),
                   jax.ShapeDtypeStruct((N_CTX_TOK, KV_WIDTH), F32)],
        name="ctx_attention",
    )(sink, u, u, u, seg, jnp.tile(q_gain, ATT_HEADS).reshape(1, ATT_WIDTH),
      jnp.tile(k_gain, ATT_KV_HEADS).reshape(1, KV_WIDTH))


def _rope_tables():
    pos = jnp.arange(L_LAT, dtype=jnp.int32)
    row = (pos // GRID_W).astype(F32)
    col = (pos % GRID_W).astype(F32)
    inv = ROPE_THETA ** (-jnp.arange(ROPE_FREQS, dtype=F32) / ROPE_FREQS)
    ar, ac = row[:, None] * inv, col[:, None] * inv
    cos = jnp.concatenate([jnp.cos(ar), jnp.cos(ar), jnp.cos(ac), jnp.cos(ac)], axis=-1)
    sin = jnp.concatenate([-jnp.sin(ar), jnp.sin(ar), -jnp.sin(ac), jnp.sin(ac)], axis=-1)
    return jnp.tile(cos, (1, ATT_HEADS)), jnp.tile(sin, (1, ATT_HEADS))


def _lat_attn_kernel(sink_ref, q_ref, k_ref, v_ref, ck_ref, cv_ref, cosq_ref, sinq_ref, cosk_ref, sink_k_ref,
                     seg_ref, qg_ref, kg_ref, o_ref, kn_ref, *, nq):
    n = pl.program_id(1)
    seg = seg_ref[...]

    @pl.when(n == 0)
    def _():
        kn = _head_rms(k_ref[...], seg[:KV_WIDTH, :KV_WIDTH], kg_ref[...])
        kn_ref[...] = _rope(kn, cosk_ref[...], sink_k_ref[...]).astype(BF16)

    k_ctx = ck_ref[0].astype(BF16)
    vt_ctx = cv_ref[0].T.astype(BF16)
    span = 3 * ATT_BLOCK
    lanes = ATT_GROUP * ATT_BLOCK
    for j in range(nq):
        blk = n * nq + j
        qrows = slice(j * ATT_BLOCK, (j + 1) * ATT_BLOCK)
        qn = _head_rms(q_ref[qrows, :], seg, qg_ref[...])
        qt = (_rope(qn, cosq_ref[qrows, :], sinq_ref[qrows, :]) * ATT_SCALE).T.astype(BF16)
        start = pl.multiple_of(jnp.clip((blk - 1) * ATT_BLOCK, 0, L_LAT - span), ATT_BLOCK)
        q_pos = blk * ATT_BLOCK + (lax.broadcasted_iota(jnp.int32, (span, lanes), 1) % ATT_BLOCK)
        k_pos = start + lax.broadcasted_iota(jnp.int32, (span, lanes), 0)
        valid = jnp.abs(q_pos - k_pos) <= WINDOW
        k_loc = kn_ref[pl.ds(start, span), :]
        vt_loc = v_ref[pl.ds(start, span), :].T.astype(BF16)
        outs = []
        for g in range(ATT_KV_HEADS):
            cols = slice(g * HEAD_DIM, (g + 1) * HEAD_DIM)
            heads = range(g * ATT_GROUP, (g + 1) * ATT_GROUP)
            q = jnp.concatenate([qt[h * HEAD_DIM:(h + 1) * HEAD_DIM, :] for h in heads], axis=1)
            sink = jnp.concatenate([jnp.full((1, ATT_BLOCK), sink_ref[h], F32) for h in heads], axis=1)
            s_loc = jnp.where(valid, jnp.dot(k_loc[:, cols], q, preferred_element_type=F32), NEG_BIG)
            s_ctx = jnp.dot(k_ctx[:, cols], q, preferred_element_type=F32)
            m = jnp.maximum(jnp.maximum(jnp.max(s_loc, axis=0, keepdims=True),
                                        jnp.max(s_ctx, axis=0, keepdims=True)), sink)
            p_loc = jnp.exp(s_loc - m)
            p_ctx = jnp.exp(s_ctx - m)
            den = (jnp.sum(p_loc, axis=0, keepdims=True) + jnp.sum(p_ctx, axis=0, keepdims=True)
                   + jnp.exp(sink - m))
            o = (jnp.dot(vt_loc[cols, :], p_loc.astype(BF16), preferred_element_type=F32)
                 + jnp.dot(vt_ctx[cols, :], p_ctx.astype(BF16), preferred_element_type=F32))
            outs.append(o / den)
        both = jnp.concatenate(outs, axis=0)
        o_ref[qrows, :] = jnp.concatenate(
            [both[:, i * ATT_BLOCK:(i + 1) * ATT_BLOCK].T for i in range(ATT_GROUP)], axis=1).astype(o_ref.dtype)


def _lat_attention(u, cache_k, cache_v, seg, q_gain, k_gain, sink):
    qcol = 3 * HY_WIDTH // ATT_WIDTH
    kcol = (3 * HY_WIDTH + ATT_WIDTH) // KV_WIDTH
    nq = 2
    qrows = nq * ATT_BLOCK
    nblk = L_LAT // qrows
    first_q_block = N_CTX_TOK // qrows
    first_seq = N_CTX_TOK // L_LAT
    cos, sin = _rope_tables()
    return pl.pallas_call(
        functools.partial(_lat_attn_kernel, nq=nq),
        grid_spec=pltpu.PrefetchScalarGridSpec(
            num_scalar_prefetch=1,
            grid=(N_LAT_SEQ, nblk),
            in_specs=[
                pl.BlockSpec((qrows, ATT_WIDTH), lambda b, n, s: (first_q_block + b * nblk + n, qcol)),
                pl.BlockSpec((L_LAT, KV_WIDTH), lambda b, n, s: (first_seq + b, kcol)),
                pl.BlockSpec((L_LAT, KV_WIDTH), lambda b, n, s: (first_seq + b, kcol + 1)),
                pl.BlockSpec((1, PAST_LEN, KV_WIDTH), lambda b, n, s: (b, 0, 0)),
                pl.BlockSpec((1, PAST_LEN, KV_WIDTH), lambda b, n, s: (b, 0, 0)),
                pl.BlockSpec((qrows, ATT_WIDTH), lambda b, n, s: (n, 0)),
                pl.BlockSpec((qrows, ATT_WIDTH), lambda b, n, s: (n, 0)),
                pl.BlockSpec((L_LAT, KV_WIDTH), lambda b, n, s: (0, 0)),
                pl.BlockSpec((L_LAT, KV_WIDTH), lambda b, n, s: (0, 0)),
                pl.BlockSpec((ATT_WIDTH, ATT_WIDTH), lambda b, n, s: (0, 0)),
                pl.BlockSpec((1, ATT_WIDTH), lambda b, n, s: (0, 0)),
                pl.BlockSpec((1, KV_WIDTH), lambda b, n, s: (0, 0)),
            ],
            out_specs=pl.BlockSpec((qrows, ATT_WIDTH), lambda b, n, s: (b * nblk + n, 0)),
            scratch_shapes=[pltpu.VMEM((L_LAT, KV_WIDTH), BF16)],
        ),
        out_shape=jax.ShapeDtypeStruct((N_LAT_SEQ * L_LAT, ATT_WIDTH), BF16),
        name="lat_attention",
    )(sink, u, u, u, cache_k.reshape(N_LAT_SEQ, PAST_LEN, KV_WIDTH), cache_v.reshape(N_LAT_SEQ, PAST_LEN, KV_WIDTH),
      cos, sin, cos[:, :KV_WIDTH], sin[:, :KV_WIDTH], seg,
      jnp.tile(q_gain, ATT_HEADS).reshape(1, ATT_WIDTH), jnp.tile(k_gain, ATT_KV_HEADS).reshape(1, KV_WIDTH))


def _ceil_to(v, m):
    return ((v + (m - 1)) // m) * m


def _post_mixer(x, mix, refs, tm):
    (w_ref, gain_ref, mod_ref, rwh_ref, rwm_ref, rb_ref, tri_ref, low_ref,
     x1_ref, sw_ref, cnt_ref, xg_ref, wb_ref) = refs

    @pl.when(pl.program_id(0) == 0)
    def _():
        wb_ref[...] = w_ref[...].astype(BF16)

    y = jnp.dot(mix, wb_ref[...], preferred_element_type=F32)
    x1 = x + mod_ref[0, 2:3, :] * y
    x1_ref[...] = x1
    xt = _norm_mod(x1, gain_ref[...], mod_ref[0, 3:4, :], mod_ref[0, 4:5, :])
    xh = xt.astype(BF16)
    xm = (xt - xh.astype(F32)).astype(BF16)
    logits = (lax.dot_general(rwh_ref[...], xh, _DOT_NT, preferred_element_type=F32)
              + lax.dot_general(rwm_ref[...], xh, _DOT_NT, preferred_element_type=F32)
              + lax.dot_general(rwh_ref[...], xm, _DOT_NT, preferred_element_type=F32)) + rb_ref[...]
    expert = lax.broadcasted_iota(jnp.int32, logits.shape, 0)
    vals, hits = [], []
    for _ in range(TOP_K):
        m = jnp.max(logits, axis=0, keepdims=True)
        sel = jnp.min(jnp.where(logits == m, expert, N_EXPERTS), axis=0, keepdims=True)
        vals.append(m)
        hits.append(expert == sel)
        logits = jnp.where(expert == sel, -jnp.inf, logits)
    es = [jnp.exp(v - vals[0]) for v in vals]
    den = es[0] + es[1] + es[2] + es[3]
    weights = [e / den for e in es]

    routed = sum(h.astype(F32) for h in hits)
    counts = jnp.sum(routed, axis=1, keepdims=True)
    cnt_ref[0] = counts.astype(jnp.int32)
    padded = _ceil_to(counts.astype(jnp.int32), MOE_GROUP).astype(F32)
    g0 = jnp.dot(low_ref[...], jnp.broadcast_to(padded, (N_EXPERTS, LANES)),
                 preferred_element_type=F32, precision=HIGHEST)[:, 0:1]
    earlier = jnp.dot(routed.astype(BF16), tri_ref[...], preferred_element_type=F32)
    row_of = g0 + earlier
    slots = [jnp.sum(jnp.where(h, row_of, 0.0), axis=0, keepdims=True) for h in hits]

    prow = lax.broadcasted_iota(jnp.int32, (LANES, tm), 0)
    packed = jnp.zeros((LANES, tm), F32)
    for k, vec in enumerate(slots + weights):
        packed = jnp.where(prow == k, vec, packed)
    sw_ref[...] = packed.T

    group_row = lax.broadcasted_iota(jnp.int32, (MOE_GROUPED_ROWS, tm), 0).astype(F32)
    perm = jnp.zeros((MOE_GROUPED_ROWS, tm), F32)
    for k in range(TOP_K):
        perm = jnp.where(group_row == slots[k], 1.0, perm)
    perm = perm.astype(BF16)
    xg_ref[...] = jnp.dot(perm, xh, preferred_element_type=F32).astype(BF16)


def _post_even_kernel(xc_ref, xl_ref, ac_ref, al_ref, tc_ref, tl_ref, *refs, tm):
    is_ctx = pl.program_id(0) * tm < N_CTX_TOK
    mix = jnp.concatenate([jnp.where(is_ctx, ac_ref[...], al_ref[...]),
                           jnp.where(is_ctx, tc_ref[...], tl_ref[...])], axis=1)
    _post_mixer(jnp.where(is_ctx, xc_ref[...], xl_ref[...]), mix, refs, tm)


def _post_odd_kernel(x_ref, yc_ref, yl_ref, *refs, tm):
    mix = jnp.where(pl.program_id(0) * tm < N_CTX_TOK, yc_ref[...], yl_ref[...])
    _post_mixer(x_ref[...], mix, refs, tm)


def _post_call(kernel_fn, name, mixer_specs, mixer_args, k_in, w_out, gain, mod, router_w, router_b):
    tm = ROW_TILE
    nsteps = N_TOK // tm
    row = lax.broadcasted_iota(jnp.int32, (tm, tm), 0)
    col = lax.broadcasted_iota(jnp.int32, (tm, tm), 1)
    tri = (row < col).astype(BF16)
    er = lax.broadcasted_iota(jnp.int32, (N_EXPERTS, N_EXPERTS), 0)
    ec = lax.broadcasted_iota(jnp.int32, (N_EXPERTS, N_EXPERTS), 1)
    low = (ec < er).astype(F32)
    rw_t = router_w.T
    rw_hi = rw_t.astype(BF16)
    rw_mid = (rw_t - rw_hi.astype(F32)).astype(BF16)
    const = lambda shape: pl.BlockSpec(shape, lambda i: (0,) * len(shape))
    return pl.pallas_call(
        functools.partial(kernel_fn, tm=tm),
        grid=(nsteps,),
        in_specs=mixer_specs + [
            const((k_in, D_MODEL)),
            const((1, D_MODEL)),
            pl.BlockSpec((1, 6, D_MODEL), lambda i: (_cond_of_tile(i, tm), 0, 0)),
            const((N_EXPERTS, D_MODEL)), const((N_EXPERTS, D_MODEL)), const((N_EXPERTS, 1)),
            const((tm, tm)), const((N_EXPERTS, N_EXPERTS)),
        ],
        out_specs=[
            pl.BlockSpec((tm, D_MODEL), lambda i: (i, 0)),
            pl.BlockSpec((tm, LANES), lambda i: (i, 0)),
            pl.BlockSpec((1, N_EXPERTS, 1), lambda i: (i, 0, 0)),
            pl.BlockSpec((MOE_GROUPED_ROWS, D_MODEL), lambda i: (i, 0)),
        ],
        out_shape=[
            jax.ShapeDtypeStruct((N_TOK, D_MODEL), F32),
            jax.ShapeDtypeStruct((N_TOK, LANES), F32),
            jax.ShapeDtypeStruct((nsteps, N_EXPERTS, 1), jnp.int32),
            jax.ShapeDtypeStruct((nsteps * MOE_GROUPED_ROWS, D_MODEL), BF16),
        ],
        scratch_shapes=[pltpu.VMEM((k_in, D_MODEL), BF16)],
        compiler_params=_vmem(56),
        name=name,
    )(*mixer_args, w_out, gain.reshape(1, D_MODEL), mod, rw_hi, rw_mid, router_b.reshape(N_EXPERTS, 1), tri, low)


def _post_even(x_ctx, x_lat, a_ctx, a_lat, t_ctx, t_lat, w_out, gain, mod, router_w, router_b):
    tm = ROW_TILE
    nctx = N_CTX_TOK // tm
    ctx_map = lambda i: (jnp.minimum(i, nctx - 1), 0)
    lat_map = lambda i: (jnp.maximum(i - nctx, 0), 0)
    specs = [pl.BlockSpec((tm, D_MODEL), ctx_map), pl.BlockSpec((tm, D_MODEL), lat_map),
             pl.BlockSpec((tm, HY_WIDTH), ctx_map), pl.BlockSpec((tm, HY_WIDTH), lat_map),
             pl.BlockSpec((tm, ATT_WIDTH), ctx_map), pl.BlockSpec((tm, ATT_WIDTH), lat_map)]
    return _post_call(_post_even_kernel, "post_even", specs, (x_ctx, x_lat, a_ctx, a_lat, t_ctx, t_lat),
                      HY_WIDTH + ATT_WIDTH, w_out, gain, mod, router_w, router_b)


def _post_odd(x, y_ctx, y_lat, w_out, gain, mod, router_w, router_b):
    tm = ROW_TILE
    nctx = N_CTX_TOK // tm
    specs = [pl.BlockSpec((tm, D_MODEL), lambda i: (i, 0)),
             pl.BlockSpec((tm, RET_V_WIDTH), lambda i: (jnp.minimum(i, nctx - 1), 0)),
             pl.BlockSpec((tm, RET_V_WIDTH), lambda i: (jnp.maximum(i - nctx, 0), 0))]
    return _post_call(_post_odd_kernel, "post_odd", specs, (x, y_ctx, y_lat),
                      RET_V_WIDTH, w_out, gain, mod, router_w, router_b)


def _moe_plan(counts):
    t, grp, big = MOE_TILE, MOE_GROUP, MOE_GROUPED_ROWS
    n16 = _ceil_to(counts.reshape(-1, N_EXPERTS), grp)
    nsteps = n16.shape[0]
    g0 = jnp.cumsum(n16, axis=1) - n16
    e0 = jnp.cumsum(n16, axis=0) - n16
    rows_e = jnp.sum(n16, axis=0)
    ntiles = (rows_e + t - 1) // t
    tile_end = jnp.cumsum(ntiles)
    tile_first = tile_end - ntiles
    total = tile_end[-1]
    experts = jnp.arange(N_EXPERTS, dtype=jnp.int32)
    tiles = jnp.arange(MOE_NUM_TILES, dtype=jnp.int32)
    valid = tiles < total
    ti = jnp.minimum(tiles, total - 1)
    e_of = jnp.sum((tile_end[None, :] <= ti[:, None]).astype(jnp.int32), axis=1)
    onehot = (e_of[:, None] == experts[None, :]).astype(jnp.int32)
    pick = lambda v: jnp.sum(onehot * v[None, :], axis=1)
    pick2 = lambda m: jnp.sum(onehot[:, None, :] * m[None, :, :], axis=2)
    r = ((ti - pick(tile_first)) * t)[:, None] + grp * jnp.arange(t // grp, dtype=jnp.int32)[None, :]
    ends = pick2(e0 + n16)
    step = jnp.sum((ends[:, None, :] <= r[:, :, None]).astype(jnp.int32), axis=2)
    step = jnp.minimum(step, nsteps - 1)
    sel = (step[:, :, None] == jnp.arange(nsteps, dtype=jnp.int32)[None, None, :]).astype(jnp.int32)
    at_step = lambda m: jnp.sum(sel * pick2(m)[:, None, :], axis=2)
    src = step * big + at_step(g0) + r - at_step(e0)
    live = jnp.logical_and(valid[:, None], r < pick(rows_e)[:, None])
    moe_src = jnp.where(live, src, 0).reshape(-1)
    g = grp * jnp.arange(big // grp, dtype=jnp.int32)
    gend = g0 + n16
    ce = jnp.sum((gend[:, None, :] <= g[None, :, None]).astype(jnp.int32), axis=2)
    used = ce < N_EXPERTS
    ce = jnp.minimum(ce, N_EXPERTS - 1)
    csel = (ce[:, :, None] == experts[None, None, :]).astype(jnp.int32)
    of_e = lambda m: jnp.sum(csel * m[:, None, :], axis=2)
    base = jnp.sum(csel * (tile_first * t)[None, None, :], axis=2)
    csrc = base + of_e(e0) + g[None, :] - of_e(g0)
    comb_src = jnp.where(used, csrc, 0).reshape(-1)
    used_e = ntiles > 0
    rank = jnp.cumsum(used_e.astype(jnp.int32)) - 1
    later = jnp.logical_and(used_e[None, :], experts[None, :] > experts[:, None])
    next_e = jnp.min(jnp.where(later, experts[None, :], N_EXPERTS), axis=1)
    next_e = jnp.where(next_e == N_EXPERTS, -1, next_e)
    half = jnp.logical_and(valid, pick(rows_e) - (ti - pick(tile_first)) * t <= t // 2)
    wbuf = jnp.stack([pick(rank) % 2, pick(next_e), half.astype(jnp.int32)], axis=1).reshape(-1)
    as_i32 = lambda v: v.astype(jnp.int32)
    return as_i32(e_of), as_i32(valid), as_i32(wbuf), as_i32(moe_src), as_i32(comb_src)


def _moe_kernel(te_ref, tv_ref, nx_ref, src_ref, xg_hbm, w1_hbm, b1_ref, w2_hbm, b2_ref, y_ref,
                xbuf, w1f, w2f, w1b, w2b, sem_in, sem_w):
    i = pl.program_id(0)
    nt = pl.num_programs(0)
    t, grp = MOE_TILE, MOE_GROUP
    slot = i % 2

    def issue_gather(tile, sl):
        for c in range(t // grp):
            src = pl.multiple_of(src_ref[tile * (t // grp) + c], grp)
            pltpu.make_async_copy(xg_hbm.at[pl.ds(src, grp), :], xbuf.at[sl, pl.ds(c * grp, grp), :],
                                  sem_in.at[sl]).start()

    def weight_copies(e, ws):
        return (pltpu.make_async_copy(w1_hbm.at[e], w1f.at[ws], sem_w.at[ws]),
                pltpu.make_async_copy(w2_hbm.at[e], w2f.at[ws], sem_w.at[ws]))

    def valid(tile):
        return tv_ref[jnp.clip(tile, 0, nt - 1)] > 0

    @pl.when(i == 0)
    def _():
        issue_gather(0, 0)
        for cp in weight_copies(te_ref[0], 0):
            cp.start()

    @pl.when(valid(i))
    def _():
        pltpu.make_async_copy(xg_hbm.at[pl.ds(0, t), :], xbuf.at[slot], sem_in.at[slot]).wait()

        @pl.when(jnp.logical_and(i + 1 < nt, valid(i + 1)))
        def _():
            issue_gather(i + 1, 1 - slot)

        e = te_ref[i]
        first = jnp.logical_or(i == 0, e != te_ref[jnp.maximum(i - 1, 0)])
        ws = nx_ref[3 * i]
        nxt = nx_ref[3 * i + 1]
        half = nx_ref[3 * i + 2] > 0

        @pl.when(first)
        def _():
            for cp in weight_copies(e, ws):
                cp.wait()
            w1b[...] = w1f[ws].astype(BF16)
            w2b[...] = w2f[ws].astype(BF16)

            @pl.when(nxt >= 0)
            def _():
                for cp in weight_copies(nxt, 1 - ws):
                    cp.start()

        def expert_mlp(rows):
            h = jnp.dot(xbuf[slot, :rows], w1b[...], preferred_element_type=F32) + b1_ref[0]
            glu = jnp.minimum(h[:, :D_FF], SWIGLU_LIMIT)
            lin = jnp.clip(h[:, D_FF:], -SWIGLU_LIMIT, SWIGLU_LIMIT)
            act = (glu * _sigmoid(SWIGLU_ALPHA * glu) * (lin + 1.0)).astype(BF16)
            y = jnp.dot(act, w2b[...], preferred_element_type=F32) + b2_ref[0]
            y_ref[:rows, :] = y.astype(y_ref.dtype)

        @pl.when(jnp.logical_not(half))
        def _():
            expert_mlp(t)

        @pl.when(half)
        def _():
            expert_mlp(t // 2)
            y_ref[t // 2:, :] = jnp.zeros((t // 2, D_MODEL), y_ref.dtype)

    @pl.when(jnp.logical_not(valid(i)))
    def _():
        y_ref[...] = jnp.zeros_like(y_ref)


def _moe_experts(xg, plan, w1, b1, w2, b2):
    te, tv, nx, src, _ = plan
    t = MOE_TILE
    return pl.pallas_call(
        _moe_kernel,
        grid_spec=pltpu.PrefetchScalarGridSpec(
            num_scalar_prefetch=4,
            grid=(MOE_NUM_TILES,),
            in_specs=[
                pl.BlockSpec(memory_space=pl.ANY),
                pl.BlockSpec(memory_space=pl.ANY),
                pl.BlockSpec((1, 1, 2 * D_FF), lambda i, te, *_: (te[i], 0, 0)),
                pl.BlockSpec(memory_space=pl.ANY),
                pl.BlockSpec((1, 1, D_MODEL), lambda i, te, *_: (te[i], 0, 0)),
            ],
            out_specs=pl.BlockSpec((t, D_MODEL), lambda i, *_: (i, 0)),
            scratch_shapes=[
                pltpu.VMEM((2, t, D_MODEL), BF16),
                pltpu.VMEM((2, D_MODEL, 2 * D_FF), F32),
                pltpu.VMEM((2, D_FF, D_MODEL), F32),
                pltpu.VMEM((D_MODEL, 2 * D_FF), BF16),
                pltpu.VMEM((D_FF, D_MODEL), BF16),
                pltpu.SemaphoreType.DMA((2,)),
                pltpu.SemaphoreType.DMA((2,)),
            ],
        ),
        out_shape=jax.ShapeDtypeStruct((MOE_NUM_TILES * t, D_MODEL), BF16),
        compiler_params=_vmem(56),
        name="moe_experts",
    )(te, tv, nx, src, xg, w1, b1.reshape(N_EXPERTS, 1, 2 * D_FF), w2, b2.reshape(N_EXPERTS, 1, D_MODEL))


def _combine_value(src_ref, x1_ref, sw_ref, mod_ref, ys_hbm, ybuf, sem, tm):
    s = pl.program_id(0)
    nsteps = pl.num_programs(0)
    slot = s % 2
    grp, big = MOE_GROUP, MOE_GROUPED_ROWS
    nchunk = big // grp

    def issue_gather(step, sl):
        for c in range(nchunk):
            src = pl.multiple_of(src_ref[step * nchunk + c], grp)
            pltpu.make_async_copy(ys_hbm.at[pl.ds(src, grp), :], ybuf.at[sl, pl.ds(c * grp, grp), :],
                                  sem.at[sl]).start()

    @pl.when(s == 0)
    def _():
        issue_gather(0, 0)

    pltpu.make_async_copy(ys_hbm.at[pl.ds(0, big), :], ybuf.at[slot], sem.at[slot]).wait()

    @pl.when(s + 1 < nsteps)
    def _():
        issue_gather(s + 1, 1 - slot)

    sw = sw_ref[...]
    col = lax.broadcasted_iota(jnp.int32, (tm, big), 1).astype(F32)
    wmat = jnp.zeros((tm, big), F32)
    for k in range(TOP_K):
        wmat = jnp.where(col == sw[:, k:k + 1], sw[:, TOP_K + k:TOP_K + k + 1], wmat)
    moe = jnp.dot(wmat.astype(BF16), ybuf[slot], preferred_element_type=F32)
    return x1_ref[...] + mod_ref[0, 5:6, :] * moe


def _combine_kernel(src_ref, x1_ref, sw_ref, mod_ref, ys_hbm, o_ref, ybuf, sem, *, tm):
    o_ref[...] = _combine_value(src_ref, x1_ref, sw_ref, mod_ref, ys_hbm, ybuf, sem, tm)


def _combine_split_kernel(src_ref, x1_ref, sw_ref, mod_ref, ys_hbm, oc_ref, ol_ref, ybuf, sem, *, tm):
    val = _combine_value(src_ref, x1_ref, sw_ref, mod_ref, ys_hbm, ybuf, sem, tm)
    is_ctx = pl.program_id(0) * tm < N_CTX_TOK

    @pl.when(is_ctx)
    def _():
        oc_ref[...] = val

    @pl.when(jnp.logical_not(is_ctx))
    def _():
        ol_ref[...] = val


def _combine(x1, ys, sw, plan, mod, split):
    tm = ROW_TILE
    nctx = N_CTX_TOK // tm
    in_specs = [
        pl.BlockSpec((tm, D_MODEL), lambda i, *_: (i, 0)),
        pl.BlockSpec((tm, LANES), lambda i, *_: (i, 0)),
        pl.BlockSpec((1, 6, D_MODEL), lambda i, *_: (_cond_of_tile(i, tm), 0, 0)),
        pl.BlockSpec(memory_space=pl.ANY),
    ]
    scratch = [pltpu.VMEM((2, MOE_GROUPED_ROWS, D_MODEL), BF16), pltpu.SemaphoreType.DMA((2,))]
    if not split:
        kernel_fn, name = _combine_kernel, "moe_combine"
        out_specs = pl.BlockSpec((tm, D_MODEL), lambda i, *_: (i, 0))
        out_shape = jax.ShapeDtypeStruct((N_TOK, D_MODEL), F32)
    else:
        kernel_fn, name = _combine_split_kernel, "moe_combine_split"
        out_specs = [pl.BlockSpec((tm, D_MODEL), lambda i, *_: (jnp.minimum(i, nctx - 1), 0)),
                     pl.BlockSpec((tm, D_MODEL), lambda i, *_: (jnp.maximum(i - nctx, 0), 0))]
        out_shape = [jax.ShapeDtypeStruct((N_CTX_TOK, D_MODEL), F32),
                     jax.ShapeDtypeStruct((N_TOK - N_CTX_TOK, D_MODEL), F32)]
    return pl.pallas_call(
        functools.partial(kernel_fn, tm=tm),
        grid_spec=pltpu.PrefetchScalarGridSpec(
            num_scalar_prefetch=1, grid=(N_TOK // tm,), in_specs=in_specs, out_specs=out_specs,
            scratch_shapes=scratch),
        out_shape=out_shape,
        compiler_params=_vmem(56),
        name=name,
    )(plan[4], x1, sw, mod, ys)


def _retention_kernel(lg_ref, q_ref, k_ref, v_ref, gf_ref, gb_ref, *rest, length, has_s0, emit_state):
    rest = list(rest)
    s0_ref = rest.pop(0) if has_s0 else None
    o_ref = rest.pop(0)
    so_ref = rest.pop(0) if emit_state else None
    s_ref, yf_ref, yb_ref = rest
    c = RET_CHUNK
    nc = length // c
    ii = lax.broadcasted_iota(jnp.int32, (c, c), 0).astype(F32)
    jj = lax.broadcasted_iota(jnp.int32, (c, c), 1).astype(F32)
    ci = lax.broadcasted_iota(jnp.int32, (c, 1), 0).astype(F32)

    def decays(direction):
        lg = -jnp.exp(lg_ref[direction, 0])
        lg1 = lg[:, 0:1]
        if direction == 0:
            diff = ii - jj
            q_decay = jnp.exp(lg1 * (ci + 1.0))
            k_decay = jnp.exp(lg1 * (c - 1.0 - ci))
        else:
            diff = jj - ii
            q_decay = jnp.exp(lg1 * (c - ci))
            k_decay = jnp.exp(lg1 * ci)
        scale = RET_DK ** -0.5
        inner = jnp.where(diff >= 0, jnp.exp(lg * jnp.maximum(diff, 0.0)), 0.0) * scale
        return inner, q_decay, k_decay * scale, jnp.exp(lg1 * float(c))

    def chunk(direction, ch, consts, g_ref, y_ref):
        inner, q_decay, k_decay, chunk_decay = consts
        rows = pl.ds(pl.multiple_of(ch * c, c), c)
        qc = q_ref[rows, :]
        kc = k_ref[rows, :]
        vc = v_ref[rows, :]
        s = s_ref[direction]
        att = lax.dot_general(qc, kc, _DOT_NT, preferred_element_type=F32) * inner
        o = (jnp.dot(att.astype(BF16), vc, preferred_element_type=F32)
             + jnp.dot(qc, s.astype(BF16), preferred_element_type=F32) * q_decay)
        kd = (kc.astype(F32) * k_decay).T.astype(BF16)
        s_ref[direction] = s * chunk_decay + jnp.dot(kd, vc, preferred_element_type=F32)
        on = o * lax.rsqrt(jnp.mean(o * o, axis=-1, keepdims=True) + EPS)
        g = g_ref[rows, :].astype(F32)
        y_ref[rows, :] = g * _sigmoid(g) * on

    for direction in range(2):
        if has_s0:
            s_ref[direction] = s0_ref[0, direction, 0]
        else:
            s_ref[direction] = jnp.zeros((RET_DK, RET_DV), F32)
    forward, backward = decays(0), decays(1)

    def body(step, carry):
        chunk(0, step, forward, gf_ref, yf_ref)
        chunk(1, nc - 1 - step, backward, gb_ref, yb_ref)
        return carry

    lax.fori_loop(0, nc, body, 0)
    o_ref[...] = (yf_ref[...] + yb_ref[...]).astype(o_ref.dtype)
    if emit_state:
        for direction in range(2):
            so_ref[0, direction, 0] = s_ref[direction]


def _retention(u, first_seq, nseq, length, decay_logit, s0, emit_state):
    row0 = first_seq
    lg = jnp.broadcast_to(decay_logit.astype(F32)[:, :, None, None], (2, RET_HEADS, 1, LANES))
    kcol = RET_QK_WIDTH // RET_DK
    vcol = 2 * RET_QK_WIDTH // RET_DV
    gfcol = vcol + RET_HEADS
    gbcol = gfcol + RET_HEADS
    in_specs = [
        pl.BlockSpec((2, 1, 1, LANES), lambda b, h: (0, h, 0, 0)),
        pl.BlockSpec((length, RET_DK), lambda b, h: (row0 + b, h)),
        pl.BlockSpec((length, RET_DK), lambda b, h: (row0 + b, kcol + h)),
        pl.BlockSpec((length, RET_DV), lambda b, h: (row0 + b, vcol + h)),
        pl.BlockSpec((length, RET_DV), lambda b, h: (row0 + b, gfcol + h)),
        pl.BlockSpec((length, RET_DV), lambda b, h: (row0 + b, gbcol + h)),
    ]
    args = [lg, u, u, u, u, u]
    state_spec = pl.BlockSpec((1, 2, 1, RET_DK, RET_DV), lambda b, h: (b, 0, h, 0, 0))
    if s0 is not None:
        in_specs.append(state_spec)
        args.append(s0)
    out_specs = [pl.BlockSpec((length, RET_DV), lambda b, h: (b, h))]
    out_shape = [jax.ShapeDtypeStruct((nseq * length, RET_V_WIDTH), BF16)]
    if emit_state:
        out_specs.append(state_spec)
        out_shape.append(jax.ShapeDtypeStruct((nseq, 2, RET_HEADS, RET_DK, RET_DV), F32))
    return pl.pallas_call(
        functools.partial(_retention_kernel, length=length, has_s0=s0 is not None, emit_state=emit_state),
        grid=(nseq, RET_HEADS),
        in_specs=in_specs,
        out_specs=out_specs,
        out_shape=out_shape,
        scratch_shapes=[pltpu.VMEM((2, RET_DK, RET_DV), F32), pltpu.VMEM((length, RET_DV), F32),
                        pltpu.VMEM((length, RET_DV), F32)],
        compiler_params=_vmem(48),
        name=f"retention_{length}",
    )(*args)


def kernel(x_prompt, x_sample, cache_k0, cache_v0, state_ret1, c, c_ctx, l0_norm_mix, l0_ada_w, l0_ada_b, l0_w_in, l0_conv_w, l0_conv_b, l0_filt_w1, l0_filt_b1, l0_filt_freq, l0_filt_w2, l0_filt_b2, l0_filt_w3, l0_filt_deltas, l0_hy_skip, l0_q_gain, l0_k_gain, l0_sink, l0_w_out, l0_norm_ffn, l0_router_w, l0_router_b, l0_moe_w1, l0_moe_b1, l0_moe_w2, l0_moe_b2, l1_norm_mix, l1_ada_w, l1_ada_b, l1_w_in, l1_ret_decay_logit, l1_w_out, l1_norm_ffn, l1_router_w, l1_router_b, l1_moe_w1, l1_moe_b1, l1_moe_w2, l1_moe_b2):
    x_ctx = x_prompt.reshape(N_CTX_TOK, D_MODEL)
    x_lat = x_sample.reshape(N_TOK - N_CTX_TOK, D_MODEL)
    cond = jnp.zeros((SUBLANES, D_MODEL), F32).at[0].set(c_ctx).at[1:1 + N_LAT_SEQ].set(c)
    mod0 = _adaln(cond, l0_ada_w, l0_ada_b)
    mod1 = _adaln(cond, l1_ada_w, l1_ada_b)

    u = _in_proj((x_ctx, x_lat), l0_norm_mix, mod0, l0_w_in, EVEN_IN // 2, F32)
    filt = (l0_filt_w1, l0_filt_b1, l0_filt_freq, l0_filt_w2, l0_filt_b2, l0_filt_w3, l0_filt_deltas)
    hy = []
    for first_block, nseq, length in ((0, N_CTX_SEQ, L_CTX), (N_CTX_TOK // L_LAT // N_LAT_SEQ, N_LAT_SEQ, L_LAT)):
        cmat, smat, stmat = _dft_matrices(length)
        tc, ts = _hyena_filter(length, _filter_features(length), cmat, smat, *filt)
        hy.append(_hyena(u, first_block, nseq, length, l0_conv_w, l0_conv_b, l0_hy_skip, tc, ts, cmat, smat, stmat))
    head = lax.broadcasted_iota(jnp.int32, (ATT_WIDTH, ATT_WIDTH), 0) // HEAD_DIM
    seg = (head == head.T).astype(BF16)
    att_ctx, new_k, new_v = _ctx_attention(u, seg, l0_q_gain, l0_k_gain, l0_sink)
    att_lat = _lat_attention(u, cache_k0, cache_v0, seg, l0_q_gain, l0_k_gain, l0_sink)
    w_att = l0_w_out[HY_WIDTH:].reshape(ATT_KV_HEADS, ATT_GROUP, HEAD_DIM, D_MODEL).swapaxes(0, 1)
    w_out0 = jnp.concatenate([l0_w_out[:HY_WIDTH], w_att.reshape(ATT_WIDTH, D_MODEL)], axis=0)
    x1, sw, counts, xg = _post_even(x_ctx, x_lat, hy[0], hy[1], att_ctx, att_lat, w_out0, l0_norm_ffn, mod0,
                                    l0_router_w, l0_router_b)
    plan = _moe_plan(counts)
    ys = _moe_experts(xg, plan, l0_moe_w1, l0_moe_b1, l0_moe_w2, l0_moe_b2)
    x = _combine(x1, ys, sw, plan, mod0, split=False)

    u = _in_proj((x,), l1_norm_mix, mod1, l1_w_in, 2048, BF16)
    y_ctx, new_state = _retention(u, 0, N_CTX_SEQ, L_CTX, l1_ret_decay_logit, None, True)
    (y_lat,) = _retention(u, N_CTX_TOK // L_LAT, N_LAT_SEQ, L_LAT, l1_ret_decay_logit, state_ret1, False)
    x1, sw, counts, xg = _post_odd(x, y_ctx, y_lat, l1_w_out, l1_norm_ffn, mod1, l1_router_w, l1_router_b)
    plan = _moe_plan(counts)
    ys = _moe_experts(xg, plan, l1_moe_w1, l1_moe_b1, l1_moe_w2, l1_moe_b2)
    y_prompt, y_sample = _combine(x1, ys, sw, plan, mod1, split=True)

    return (y_prompt.reshape(N_CTX_SEQ, L_CTX, D_MODEL), y_sample.reshape(N_LAT_SEQ, L_LAT, D_MODEL),
            new_k.reshape(N_CTX_SEQ, L_CTX, ATT_KV_HEADS, HEAD_DIM),
            new_v.reshape(N_CTX_SEQ, L_CTX, ATT_KV_HEADS, HEAD_DIM), new_state)
```

```python
import functools
import math

import jax
import jax.numpy as jnp
from jax import lax
from jax.experimental import pallas as pl
from jax.experimental.pallas import tpu as pltpu

F32 = jnp.float32
BF16 = jnp.bfloat16

D_MODEL = 1024
N_CTX_SEQ, L_CTX = 16, 256
N_LAT_SEQ, L_LAT = 2, 2048
N_CTX_TOK = N_CTX_SEQ * L_CTX
N_TOK = N_CTX_TOK + N_LAT_SEQ * L_LAT
PAST_LEN = 512
EPS = 1e-6
NEG_BIG = -1e30

HY_WIDTH = 512
HY_BANDS = 16
HY_FILTER_HIDDEN = 64
HY_FEAT_PAD = 64

ATT_HEADS, ATT_KV_HEADS, HEAD_DIM = 8, 2, 64
ATT_GROUP = ATT_HEADS // ATT_KV_HEADS
ATT_WIDTH = ATT_HEADS * HEAD_DIM
KV_WIDTH = ATT_KV_HEADS * HEAD_DIM
ATT_SCALE = HEAD_DIM ** -0.5
WINDOW = 128
ATT_BLOCK = 128
ROPE_THETA = 10000.0
ROPE_FREQS = HEAD_DIM // 4
GRID_W = 64
EVEN_IN = 3 * HY_WIDTH + ATT_WIDTH + 2 * KV_WIDTH

RET_HEADS = 4
RET_DK = 256
RET_DV = 512
RET_CHUNK = 128
RET_QK_WIDTH = RET_HEADS * RET_DK
RET_V_WIDTH = RET_HEADS * RET_DV
ODD_IN = 2 * RET_QK_WIDTH + 3 * RET_V_WIDTH

N_EXPERTS = 32
TOP_K = 4
D_FF = 1024
SWIGLU_ALPHA = 1.702
SWIGLU_LIMIT = 7.0

SUBLANES = 8
LANES = 128

MOE_TILE = 512
MOE_GROUP = 16
ROW_TILE = 512
IN_PROJ_TILE = 1024
MOE_GROUPED_ROWS = 2560
MOE_NUM_TILES = 112


def _vmem(mib):
    return pltpu.CompilerParams(vmem_limit_bytes=mib * 1024 * 1024)


def _cond_of_tile(i, tm):
    row = i * tm
    return jnp.where(row < N_CTX_TOK, 0, 1 + (row - N_CTX_TOK) // L_LAT)


def _sigmoid(x):
    return 1.0 / (1.0 + jnp.exp(-x))


def _split_bf16(a):
    hi = a.astype(BF16)
    return hi, (a - hi.astype(F32)).astype(BF16)


def _dot3(a, b):
    a_hi, a_mid = _split_bf16(a)
    b_hi, b_mid = _split_bf16(b)
    return (jnp.dot(a_hi, b_hi, preferred_element_type=F32) + jnp.dot(a_hi, b_mid, preferred_element_type=F32)
            + jnp.dot(a_mid, b_hi, preferred_element_type=F32))


def _norm_mod(x, gain, shift, scale):
    ms = jnp.mean(x * x, axis=-1, keepdims=True)
    return (x * lax.rsqrt(ms + EPS) * gain) * (1.0 + scale) + shift


def _adaln_kernel(c_ref, w_ref, b_ref, o_ref):
    c = c_ref[...]
    s = c * _sigmoid(c)
    o_ref[...] = _dot3(s, w_ref[...]) + b_ref[...]


def _adaln(cond, w, b):
    n = w.shape[1]
    tn = 1024
    out = pl.pallas_call(
        _adaln_kernel,
        grid=(n // tn,),
        in_specs=[
            pl.BlockSpec((SUBLANES, D_MODEL), lambda j: (0, 0)),
            pl.BlockSpec((D_MODEL, tn), lambda j: (0, j)),
            pl.BlockSpec((1, tn), lambda j: (0, j)),
        ],
        out_specs=pl.BlockSpec((SUBLANES, tn), lambda j: (0, j)),
        out_shape=jax.ShapeDtypeStruct((SUBLANES, n), F32),
        name="adaln",
    )(cond, w, b.reshape(1, n))
    return out.reshape(SUBLANES, 6, D_MODEL)


def _in_proj_kernel(*refs, tm):
    *x_refs, gain_ref, mod_ref, w_ref, o_ref, wb_ref = refs
    i = pl.program_id(1)

    @pl.when(i == 0)
    def _():
        wb_ref[...] = w_ref[...].astype(BF16)

    if len(x_refs) == 2:
        x = jnp.where(i * tm < N_CTX_TOK, x_refs[0][...], x_refs[1][...])
    else:
        x = x_refs[0][...]
    h = _norm_mod(x, gain_ref[...], mod_ref[0, 0:1, :], mod_ref[0, 1:2, :])
    o_ref[...] = jnp.dot(h.astype(BF16), wb_ref[...], preferred_element_type=F32).astype(o_ref.dtype)


def _in_proj(xs, gain, mod, w, tn, out_dtype):
    n = w.shape[1]
    tm = IN_PROJ_TILE
    nctx = N_CTX_TOK // tm
    if len(xs) == 2:
        x_specs = [pl.BlockSpec((tm, D_MODEL), lambda j, i: (jnp.minimum(i, nctx - 1), 0)),
                   pl.BlockSpec((tm, D_MODEL), lambda j, i: (jnp.maximum(i - nctx, 0), 0))]
    else:
        x_specs = [pl.BlockSpec((tm, D_MODEL), lambda j, i: (i, 0))]
    return pl.pallas_call(
        functools.partial(_in_proj_kernel, tm=tm),
        grid=(n // tn, N_TOK // tm),
        in_specs=x_specs + [
            pl.BlockSpec((1, D_MODEL), lambda j, i: (0, 0)),
            pl.BlockSpec((1, 6, D_MODEL), lambda j, i: (_cond_of_tile(i, tm), 0, 0)),
            pl.BlockSpec((D_MODEL, tn), lambda j, i: (0, j)),
        ],
        out_specs=pl.BlockSpec((tm, tn), lambda j, i: (i, j)),
        out_shape=jax.ShapeDtypeStruct((N_TOK, n), out_dtype),
        scratch_shapes=[pltpu.VMEM((D_MODEL, tn), BF16)],
        compiler_params=_vmem(48),
        name="in_proj",
    )(*xs, gain.reshape(1, D_MODEL), mod, w)


def _dft_matrices(length):
    n = 2 * length
    lo = 16
    s = jnp.arange(length, dtype=jnp.int32)
    k1 = jnp.arange(length // lo, dtype=jnp.int32) * lo
    k0 = jnp.arange(lo, dtype=jnp.int32)
    ang1 = (2.0 * math.pi / n) * ((k1[:, None] * s[None, :]) % n).astype(F32)
    ang0 = (2.0 * math.pi / n) * ((k0[:, None] * s[None, :]) % n).astype(F32)
    c1, s1 = jnp.cos(ang1)[:, None, :], jnp.sin(ang1)[:, None, :]
    c0, s0 = jnp.cos(ang0)[None, :, :], jnp.sin(ang0)[None, :, :]
    cmat = (c1 * c0 - s1 * s0).reshape(length, length)
    smat = (s1 * c0 + c1 * s0).reshape(length, length)
    sign = jnp.where(s % 2 == 0, 1.0, -1.0).astype(F32)
    row = lax.broadcasted_iota(jnp.int32, (length, length), 0)
    col = lax.broadcasted_iota(jnp.int32, (length, length), 1)
    s_nyq = jnp.where(row == 0, sign[None, :], smat)
    st_nyq = jnp.where(col == 0, sign[:, None], smat)
    return cmat.astype(BF16), s_nyq.astype(BF16), st_nyq.astype(BF16)


def _filter_features(length):
    t = jnp.linspace(0.0, 1.0, length, dtype=F32)[:, None]
    w = 2.0 * math.pi * jnp.arange(length, dtype=F32)[:, None] / length
    f = jnp.linspace(1e-4, HY_BANDS - 1, HY_BANDS, dtype=F32)[None, :]
    z = jnp.concatenate([t, jnp.cos(f * w), -jnp.sin(f * w)], axis=-1)
    return jnp.pad(z, ((0, 0), (0, HY_FEAT_PAD - z.shape[1])))


def _filter_kernel(z_ref, w1_ref, b1_ref, fr_ref, w2_ref, b2_ref, w3_ref, dl_ref, c_ref, s_ref,
                   tc_ref, ts_ref, taps_ref, *, length, rb):
    r = pl.program_id(0)

    @pl.when(r == 0)
    def _():
        z = z_ref[...]
        fr = fr_ref[...]
        h = jnp.sin(fr * (_dot3(z, w1_ref[...]) + b1_ref[...]))
        h = jnp.sin(fr * (_dot3(h, w2_ref[...]) + b2_ref[...]))
        h = _dot3(h, w3_ref[...])
        win = jnp.exp(-z[:, 0:1] * jnp.abs(dl_ref[...]))
        hf = h[:, :HY_WIDTH] * win
        hb = h[:, HY_WIDTH:] * win
        row = lax.broadcasted_iota(jnp.int32, (length, HY_WIDTH), 0)
        hb = jnp.where(row == 0, 0.0, hb)
        l1 = jnp.sum(jnp.abs(hf), axis=0, keepdims=True) + jnp.sum(jnp.abs(hb), axis=0, keepdims=True)
        inv = 1.0 / l1
        taps_ref[:, :HY_WIDTH] = (hf * inv).astype(BF16)
        taps_ref[:, HY_WIDTH:] = (hb * inv).astype(BF16)

    taps = taps_ref[...]
    rc = jnp.dot(c_ref[...], taps, preferred_element_type=F32)
    rs = jnp.dot(s_ref[...], taps, preferred_element_type=F32)
    tc = rc[:, :HY_WIDTH] + rc[:, HY_WIDTH:]
    ts = rs[:, :HY_WIDTH] - rs[:, HY_WIDTH:]
    grow = r * rb + lax.broadcasted_iota(jnp.int32, (rb, HY_WIDTH), 0)
    is0 = grow == 0
    ts = jnp.where(is0, rs[:, :HY_WIDTH] + rs[:, HY_WIDTH:], ts)
    wgt = jnp.where(is0, 1.0 / (2 * length), 2.0 / (2 * length))
    tc_ref[...] = tc * wgt
    ts_ref[...] = ts * wgt


def _hyena_filter(length, feats, cmat, smat, w1, b1, freq, w2, b2, w3, deltas):
    rb = min(length, 512)
    w1p = jnp.pad(w1, ((0, HY_FEAT_PAD - w1.shape[0]), (0, 0)))
    full = lambda shape: pl.BlockSpec(shape, lambda r: (0,) * len(shape))
    hid = HY_FILTER_HIDDEN
    return pl.pallas_call(
        functools.partial(_filter_kernel, length=length, rb=rb),
        grid=(length // rb,),
        in_specs=[
            full((length, HY_FEAT_PAD)), full((HY_FEAT_PAD, hid)), full((1, hid)), full((1, hid)),
            full((hid, hid)), full((1, hid)), full((hid, 2 * HY_WIDTH)), full((1, HY_WIDTH)),
            pl.BlockSpec((rb, length), lambda r: (r, 0)),
            pl.BlockSpec((rb, length), lambda r: (r, 0)),
        ],
        out_specs=[pl.BlockSpec((rb, HY_WIDTH), lambda r: (r, 0)),
                   pl.BlockSpec((rb, HY_WIDTH), lambda r: (r, 0))],
        out_shape=[jax.ShapeDtypeStruct((length, HY_WIDTH), F32)] * 2,
        scratch_shapes=[pltpu.VMEM((length, 2 * HY_WIDTH), BF16)],
        compiler_params=_vmem(48),
        name=f"hyena_filter_{length}",
    )(feats, w1p, b1.reshape(1, hid), freq.reshape(1, hid), w2, b2.reshape(1, hid), w3,
      deltas.reshape(1, HY_WIDTH), cmat, smat)


def _hyena_kernel(x0_ref, x1_ref, v_ref, w0_ref, w1_ref, w2_ref, b0_ref, b1_ref, b2_ref, skip_ref,
                  tc_ref, ts_ref, c_ref, s_ref, ct_ref, st_ref, o_ref,
                  zb_ref, zs_ref, x0c_ref, acc_ref, *, nseq, length, cb, fb):
    f = pl.program_id(1)
    nf = pl.num_programs(1)

    def short_conv(u, w_ref, b_ref):
        row = lax.broadcasted_iota(jnp.int32, u.shape, 0)
        prev = jnp.where(row == 0, 0.0, pltpu.roll(u, 1, 0))
        nxt = jnp.where(row == length - 1, 0.0, pltpu.roll(u, length - 1, 0))
        return prev * w_ref[0:1, :] + u * w_ref[1:2, :] + nxt * w_ref[2:3, :] + b_ref[...]

    @pl.when(f == 0)
    def _():
        for b in range(nseq):
            cols = slice(b * cb, (b + 1) * cb)
            x0c_ref[:, cols] = short_conv(x0_ref[b], w0_ref, b0_ref)
            z = short_conv(x1_ref[b], w1_ref, b1_ref) * short_conv(v_ref[b], w2_ref, b2_ref)
            zb_ref[:, cols] = z.astype(BF16)
            zs_ref[:, cols] = z * skip_ref[...]
        acc_ref[...] = jnp.zeros_like(acc_ref)

    zb = zb_ref[...]
    zc = jnp.dot(c_ref[...], zb, preferred_element_type=F32)
    zsn = jnp.dot(s_ref[...], zb, preferred_element_type=F32)
    tc = jnp.concatenate([tc_ref[...]] * nseq, axis=1)
    ts = jnp.concatenate([ts_ref[...]] * nseq, axis=1)
    grow = f * fb + lax.broadcasted_iota(jnp.int32, zc.shape, 0)
    is0 = grow == 0
    yc = jnp.where(is0, zc * tc, zc * tc - zsn * ts)
    ys = jnp.where(is0, zsn * ts, zc * ts + zsn * tc)
    acc_ref[...] += (jnp.dot(ct_ref[...], yc.astype(BF16), preferred_element_type=F32)
                     + jnp.dot(st_ref[...], ys.astype(BF16), preferred_element_type=F32))

    @pl.when(f == nf - 1)
    def _():
        for b in range(nseq):
            cols = slice(b * cb, (b + 1) * cb)
            o_ref[b] = (x0c_ref[:, cols] * (acc_ref[:, cols] + zs_ref[:, cols])).astype(o_ref.dtype)


def _hyena(u, first_seq_block, nseq, length, conv_w, conv_b, skip, tc, ts, cmat, smat, stmat):
    cb = 128
    fb = min(length, 512)
    u3 = u.reshape(N_TOK // length, length, EVEN_IN)
    ncb = HY_WIDTH // cb
    width = nseq * cb

    def ublock(part):
        return pl.BlockSpec((nseq, length, cb), lambda c, f: (first_seq_block, 0, part * ncb + c))

    def wblock(part, rows):
        return pl.BlockSpec((rows, cb), lambda c, f: (0, part * ncb + c))

    return pl.pallas_call(
        functools.partial(_hyena_kernel, nseq=nseq, length=length, cb=cb, fb=fb),
        grid=(ncb, length // fb),
        in_specs=[
            ublock(0), ublock(1), ublock(2),
            wblock(0, 3), wblock(1, 3), wblock(2, 3),
            wblock(0, 1), wblock(1, 1), wblock(2, 1),
            pl.BlockSpec((1, cb), lambda c, f: (0, c)),
            pl.BlockSpec((fb, cb), lambda c, f: (f, c)),
            pl.BlockSpec((fb, cb), lambda c, f: (f, c)),
            pl.BlockSpec((fb, length), lambda c, f: (f, 0)),
            pl.BlockSpec((fb, length), lambda c, f: (f, 0)),
            pl.BlockSpec((length, fb), lambda c, f: (0, f)),
            pl.BlockSpec((length, fb), lambda c, f: (0, f)),
        ],
        out_specs=pl.BlockSpec((nseq, length, cb), lambda c, f: (0, 0, c)),
        out_shape=jax.ShapeDtypeStruct((nseq, length, HY_WIDTH), BF16),
        scratch_shapes=[pltpu.VMEM((length, width), BF16), pltpu.VMEM((length, width), F32),
                        pltpu.VMEM((length, width), F32), pltpu.VMEM((length, width), F32)],
        compiler_params=_vmem(48),
        name=f"hyena_{length}",
    )(u3, u3, u3, conv_w, conv_w, conv_w, conv_b.reshape(1, -1), conv_b.reshape(1, -1),
      conv_b.reshape(1, -1), skip.reshape(1, HY_WIDTH), tc, ts, cmat, smat, cmat, stmat
      ).reshape(nseq * length, HY_WIDTH)


def _head_rms(x, seg, gain):
    x2 = x * x
    hi = x2.astype(BF16)
    lo = (x2 - hi.astype(F32)).astype(BF16)
    ss = jnp.dot(hi, seg, preferred_element_type=F32) + jnp.dot(lo, seg, preferred_element_type=F32)
    return x * lax.rsqrt(ss * (1.0 / HEAD_DIM) + EPS) * gain


def _rope(x, cos, sin_signed):
    width = x.shape[1]
    lane = lax.broadcasted_iota(jnp.int32, x.shape, 1)
    first = (lane // ROPE_FREQS) % 2 == 0
    partner = jnp.where(first, pltpu.roll(x, width - ROPE_FREQS, 1), pltpu.roll(x, ROPE_FREQS, 1))
    return x * cos + partner * sin_signed


_DOT_NT = (((1,), (1,)), ((), ()))


def _softmax_over_keys(s, sink_row):
    m = jnp.maximum(jnp.max(s, axis=0, keepdims=True), sink_row)
    p = jnp.exp(s - m)
    return p, jnp.sum(p, axis=0, keepdims=True) + jnp.exp(sink_row - m)


def _ctx_attn_kernel(sink_ref, q_ref, k_ref, v_ref, seg_ref, qg_ref, kg_ref, o_ref, ko_ref, vo_ref, *, nseq):
    seg = seg_ref[...]
    for b in range(nseq):
        rows = slice(b * L_CTX, (b + 1) * L_CTX)
        qt = (_head_rms(q_ref[rows, :], seg, qg_ref[...]) * ATT_SCALE).T.astype(BF16)
        kn = _head_rms(k_ref[rows, :], seg[:KV_WIDTH, :KV_WIDTH], kg_ref[...])
        v = v_ref[rows, :]
        ko_ref[rows, :] = kn
        vo_ref[rows, :] = v
        kb = kn.astype(BF16)
        vt = v.T.astype(BF16)
        blocks = []
        for j in range(ATT_GROUP):
            outs = []
            for g in range(ATT_KV_HEADS):
                h = g * ATT_GROUP + j
                cols = slice(g * HEAD_DIM, (g + 1) * HEAD_DIM)
                s = jnp.dot(kb[:, cols], qt[h * HEAD_DIM:(h + 1) * HEAD_DIM, :], preferred_element_type=F32)
                p, den = _softmax_over_keys(s, jnp.full((1, L_CTX), sink_ref[h], F32))
                outs.append(jnp.dot(vt[cols, :], p.astype(BF16), preferred_element_type=F32) / den)
            blocks.append(jnp.concatenate(outs, axis=0).T)
        o_ref[rows, :] = jnp.concatenate(blocks, axis=1).astype(o_ref.dtype)


def _ctx_attention(u, seg, q_gain, k_gain, sink):
    qcol = 3 * HY_WIDTH // ATT_WIDTH
    kcol = (3 * HY_WIDTH + ATT_WIDTH) // KV_WIDTH
    nseq = 2
    rows = nseq * L_CTX
    return pl.pallas_call(
        functools.partial(_ctx_attn_kernel, nseq=nseq),
        grid_spec=pltpu.PrefetchScalarGridSpec(
            num_scalar_prefetch=1,
            grid=(N_CTX_SEQ // nseq,),
            in_specs=[
                pl.BlockSpec((rows, ATT_WIDTH), lambda b, s: (b, qcol)),
                pl.BlockSpec((rows, KV_WIDTH), lambda b, s: (b, kcol)),
                pl.BlockSpec((rows, KV_WIDTH), lambda b, s: (b, kcol + 1)),
                pl.BlockSpec((ATT_WIDTH, ATT_WIDTH), lambda b, s: (0, 0)),
                pl.BlockSpec((1, ATT_WIDTH), lambda b, s: (0, 0)),
                pl.BlockSpec((1, KV_WIDTH), lambda b, s: (0, 0)),
            ],
            out_specs=[
                pl.BlockSpec((rows, ATT_WIDTH), lambda b, s: (b, 0)),
                pl.BlockSpec((rows, KV_WIDTH), lambda b, s: (b, 0)),
                pl.BlockSpec((rows, KV_WIDTH), lambda b, s: (b, 0)),
            ],
        ),
        out_shape=[jax.ShapeDtypeStruct((N_CTX_TOK, ATT_WIDTH), BF16),
                   jax.ShapeDtypeStruct((N_CTX_TOK, KV_WIDTH), F32),
                   jax.ShapeDtypeStruct((N_CTX_TOK, KV_WIDTH), F32)],
        name="ctx_attention",
    )(sink, u, u, u, seg, jnp.tile(q_gain, ATT_HEADS).reshape(1, ATT_WIDTH),
      jnp.tile(k_gain, ATT_KV_HEADS).reshape(1, KV_WIDTH))


def _rope_tables():
    pos = jnp.arange(L_LAT, dtype=jnp.int32)
    row = (pos // GRID_W).astype(F32)
    col = (pos % GRID_W).astype(F32)
    inv = ROPE_THETA ** (-jnp.arange(ROPE_FREQS, dtype=F32) / ROPE_FREQS)
    ar, ac = row[:, None] * inv, col[:, None] * inv
    cos = jnp.concatenate([jnp.cos(ar), jnp.cos(ar), jnp.cos(ac), jnp.cos(ac)], axis=-1)
    sin = jnp.concatenate([-jnp.sin(ar), jnp.sin(ar), -jnp.sin(ac), jnp.sin(ac)], axis=-1)
    return jnp.tile(cos, (1, ATT_HEADS)), jnp.tile(sin, (1, ATT_HEADS))


def _lat_attn_kernel(sink_ref, q_ref, k_ref, v_ref, ck_ref, cv_ref, cosq_ref, sinq_ref, cosk_ref, sink_k_ref,
                     seg_ref, qg_ref, kg_ref, o_ref, kn_ref, *, nq):
    n = pl.program_id(1)
    seg = seg_ref[...]

    @pl.when(n == 0)
    def _():
        kn = _head_rms(k_ref[...], seg[:KV_WIDTH, :KV_WIDTH], kg_ref[...])
        kn_ref[...] = _rope(kn, cosk_ref[...], sink_k_ref[...]).astype(BF16)

    k_ctx = ck_ref[0].astype(BF16)
    vt_ctx = cv_ref[0].T.astype(BF16)
    span = 3 * ATT_BLOCK
    lanes = ATT_GROUP * ATT_BLOCK
    for j in range(nq):
        blk = n * nq + j
        qrows = slice(j * ATT_BLOCK, (j + 1) * ATT_BLOCK)
        qn = _head_rms(q_ref[qrows, :], seg, qg_ref[...])
        qt = (_rope(qn, cosq_ref[qrows, :], sinq_ref[qrows, :]) * ATT_SCALE).T.astype(BF16)
        start = pl.multiple_of(jnp.clip((blk - 1) * ATT_BLOCK, 0, L_LAT - span), ATT_BLOCK)
        q_pos = blk * ATT_BLOCK + (lax.broadcasted_iota(jnp.int32, (span, lanes), 1) % ATT_BLOCK)
        k_pos = start + lax.broadcasted_iota(jnp.int32, (span, lanes), 0)
        valid = jnp.abs(q_pos - k_pos) <= WINDOW
        k_loc = kn_ref[pl.ds(start, span), :]
        vt_loc = v_ref[pl.ds(start, span), :].T.astype(BF16)
        outs = []
        for g in range(ATT_KV_HEADS):
            cols = slice(g * HEAD_DIM, (g + 1) * HEAD_DIM)
            heads = range(g * ATT_GROUP, (g + 1) * ATT_GROUP)
            q = jnp.concatenate([qt[h * HEAD_DIM:(h + 1) * HEAD_DIM, :] for h in heads], axis=1)
            sink = jnp.concatenate([jnp.full((1, ATT_BLOCK), sink_ref[h], F32) for h in heads], axis=1)
            s_loc = jnp.where(valid, jnp.dot(k_loc[:, cols], q, preferred_element_type=F32), NEG_BIG)
            s_ctx = jnp.dot(k_ctx[:, cols], q, preferred_element_type=F32)
            m = jnp.maximum(jnp.maximum(jnp.max(s_loc, axis=0, keepdims=True),
                                        jnp.max(s_ctx, axis=0, keepdims=True)), sink)
            p_loc = jnp.exp(s_loc - m)
            p_ctx = jnp.exp(s_ctx - m)
            den = (jnp.sum(p_loc, axis=0, keepdims=True) + jnp.sum(p_ctx, axis=0, keepdims=True)
                   + jnp.exp(sink - m))
            o = (jnp.dot(vt_loc[cols, :], p_loc.astype(BF16), preferred_element_type=F32)
                 + jnp.dot(vt_ctx[cols, :], p_ctx.astype(BF16), preferred_element_type=F32))
            outs.append(o / den)
        both = jnp.concatenate(outs, axis=0)
        o_ref[qrows, :] = jnp.concatenate(
            [both[:, i * ATT_BLOCK:(i + 1) * ATT_BLOCK].T for i in range(ATT_GROUP)], axis=1).astype(o_ref.dtype)


def _lat_attention(u, cache_k, cache_v, seg, q_gain, k_gain, sink):
    qcol = 3 * HY_WIDTH // ATT_WIDTH
    kcol = (3 * HY_WIDTH + ATT_WIDTH) // KV_WIDTH
    nq = 2
    qrows = nq * ATT_BLOCK
    nblk = L_LAT // qrows
    first_q_block = N_CTX_TOK // qrows
    first_seq = N_CTX_TOK // L_LAT
    cos, sin = _rope_tables()
    return pl.pallas_call(
        functools.partial(_lat_attn_kernel, nq=nq),
        grid_spec=pltpu.PrefetchScalarGridSpec(
            num_scalar_prefetch=1,
            grid=(N_LAT_SEQ, nblk),
            in_specs=[
                pl.BlockSpec((qrows, ATT_WIDTH), lambda b, n, s: (first_q_block + b * nblk + n, qcol)),
                pl.BlockSpec((L_LAT, KV_WIDTH), lambda b, n, s: (first_seq + b, kcol)),
                pl.BlockSpec((L_LAT, KV_WIDTH), lambda b, n, s: (first_seq + b, kcol + 1)),
                pl.BlockSpec((1, PAST_LEN, KV_WIDTH), lambda b, n, s: (b, 0, 0)),
                pl.BlockSpec((1, PAST_LEN, KV_WIDTH), lambda b, n, s: (b, 0, 0)),
                pl.BlockSpec((qrows, ATT_WIDTH), lambda b, n, s: (n, 0)),
                pl.BlockSpec((qrows, ATT_WIDTH), lambda b, n, s: (n, 0)),
                pl.BlockSpec((L_LAT, KV_WIDTH), lambda b, n, s: (0, 0)),
                pl.BlockSpec((L_LAT, KV_WIDTH), lambda b, n, s: (0, 0)),
                pl.BlockSpec((ATT_WIDTH, ATT_WIDTH), lambda b, n, s: (0, 0)),
                pl.BlockSpec((1, ATT_WIDTH), lambda b, n, s: (0, 0)),
                pl.BlockSpec((1, KV_WIDTH), lambda b, n, s: (0, 0)),
            ],
            out_specs=pl.BlockSpec((qrows, ATT_WIDTH), lambda b, n, s: (b * nblk + n, 0)),
            scratch_shapes=[pltpu.VMEM((L_LAT, KV_WIDTH), BF16)],
        ),
        out_shape=jax.ShapeDtypeStruct((N_LAT_SEQ * L_LAT, ATT_WIDTH), BF16),
        name="lat_attention",
    )(sink, u, u, u, cache_k.reshape(N_LAT_SEQ, PAST_LEN, KV_WIDTH), cache_v.reshape(N_LAT_SEQ, PAST_LEN, KV_WIDTH),
      cos, sin, cos[:, :KV_WIDTH], sin[:, :KV_WIDTH], seg,
      jnp.tile(q_gain, ATT_HEADS).reshape(1, ATT_WIDTH), jnp.tile(k_gain, ATT_KV_HEADS).reshape(1, KV_WIDTH))


def _ceil_to(v, m):
    return ((v + (m - 1)) // m) * m


def _post_mixer(x, mix, refs, tm):
    (w_ref, gain_ref, mod_ref, rwh_ref, rwm_ref, rb_ref, tri_ref, low_ref,
     x1_ref, sw_ref, cnt_ref, xg_ref, wb_ref) = refs

    @pl.when(pl.program_id(0) == 0)
    def _():
        wb_ref[...] = w_ref[...].astype(BF16)

    y = jnp.dot(mix, wb_ref[...], preferred_element_type=F32)
    x1 = x + mod_ref[0, 2:3, :] * y
    x1_ref[...] = x1
    xt = _norm_mod(x1, gain_ref[...], mod_ref[0, 3:4, :], mod_ref[0, 4:5, :])
    xh = xt.astype(BF16)
    xm = (xt - xh.astype(F32)).astype(BF16)
    logits = (lax.dot_general(rwh_ref[...], xh, _DOT_NT, preferred_element_type=F32)
              + lax.dot_general(rwm_ref[...], xh, _DOT_NT, preferred_element_type=F32)
              + lax.dot_general(rwh_ref[...], xm, _DOT_NT, preferred_element_type=F32)) + rb_ref[...]
    expert = lax.broadcasted_iota(jnp.int32, logits.shape, 0)
    vals, hits = [], []
    for _ in range(TOP_K):
        m = jnp.max(logits, axis=0, keepdims=True)
        sel = jnp.min(jnp.where(logits == m, expert, N_EXPERTS), axis=0, keepdims=True)
        vals.append(m)
        hits.append(expert == sel)
        logits = jnp.where(expert == sel, -jnp.inf, logits)
    es = [jnp.exp(v - vals[0]) for v in vals]
    den = es[0] + es[1] + es[2] + es[3]
    weights = [e / den for e in es]

    routed = sum(h.astype(F32) for h in hits)
    counts = jnp.sum(routed, axis=1, keepdims=True)
    cnt_ref[0] = counts.astype(jnp.int32)
    units = (_ceil_to(counts.astype(jnp.int32), MOE_GROUP) // MOE_GROUP).astype(F32)
    g0 = MOE_GROUP * jnp.dot(low_ref[...], jnp.broadcast_to(units, (N_EXPERTS, LANES)).astype(BF16),
                             preferred_element_type=F32)[:, 0:1]
    earlier = jnp.dot(routed.astype(BF16), tri_ref[...], preferred_element_type=F32)
    row_of = g0 + earlier
    slots = [jnp.sum(jnp.where(h, row_of, 0.0), axis=0, keepdims=True) for h in hits]

    prow = lax.broadcasted_iota(jnp.int32, (LANES, tm), 0)
    packed = jnp.zeros((LANES, tm), F32)
    for k, vec in enumerate(slots + weights):
        packed = jnp.where(prow == k, vec, packed)
    sw_ref[...] = packed.T

    group_row = lax.broadcasted_iota(jnp.int32, (MOE_GROUPED_ROWS, tm), 0).astype(F32)
    perm = jnp.zeros((MOE_GROUPED_ROWS, tm), F32)
    for k in range(TOP_K):
        perm = jnp.where(group_row == slots[k], 1.0, perm)
    perm = perm.astype(BF16)
    xg_ref[...] = jnp.dot(perm, xh, preferred_element_type=F32).astype(BF16)


def _post_even_kernel(xc_ref, xl_ref, ac_ref, al_ref, tc_ref, tl_ref, *refs, tm):
    is_ctx = pl.program_id(0) * tm < N_CTX_TOK
    mix = jnp.concatenate([jnp.where(is_ctx, ac_ref[...], al_ref[...]),
                           jnp.where(is_ctx, tc_ref[...], tl_ref[...])], axis=1)
    _post_mixer(jnp.where(is_ctx, xc_ref[...], xl_ref[...]), mix, refs, tm)


def _post_odd_kernel(x_ref, yc_ref, yl_ref, *refs, tm):
    mix = jnp.where(pl.program_id(0) * tm < N_CTX_TOK, yc_ref[...], yl_ref[...])
    _post_mixer(x_ref[...], mix, refs, tm)


def _post_call(kernel_fn, name, mixer_specs, mixer_args, k_in, w_out, gain, mod, router_w, router_b):
    tm = ROW_TILE
    nsteps = N_TOK // tm
    row = lax.broadcasted_iota(jnp.int32, (tm, tm), 0)
    col = lax.broadcasted_iota(jnp.int32, (tm, tm), 1)
    tri = (row < col).astype(BF16)
    er = lax.broadcasted_iota(jnp.int32, (N_EXPERTS, N_EXPERTS), 0)
    ec = lax.broadcasted_iota(jnp.int32, (N_EXPERTS, N_EXPERTS), 1)
    low = (ec < er).astype(BF16)
    rw_t = router_w.T
    rw_hi = rw_t.astype(BF16)
    rw_mid = (rw_t - rw_hi.astype(F32)).astype(BF16)
    const = lambda shape: pl.BlockSpec(shape, lambda i: (0,) * len(shape))
    return pl.pallas_call(
        functools.partial(kernel_fn, tm=tm),
        grid=(nsteps,),
        in_specs=mixer_specs + [
            const((k_in, D_MODEL)),
            const((1, D_MODEL)),
            pl.BlockSpec((1, 6, D_MODEL), lambda i: (_cond_of_tile(i, tm), 0, 0)),
            const((N_EXPERTS, D_MODEL)), const((N_EXPERTS, D_MODEL)), const((N_EXPERTS, 1)),
            const((tm, tm)), const((N_EXPERTS, N_EXPERTS)),
        ],
        out_specs=[
            pl.BlockSpec((tm, D_MODEL), lambda i: (i, 0)),
            pl.BlockSpec((tm, LANES), lambda i: (i, 0)),
            pl.BlockSpec((1, N_EXPERTS, 1), lambda i: (i, 0, 0)),
            pl.BlockSpec((MOE_GROUPED_ROWS, D_MODEL), lambda i: (i, 0)),
        ],
        out_shape=[
            jax.ShapeDtypeStruct((N_TOK, D_MODEL), F32),
            jax.ShapeDtypeStruct((N_TOK, LANES), F32),
            jax.ShapeDtypeStruct((nsteps, N_EXPERTS, 1), jnp.int32),
            jax.ShapeDtypeStruct((nsteps * MOE_GROUPED_ROWS, D_MODEL), BF16),
        ],
        scratch_shapes=[pltpu.VMEM((k_in, D_MODEL), BF16)],
        compiler_params=_vmem(56),
        name=name,
    )(*mixer_args, w_out, gain.reshape(1, D_MODEL), mod, rw_hi, rw_mid, router_b.reshape(N_EXPERTS, 1), tri, low)


def _post_even(x_ctx, x_lat, a_ctx, a_lat, t_ctx, t_lat, w_out, gain, mod, router_w, router_b):
    tm = ROW_TILE
    nctx = N_CTX_TOK // tm
    ctx_map = lambda i: (jnp.minimum(i, nctx - 1), 0)
    lat_map = lambda i: (jnp.maximum(i - nctx, 0), 0)
    specs = [pl.BlockSpec((tm, D_MODEL), ctx_map), pl.BlockSpec((tm, D_MODEL), lat_map),
             pl.BlockSpec((tm, HY_WIDTH), ctx_map), pl.BlockSpec((tm, HY_WIDTH), lat_map),
             pl.BlockSpec((tm, ATT_WIDTH), ctx_map), pl.BlockSpec((tm, ATT_WIDTH), lat_map)]
    return _post_call(_post_even_kernel, "post_even", specs, (x_ctx, x_lat, a_ctx, a_lat, t_ctx, t_lat),
                      HY_WIDTH + ATT_WIDTH, w_out, gain, mod, router_w, router_b)


def _post_odd(x, y_ctx, y_lat, w_out, gain, mod, router_w, router_b):
    tm = ROW_TILE
    nctx = N_CTX_TOK // tm
    specs = [pl.BlockSpec((tm, D_MODEL), lambda i: (i, 0)),
             pl.BlockSpec((tm, RET_V_WIDTH), lambda i: (jnp.minimum(i, nctx - 1), 0)),
             pl.BlockSpec((tm, RET_V_WIDTH), lambda i: (jnp.maximum(i - nctx, 0), 0))]
    return _post_call(_post_odd_kernel, "post_odd", specs, (x, y_ctx, y_lat),
                      RET_V_WIDTH, w_out, gain, mod, router_w, router_b)


def _moe_plan(counts):
    t, grp, big = MOE_TILE, MOE_GROUP, MOE_GROUPED_ROWS
    n16 = _ceil_to(counts.reshape(-1, N_EXPERTS), grp)
    nsteps = n16.shape[0]
    g0 = jnp.cumsum(n16, axis=1) - n16
    e0 = jnp.cumsum(n16, axis=0) - n16
    rows_e = jnp.sum(n16, axis=0)
    ntiles = (rows_e + t - 1) // t
    tile_end = jnp.cumsum(ntiles)
    tile_first = tile_end - ntiles
    total = tile_end[-1]
    experts = jnp.arange(N_EXPERTS, dtype=jnp.int32)
    tiles = jnp.arange(MOE_NUM_TILES, dtype=jnp.int32)
    valid = tiles < total
    ti = jnp.minimum(tiles, total - 1)
    e_of = jnp.sum((tile_end[None, :] <= ti[:, None]).astype(jnp.int32), axis=1)
    onehot = (e_of[:, None] == experts[None, :]).astype(jnp.int32)
    pick = lambda v: jnp.sum(onehot * v[None, :], axis=1)
    pick2 = lambda m: jnp.sum(onehot[:, None, :] * m[None, :, :], axis=2)
    r = ((ti - pick(tile_first)) * t)[:, None] + grp * jnp.arange(t // grp, dtype=jnp.int32)[None, :]
    ends = pick2(e0 + n16)
    step = jnp.sum((ends[:, None, :] <= r[:, :, None]).astype(jnp.int32), axis=2)
    step = jnp.minimum(step, nsteps - 1)
    sel = (step[:, :, None] == jnp.arange(nsteps, dtype=jnp.int32)[None, None, :]).astype(jnp.int32)
    at_step = lambda m: jnp.sum(sel * pick2(m)[:, None, :], axis=2)
    src = step * big + at_step(g0) + r - at_step(e0)
    live = jnp.logical_and(valid[:, None], r < pick(rows_e)[:, None])
    moe_src = jnp.where(live, src, 0).reshape(-1)
    g = grp * jnp.arange(big // grp, dtype=jnp.int32)
    gend = g0 + n16
    ce = jnp.sum((gend[:, None, :] <= g[None, :, None]).astype(jnp.int32), axis=2)
    used = ce < N_EXPERTS
    ce = jnp.minimum(ce, N_EXPERTS - 1)
    csel = (ce[:, :, None] == experts[None, None, :]).astype(jnp.int32)
    of_e = lambda m: jnp.sum(csel * m[:, None, :], axis=2)
    base = jnp.sum(csel * (tile_first * t)[None, None, :], axis=2)
    csrc = base + of_e(e0) + g[None, :] - of_e(g0)
    comb_src = jnp.where(used, csrc, 0).reshape(-1)
    used_e = ntiles > 0
    rank = jnp.cumsum(used_e.astype(jnp.int32)) - 1
    later = jnp.logical_and(used_e[None, :], experts[None, :] > experts[:, None])
    next_e = jnp.min(jnp.where(later, experts[None, :], N_EXPERTS), axis=1)
    next_e = jnp.where(next_e == N_EXPERTS, -1, next_e)
    half = jnp.logical_and(valid, pick(rows_e) - (ti - pick(tile_first)) * t <= t // 2)
    wbuf = jnp.stack([pick(rank) % 2, pick(next_e), half.astype(jnp.int32)], axis=1).reshape(-1)
    as_i32 = lambda v: v.astype(jnp.int32)
    return as_i32(e_of), as_i32(valid), as_i32(wbuf), as_i32(moe_src), as_i32(comb_src)


def _moe_kernel(te_ref, tv_ref, nx_ref, src_ref, xg_hbm, w1_hbm, b1_ref, w2_hbm, b2_ref, y_ref,
                xbuf, w1f, w2f, w1b, w2b, sem_in, sem_w):
    i = pl.program_id(0)
    nt = pl.num_programs(0)
    t, grp = MOE_TILE, MOE_GROUP
    slot = i % 2

    def issue_gather(tile, sl):
        for c in range(t // grp):
            src = pl.multiple_of(src_ref[tile * (t // grp) + c], grp)
            pltpu.make_async_copy(xg_hbm.at[pl.ds(src, grp), :], xbuf.at[sl, pl.ds(c * grp, grp), :],
                                  sem_in.at[sl]).start()

    def weight_copies(e, ws):
        return (pltpu.make_async_copy(w1_hbm.at[e], w1f.at[ws], sem_w.at[ws]),
                pltpu.make_async_copy(w2_hbm.at[e], w2f.at[ws], sem_w.at[ws]))

    def valid(tile):
        return tv_ref[jnp.clip(tile, 0, nt - 1)] > 0

    @pl.when(i == 0)
    def _():
        issue_gather(0, 0)
        for cp in weight_copies(te_ref[0], 0):
            cp.start()

    @pl.when(valid(i))
    def _():
        pltpu.make_async_copy(xg_hbm.at[pl.ds(0, t), :], xbuf.at[slot], sem_in.at[slot]).wait()

        @pl.when(jnp.logical_and(i + 1 < nt, valid(i + 1)))
        def _():
            issue_gather(i + 1, 1 - slot)

        e = te_ref[i]
        first = jnp.logical_or(i == 0, e != te_ref[jnp.maximum(i - 1, 0)])
        ws = nx_ref[3 * i]
        nxt = nx_ref[3 * i + 1]
        half = nx_ref[3 * i + 2] > 0

        @pl.when(first)
        def _():
            for cp in weight_copies(e, ws):
                cp.wait()
            w1b[...] = w1f[ws].astype(BF16)
            w2b[...] = w2f[ws].astype(BF16)

            @pl.when(nxt >= 0)
            def _():
                for cp in weight_copies(nxt, 1 - ws):
                    cp.start()

        def expert_mlp(rows):
            h = jnp.dot(xbuf[slot, :rows], w1b[...], preferred_element_type=F32) + b1_ref[0]
            glu = jnp.minimum(h[:, :D_FF], SWIGLU_LIMIT)
            lin = jnp.clip(h[:, D_FF:], -SWIGLU_LIMIT, SWIGLU_LIMIT)
            act = (glu * _sigmoid(SWIGLU_ALPHA * glu) * (lin + 1.0)).astype(BF16)
            y = jnp.dot(act, w2b[...], preferred_element_type=F32) + b2_ref[0]
            y_ref[:rows, :] = y.astype(y_ref.dtype)

        @pl.when(jnp.logical_not(half))
        def _():
            expert_mlp(t)

        @pl.when(half)
        def _():
            expert_mlp(t // 2)
            y_ref[t // 2:, :] = jnp.zeros((t // 2, D_MODEL), y_ref.dtype)

    @pl.when(jnp.logical_not(valid(i)))
    def _():
        y_ref[...] = jnp.zeros_like(y_ref)


def _moe_experts(xg, plan, w1, b1, w2, b2):
    te, tv, nx, src, _ = plan
    t = MOE_TILE
    return pl.pallas_call(
        _moe_kernel,
        grid_spec=pltpu.PrefetchScalarGridSpec(
            num_scalar_prefetch=4,
            grid=(MOE_NUM_TILES,),
            in_specs=[
                pl.BlockSpec(memory_space=pl.ANY),
                pl.BlockSpec(memory_space=pl.ANY),
                pl.BlockSpec((1, 1, 2 * D_FF), lambda i, te, *_: (te[i], 0, 0)),
                pl.BlockSpec(memory_space=pl.ANY),
                pl.BlockSpec((1, 1, D_MODEL), lambda i, te, *_: (te[i], 0, 0)),
            ],
            out_specs=pl.BlockSpec((t, D_MODEL), lambda i, *_: (i, 0)),
            scratch_shapes=[
                pltpu.VMEM((2, t, D_MODEL), BF16),
                pltpu.VMEM((2, D_MODEL, 2 * D_FF), F32),
                pltpu.VMEM((2, D_FF, D_MODEL), F32),
                pltpu.VMEM((D_MODEL, 2 * D_FF), BF16),
                pltpu.VMEM((D_FF, D_MODEL), BF16),
                pltpu.SemaphoreType.DMA((2,)),
                pltpu.SemaphoreType.DMA((2,)),
            ],
        ),
        out_shape=jax.ShapeDtypeStruct((MOE_NUM_TILES * t, D_MODEL), BF16),
        compiler_params=_vmem(56),
        name="moe_experts",
    )(te, tv, nx, src, xg, w1, b1.reshape(N_EXPERTS, 1, 2 * D_FF), w2, b2.reshape(N_EXPERTS, 1, D_MODEL))


def _combine_value(src_ref, x1_ref, sw_ref, mod_ref, ys_hbm, ybuf, sem, tm):
    s = pl.program_id(0)
    nsteps = pl.num_programs(0)
    slot = s % 2
    grp, big = MOE_GROUP, MOE_GROUPED_ROWS
    nchunk = big // grp

    def issue_gather(step, sl):
        for c in range(nchunk):
            src = pl.multiple_of(src_ref[step * nchunk + c], grp)
            pltpu.make_async_copy(ys_hbm.at[pl.ds(src, grp), :], ybuf.at[sl, pl.ds(c * grp, grp), :],
                                  sem.at[sl]).start()

    @pl.when(s == 0)
    def _():
        issue_gather(0, 0)

    pltpu.make_async_copy(ys_hbm.at[pl.ds(0, big), :], ybuf.at[slot], sem.at[slot]).wait()

    @pl.when(s + 1 < nsteps)
    def _():
        issue_gather(s + 1, 1 - slot)

    sw = sw_ref[...]
    col = lax.broadcasted_iota(jnp.int32, (tm, big), 1).astype(F32)
    wmat = jnp.zeros((tm, big), F32)
    for k in range(TOP_K):
        wmat = jnp.where(col == sw[:, k:k + 1], sw[:, TOP_K + k:TOP_K + k + 1], wmat)
    moe = jnp.dot(wmat.astype(BF16), ybuf[slot], preferred_element_type=F32)
    return x1_ref[...] + mod_ref[0, 5:6, :] * moe


def _combine_kernel(src_ref, x1_ref, sw_ref, mod_ref, ys_hbm, o_ref, ybuf, sem, *, tm):
    o_ref[...] = _combine_value(src_ref, x1_ref, sw_ref, mod_ref, ys_hbm, ybuf, sem, tm)


def _combine_split_kernel(src_ref, x1_ref, sw_ref, mod_ref, ys_hbm, oc_ref, ol_ref, ybuf, sem, *, tm):
    val = _combine_value(src_ref, x1_ref, sw_ref, mod_ref, ys_hbm, ybuf, sem, tm)
    is_ctx = pl.program_id(0) * tm < N_CTX_TOK

    @pl.when(is_ctx)
    def _():
        oc_ref[...] = val

    @pl.when(jnp.logical_not(is_ctx))
    def _():
        ol_ref[...] = val


def _combine(x1, ys, sw, plan, mod, split):
    tm = ROW_TILE
    nctx = N_CTX_TOK // tm
    in_specs = [
        pl.BlockSpec((tm, D_MODEL), lambda i, *_: (i, 0)),
        pl.BlockSpec((tm, LANES), lambda i, *_: (i, 0)),
        pl.BlockSpec((1, 6, D_MODEL), lambda i, *_: (_cond_of_tile(i, tm), 0, 0)),
        pl.BlockSpec(memory_space=pl.ANY),
    ]
    scratch = [pltpu.VMEM((2, MOE_GROUPED_ROWS, D_MODEL), BF16), pltpu.SemaphoreType.DMA((2,))]
    if not split:
        kernel_fn, name = _combine_kernel, "moe_combine"
        out_specs = pl.BlockSpec((tm, D_MODEL), lambda i, *_: (i, 0))
        out_shape = jax.ShapeDtypeStruct((N_TOK, D_MODEL), F32)
    else:
        kernel_fn, name = _combine_split_kernel, "moe_combine_split"
        out_specs = [pl.BlockSpec((tm, D_MODEL), lambda i, *_: (jnp.minimum(i, nctx - 1), 0)),
                     pl.BlockSpec((tm, D_MODEL), lambda i, *_: (jnp.maximum(i - nctx, 0), 0))]
        out_shape = [jax.ShapeDtypeStruct((N_CTX_TOK, D_MODEL), F32),
                     jax.ShapeDtypeStruct((N_TOK - N_CTX_TOK, D_MODEL), F32)]
    return pl.pallas_call(
        functools.partial(kernel_fn, tm=tm),
        grid_spec=pltpu.PrefetchScalarGridSpec(
            num_scalar_prefetch=1, grid=(N_TOK // tm,), in_specs=in_specs, out_specs=out_specs,
            scratch_shapes=scratch),
        out_shape=out_shape,
        compiler_params=_vmem(56),
        name=name,
    )(plan[4], x1, sw, mod, ys)


def _retention_kernel(lg_ref, q_ref, k_ref, v_ref, gf_ref, gb_ref, *rest, length, has_s0, emit_state):
    rest = list(rest)
    s0_ref = rest.pop(0) if has_s0 else None
    o_ref = rest.pop(0)
    so_ref = rest.pop(0) if emit_state else None
    s_ref, yf_ref, yb_ref = rest
    c = RET_CHUNK
    nc = length // c
    ii = lax.broadcasted_iota(jnp.int32, (c, c), 0).astype(F32)
    jj = lax.broadcasted_iota(jnp.int32, (c, c), 1).astype(F32)
    ci = lax.broadcasted_iota(jnp.int32, (c, 1), 0).astype(F32)

    def decays(direction):
        lg = -jnp.exp(lg_ref[direction, 0])
        lg1 = lg[:, 0:1]
        if direction == 0:
            diff = ii - jj
            q_decay = jnp.exp(lg1 * (ci + 1.0))
            k_decay = jnp.exp(lg1 * (c - 1.0 - ci))
        else:
            diff = jj - ii
            q_decay = jnp.exp(lg1 * (c - ci))
            k_decay = jnp.exp(lg1 * ci)
        scale = RET_DK ** -0.5
        inner = jnp.where(diff >= 0, jnp.exp(lg * jnp.maximum(diff, 0.0)), 0.0) * scale
        return inner, q_decay, k_decay * scale, jnp.exp(lg1 * float(c))

    def chunk(direction, ch, consts, g_ref, y_ref):
        inner, q_decay, k_decay, chunk_decay = consts
        rows = pl.ds(pl.multiple_of(ch * c, c), c)
        qc = q_ref[rows, :]
        kc = k_ref[rows, :]
        vc = v_ref[rows, :]
        s = s_ref[direction]
        att = lax.dot_general(qc, kc, _DOT_NT, preferred_element_type=F32) * inner
        o = (jnp.dot(att.astype(BF16), vc, preferred_element_type=F32)
             + jnp.dot(qc, s.astype(BF16), preferred_element_type=F32) * q_decay)
        kd = (kc.astype(F32) * k_decay).T.astype(BF16)
        s_ref[direction] = s * chunk_decay + jnp.dot(kd, vc, preferred_element_type=F32)
        on = o * lax.rsqrt(jnp.mean(o * o, axis=-1, keepdims=True) + EPS)
        g = g_ref[rows, :].astype(F32)
        y_ref[rows, :] = g * _sigmoid(g) * on

    for direction in range(2):
        if has_s0:
            s_ref[direction] = s0_ref[0, direction, 0]
        else:
            s_ref[direction] = jnp.zeros((RET_DK, RET_DV), F32)
    forward, backward = decays(0), decays(1)

    def body(step, carry):
        chunk(0, step, forward, gf_ref, yf_ref)
        chunk(1, nc - 1 - step, backward, gb_ref, yb_ref)
        return carry

    lax.fori_loop(0, nc, body, 0)
    o_ref[...] = (yf_ref[...] + yb_ref[...]).astype(o_ref.dtype)
    if emit_state:
        for direction in range(2):
            so_ref[0, direction, 0] = s_ref[direction]


def _retention(u, first_seq, nseq, length, decay_logit, s0, emit_state):
    row0 = first_seq
    lg = jnp.broadcast_to(decay_logit.astype(F32)[:, :, None, None], (2, RET_HEADS, 1, LANES))
    kcol = RET_QK_WIDTH // RET_DK
    vcol = 2 * RET_QK_WIDTH // RET_DV
    gfcol = vcol + RET_HEADS
    gbcol = gfcol + RET_HEADS
    in_specs = [
        pl.BlockSpec((2, 1, 1, LANES), lambda b, h: (0, h, 0, 0)),
        pl.BlockSpec((length, RET_DK), lambda b, h: (row0 + b, h)),
        pl.BlockSpec((length, RET_DK), lambda b, h: (row0 + b, kcol + h)),
        pl.BlockSpec((length, RET_DV), lambda b, h: (row0 + b, vcol + h)),
        pl.BlockSpec((length, RET_DV), lambda b, h: (row0 + b, gfcol + h)),
        pl.BlockSpec((length, RET_DV), lambda b, h: (row0 + b, gbcol + h)),
    ]
    args = [lg, u, u, u, u, u]
    state_spec = pl.BlockSpec((1, 2, 1, RET_DK, RET_DV), lambda b, h: (b, 0, h, 0, 0))
    if s0 is not None:
        in_specs.append(state_spec)
        args.append(s0)
    out_specs = [pl.BlockSpec((length, RET_DV), lambda b, h: (b, h))]
    out_shape = [jax.ShapeDtypeStruct((nseq * length, RET_V_WIDTH), BF16)]
    if emit_state:
        out_specs.append(state_spec)
        out_shape.append(jax.ShapeDtypeStruct((nseq, 2, RET_HEADS, RET_DK, RET_DV), F32))
    return pl.pallas_call(
        functools.partial(_retention_kernel, length=length, has_s0=s0 is not None, emit_state=emit_state),
        grid=(nseq, RET_HEADS),
        in_specs=in_specs,
        out_specs=out_specs,
        out_shape=out_shape,
        scratch_shapes=[pltpu.VMEM((2, RET_DK, RET_DV), F32), pltpu.VMEM((length, RET_DV), F32),
                        pltpu.VMEM((length, RET_DV), F32)],
        compiler_params=_vmem(48),
        name=f"retention_{length}",
    )(*args)


def kernel(x_prompt, x_sample, cache_k0, cache_v0, state_ret1, c, c_ctx, l0_norm_mix, l0_ada_w, l0_ada_b, l0_w_in, l0_conv_w, l0_conv_b, l0_filt_w1, l0_filt_b1, l0_filt_freq, l0_filt_w2, l0_filt_b2, l0_filt_w3, l0_filt_deltas, l0_hy_skip, l0_q_gain, l0_k_gain, l0_sink, l0_w_out, l0_norm_ffn, l0_router_w, l0_router_b, l0_moe_w1, l0_moe_b1, l0_moe_w2, l0_moe_b2, l1_norm_mix, l1_ada_w, l1_ada_b, l1_w_in, l1_ret_decay_logit, l1_w_out, l1_norm_ffn, l1_router_w, l1_router_b, l1_moe_w1, l1_moe_b1, l1_moe_w2, l1_moe_b2):
    x_ctx = x_prompt.reshape(N_CTX_TOK, D_MODEL)
    x_lat = x_sample.reshape(N_TOK - N_CTX_TOK, D_MODEL)
    cond = jnp.zeros((SUBLANES, D_MODEL), F32).at[0].set(c_ctx).at[1:1 + N_LAT_SEQ].set(c)
    mod0 = _adaln(cond, l0_ada_w, l0_ada_b)
    mod1 = _adaln(cond, l1_ada_w, l1_ada_b)

    u = _in_proj((x_ctx, x_lat), l0_norm_mix, mod0, l0_w_in, EVEN_IN // 2, F32)
    filt = (l0_filt_w1, l0_filt_b1, l0_filt_freq, l0_filt_w2, l0_filt_b2, l0_filt_w3, l0_filt_deltas)
    hy = []
    for first_block, nseq, length in ((0, N_CTX_SEQ, L_CTX), (N_CTX_TOK // L_LAT // N_LAT_SEQ, N_LAT_SEQ, L_LAT)):
        cmat, smat, stmat = _dft_matrices(length)
        tc, ts = _hyena_filter(length, _filter_features(length), cmat, smat, *filt)
        hy.append(_hyena(u, first_block, nseq, length, l0_conv_w, l0_conv_b, l0_hy_skip, tc, ts, cmat, smat, stmat))
    head = lax.broadcasted_iota(jnp.int32, (ATT_WIDTH, ATT_WIDTH), 0) // HEAD_DIM
    seg = (head == head.T).astype(BF16)
    att_ctx, new_k, new_v = _ctx_attention(u, seg, l0_q_gain, l0_k_gain, l0_sink)
    att_lat = _lat_attention(u, cache_k0, cache_v0, seg, l0_q_gain, l0_k_gain, l0_sink)
    w_att = l0_w_out[HY_WIDTH:].reshape(ATT_KV_HEADS, ATT_GROUP, HEAD_DIM, D_MODEL).swapaxes(0, 1)
    w_out0 = jnp.concatenate([l0_w_out[:HY_WIDTH], w_att.reshape(ATT_WIDTH, D_MODEL)], axis=0)
    x1, sw, counts, xg = _post_even(x_ctx, x_lat, hy[0], hy[1], att_ctx, att_lat, w_out0, l0_norm_ffn, mod0,
                                    l0_router_w, l0_router_b)
    plan = _moe_plan(counts)
    ys = _moe_experts(xg, plan, l0_moe_w1, l0_moe_b1, l0_moe_w2, l0_moe_b2)
    x = _combine(x1, ys, sw, plan, mod0, split=False)

    u = _in_proj((x,), l1_norm_mix, mod1, l1_w_in, 2048, BF16)
    y_ctx, new_state = _retention(u, 0, N_CTX_SEQ, L_CTX, l1_ret_decay_logit, None, True)
    (y_lat,) = _retention(u, N_CTX_TOK // L_LAT, N_LAT_SEQ, L_LAT, l1_ret_decay_logit, state_ret1, False)
    x1, sw, counts, xg = _post_odd(x, y_ctx, y_lat, l1_w_out, l1_norm_ffn, mod1, l1_router_w, l1_router_b)
    plan = _moe_plan(counts)
    ys = _moe_experts(xg, plan, l1_moe_w1, l1_moe_b1, l1_moe_w2, l1_moe_b2)
    y_prompt, y_sample = _combine(x1, ys, sw, plan, mod1, split=True)

    return (y_prompt.reshape(N_CTX_SEQ, L_CTX, D_MODEL), y_sample.reshape(N_LAT_SEQ, L_LAT, D_MODEL),
            new_k.reshape(N_CTX_SEQ, L_CTX, ATT_KV_HEADS, HEAD_DIM),
            new_v.reshape(N_CTX_SEQ, L_CTX, ATT_KV_HEADS, HEAD_DIM), new_state)
```

```python
import functools
import math

import jax
import jax.numpy as jnp
from jax import lax
from jax.experimental import pallas as pl
from jax.experimental.pallas import tpu as pltpu

F32 = jnp.float32
BF16 = jnp.bfloat16

D_MODEL = 1024
N_CTX_SEQ, L_CTX = 16, 256
N_LAT_SEQ, L_LAT = 2, 2048
N_CTX_TOK = N_CTX_SEQ * L_CTX
N_TOK = N_CTX_TOK + N_LAT_SEQ * L_LAT
PAST_LEN = 512
EPS = 1e-6
NEG_BIG = -1e30

HY_WIDTH = 512
HY_BANDS = 16
HY_FILTER_HIDDEN = 64
HY_FEAT_PAD = 64

ATT_HEADS, ATT_KV_HEADS, HEAD_DIM = 8, 2, 64
ATT_GROUP = ATT_HEADS // ATT_KV_HEADS
ATT_WIDTH = ATT_HEADS * HEAD_DIM
KV_WIDTH = ATT_KV_HEADS * HEAD_DIM
ATT_SCALE = HEAD_DIM ** -0.5
WINDOW = 128
ATT_BLOCK = 128
ROPE_THETA = 10000.0
ROPE_FREQS = HEAD_DIM // 4
GRID_W = 64
EVEN_IN = 3 * HY_WIDTH + ATT_WIDTH + 2 * KV_WIDTH

RET_HEADS = 4
RET_DK = 256
RET_DV = 512
RET_CHUNK = 128
RET_QK_WIDTH = RET_HEADS * RET_DK
RET_V_WIDTH = RET_HEADS * RET_DV
ODD_IN = 2 * RET_QK_WIDTH + 3 * RET_V_WIDTH

N_EXPERTS = 32
TOP_K = 4
D_FF = 1024
SWIGLU_ALPHA = 1.702
SWIGLU_LIMIT = 7.0

SUBLANES = 8
LANES = 128

MOE_TILE = 512
MOE_GROUP = 16
ROW_TILE = 512
IN_PROJ_TILE = 1024
MOE_GROUPED_ROWS = 2560
MOE_NUM_TILES = 112


def _vmem(mib):
    return pltpu.CompilerParams(vmem_limit_bytes=mib * 1024 * 1024)


def _cond_of_tile(i, tm):
    row = i * tm
    return jnp.where(row < N_CTX_TOK, 0, 1 + (row - N_CTX_TOK) // L_LAT)


def _sigmoid(x):
    return 1.0 / (1.0 + jnp.exp(-x))


def _split_bf16(a):
    hi = a.astype(BF16)
    return hi, (a - hi.astype(F32)).astype(BF16)


def _dot3(a, b):
    a_hi, a_mid = _split_bf16(a)
    b_hi, b_mid = _split_bf16(b)
    return (jnp.dot(a_hi, b_hi, preferred_element_type=F32) + jnp.dot(a_hi, b_mid, preferred_element_type=F32)
            + jnp.dot(a_mid, b_hi, preferred_element_type=F32))


def _norm_mod(x, gain, shift, scale):
    ms = jnp.mean(x * x, axis=-1, keepdims=True)
    return (x * lax.rsqrt(ms + EPS) * gain) * (1.0 + scale) + shift


def _adaln_kernel(c_ref, w_ref, b_ref, o_ref):
    c = c_ref[...]
    s = c * _sigmoid(c)
    o_ref[...] = _dot3(s, w_ref[...]) + b_ref[...]


def _adaln(cond, w, b):
    n = w.shape[1]
    tn = 1024
    out = pl.pallas_call(
        _adaln_kernel,
        grid=(n // tn,),
        in_specs=[
            pl.BlockSpec((SUBLANES, D_MODEL), lambda j: (0, 0)),
            pl.BlockSpec((D_MODEL, tn), lambda j: (0, j)),
            pl.BlockSpec((1, tn), lambda j: (0, j)),
        ],
        out_specs=pl.BlockSpec((SUBLANES, tn), lambda j: (0, j)),
        out_shape=jax.ShapeDtypeStruct((SUBLANES, n), F32),
        name="adaln",
    )(cond, w, b.reshape(1, n))
    return out.reshape(SUBLANES, 6, D_MODEL)


def _in_proj_kernel(*refs, tm):
    *x_refs, gain_ref, mod_ref, w_ref, o_ref, wb_ref = refs
    i = pl.program_id(1)

    @pl.when(i == 0)
    def _():
        wb_ref[...] = w_ref[...].astype(BF16)

    if len(x_refs) == 2:
        x = jnp.where(i * tm < N_CTX_TOK, x_refs[0][...], x_refs[1][...])
    else:
        x = x_refs[0][...]
    h = _norm_mod(x, gain_ref[...], mod_ref[0, 0:1, :], mod_ref[0, 1:2, :])
    o_ref[...] = jnp.dot(h.astype(BF16), wb_ref[...], preferred_element_type=F32).astype(o_ref.dtype)


def _in_proj(xs, gain, mod, w, tn, out_dtype):
    n = w.shape[1]
    tm = IN_PROJ_TILE
    nctx = N_CTX_TOK // tm
    if len(xs) == 2:
        x_specs = [pl.BlockSpec((tm, D_MODEL), lambda j, i: (jnp.minimum(i, nctx - 1), 0)),
                   pl.BlockSpec((tm, D_MODEL), lambda j, i: (jnp.maximum(i - nctx, 0), 0))]
    else:
        x_specs = [pl.BlockSpec((tm, D_MODEL), lambda j, i: (i, 0))]
    return pl.pallas_call(
        functools.partial(_in_proj_kernel, tm=tm),
        grid=(n // tn, N_TOK // tm),
        in_specs=x_specs + [
            pl.BlockSpec((1, D_MODEL), lambda j, i: (0, 0)),
            pl.BlockSpec((1, 6, D_MODEL), lambda j, i: (_cond_of_tile(i, tm), 0, 0)),
            pl.BlockSpec((D_MODEL, tn), lambda j, i: (0, j)),
        ],
        out_specs=pl.BlockSpec((tm, tn), lambda j, i: (i, j)),
        out_shape=jax.ShapeDtypeStruct((N_TOK, n), out_dtype),
        scratch_shapes=[pltpu.VMEM((D_MODEL, tn), BF16)],
        compiler_params=_vmem(48),
        name="in_proj",
    )(*xs, gain.reshape(1, D_MODEL), mod, w)


def _dft_kernel(c1_ref, s1_ref, c0_ref, s0_ref, sign_row_ref, sign_col_ref, c_ref, s_ref, st_ref, *, lo, nk1):
    r = pl.program_id(0)
    c0, s0 = c0_ref[...], s0_ref[...]
    for a in range(nk1):
        c1 = c1_ref[a:a + 1, :]
        s1 = s1_ref[a:a + 1, :]
        rows = slice(a * lo, (a + 1) * lo)
        c_ref[rows, :] = (c1 * c0 - s1 * s0).astype(BF16)
        sin = s1 * c0 + c1 * s0
        first_row = jnp.logical_and(r == 0, lax.broadcasted_iota(jnp.int32, sin.shape, 0) + a * lo == 0)
        first_col = lax.broadcasted_iota(jnp.int32, sin.shape, 1) == 0
        s_ref[rows, :] = jnp.where(first_row, sign_row_ref[...], sin).astype(BF16)
        st_ref[rows, :] = jnp.where(first_col, sign_col_ref[rows, :], sin).astype(BF16)


def _dft_matrices(length):
    n = 2 * length
    lo = 16
    nk1 = 8
    block = lo * nk1
    s = jnp.arange(length, dtype=jnp.int32)
    k1 = jnp.arange(length // lo, dtype=jnp.int32) * lo
    k0 = jnp.arange(lo, dtype=jnp.int32)
    ang1 = (2.0 * math.pi / n) * ((k1[:, None] * s[None, :]) % n).astype(F32)
    ang0 = (2.0 * math.pi / n) * ((k0[:, None] * s[None, :]) % n).astype(F32)
    sign = jnp.where(s % 2 == 0, 1.0, -1.0).astype(F32)
    table = lambda rows: pl.BlockSpec((rows, length), lambda r: (r, 0))
    whole = lambda shape: pl.BlockSpec(shape, lambda r: (0, 0))
    return pl.pallas_call(
        functools.partial(_dft_kernel, lo=lo, nk1=nk1),
        grid=(length // block,),
        in_specs=[table(nk1), table(nk1), whole((lo, length)), whole((lo, length)), whole((1, length)),
                  pl.BlockSpec((block, 1), lambda r: (r, 0))],
        out_specs=[table(block)] * 3,
        out_shape=[jax.ShapeDtypeStruct((length, length), BF16)] * 3,
        name=f"dft_tables_{length}",
    )(jnp.cos(ang1), jnp.sin(ang1), jnp.cos(ang0), jnp.sin(ang0), sign[None, :], sign[:, None])


def _filter_features(length):
    t = jnp.linspace(0.0, 1.0, length, dtype=F32)[:, None]
    w = 2.0 * math.pi * jnp.arange(length, dtype=F32)[:, None] / length
    f = jnp.linspace(1e-4, HY_BANDS - 1, HY_BANDS, dtype=F32)[None, :]
    z = jnp.concatenate([t, jnp.cos(f * w), -jnp.sin(f * w)], axis=-1)
    return jnp.pad(z, ((0, 0), (0, HY_FEAT_PAD - z.shape[1])))


def _filter_kernel(z_ref, w1_ref, b1_ref, fr_ref, w2_ref, b2_ref, w3_ref, dl_ref, c_ref, s_ref,
                   tc_ref, ts_ref, taps_ref, *, length, rb):
    r = pl.program_id(0)

    @pl.when(r == 0)
    def _():
        z = z_ref[...]
        fr = fr_ref[...]
        h = jnp.sin(fr * (_dot3(z, w1_ref[...]) + b1_ref[...]))
        h = jnp.sin(fr * (_dot3(h, w2_ref[...]) + b2_ref[...]))
        h = _dot3(h, w3_ref[...])
        win = jnp.exp(-z[:, 0:1] * jnp.abs(dl_ref[...]))
        hf = h[:, :HY_WIDTH] * win
        hb = h[:, HY_WIDTH:] * win
        row = lax.broadcasted_iota(jnp.int32, (length, HY_WIDTH), 0)
        hb = jnp.where(row == 0, 0.0, hb)
        l1 = jnp.sum(jnp.abs(hf), axis=0, keepdims=True) + jnp.sum(jnp.abs(hb), axis=0, keepdims=True)
        inv = 1.0 / l1
        taps_ref[:, :HY_WIDTH] = (hf * inv).astype(BF16)
        taps_ref[:, HY_WIDTH:] = (hb * inv).astype(BF16)

    taps = taps_ref[...]
    rc = jnp.dot(c_ref[...], taps, preferred_element_type=F32)
    rs = jnp.dot(s_ref[...], taps, preferred_element_type=F32)
    tc = rc[:, :HY_WIDTH] + rc[:, HY_WIDTH:]
    ts = rs[:, :HY_WIDTH] - rs[:, HY_WIDTH:]
    grow = r * rb + lax.broadcasted_iota(jnp.int32, (rb, HY_WIDTH), 0)
    is0 = grow == 0
    ts = jnp.where(is0, rs[:, :HY_WIDTH] + rs[:, HY_WIDTH:], ts)
    wgt = jnp.where(is0, 1.0 / (2 * length), 2.0 / (2 * length))
    tc_ref[...] = tc * wgt
    ts_ref[...] = ts * wgt


def _hyena_filter(length, feats, cmat, smat, w1, b1, freq, w2, b2, w3, deltas):
    rb = min(length, 512)
    w1p = jnp.pad(w1, ((0, HY_FEAT_PAD - w1.shape[0]), (0, 0)))
    full = lambda shape: pl.BlockSpec(shape, lambda r: (0,) * len(shape))
    hid = HY_FILTER_HIDDEN
    return pl.pallas_call(
        functools.partial(_filter_kernel, length=length, rb=rb),
        grid=(length // rb,),
        in_specs=[
            full((length, HY_FEAT_PAD)), full((HY_FEAT_PAD, hid)), full((1, hid)), full((1, hid)),
            full((hid, hid)), full((1, hid)), full((hid, 2 * HY_WIDTH)), full((1, HY_WIDTH)),
            pl.BlockSpec((rb, length), lambda r: (r, 0)),
            pl.BlockSpec((rb, length), lambda r: (r, 0)),
        ],
        out_specs=[pl.BlockSpec((rb, HY_WIDTH), lambda r: (r, 0)),
                   pl.BlockSpec((rb, HY_WIDTH), lambda r: (r, 0))],
        out_shape=[jax.ShapeDtypeStruct((length, HY_WIDTH), F32)] * 2,
        scratch_shapes=[pltpu.VMEM((length, 2 * HY_WIDTH), BF16)],
        compiler_params=_vmem(48),
        name=f"hyena_filter_{length}",
    )(feats, w1p, b1.reshape(1, hid), freq.reshape(1, hid), w2, b2.reshape(1, hid), w3,
      deltas.reshape(1, HY_WIDTH), cmat, smat)


def _hyena_kernel(x0_ref, x1_ref, v_ref, w0_ref, w1_ref, w2_ref, b0_ref, b1_ref, b2_ref, skip_ref,
                  tc_ref, ts_ref, c_ref, s_ref, ct_ref, st_ref, o_ref,
                  zb_ref, zs_ref, x0c_ref, acc_ref, *, nseq, length, cb, fb):
    f = pl.program_id(1)
    nf = pl.num_programs(1)

    def short_conv(u, w_ref, b_ref):
        row = lax.broadcasted_iota(jnp.int32, u.shape, 0)
        prev = jnp.where(row == 0, 0.0, pltpu.roll(u, 1, 0))
        nxt = jnp.where(row == length - 1, 0.0, pltpu.roll(u, length - 1, 0))
        return prev * w_ref[0:1, :] + u * w_ref[1:2, :] + nxt * w_ref[2:3, :] + b_ref[...]

    @pl.when(f == 0)
    def _():
        for b in range(nseq):
            cols = slice(b * cb, (b + 1) * cb)
            x0c_ref[:, cols] = short_conv(x0_ref[b], w0_ref, b0_ref)
            z = short_conv(x1_ref[b], w1_ref, b1_ref) * short_conv(v_ref[b], w2_ref, b2_ref)
            zb_ref[:, cols] = z.astype(BF16)
            zs_ref[:, cols] = z * skip_ref[...]
        acc_ref[...] = jnp.zeros_like(acc_ref)

    zb = zb_ref[...]
    zc = jnp.dot(c_ref[...], zb, preferred_element_type=F32)
    zsn = jnp.dot(s_ref[...], zb, preferred_element_type=F32)
    tc = jnp.concatenate([tc_ref[...]] * nseq, axis=1)
    ts = jnp.concatenate([ts_ref[...]] * nseq, axis=1)
    grow = f * fb + lax.broadcasted_iota(jnp.int32, zc.shape, 0)
    is0 = grow == 0
    yc = jnp.where(is0, zc * tc, zc * tc - zsn * ts)
    ys = jnp.where(is0, zsn * ts, zc * ts + zsn * tc)
    acc_ref[...] += (jnp.dot(ct_ref[...], yc.astype(BF16), preferred_element_type=F32)
                     + jnp.dot(st_ref[...], ys.astype(BF16), preferred_element_type=F32))

    @pl.when(f == nf - 1)
    def _():
        for b in range(nseq):
            cols = slice(b * cb, (b + 1) * cb)
            o_ref[b] = (x0c_ref[:, cols] * (acc_ref[:, cols] + zs_ref[:, cols])).astype(o_ref.dtype)


def _hyena(u, first_seq_block, nseq, length, conv_w, conv_b, skip, tc, ts, cmat, smat, stmat):
    cb = 128
    fb = min(length, 512)
    u3 = u.reshape(N_TOK // length, length, EVEN_IN)
    ncb = HY_WIDTH // cb
    width = nseq * cb

    def ublock(part):
        return pl.BlockSpec((nseq, length, cb), lambda c, f: (first_seq_block, 0, part * ncb + c))

    def wblock(part, rows):
        return pl.BlockSpec((rows, cb), lambda c, f: (0, part * ncb + c))

    return pl.pallas_call(
        functools.partial(_hyena_kernel, nseq=nseq, length=length, cb=cb, fb=fb),
        grid=(ncb, length // fb),
        in_specs=[
            ublock(0), ublock(1), ublock(2),
            wblock(0, 3), wblock(1, 3), wblock(2, 3),
            wblock(0, 1), wblock(1, 1), wblock(2, 1),
            pl.BlockSpec((1, cb), lambda c, f: (0, c)),
            pl.BlockSpec((fb, cb), lambda c, f: (f, c)),
            pl.BlockSpec((fb, cb), lambda c, f: (f, c)),
            pl.BlockSpec((fb, length), lambda c, f: (f, 0)),
            pl.BlockSpec((fb, length), lambda c, f: (f, 0)),
            pl.BlockSpec((length, fb), lambda c, f: (0, f)),
            pl.BlockSpec((length, fb), lambda c, f: (0, f)),
        ],
        out_specs=pl.BlockSpec((nseq, length, cb), lambda c, f: (0, 0, c)),
        out_shape=jax.ShapeDtypeStruct((nseq, length, HY_WIDTH), BF16),
        scratch_shapes=[pltpu.VMEM((length, width), BF16), pltpu.VMEM((length, width), F32),
                        pltpu.VMEM((length, width), F32), pltpu.VMEM((length, width), F32)],
        compiler_params=_vmem(48),
        name=f"hyena_{length}",
    )(u3, u3, u3, conv_w, conv_w, conv_w, conv_b.reshape(1, -1), conv_b.reshape(1, -1),
      conv_b.reshape(1, -1), skip.reshape(1, HY_WIDTH), tc, ts, cmat, smat, cmat, stmat
      ).reshape(nseq * length, HY_WIDTH)


def _head_rms(x, seg, gain):
    x2 = x * x
    hi = x2.astype(BF16)
    lo = (x2 - hi.astype(F32)).astype(BF16)
    ss = jnp.dot(hi, seg, preferred_element_type=F32) + jnp.dot(lo, seg, preferred_element_type=F32)
    return x * lax.rsqrt(ss * (1.0 / HEAD_DIM) + EPS) * gain


def _rope(x, cos, sin_signed):
    width = x.shape[1]
    lane = lax.broadcasted_iota(jnp.int32, x.shape, 1)
    first = (lane // ROPE_FREQS) % 2 == 0
    partner = jnp.where(first, pltpu.roll(x, width - ROPE_FREQS, 1), pltpu.roll(x, ROPE_FREQS, 1))
    return x * cos + partner * sin_signed


_DOT_NT = (((1,), (1,)), ((), ()))


def _softmax_over_keys(s, sink_row):
    m = jnp.maximum(jnp.max(s, axis=0, keepdims=True), sink_row)
    p = jnp.exp(s - m)
    return p, jnp.sum(p, axis=0, keepdims=True) + jnp.exp(sink_row - m)


def _ctx_attn_kernel(sink_ref, q_ref, k_ref, v_ref, seg_ref, qg_ref, kg_ref, o_ref, ko_ref, vo_ref, *, nseq):
    seg = seg_ref[...]
    for b in range(nseq):
        rows = slice(b * L_CTX, (b + 1) * L_CTX)
        qt = (_head_rms(q_ref[rows, :], seg, qg_ref[...]) * ATT_SCALE).T.astype(BF16)
        kn = _head_rms(k_ref[rows, :], seg[:KV_WIDTH, :KV_WIDTH], kg_ref[...])
        v = v_ref[rows, :]
        ko_ref[rows, :] = kn
        vo_ref[rows, :] = v
        kb = kn.astype(BF16)
        vt = v.T.astype(BF16)
        blocks = []
        for j in range(ATT_GROUP):
            outs = []
            for g in range(ATT_KV_HEADS):
                h = g * ATT_GROUP + j
                cols = slice(g * HEAD_DIM, (g + 1) * HEAD_DIM)
                s = jnp.dot(kb[:, cols], qt[h * HEAD_DIM:(h + 1) * HEAD_DIM, :], preferred_element_type=F32)
                p, den = _softmax_over_keys(s, jnp.full((1, L_CTX), sink_ref[h], F32))
                outs.append(jnp.dot(vt[cols, :], p.astype(BF16), preferred_element_type=F32) / den)
            blocks.append(jnp.concatenate(outs, axis=0).T)
        o_ref[rows, :] = jnp.concatenate(blocks, axis=1).astype(o_ref.dtype)


def _ctx_attention(u, seg, q_gain, k_gain, sink):
    qcol = 3 * HY_WIDTH // ATT_WIDTH
    kcol = (3 * HY_WIDTH + ATT_WIDTH) // KV_WIDTH
    nseq = 2
    rows = nseq * L_CTX
    return pl.pallas_call(
        functools.partial(_ctx_attn_kernel, nseq=nseq),
        grid_spec=pltpu.PrefetchScalarGridSpec(
            num_scalar_prefetch=1,
            grid=(N_CTX_SEQ // nseq,),
            in_specs=[
                pl.BlockSpec((rows, ATT_WIDTH), lambda b, s: (b, qcol)),
                pl.BlockSpec((rows, KV_WIDTH), lambda b, s: (b, kcol)),
                pl.BlockSpec((rows, KV_WIDTH), lambda b, s: (b, kcol + 1)),
                pl.BlockSpec((ATT_WIDTH, ATT_WIDTH), lambda b, s: (0, 0)),
                pl.BlockSpec((1, ATT_WIDTH), lambda b, s: (0, 0)),
                pl.BlockSpec((1, KV_WIDTH), lambda b, s: (0, 0)),
            ],
            out_specs=[
                pl.BlockSpec((rows, ATT_WIDTH), lambda b, s: (b, 0)),
                pl.BlockSpec((rows, KV_WIDTH), lambda b, s: (b, 0)),
                pl.BlockSpec((rows, KV_WIDTH), lambda b, s: (b, 0)),
            ],
        ),
        out_shape=[jax.ShapeDtypeStruct((N_CTX_TOK, ATT_WIDTH), BF16),
                   jax.ShapeDtypeStruct((N_CTX_TOK, KV_WIDTH), F32),
                   jax.ShapeDtypeStruct((N_CTX_TOK, KV_WIDTH), F32)],
        name="ctx_attention",
    )(sink, u, u, u, seg, jnp.tile(q_gain, ATT_HEADS).reshape(1, ATT_WIDTH),
      jnp.tile(k_gain, ATT_KV_HEADS).reshape(1, KV_WIDTH))


def _rope_tables():
    pos = jnp.arange(L_LAT, dtype=jnp.int32)
    row = (pos // GRID_W).astype(F32)
    col = (pos % GRID_W).astype(F32)
    inv = ROPE_THETA ** (-jnp.arange(ROPE_FREQS, dtype=F32) / ROPE_FREQS)
    ar, ac = row[:, None] * inv, col[:, None] * inv
    cos = jnp.concatenate([jnp.cos(ar), jnp.cos(ar), jnp.cos(ac), jnp.cos(ac)], axis=-1)
    sin = jnp.concatenate([-jnp.sin(ar), jnp.sin(ar), -jnp.sin(ac), jnp.sin(ac)], axis=-1)
    return jnp.tile(cos, (1, ATT_HEADS)), jnp.tile(sin, (1, ATT_HEADS))


def _lat_attn_kernel(sink_ref, q_ref, k_ref, v_ref, ck_ref, cv_ref, cosq_ref, sinq_ref, cosk_ref, sink_k_ref,
                     seg_ref, qg_ref, kg_ref, o_ref, kn_ref, *, nq):
    n = pl.program_id(1)
    seg = seg_ref[...]

    @pl.when(n == 0)
    def _():
        kn = _head_rms(k_ref[...], seg[:KV_WIDTH, :KV_WIDTH], kg_ref[...])
        kn_ref[...] = _rope(kn, cosk_ref[...], sink_k_ref[...]).astype(BF16)

    k_ctx = ck_ref[0].astype(BF16)
    vt_ctx = cv_ref[0].T.astype(BF16)
    span = 3 * ATT_BLOCK
    lanes = ATT_GROUP * ATT_BLOCK
    for j in range(nq):
        blk = n * nq + j
        qrows = slice(j * ATT_BLOCK, (j + 1) * ATT_BLOCK)
        qn = _head_rms(q_ref[qrows, :], seg, qg_ref[...])
        qt = (_rope(qn, cosq_ref[qrows, :], sinq_ref[qrows, :]) * ATT_SCALE).T.astype(BF16)
        start = pl.multiple_of(jnp.clip((blk - 1) * ATT_BLOCK, 0, L_LAT - span), ATT_BLOCK)
        q_pos = blk * ATT_BLOCK + (lax.broadcasted_iota(jnp.int32, (span, lanes), 1) % ATT_BLOCK)
        k_pos = start + lax.broadcasted_iota(jnp.int32, (span, lanes), 0)
        valid = jnp.abs(q_pos - k_pos) <= WINDOW
        k_loc = kn_ref[pl.ds(start, span), :]
        vt_loc = v_ref[pl.ds(start, span), :].T.astype(BF16)
        outs = []
        for g in range(ATT_KV_HEADS):
            cols = slice(g * HEAD_DIM, (g + 1) * HEAD_DIM)
            heads = range(g * ATT_GROUP, (g + 1) * ATT_GROUP)
            q = jnp.concatenate([qt[h * HEAD_DIM:(h + 1) * HEAD_DIM, :] for h in heads], axis=1)
            sink = jnp.concatenate([jnp.full((1, ATT_BLOCK), sink_ref[h], F32) for h in heads], axis=1)
            s_loc = jnp.where(valid, jnp.dot(k_loc[:, cols], q, preferred_element_type=F32), NEG_BIG)
            s_ctx = jnp.dot(k_ctx[:, cols], q, preferred_element_type=F32)
            m = jnp.maximum(jnp.maximum(jnp.max(s_loc, axis=0, keepdims=True),
                                        jnp.max(s_ctx, axis=0, keepdims=True)), sink)
            p_loc = jnp.exp(s_loc - m)
            p_ctx = jnp.exp(s_ctx - m)
            den = (jnp.sum(p_loc, axis=0, keepdims=True) + jnp.sum(p_ctx, axis=0, keepdims=True)
                   + jnp.exp(sink - m))
            o = (jnp.dot(vt_loc[cols, :], p_loc.astype(BF16), preferred_element_type=F32)
                 + jnp.dot(vt_ctx[cols, :], p_ctx.astype(BF16), preferred_element_type=F32))
            outs.append(o / den)
        both = jnp.concatenate(outs, axis=0)
        o_ref[qrows, :] = jnp.concatenate(
            [both[:, i * ATT_BLOCK:(i + 1) * ATT_BLOCK].T for i in range(ATT_GROUP)], axis=1).astype(o_ref.dtype)


def _lat_attention(u, cache_k, cache_v, seg, q_gain, k_gain, sink):
    qcol = 3 * HY_WIDTH // ATT_WIDTH
    kcol = (3 * HY_WIDTH + ATT_WIDTH) // KV_WIDTH
    nq = 2
    qrows = nq * ATT_BLOCK
    nblk = L_LAT // qrows
    first_q_block = N_CTX_TOK // qrows
    first_seq = N_CTX_TOK // L_LAT
    cos, sin = _rope_tables()
    return pl.pallas_call(
        functools.partial(_lat_attn_kernel, nq=nq),
        grid_spec=pltpu.PrefetchScalarGridSpec(
            num_scalar_prefetch=1,
            grid=(N_LAT_SEQ, nblk),
            in_specs=[
                pl.BlockSpec((qrows, ATT_WIDTH), lambda b, n, s: (first_q_block + b * nblk + n, qcol)),
                pl.BlockSpec((L_LAT, KV_WIDTH), lambda b, n, s: (first_seq + b, kcol)),
                pl.BlockSpec((L_LAT, KV_WIDTH), lambda b, n, s: (first_seq + b, kcol + 1)),
                pl.BlockSpec((1, PAST_LEN, KV_WIDTH), lambda b, n, s: (b, 0, 0)),
                pl.BlockSpec((1, PAST_LEN, KV_WIDTH), lambda b, n, s: (b, 0, 0)),
                pl.BlockSpec((qrows, ATT_WIDTH), lambda b, n, s: (n, 0)),
                pl.BlockSpec((qrows, ATT_WIDTH), lambda b, n, s: (n, 0)),
                pl.BlockSpec((L_LAT, KV_WIDTH), lambda b, n, s: (0, 0)),
                pl.BlockSpec((L_LAT, KV_WIDTH), lambda b, n, s: (0, 0)),
                pl.BlockSpec((ATT_WIDTH, ATT_WIDTH), lambda b, n, s: (0, 0)),
                pl.BlockSpec((1, ATT_WIDTH), lambda b, n, s: (0, 0)),
                pl.BlockSpec((1, KV_WIDTH), lambda b, n, s: (0, 0)),
            ],
            out_specs=pl.BlockSpec((qrows, ATT_WIDTH), lambda b, n, s: (b * nblk + n, 0)),
            scratch_shapes=[pltpu.VMEM((L_LAT, KV_WIDTH), BF16)],
        ),
        out_shape=jax.ShapeDtypeStruct((N_LAT_SEQ * L_LAT, ATT_WIDTH), BF16),
        name="lat_attention",
    )(sink, u, u, u, cache_k.reshape(N_LAT_SEQ, PAST_LEN, KV_WIDTH), cache_v.reshape(N_LAT_SEQ, PAST_LEN, KV_WIDTH),
      cos, sin, cos[:, :KV_WIDTH], sin[:, :KV_WIDTH], seg,
      jnp.tile(q_gain, ATT_HEADS).reshape(1, ATT_WIDTH), jnp.tile(k_gain, ATT_KV_HEADS).reshape(1, KV_WIDTH))


def _ceil_to(v, m):
    return ((v + (m - 1)) // m) * m


def _post_mixer(x, mix, refs, tm):
    (w_ref, gain_ref, mod_ref, rwh_ref, rwm_ref, rb_ref, tri_ref, low_ref,
     x1_ref, sw_ref, cnt_ref, xg_ref, wb_ref) = refs

    @pl.when(pl.program_id(0) == 0)
    def _():
        wb_ref[...] = w_ref[...].astype(BF16)

    y = jnp.dot(mix, wb_ref[...], preferred_element_type=F32)
    x1 = x + mod_ref[0, 2:3, :] * y
    x1_ref[...] = x1
    xt = _norm_mod(x1, gain_ref[...], mod_ref[0, 3:4, :], mod_ref[0, 4:5, :])
    xh = xt.astype(BF16)
    xm = (xt - xh.astype(F32)).astype(BF16)
    logits = (lax.dot_general(rwh_ref[...], xh, _DOT_NT, preferred_element_type=F32)
              + lax.dot_general(rwm_ref[...], xh, _DOT_NT, preferred_element_type=F32)
              + lax.dot_general(rwh_ref[...], xm, _DOT_NT, preferred_element_type=F32)) + rb_ref[...]
    expert = lax.broadcasted_iota(jnp.int32, logits.shape, 0)
    vals, hits = [], []
    for _ in range(TOP_K):
        m = jnp.max(logits, axis=0, keepdims=True)
        sel = jnp.min(jnp.where(logits == m, expert, N_EXPERTS), axis=0, keepdims=True)
        vals.append(m)
        hits.append(expert == sel)
        logits = jnp.where(expert == sel, -jnp.inf, logits)
    es = [jnp.exp(v - vals[0]) for v in vals]
    den = es[0] + es[1] + es[2] + es[3]
    weights = [e / den for e in es]

    routed = sum(h.astype(F32) for h in hits)
    counts = jnp.sum(routed, axis=1, keepdims=True)
    cnt_ref[0] = counts.astype(jnp.int32)
    units = (_ceil_to(counts.astype(jnp.int32), MOE_GROUP) // MOE_GROUP).astype(F32)
    g0 = MOE_GROUP * jnp.dot(low_ref[...], jnp.broadcast_to(units, (N_EXPERTS, LANES)).astype(BF16),
                             preferred_element_type=F32)[:, 0:1]
    earlier = jnp.dot(routed.astype(BF16), tri_ref[...], preferred_element_type=F32)
    row_of = g0 + earlier
    slots = [jnp.sum(jnp.where(h, row_of, 0.0), axis=0, keepdims=True) for h in hits]

    prow = lax.broadcasted_iota(jnp.int32, (LANES, tm), 0)
    packed = jnp.zeros((LANES, tm), F32)
    for k, vec in enumerate(slots + weights):
        packed = jnp.where(prow == k, vec, packed)
    sw_ref[...] = packed.T

    group_row = lax.broadcasted_iota(jnp.int32, (MOE_GROUPED_ROWS, tm), 0).astype(F32)
    perm = jnp.zeros((MOE_GROUPED_ROWS, tm), F32)
    for k in range(TOP_K):
        perm = jnp.where(group_row == slots[k], 1.0, perm)
    perm = perm.astype(BF16)
    xg_ref[...] = jnp.dot(perm, xh, preferred_element_type=F32).astype(BF16)


def _post_even_kernel(xc_ref, xl_ref, ac_ref, al_ref, tc_ref, tl_ref, *refs, tm):
    is_ctx = pl.program_id(0) * tm < N_CTX_TOK
    mix = jnp.concatenate([jnp.where(is_ctx, ac_ref[...], al_ref[...]),
                           jnp.where(is_ctx, tc_ref[...], tl_ref[...])], axis=1)
    _post_mixer(jnp.where(is_ctx, xc_ref[...], xl_ref[...]), mix, refs, tm)


def _post_odd_kernel(x_ref, yc_ref, yl_ref, *refs, tm):
    mix = jnp.where(pl.program_id(0) * tm < N_CTX_TOK, yc_ref[...], yl_ref[...])
    _post_mixer(x_ref[...], mix, refs, tm)


def _post_call(kernel_fn, name, mixer_specs, mixer_args, k_in, w_out, gain, mod, router_w, router_b):
    tm = ROW_TILE
    nsteps = N_TOK // tm
    row = lax.broadcasted_iota(jnp.int32, (tm, tm), 0)
    col = lax.broadcasted_iota(jnp.int32, (tm, tm), 1)
    tri = (row < col).astype(BF16)
    er = lax.broadcasted_iota(jnp.int32, (N_EXPERTS, N_EXPERTS), 0)
    ec = lax.broadcasted_iota(jnp.int32, (N_EXPERTS, N_EXPERTS), 1)
    low = (ec < er).astype(BF16)
    rw_t = router_w.T
    rw_hi = rw_t.astype(BF16)
    rw_mid = (rw_t - rw_hi.astype(F32)).astype(BF16)
    const = lambda shape: pl.BlockSpec(shape, lambda i: (0,) * len(shape))
    return pl.pallas_call(
        functools.partial(kernel_fn, tm=tm),
        grid=(nsteps,),
        in_specs=mixer_specs + [
            const((k_in, D_MODEL)),
            const((1, D_MODEL)),
            pl.BlockSpec((1, 6, D_MODEL), lambda i: (_cond_of_tile(i, tm), 0, 0)),
            const((N_EXPERTS, D_MODEL)), const((N_EXPERTS, D_MODEL)), const((N_EXPERTS, 1)),
            const((tm, tm)), const((N_EXPERTS, N_EXPERTS)),
        ],
        out_specs=[
            pl.BlockSpec((tm, D_MODEL), lambda i: (i, 0)),
            pl.BlockSpec((tm, LANES), lambda i: (i, 0)),
            pl.BlockSpec((1, N_EXPERTS, 1), lambda i: (i, 0, 0)),
            pl.BlockSpec((MOE_GROUPED_ROWS, D_MODEL), lambda i: (i, 0)),
        ],
        out_shape=[
            jax.ShapeDtypeStruct((N_TOK, D_MODEL), F32),
            jax.ShapeDtypeStruct((N_TOK, LANES), F32),
            jax.ShapeDtypeStruct((nsteps, N_EXPERTS, 1), jnp.int32),
            jax.ShapeDtypeStruct((nsteps * MOE_GROUPED_ROWS, D_MODEL), BF16),
        ],
        scratch_shapes=[pltpu.VMEM((k_in, D_MODEL), BF16)],
        compiler_params=_vmem(56),
        name=name,
    )(*mixer_args, w_out, gain.reshape(1, D_MODEL), mod, rw_hi, rw_mid, router_b.reshape(N_EXPERTS, 1), tri, low)


def _post_even(x_ctx, x_lat, a_ctx, a_lat, t_ctx, t_lat, w_out, gain, mod, router_w, router_b):
    tm = ROW_TILE
    nctx = N_CTX_TOK // tm
    ctx_map = lambda i: (jnp.minimum(i, nctx - 1), 0)
    lat_map = lambda i: (jnp.maximum(i - nctx, 0), 0)
    specs = [pl.BlockSpec((tm, D_MODEL), ctx_map), pl.BlockSpec((tm, D_MODEL), lat_map),
             pl.BlockSpec((tm, HY_WIDTH), ctx_map), pl.BlockSpec((tm, HY_WIDTH), lat_map),
             pl.BlockSpec((tm, ATT_WIDTH), ctx_map), pl.BlockSpec((tm, ATT_WIDTH), lat_map)]
    return _post_call(_post_even_kernel, "post_even", specs, (x_ctx, x_lat, a_ctx, a_lat, t_ctx, t_lat),
                      HY_WIDTH + ATT_WIDTH, w_out, gain, mod, router_w, router_b)


def _post_odd(x, y_ctx, y_lat, w_out, gain, mod, router_w, router_b):
    tm = ROW_TILE
    nctx = N_CTX_TOK // tm
    specs = [pl.BlockSpec((tm, D_MODEL), lambda i: (i, 0)),
             pl.BlockSpec((tm, RET_V_WIDTH), lambda i: (jnp.minimum(i, nctx - 1), 0)),
             pl.BlockSpec((tm, RET_V_WIDTH), lambda i: (jnp.maximum(i - nctx, 0), 0))]
    return _post_call(_post_odd_kernel, "post_odd", specs, (x, y_ctx, y_lat),
                      RET_V_WIDTH, w_out, gain, mod, router_w, router_b)


def _moe_plan(counts):
    t, grp, big = MOE_TILE, MOE_GROUP, MOE_GROUPED_ROWS
    n16 = _ceil_to(counts.reshape(-1, N_EXPERTS), grp)
    nsteps = n16.shape[0]
    g0 = jnp.cumsum(n16, axis=1) - n16
    e0 = jnp.cumsum(n16, axis=0) - n16
    rows_e = jnp.sum(n16, axis=0)
    ntiles = (rows_e + t - 1) // t
    tile_end = jnp.cumsum(ntiles)
    tile_first = tile_end - ntiles
    total = tile_end[-1]
    experts = jnp.arange(N_EXPERTS, dtype=jnp.int32)
    tiles = jnp.arange(MOE_NUM_TILES, dtype=jnp.int32)
    valid = tiles < total
    ti = jnp.minimum(tiles, total - 1)
    e_of = jnp.sum((tile_end[None, :] <= ti[:, None]).astype(jnp.int32), axis=1)
    onehot = (e_of[:, None] == experts[None, :]).astype(jnp.int32)
    pick = lambda v: jnp.sum(onehot * v[None, :], axis=1)
    pick2 = lambda m: jnp.sum(onehot[:, None, :] * m[None, :, :], axis=2)
    r = ((ti - pick(tile_first)) * t)[:, None] + grp * jnp.arange(t // grp, dtype=jnp.int32)[None, :]
    ends = pick2(e0 + n16)
    step = jnp.sum((ends[:, None, :] <= r[:, :, None]).astype(jnp.int32), axis=2)
    step = jnp.minimum(step, nsteps - 1)
    sel = (step[:, :, None] == jnp.arange(nsteps, dtype=jnp.int32)[None, None, :]).astype(jnp.int32)
    at_step = lambda m: jnp.sum(sel * pick2(m)[:, None, :], axis=2)
    src = step * big + at_step(g0) + r - at_step(e0)
    live = jnp.logical_and(valid[:, None], r < pick(rows_e)[:, None])
    moe_src = jnp.where(live, src, 0).reshape(-1)
    g = grp * jnp.arange(big // grp, dtype=jnp.int32)
    gend = g0 + n16
    ce = jnp.sum((gend[:, None, :] <= g[None, :, None]).astype(jnp.int32), axis=2)
    used = ce < N_EXPERTS
    ce = jnp.minimum(ce, N_EXPERTS - 1)
    csel = (ce[:, :, None] == experts[None, None, :]).astype(jnp.int32)
    of_e = lambda m: jnp.sum(csel * m[:, None, :], axis=2)
    base = jnp.sum(csel * (tile_first * t)[None, None, :], axis=2)
    csrc = base + of_e(e0) + g[None, :] - of_e(g0)
    comb_src = jnp.where(used, csrc, 0).reshape(-1)
    used_e = ntiles > 0
    rank = jnp.cumsum(used_e.astype(jnp.int32)) - 1
    later = jnp.logical_and(used_e[None, :], experts[None, :] > experts[:, None])
    next_e = jnp.min(jnp.where(later, experts[None, :], N_EXPERTS), axis=1)
    next_e = jnp.where(next_e == N_EXPERTS, -1, next_e)
    half = jnp.logical_and(valid, pick(rows_e) - (ti - pick(tile_first)) * t <= t // 2)
    wbuf = jnp.stack([pick(rank) % 2, pick(next_e), half.astype(jnp.int32)], axis=1).reshape(-1)
    as_i32 = lambda v: v.astype(jnp.int32)
    return as_i32(e_of), as_i32(valid), as_i32(wbuf), as_i32(moe_src), as_i32(comb_src)


def _moe_kernel(te_ref, tv_ref, nx_ref, src_ref, xg_hbm, w1_hbm, b1_ref, w2_hbm, b2_ref, y_ref,
                xbuf, w1f, w2f, w1b, w2b, sem_in, sem_w):
    i = pl.program_id(0)
    nt = pl.num_programs(0)
    t, grp = MOE_TILE, MOE_GROUP
    slot = i % 2

    def issue_gather(tile, sl):
        for c in range(t // grp):
            src = pl.multiple_of(src_ref[tile * (t // grp) + c], grp)
            pltpu.make_async_copy(xg_hbm.at[pl.ds(src, grp), :], xbuf.at[sl, pl.ds(c * grp, grp), :],
                                  sem_in.at[sl]).start()

    def weight_copies(e, ws):
        return (pltpu.make_async_copy(w1_hbm.at[e], w1f.at[ws], sem_w.at[ws]),
                pltpu.make_async_copy(w2_hbm.at[e], w2f.at[ws], sem_w.at[ws]))

    def valid(tile):
        return tv_ref[jnp.clip(tile, 0, nt - 1)] > 0

    @pl.when(i == 0)
    def _():
        issue_gather(0, 0)
        for cp in weight_copies(te_ref[0], 0):
            cp.start()

    @pl.when(valid(i))
    def _():
        pltpu.make_async_copy(xg_hbm.at[pl.ds(0, t), :], xbuf.at[slot], sem_in.at[slot]).wait()

        @pl.when(jnp.logical_and(i + 1 < nt, valid(i + 1)))
        def _():
            issue_gather(i + 1, 1 - slot)

        e = te_ref[i]
        first = jnp.logical_or(i == 0, e != te_ref[jnp.maximum(i - 1, 0)])
        ws = nx_ref[3 * i]
        nxt = nx_ref[3 * i + 1]
        half = nx_ref[3 * i + 2] > 0

        @pl.when(first)
        def _():
            for cp in weight_copies(e, ws):
                cp.wait()
            w1b[...] = w1f[ws].astype(BF16)
            w2b[...] = w2f[ws].astype(BF16)

            @pl.when(nxt >= 0)
            def _():
                for cp in weight_copies(nxt, 1 - ws):
                    cp.start()

        def expert_mlp(rows):
            h = jnp.dot(xbuf[slot, :rows], w1b[...], preferred_element_type=F32) + b1_ref[0]
            glu = jnp.minimum(h[:, :D_FF], SWIGLU_LIMIT)
            lin = jnp.clip(h[:, D_FF:], -SWIGLU_LIMIT, SWIGLU_LIMIT)
            act = (glu * _sigmoid(SWIGLU_ALPHA * glu) * (lin + 1.0)).astype(BF16)
            y = jnp.dot(act, w2b[...], preferred_element_type=F32) + b2_ref[0]
            y_ref[:rows, :] = y.astype(y_ref.dtype)

        @pl.when(jnp.logical_not(half))
        def _():
            expert_mlp(t)

        @pl.when(half)
        def _():
            expert_mlp(t // 2)
            y_ref[t // 2:, :] = jnp.zeros((t // 2, D_MODEL), y_ref.dtype)

    @pl.when(jnp.logical_not(valid(i)))
    def _():
        y_ref[...] = jnp.zeros_like(y_ref)


def _moe_experts(xg, plan, w1, b1, w2, b2):
    te, tv, nx, src, _ = plan
    t = MOE_TILE
    return pl.pallas_call(
        _moe_kernel,
        grid_spec=pltpu.PrefetchScalarGridSpec(
            num_scalar_prefetch=4,
            grid=(MOE_NUM_TILES,),
            in_specs=[
                pl.BlockSpec(memory_space=pl.ANY),
                pl.BlockSpec(memory_space=pl.ANY),
                pl.BlockSpec((1, 1, 2 * D_FF), lambda i, te, *_: (te[i], 0, 0)),
                pl.BlockSpec(memory_space=pl.ANY),
                pl.BlockSpec((1, 1, D_MODEL), lambda i, te, *_: (te[i], 0, 0)),
            ],
            out_specs=pl.BlockSpec((t, D_MODEL), lambda i, *_: (i, 0)),
            scratch_shapes=[
                pltpu.VMEM((2, t, D_MODEL), BF16),
                pltpu.VMEM((2, D_MODEL, 2 * D_FF), F32),
                pltpu.VMEM((2, D_FF, D_MODEL), F32),
                pltpu.VMEM((D_MODEL, 2 * D_FF), BF16),
                pltpu.VMEM((D_FF, D_MODEL), BF16),
                pltpu.SemaphoreType.DMA((2,)),
                pltpu.SemaphoreType.DMA((2,)),
            ],
        ),
        out_shape=jax.ShapeDtypeStruct((MOE_NUM_TILES * t, D_MODEL), BF16),
        compiler_params=_vmem(56),
        name="moe_experts",
    )(te, tv, nx, src, xg, w1, b1.reshape(N_EXPERTS, 1, 2 * D_FF), w2, b2.reshape(N_EXPERTS, 1, D_MODEL))


def _combine_value(src_ref, x1_ref, sw_ref, mod_ref, ys_hbm, ybuf, sem, tm):
    s = pl.program_id(0)
    nsteps = pl.num_programs(0)
    slot = s % 2
    grp, big = MOE_GROUP, MOE_GROUPED_ROWS
    nchunk = big // grp

    def issue_gather(step, sl):
        for c in range(nchunk):
            src = pl.multiple_of(src_ref[step * nchunk + c], grp)
            pltpu.make_async_copy(ys_hbm.at[pl.ds(src, grp), :], ybuf.at[sl, pl.ds(c * grp, grp), :],
                                  sem.at[sl]).start()

    @pl.when(s == 0)
    def _():
        issue_gather(0, 0)

    pltpu.make_async_copy(ys_hbm.at[pl.ds(0, big), :], ybuf.at[slot], sem.at[slot]).wait()

    @pl.when(s + 1 < nsteps)
    def _():
        issue_gather(s + 1, 1 - slot)

    sw = sw_ref[...]
    col = lax.broadcasted_iota(jnp.int32, (tm, big), 1).astype(F32)
    wmat = jnp.zeros((tm, big), F32)
    for k in range(TOP_K):
        wmat = jnp.where(col == sw[:, k:k + 1], sw[:, TOP_K + k:TOP_K + k + 1], wmat)
    moe = jnp.dot(wmat.astype(BF16), ybuf[slot], preferred_element_type=F32)
    return x1_ref[...] + mod_ref[0, 5:6, :] * moe


def _combine_kernel(src_ref, x1_ref, sw_ref, mod_ref, ys_hbm, o_ref, ybuf, sem, *, tm):
    o_ref[...] = _combine_value(src_ref, x1_ref, sw_ref, mod_ref, ys_hbm, ybuf, sem, tm)


def _combine_split_kernel(src_ref, x1_ref, sw_ref, mod_ref, ys_hbm, oc_ref, ol_ref, ybuf, sem, *, tm):
    val = _combine_value(src_ref, x1_ref, sw_ref, mod_ref, ys_hbm, ybuf, sem, tm)
    is_ctx = pl.program_id(0) * tm < N_CTX_TOK

    @pl.when(is_ctx)
    def _():
        oc_ref[...] = val

    @pl.when(jnp.logical_not(is_ctx))
    def _():
        ol_ref[...] = val


def _combine(x1, ys, sw, plan, mod, split):
    tm = ROW_TILE
    nctx = N_CTX_TOK // tm
    in_specs = [
        pl.BlockSpec((tm, D_MODEL), lambda i, *_: (i, 0)),
        pl.BlockSpec((tm, LANES), lambda i, *_: (i, 0)),
        pl.BlockSpec((1, 6, D_MODEL), lambda i, *_: (_cond_of_tile(i, tm), 0, 0)),
        pl.BlockSpec(memory_space=pl.ANY),
    ]
    scratch = [pltpu.VMEM((2, MOE_GROUPED_ROWS, D_MODEL), BF16), pltpu.SemaphoreType.DMA((2,))]
    if not split:
        kernel_fn, name = _combine_kernel, "moe_combine"
        out_specs = pl.BlockSpec((tm, D_MODEL), lambda i, *_: (i, 0))
        out_shape = jax.ShapeDtypeStruct((N_TOK, D_MODEL), F32)
    else:
        kernel_fn, name = _combine_split_kernel, "moe_combine_split"
        out_specs = [pl.BlockSpec((tm, D_MODEL), lambda i, *_: (jnp.minimum(i, nctx - 1), 0)),
                     pl.BlockSpec((tm, D_MODEL), lambda i, *_: (jnp.maximum(i - nctx, 0), 0))]
        out_shape = [jax.ShapeDtypeStruct((N_CTX_TOK, D_MODEL), F32),
                     jax.ShapeDtypeStruct((N_TOK - N_CTX_TOK, D_MODEL), F32)]
    return pl.pallas_call(
        functools.partial(kernel_fn, tm=tm),
        grid_spec=pltpu.PrefetchScalarGridSpec(
            num_scalar_prefetch=1, grid=(N_TOK // tm,), in_specs=in_specs, out_specs=out_specs,
            scratch_shapes=scratch),
        out_shape=out_shape,
        compiler_params=_vmem(56),
        name=name,
    )(plan[4], x1, sw, mod, ys)


def _retention_kernel(lg_ref, q_ref, k_ref, v_ref, gf_ref, gb_ref, *rest, length, nh, has_s0, emit_state):
    rest = list(rest)
    s0_ref = rest.pop(0) if has_s0 else None
    o_ref = rest.pop(0)
    so_ref = rest.pop(0) if emit_state else None
    s_ref, yf_ref, yb_ref = rest
    c = RET_CHUNK
    nc = length // c
    ii = lax.broadcasted_iota(jnp.int32, (c, c), 0).astype(F32)
    jj = lax.broadcasted_iota(jnp.int32, (c, c), 1).astype(F32)
    ci = lax.broadcasted_iota(jnp.int32, (c, 1), 0).astype(F32)

    def decays(direction, hh):
        lg = -jnp.exp(lg_ref[direction, hh])
        lg1 = lg[:, 0:1]
        if direction == 0:
            diff = ii - jj
            q_decay = jnp.exp(lg1 * (ci + 1.0))
            k_decay = jnp.exp(lg1 * (c - 1.0 - ci))
        else:
            diff = jj - ii
            q_decay = jnp.exp(lg1 * (c - ci))
            k_decay = jnp.exp(lg1 * ci)
        scale = RET_DK ** -0.5
        inner = jnp.where(diff >= 0, jnp.exp(lg * jnp.maximum(diff, 0.0)), 0.0) * scale
        return inner, q_decay, k_decay * scale, jnp.exp(lg1 * float(c))

    def chunk(direction, hh, ch, consts, g_ref, y_ref):
        inner, q_decay, k_decay, chunk_decay = consts
        rows = pl.ds(pl.multiple_of(ch * c, c), c)
        kcols = slice(hh * RET_DK, (hh + 1) * RET_DK)
        vcols = slice(hh * RET_DV, (hh + 1) * RET_DV)
        qc = q_ref[rows, kcols]
        kc = k_ref[rows, kcols]
        vc = v_ref[rows, vcols]
        s = s_ref[direction * nh + hh]
        att = lax.dot_general(qc, kc, _DOT_NT, preferred_element_type=F32) * inner
        o = (jnp.dot(att.astype(BF16), vc, preferred_element_type=F32)
             + jnp.dot(qc, s.astype(BF16), preferred_element_type=F32) * q_decay)
        kd = (kc.astype(F32) * k_decay).T.astype(BF16)
        s_ref[direction * nh + hh] = s * chunk_decay + jnp.dot(kd, vc, preferred_element_type=F32)
        on = o * lax.rsqrt(jnp.mean(o * o, axis=-1, keepdims=True) + EPS)
        g = g_ref[rows, vcols].astype(F32)
        y_ref[rows, vcols] = g * _sigmoid(g) * on

    for direction in range(2):
        for hh in range(nh):
            if has_s0:
                s_ref[direction * nh + hh] = s0_ref[0, direction, hh]
            else:
                s_ref[direction * nh + hh] = jnp.zeros((RET_DK, RET_DV), F32)
    consts = [[decays(direction, hh) for hh in range(nh)] for direction in range(2)]

    def body(step, carry):
        for hh in range(nh):
            chunk(0, hh, step, consts[0][hh], gf_ref, yf_ref)
            chunk(1, hh, nc - 1 - step, consts[1][hh], gb_ref, yb_ref)
        return carry

    lax.fori_loop(0, nc, body, 0)
    o_ref[...] = (yf_ref[...] + yb_ref[...]).astype(o_ref.dtype)
    if emit_state:
        for direction in range(2):
            for hh in range(nh):
                so_ref[0, direction, hh] = s_ref[direction * nh + hh]


def _retention(u, first_seq, nseq, length, nh, decay_logit, s0, emit_state):
    row0 = first_seq
    lg = jnp.broadcast_to(decay_logit.astype(F32)[:, :, None, None], (2, RET_HEADS, 1, LANES))
    kcol = RET_QK_WIDTH // (nh * RET_DK)
    vcol = 2 * RET_QK_WIDTH // (nh * RET_DV)
    gfcol = vcol + RET_HEADS // nh
    gbcol = gfcol + RET_HEADS // nh
    in_specs = [
        pl.BlockSpec((2, nh, 1, LANES), lambda b, h: (0, h, 0, 0)),
        pl.BlockSpec((length, nh * RET_DK), lambda b, h: (row0 + b, h)),
        pl.BlockSpec((length, nh * RET_DK), lambda b, h: (row0 + b, kcol + h)),
        pl.BlockSpec((length, nh * RET_DV), lambda b, h: (row0 + b, vcol + h)),
        pl.BlockSpec((length, nh * RET_DV), lambda b, h: (row0 + b, gfcol + h)),
        pl.BlockSpec((length, nh * RET_DV), lambda b, h: (row0 + b, gbcol + h)),
    ]
    args = [lg, u, u, u, u, u]
    state_spec = pl.BlockSpec((1, 2, nh, RET_DK, RET_DV), lambda b, h: (b, 0, h, 0, 0))
    if s0 is not None:
        in_specs.append(state_spec)
        args.append(s0)
    out_specs = [pl.BlockSpec((length, nh * RET_DV), lambda b, h: (b, h))]
    out_shape = [jax.ShapeDtypeStruct((nseq * length, RET_V_WIDTH), BF16)]
    if emit_state:
        out_specs.append(state_spec)
        out_shape.append(jax.ShapeDtypeStruct((nseq, 2, RET_HEADS, RET_DK, RET_DV), F32))
    return pl.pallas_call(
        functools.partial(_retention_kernel, length=length, nh=nh, has_s0=s0 is not None, emit_state=emit_state),
        grid=(nseq, RET_HEADS // nh),
        in_specs=in_specs,
        out_specs=out_specs,
        out_shape=out_shape,
        scratch_shapes=[pltpu.VMEM((2 * nh, RET_DK, RET_DV), F32), pltpu.VMEM((length, nh * RET_DV), F32),
                        pltpu.VMEM((length, nh * RET_DV), F32)],
        compiler_params=_vmem(48),
        name=f"retention_{length}",
    )(*args)


def kernel(x_prompt, x_sample, cache_k0, cache_v0, state_ret1, c, c_ctx, l0_norm_mix, l0_ada_w, l0_ada_b, l0_w_in, l0_conv_w, l0_conv_b, l0_filt_w1, l0_filt_b1, l0_filt_freq, l0_filt_w2, l0_filt_b2, l0_filt_w3, l0_filt_deltas, l0_hy_skip, l0_q_gain, l0_k_gain, l0_sink, l0_w_out, l0_norm_ffn, l0_router_w, l0_router_b, l0_moe_w1, l0_moe_b1, l0_moe_w2, l0_moe_b2, l1_norm_mix, l1_ada_w, l1_ada_b, l1_w_in, l1_ret_decay_logit, l1_w_out, l1_norm_ffn, l1_router_w, l1_router_b, l1_moe_w1, l1_moe_b1, l1_moe_w2, l1_moe_b2):
    x_ctx = x_prompt.reshape(N_CTX_TOK, D_MODEL)
    x_lat = x_sample.reshape(N_TOK - N_CTX_TOK, D_MODEL)
    cond = jnp.zeros((SUBLANES, D_MODEL), F32).at[0].set(c_ctx).at[1:1 + N_LAT_SEQ].set(c)
    mod0 = _adaln(cond, l0_ada_w, l0_ada_b)
    mod1 = _adaln(cond, l1_ada_w, l1_ada_b)

    u = _in_proj((x_ctx, x_lat), l0_norm_mix, mod0, l0_w_in, EVEN_IN // 2, F32)
    filt = (l0_filt_w1, l0_filt_b1, l0_filt_freq, l0_filt_w2, l0_filt_b2, l0_filt_w3, l0_filt_deltas)
    hy = []
    for first_block, nseq, length in ((0, N_CTX_SEQ, L_CTX), (N_CTX_TOK // L_LAT // N_LAT_SEQ, N_LAT_SEQ, L_LAT)):
        cmat, smat, stmat = _dft_matrices(length)
        tc, ts = _hyena_filter(length, _filter_features(length), cmat, smat, *filt)
        hy.append(_hyena(u, first_block, nseq, length, l0_conv_w, l0_conv_b, l0_hy_skip, tc, ts, cmat, smat, stmat))
    head = lax.broadcasted_iota(jnp.int32, (ATT_WIDTH, ATT_WIDTH), 0) // HEAD_DIM
    seg = (head == head.T).astype(BF16)
    att_ctx, new_k, new_v = _ctx_attention(u, seg, l0_q_gain, l0_k_gain, l0_sink)
    att_lat = _lat_attention(u, cache_k0, cache_v0, seg, l0_q_gain, l0_k_gain, l0_sink)
    w_att = l0_w_out[HY_WIDTH:].reshape(ATT_KV_HEADS, ATT_GROUP, HEAD_DIM, D_MODEL).swapaxes(0, 1)
    w_out0 = jnp.concatenate([l0_w_out[:HY_WIDTH], w_att.reshape(ATT_WIDTH, D_MODEL)], axis=0)
    x1, sw, counts, xg = _post_even(x_ctx, x_lat, hy[0], hy[1], att_ctx, att_lat, w_out0, l0_norm_ffn, mod0,
                                    l0_router_w, l0_router_b)
    plan = _moe_plan(counts)
    ys = _moe_experts(xg, plan, l0_moe_w1, l0_moe_b1, l0_moe_w2, l0_moe_b2)
    x = _combine(x1, ys, sw, plan, mod0, split=False)

    u = _in_proj((x,), l1_norm_mix, mod1, l1_w_in, 2048, BF16)
    y_ctx, new_state = _retention(u, 0, N_CTX_SEQ, L_CTX, 2, l1_ret_decay_logit, None, True)
    (y_lat,) = _retention(u, N_CTX_TOK // L_LAT, N_LAT_SEQ, L_LAT, 1, l1_ret_decay_logit, state_ret1, False)
    x1, sw, counts, xg = _post_odd(x, y_ctx, y_lat, l1_w_out, l1_norm_ffn, mod1, l1_router_w, l1_router_b)
    plan = _moe_plan(counts)
    ys = _moe_experts(xg, plan, l1_moe_w1, l1_moe_b1, l1_moe_w2, l1_moe_b2)
    y_prompt, y_sample = _combine(x1, ys, sw, plan, mod1, split=True)

    return (y_prompt.reshape(N_CTX_SEQ, L_CTX, D_MODEL), y_sample.reshape(N_LAT_SEQ, L_LAT, D_MODEL),
            new_k.reshape(N_CTX_SEQ, L_CTX, ATT_KV_HEADS, HEAD_DIM),
            new_v.reshape(N_CTX_SEQ, L_CTX, ATT_KV_HEADS, HEAD_DIM), new_state)
```

```python
import functools
import math

import jax
import jax.numpy as jnp
from jax import lax
from jax.experimental import pallas as pl
from jax.experimental.pallas import tpu as pltpu

F32 = jnp.float32
BF16 = jnp.bfloat16

D_MODEL = 1024
N_CTX_SEQ, L_CTX = 16, 256
N_LAT_SEQ, L_LAT = 2, 2048
N_CTX_TOK = N_CTX_SEQ * L_CTX
N_TOK = N_CTX_TOK + N_LAT_SEQ * L_LAT
PAST_LEN = 512
EPS = 1e-6
NEG_BIG = -1e30

HY_WIDTH = 512
HY_BANDS = 16
HY_FILTER_HIDDEN = 64
HY_FEAT_PAD = 64

ATT_HEADS, ATT_KV_HEADS, HEAD_DIM = 8, 2, 64
ATT_GROUP = ATT_HEADS // ATT_KV_HEADS
ATT_WIDTH = ATT_HEADS * HEAD_DIM
KV_WIDTH = ATT_KV_HEADS * HEAD_DIM
ATT_SCALE = HEAD_DIM ** -0.5
WINDOW = 128
ATT_BLOCK = 128
ROPE_THETA = 10000.0
ROPE_FREQS = HEAD_DIM // 4
GRID_W = 64
EVEN_IN = 3 * HY_WIDTH + ATT_WIDTH + 2 * KV_WIDTH

RET_HEADS = 4
RET_DK = 256
RET_DV = 512
RET_CHUNK = 128
RET_QK_WIDTH = RET_HEADS * RET_DK
RET_V_WIDTH = RET_HEADS * RET_DV
ODD_IN = 2 * RET_QK_WIDTH + 3 * RET_V_WIDTH

N_EXPERTS = 32
TOP_K = 4
D_FF = 1024
SWIGLU_ALPHA = 1.702
SWIGLU_LIMIT = 7.0

SUBLANES = 8
LANES = 128

MOE_TILE = 512
MOE_GROUP = 16
ROW_TILE = 512
IN_PROJ_TILE = 1024
MOE_GROUPED_ROWS = 2560
MOE_NUM_TILES = 112


def _vmem(mib):
    return pltpu.CompilerParams(vmem_limit_bytes=mib * 1024 * 1024)


def _cond_of_tile(i, tm):
    row = i * tm
    return jnp.where(row < N_CTX_TOK, 0, 1 + (row - N_CTX_TOK) // L_LAT)


def _sigmoid(x):
    return 1.0 / (1.0 + jnp.exp(-x))


def _split_bf16(a):
    hi = a.astype(BF16)
    return hi, (a - hi.astype(F32)).astype(BF16)


def _dot3(a, b):
    a_hi, a_mid = _split_bf16(a)
    b_hi, b_mid = _split_bf16(b)
    return (jnp.dot(a_hi, b_hi, preferred_element_type=F32) + jnp.dot(a_hi, b_mid, preferred_element_type=F32)
            + jnp.dot(a_mid, b_hi, preferred_element_type=F32))


def _norm_mod(x, gain, shift, scale):
    ms = jnp.mean(x * x, axis=-1, keepdims=True)
    return (x * lax.rsqrt(ms + EPS) * gain) * (1.0 + scale) + shift


def _adaln_kernel(c_ref, w_ref, b_ref, o_ref):
    c = c_ref[...]
    s = c * _sigmoid(c)
    o_ref[...] = _dot3(s, w_ref[...]) + b_ref[...]


def _adaln(cond, w, b):
    n = w.shape[1]
    tn = 1024
    out = pl.pallas_call(
        _adaln_kernel,
        grid=(n // tn,),
        in_specs=[
            pl.BlockSpec((SUBLANES, D_MODEL), lambda j: (0, 0)),
            pl.BlockSpec((D_MODEL, tn), lambda j: (0, j)),
            pl.BlockSpec((1, tn), lambda j: (0, j)),
        ],
        out_specs=pl.BlockSpec((SUBLANES, tn), lambda j: (0, j)),
        out_shape=jax.ShapeDtypeStruct((SUBLANES, n), F32),
        name="adaln",
    )(cond, w, b.reshape(1, n))
    return out.reshape(SUBLANES, 6, D_MODEL)


def _in_proj_kernel(*refs, tm):
    *x_refs, gain_ref, mod_ref, w_ref, o_ref, wb_ref = refs
    i = pl.program_id(1)

    @pl.when(i == 0)
    def _():
        wb_ref[...] = w_ref[...].astype(BF16)

    if len(x_refs) == 2:
        x = jnp.where(i * tm < N_CTX_TOK, x_refs[0][...].reshape(tm, D_MODEL), x_refs[1][...].reshape(tm, D_MODEL))
    else:
        x = x_refs[0][...]
    h = _norm_mod(x, gain_ref[...], mod_ref[0, 0:1, :], mod_ref[0, 1:2, :])
    o_ref[...] = jnp.dot(h.astype(BF16), wb_ref[...], preferred_element_type=F32).astype(o_ref.dtype)


def _split_row_specs(tm, tile_of):
    nctx = N_CTX_TOK // tm
    per_seq = L_LAT // tm

    def ctx_map(*ids):
        return (jnp.minimum(tile_of(*ids), nctx - 1), 0, 0)

    def lat_map(*ids):
        t = jnp.maximum(tile_of(*ids) - nctx, 0)
        return (t // per_seq, t % per_seq, 0)

    return [pl.BlockSpec((tm // L_CTX, L_CTX, D_MODEL), ctx_map), pl.BlockSpec((1, tm, D_MODEL), lat_map)]


def _in_proj(xs, gain, mod, w, tn, out_dtype):
    n = w.shape[1]
    tm = IN_PROJ_TILE
    if len(xs) == 2:
        x_specs = _split_row_specs(tm, lambda j, i: i)
    else:
        x_specs = [pl.BlockSpec((tm, D_MODEL), lambda j, i: (i, 0))]
    return pl.pallas_call(
        functools.partial(_in_proj_kernel, tm=tm),
        grid=(n // tn, N_TOK // tm),
        in_specs=x_specs + [
            pl.BlockSpec((1, D_MODEL), lambda j, i: (0, 0)),
            pl.BlockSpec((1, 6, D_MODEL), lambda j, i: (_cond_of_tile(i, tm), 0, 0)),
            pl.BlockSpec((D_MODEL, tn), lambda j, i: (0, j)),
        ],
        out_specs=pl.BlockSpec((tm, tn), lambda j, i: (i, j)),
        out_shape=jax.ShapeDtypeStruct((N_TOK, n), out_dtype),
        scratch_shapes=[pltpu.VMEM((D_MODEL, tn), BF16)],
        compiler_params=_vmem(48),
        name="in_proj",
    )(*xs, gain.reshape(1, D_MODEL), mod, w)


def _dft_kernel(c1_ref, s1_ref, c0_ref, s0_ref, sign_row_ref, sign_col_ref, c_ref, s_ref, st_ref, *, lo, nk1):
    r = pl.program_id(0)
    c0, s0 = c0_ref[...], s0_ref[...]
    for a in range(nk1):
        c1 = c1_ref[a:a + 1, :]
        s1 = s1_ref[a:a + 1, :]
        rows = slice(a * lo, (a + 1) * lo)
        c_ref[rows, :] = (c1 * c0 - s1 * s0).astype(BF16)
        sin = s1 * c0 + c1 * s0
        first_row = jnp.logical_and(r == 0, lax.broadcasted_iota(jnp.int32, sin.shape, 0) + a * lo == 0)
        first_col = lax.broadcasted_iota(jnp.int32, sin.shape, 1) == 0
        s_ref[rows, :] = jnp.where(first_row, sign_row_ref[...], sin).astype(BF16)
        st_ref[rows, :] = jnp.where(first_col, sign_col_ref[rows, :], sin).astype(BF16)


def _dft_matrices(length):
    n = 2 * length
    lo = 16
    nk1 = 8
    block = lo * nk1
    s = jnp.arange(length, dtype=jnp.int32)
    k1 = jnp.arange(length // lo, dtype=jnp.int32) * lo
    k0 = jnp.arange(lo, dtype=jnp.int32)
    ang1 = (2.0 * math.pi / n) * ((k1[:, None] * s[None, :]) % n).astype(F32)
    ang0 = (2.0 * math.pi / n) * ((k0[:, None] * s[None, :]) % n).astype(F32)
    sign = jnp.where(s % 2 == 0, 1.0, -1.0).astype(F32)
    table = lambda rows: pl.BlockSpec((rows, length), lambda r: (r, 0))
    whole = lambda shape: pl.BlockSpec(shape, lambda r: (0, 0))
    return pl.pallas_call(
        functools.partial(_dft_kernel, lo=lo, nk1=nk1),
        grid=(length // block,),
        in_specs=[table(nk1), table(nk1), whole((lo, length)), whole((lo, length)), whole((1, length)),
                  pl.BlockSpec((block, 1), lambda r: (r, 0))],
        out_specs=[table(block)] * 3,
        out_shape=[jax.ShapeDtypeStruct((length, length), BF16)] * 3,
        name=f"dft_tables_{length}",
    )(jnp.cos(ang1), jnp.sin(ang1), jnp.cos(ang0), jnp.sin(ang0), sign[None, :], sign[:, None])


def _filter_features(length):
    t = jnp.linspace(0.0, 1.0, length, dtype=F32)[:, None]
    w = 2.0 * math.pi * jnp.arange(length, dtype=F32)[:, None] / length
    f = jnp.linspace(1e-4, HY_BANDS - 1, HY_BANDS, dtype=F32)[None, :]
    z = jnp.concatenate([t, jnp.cos(f * w), -jnp.sin(f * w)], axis=-1)
    return jnp.pad(z, ((0, 0), (0, HY_FEAT_PAD - z.shape[1])))


def _filter_kernel(z_ref, w1_ref, b1_ref, fr_ref, w2_ref, b2_ref, w3_ref, dl_ref, c_ref, s_ref,
                   tc_ref, ts_ref, taps_ref, *, length, rb):
    r = pl.program_id(0)

    @pl.when(r == 0)
    def _():
        z = z_ref[...]
        fr = fr_ref[...]
        h = jnp.sin(fr * (_dot3(z, w1_ref[...]) + b1_ref[...]))
        h = jnp.sin(fr * (_dot3(h, w2_ref[...]) + b2_ref[...]))
        h = _dot3(h, w3_ref[...])
        win = jnp.exp(-z[:, 0:1] * jnp.abs(dl_ref[...]))
        hf = h[:, :HY_WIDTH] * win
        hb = h[:, HY_WIDTH:] * win
        row = lax.broadcasted_iota(jnp.int32, (length, HY_WIDTH), 0)
        hb = jnp.where(row == 0, 0.0, hb)
        l1 = jnp.sum(jnp.abs(hf), axis=0, keepdims=True) + jnp.sum(jnp.abs(hb), axis=0, keepdims=True)
        inv = 1.0 / l1
        taps_ref[:, :HY_WIDTH] = (hf * inv).astype(BF16)
        taps_ref[:, HY_WIDTH:] = (hb * inv).astype(BF16)

    taps = taps_ref[...]
    rc = jnp.dot(c_ref[...], taps, preferred_element_type=F32)
    rs = jnp.dot(s_ref[...], taps, preferred_element_type=F32)
    tc = rc[:, :HY_WIDTH] + rc[:, HY_WIDTH:]
    ts = rs[:, :HY_WIDTH] - rs[:, HY_WIDTH:]
    grow = r * rb + lax.broadcasted_iota(jnp.int32, (rb, HY_WIDTH), 0)
    is0 = grow == 0
    ts = jnp.where(is0, rs[:, :HY_WIDTH] + rs[:, HY_WIDTH:], ts)
    wgt = jnp.where(is0, 1.0 / (2 * length), 2.0 / (2 * length))
    tc_ref[...] = tc * wgt
    ts_ref[...] = ts * wgt


def _hyena_filter(length, feats, cmat, smat, w1, b1, freq, w2, b2, w3, deltas):
    rb = min(length, 512)
    w1p = jnp.pad(w1, ((0, HY_FEAT_PAD - w1.shape[0]), (0, 0)))
    full = lambda shape: pl.BlockSpec(shape, lambda r: (0,) * len(shape))
    hid = HY_FILTER_HIDDEN
    return pl.pallas_call(
        functools.partial(_filter_kernel, length=length, rb=rb),
        grid=(length // rb,),
        in_specs=[
            full((length, HY_FEAT_PAD)), full((HY_FEAT_PAD, hid)), full((1, hid)), full((1, hid)),
            full((hid, hid)), full((1, hid)), full((hid, 2 * HY_WIDTH)), full((1, HY_WIDTH)),
            pl.BlockSpec((rb, length), lambda r: (r, 0)),
            pl.BlockSpec((rb, length), lambda r: (r, 0)),
        ],
        out_specs=[pl.BlockSpec((rb, HY_WIDTH), lambda r: (r, 0)),
                   pl.BlockSpec((rb, HY_WIDTH), lambda r: (r, 0))],
        out_shape=[jax.ShapeDtypeStruct((length, HY_WIDTH), F32)] * 2,
        scratch_shapes=[pltpu.VMEM((length, 2 * HY_WIDTH), BF16)],
        compiler_params=_vmem(48),
        name=f"hyena_filter_{length}",
    )(feats, w1p, b1.reshape(1, hid), freq.reshape(1, hid), w2, b2.reshape(1, hid), w3,
      deltas.reshape(1, HY_WIDTH), cmat, smat)


def _hyena_kernel(x0_ref, x1_ref, v_ref, w0_ref, w1_ref, w2_ref, b0_ref, b1_ref, b2_ref, skip_ref,
                  tc_ref, ts_ref, c_ref, s_ref, ct_ref, st_ref, o_ref,
                  zb_ref, x0c_ref, acc_ref, *, nseq, length, cb, fb):
    f = pl.program_id(0)
    nf = pl.num_programs(0)
    cbi = pl.program_id(1)

    def short_conv(u, w_ref, b_ref):
        row = lax.broadcasted_iota(jnp.int32, u.shape, 0)
        prev = jnp.where(row == 0, 0.0, pltpu.roll(u, 1, 0))
        nxt = jnp.where(row == length - 1, 0.0, pltpu.roll(u, length - 1, 0))
        return prev * w_ref[0:1, :] + u * w_ref[1:2, :] + nxt * w_ref[2:3, :] + b_ref[...]

    @pl.when(f == 0)
    def _():
        for b in range(nseq):
            cols = slice(b * cb, (b + 1) * cb)
            x0c_ref[cbi, :, cols] = short_conv(x0_ref[b], w0_ref, b0_ref).astype(BF16)
            z = short_conv(x1_ref[b], w1_ref, b1_ref) * short_conv(v_ref[b], w2_ref, b2_ref)
            zb_ref[cbi, :, cols] = z.astype(BF16)
        acc_ref[cbi] = jnp.zeros(acc_ref.shape[1:], F32)

    zb = zb_ref[cbi]
    zc = jnp.dot(c_ref[...], zb, preferred_element_type=F32)
    zsn = jnp.dot(s_ref[...], zb, preferred_element_type=F32)
    tc = jnp.concatenate([tc_ref[...]] * nseq, axis=1)
    ts = jnp.concatenate([ts_ref[...]] * nseq, axis=1)
    grow = f * fb + lax.broadcasted_iota(jnp.int32, zc.shape, 0)
    is0 = grow == 0
    yc = jnp.where(is0, zc * tc, zc * tc - zsn * ts)
    ys = jnp.where(is0, zsn * ts, zc * ts + zsn * tc)
    acc_ref[cbi] += (jnp.dot(ct_ref[...], yc.astype(BF16), preferred_element_type=F32)
                     + jnp.dot(st_ref[...], ys.astype(BF16), preferred_element_type=F32))

    @pl.when(f == nf - 1)
    def _():
        for b in range(nseq):
            cols = slice(b * cb, (b + 1) * cb)
            z = zb_ref[cbi, :, cols].astype(F32)
            o_ref[b] = (x0c_ref[cbi, :, cols].astype(F32) * (acc_ref[cbi, :, cols] + z * skip_ref[...])).astype(o_ref.dtype)


def _hyena(u, first_seq_block, nseq, length, conv_w, conv_b, skip, tc, ts, cmat, smat, stmat):
    cb = 128
    fb = min(length, 512)
    u3 = u.reshape(N_TOK // length, length, EVEN_IN)
    ncb = HY_WIDTH // cb
    nf = length // fb
    width = nseq * cb
    first = lambda f, c: jnp.where(f == 0, c, ncb - 1)
    last = lambda f, c: jnp.where(f == nf - 1, c, 0)

    def ublock(part):
        return pl.BlockSpec((nseq, length, cb), lambda f, c: (first_seq_block, 0, part * ncb + first(f, c)))

    def wblock(part, rows):
        return pl.BlockSpec((rows, cb), lambda f, c: (0, part * ncb + first(f, c)))

    return pl.pallas_call(
        functools.partial(_hyena_kernel, nseq=nseq, length=length, cb=cb, fb=fb),
        grid=(nf, ncb),
        in_specs=[
            ublock(0), ublock(1), ublock(2),
            wblock(0, 3), wblock(1, 3), wblock(2, 3),
            wblock(0, 1), wblock(1, 1), wblock(2, 1),
            pl.BlockSpec((1, cb), lambda f, c: (0, last(f, c))),
            pl.BlockSpec((fb, cb), lambda f, c: (f, c)),
            pl.BlockSpec((fb, cb), lambda f, c: (f, c)),
            pl.BlockSpec((fb, length), lambda f, c: (f, 0)),
            pl.BlockSpec((fb, length), lambda f, c: (f, 0)),
            pl.BlockSpec((length, fb), lambda f, c: (0, f)),
            pl.BlockSpec((length, fb), lambda f, c: (0, f)),
        ],
        out_specs=pl.BlockSpec((nseq, length, cb), lambda f, c: (0, 0, last(f, c))),
        out_shape=jax.ShapeDtypeStruct((nseq, length, HY_WIDTH), BF16),
        scratch_shapes=[pltpu.VMEM((ncb, length, width), BF16), pltpu.VMEM((ncb, length, width), BF16),
                        pltpu.VMEM((ncb, length, width), F32)],
        compiler_params=_vmem(56),
        name=f"hyena_{length}",
    )(u3, u3, u3, conv_w, conv_w, conv_w, conv_b.reshape(1, -1), conv_b.reshape(1, -1),
      conv_b.reshape(1, -1), skip.reshape(1, HY_WIDTH), tc, ts, cmat, smat, cmat, stmat
      ).reshape(nseq * length, HY_WIDTH)


def _head_rms(x, seg, gain):
    x2 = x * x
    hi = x2.astype(BF16)
    lo = (x2 - hi.astype(F32)).astype(BF16)
    ss = jnp.dot(hi, seg, preferred_element_type=F32) + jnp.dot(lo, seg, preferred_element_type=F32)
    return x * lax.rsqrt(ss * (1.0 / HEAD_DIM) + EPS) * gain


def _rope(x, cos, sin_signed):
    width = x.shape[1]
    lane = lax.broadcasted_iota(jnp.int32, x.shape, 1)
    first = (lane // ROPE_FREQS) % 2 == 0
    partner = jnp.where(first, pltpu.roll(x, width - ROPE_FREQS, 1), pltpu.roll(x, ROPE_FREQS, 1))
    return x * cos + partner * sin_signed


_DOT_NT = (((1,), (1,)), ((), ()))


def _softmax_over_keys(s, sink_row):
    m = jnp.maximum(jnp.max(s, axis=0, keepdims=True), sink_row)
    p = jnp.exp(s - m)
    return p, jnp.sum(p, axis=0, keepdims=True) + jnp.exp(sink_row - m)


def _ctx_attn_kernel(sink_ref, q_ref, k_ref, v_ref, seg_ref, qg_ref, kg_ref, o_ref, ko_ref, vo_ref, *, nseq):
    seg = seg_ref[...]
    for b in range(nseq):
        rows = slice(b * L_CTX, (b + 1) * L_CTX)
        qt = (_head_rms(q_ref[rows, :], seg, qg_ref[...]) * ATT_SCALE).T.astype(BF16)
        kn = _head_rms(k_ref[rows, :], seg[:KV_WIDTH, :KV_WIDTH], kg_ref[...])
        v = v_ref[rows, :]
        ko_ref[rows, :] = kn
        vo_ref[rows, :] = v
        kb = kn.astype(BF16)
        vt = v.T.astype(BF16)
        blocks = []
        for j in range(ATT_GROUP):
            outs = []
            for g in range(ATT_KV_HEADS):
                h = g * ATT_GROUP + j
                cols = slice(g * HEAD_DIM, (g + 1) * HEAD_DIM)
                s = jnp.dot(kb[:, cols], qt[h * HEAD_DIM:(h + 1) * HEAD_DIM, :], preferred_element_type=F32)
                p, den = _softmax_over_keys(s, jnp.full((1, L_CTX), sink_ref[h], F32))
                outs.append(jnp.dot(vt[cols, :], p.astype(BF16), preferred_element_type=F32) / den)
            blocks.append(jnp.concatenate(outs, axis=0).T)
        o_ref[rows, :] = jnp.concatenate(blocks, axis=1).astype(o_ref.dtype)


def _ctx_attention(u, seg, q_gain, k_gain, sink):
    qcol = 3 * HY_WIDTH // ATT_WIDTH
    kcol = (3 * HY_WIDTH + ATT_WIDTH) // KV_WIDTH
    nseq = 2
    rows = nseq * L_CTX
    return pl.pallas_call(
        functools.partial(_ctx_attn_kernel, nseq=nseq),
        grid_spec=pltpu.PrefetchScalarGridSpec(
            num_scalar_prefetch=1,
            grid=(N_CTX_SEQ // nseq,),
            in_specs=[
                pl.BlockSpec((rows, ATT_WIDTH), lambda b, s: (b, qcol)),
                pl.BlockSpec((rows, KV_WIDTH), lambda b, s: (b, kcol)),
                pl.BlockSpec((rows, KV_WIDTH), lambda b, s: (b, kcol + 1)),
                pl.BlockSpec((ATT_WIDTH, ATT_WIDTH), lambda b, s: (0, 0)),
                pl.BlockSpec((1, ATT_WIDTH), lambda b, s: (0, 0)),
                pl.BlockSpec((1, KV_WIDTH), lambda b, s: (0, 0)),
            ],
            out_specs=[
                pl.BlockSpec((rows, ATT_WIDTH), lambda b, s: (b, 0)),
                pl.BlockSpec((rows, KV_WIDTH), lambda b, s: (b, 0)),
                pl.BlockSpec((rows, KV_WIDTH), lambda b, s: (b, 0)),
            ],
        ),
        out_shape=[jax.ShapeDtypeStruct((N_CTX_TOK, ATT_WIDTH), BF16),
                   jax.ShapeDtypeStruct((N_CTX_TOK, KV_WIDTH), F32),
                   jax.ShapeDtypeStruct((N_CTX_TOK, KV_WIDTH), F32)],
        name="ctx_attention",
    )(sink, u, u, u, seg, jnp.tile(q_gain, ATT_HEADS).reshape(1, ATT_WIDTH),
      jnp.tile(k_gain, ATT_KV_HEADS).reshape(1, KV_WIDTH))


def _rope_tables():
    pos = jnp.arange(L_LAT, dtype=jnp.int32)
    row = (pos // GRID_W).astype(F32)
    col = (pos % GRID_W).astype(F32)
    inv = ROPE_THETA ** (-jnp.arange(ROPE_FREQS, dtype=F32) / ROPE_FREQS)
    ar, ac = row[:, None] * inv, col[:, None] * inv
    cos = jnp.concatenate([jnp.cos(ar), jnp.cos(ar), jnp.cos(ac), jnp.cos(ac)], axis=-1)
    sin = jnp.concatenate([-jnp.sin(ar), jnp.sin(ar), -jnp.sin(ac), jnp.sin(ac)], axis=-1)
    return jnp.tile(cos, (1, ATT_HEADS)), jnp.tile(sin, (1, ATT_HEADS))


def _lat_attn_kernel(sink_ref, q_ref, k_ref, v_ref, ck_ref, cv_ref, cosq_ref, sinq_ref, cosk_ref, sink_k_ref,
                     seg_ref, qg_ref, kg_ref, o_ref, kn_ref, *, nq):
    n = pl.program_id(1)
    seg = seg_ref[...]

    @pl.when(n == 0)
    def _():
        kn = _head_rms(k_ref[...], seg[:KV_WIDTH, :KV_WIDTH], kg_ref[...])
        kn_ref[...] = _rope(kn, cosk_ref[...], sink_k_ref[...]).astype(BF16)

    k_ctx = ck_ref[0].astype(BF16)
    vt_ctx = cv_ref[0].T.astype(BF16)
    span = 3 * ATT_BLOCK
    lanes = ATT_GROUP * ATT_BLOCK
    for j in range(nq):
        blk = n * nq + j
        qrows = slice(j * ATT_BLOCK, (j + 1) * ATT_BLOCK)
        qn = _head_rms(q_ref[qrows, :], seg, qg_ref[...])
        qt = (_rope(qn, cosq_ref[qrows, :], sinq_ref[qrows, :]) * ATT_SCALE).T.astype(BF16)
        start = pl.multiple_of(jnp.clip((blk - 1) * ATT_BLOCK, 0, L_LAT - span), ATT_BLOCK)
        q_pos = blk * ATT_BLOCK + (lax.broadcasted_iota(jnp.int32, (span, lanes), 1) % ATT_BLOCK)
        k_pos = start + lax.broadcasted_iota(jnp.int32, (span, lanes), 0)
        valid = jnp.abs(q_pos - k_pos) <= WINDOW
        k_loc = kn_ref[pl.ds(start, span), :]
        vt_loc = v_ref[pl.ds(start, span), :].T.astype(BF16)
        outs = []
        for g in range(ATT_KV_HEADS):
            cols = slice(g * HEAD_DIM, (g + 1) * HEAD_DIM)
            heads = range(g * ATT_GROUP, (g + 1) * ATT_GROUP)
            q = jnp.concatenate([qt[h * HEAD_DIM:(h + 1) * HEAD_DIM, :] for h in heads], axis=1)
            sink = jnp.concatenate([jnp.full((1, ATT_BLOCK), sink_ref[h], F32) for h in heads], axis=1)
            s_loc = jnp.where(valid, jnp.dot(k_loc[:, cols], q, preferred_element_type=F32), NEG_BIG)
            s_ctx = jnp.dot(k_ctx[:, cols], q, preferred_element_type=F32)
            m = jnp.maximum(jnp.maximum(jnp.max(s_loc, axis=0, keepdims=True),
                                        jnp.max(s_ctx, axis=0, keepdims=True)), sink)
            p_loc = jnp.exp(s_loc - m)
            p_ctx = jnp.exp(s_ctx - m)
            den = (jnp.sum(p_loc, axis=0, keepdims=True) + jnp.sum(p_ctx, axis=0, keepdims=True)
                   + jnp.exp(sink - m))
            o = (jnp.dot(vt_loc[cols, :], p_loc.astype(BF16), preferred_element_type=F32)
                 + jnp.dot(vt_ctx[cols, :], p_ctx.astype(BF16), preferred_element_type=F32))
            outs.append(o / den)
        both = jnp.concatenate(outs, axis=0)
        o_ref[qrows, :] = jnp.concatenate(
            [both[:, i * ATT_BLOCK:(i + 1) * ATT_BLOCK].T for i in range(ATT_GROUP)], axis=1).astype(o_ref.dtype)


def _lat_attention(u, cache_k, cache_v, seg, q_gain, k_gain, sink):
    qcol = 3 * HY_WIDTH // ATT_WIDTH
    kcol = (3 * HY_WIDTH + ATT_WIDTH) // KV_WIDTH
    nq = 2
    qrows = nq * ATT_BLOCK
    nblk = L_LAT // qrows
    first_q_block = N_CTX_TOK // qrows
    first_seq = N_CTX_TOK // L_LAT
    cos, sin = _rope_tables()
    return pl.pallas_call(
        functools.partial(_lat_attn_kernel, nq=nq),
        grid_spec=pltpu.PrefetchScalarGridSpec(
            num_scalar_prefetch=1,
            grid=(N_LAT_SEQ, nblk),
            in_specs=[
                pl.BlockSpec((qrows, ATT_WIDTH), lambda b, n, s: (first_q_block + b * nblk + n, qcol)),
                pl.BlockSpec((L_LAT, KV_WIDTH), lambda b, n, s: (first_seq + b, kcol)),
                pl.BlockSpec((L_LAT, KV_WIDTH), lambda b, n, s: (first_seq + b, kcol + 1)),
                pl.BlockSpec((1, PAST_LEN, KV_WIDTH), lambda b, n, s: (b, 0, 0)),
                pl.BlockSpec((1, PAST_LEN, KV_WIDTH), lambda b, n, s: (b, 0, 0)),
                pl.BlockSpec((qrows, ATT_WIDTH), lambda b, n, s: (n, 0)),
                pl.BlockSpec((qrows, ATT_WIDTH), lambda b, n, s: (n, 0)),
                pl.BlockSpec((L_LAT, KV_WIDTH), lambda b, n, s: (0, 0)),
                pl.BlockSpec((L_LAT, KV_WIDTH), lambda b, n, s: (0, 0)),
                pl.BlockSpec((ATT_WIDTH, ATT_WIDTH), lambda b, n, s: (0, 0)),
                pl.BlockSpec((1, ATT_WIDTH), lambda b, n, s: (0, 0)),
                pl.BlockSpec((1, KV_WIDTH), lambda b, n, s: (0, 0)),
            ],
            out_specs=pl.BlockSpec((qrows, ATT_WIDTH), lambda b, n, s: (b * nblk + n, 0)),
            scratch_shapes=[pltpu.VMEM((L_LAT, KV_WIDTH), BF16)],
        ),
        out_shape=jax.ShapeDtypeStruct((N_LAT_SEQ * L_LAT, ATT_WIDTH), BF16),
        name="lat_attention",
    )(sink, u, u, u, cache_k.reshape(N_LAT_SEQ, PAST_LEN, KV_WIDTH), cache_v.reshape(N_LAT_SEQ, PAST_LEN, KV_WIDTH),
      cos, sin, cos[:, :KV_WIDTH], sin[:, :KV_WIDTH], seg,
      jnp.tile(q_gain, ATT_HEADS).reshape(1, ATT_WIDTH), jnp.tile(k_gain, ATT_KV_HEADS).reshape(1, KV_WIDTH))


def _ceil_to(v, m):
    return ((v + (m - 1)) // m) * m


def _post_mixer(x, mix, refs, tm):
    (w_ref, gain_ref, mod_ref, rwh_ref, rwm_ref, rb_ref, tri_ref, low_ref,
     x1_ref, sw_ref, cnt_ref, xg_ref, wb_ref) = refs

    @pl.when(pl.program_id(0) == 0)
    def _():
        wb_ref[...] = w_ref[...].astype(BF16)

    y = jnp.dot(mix, wb_ref[...], preferred_element_type=F32)
    x1 = x + mod_ref[0, 2:3, :] * y
    x1_ref[...] = x1
    xt = _norm_mod(x1, gain_ref[...], mod_ref[0, 3:4, :], mod_ref[0, 4:5, :])
    xh = xt.astype(BF16)
    xm = (xt - xh.astype(F32)).astype(BF16)
    logits = (lax.dot_general(rwh_ref[...], xh, _DOT_NT, preferred_element_type=F32)
              + lax.dot_general(rwm_ref[...], xh, _DOT_NT, preferred_element_type=F32)
              + lax.dot_general(rwh_ref[...], xm, _DOT_NT, preferred_element_type=F32)) + rb_ref[...]
    expert = lax.broadcasted_iota(jnp.int32, logits.shape, 0)
    vals, hits = [], []
    for _ in range(TOP_K):
        m = jnp.max(logits, axis=0, keepdims=True)
        sel = jnp.min(jnp.where(logits == m, expert, N_EXPERTS), axis=0, keepdims=True)
        vals.append(m)
        hits.append(expert == sel)
        logits = jnp.where(expert == sel, -jnp.inf, logits)
    es = [jnp.exp(v - vals[0]) for v in vals]
    den = es[0] + es[1] + es[2] + es[3]
    weights = [e / den for e in es]

    routed = sum(h.astype(F32) for h in hits)
    counts = jnp.sum(routed, axis=1, keepdims=True)
    cnt_ref[0] = counts.astype(jnp.int32)
    units = (_ceil_to(counts.astype(jnp.int32), MOE_GROUP) // MOE_GROUP).astype(F32)
    g0 = MOE_GROUP * jnp.dot(low_ref[...], jnp.broadcast_to(units, (N_EXPERTS, LANES)).astype(BF16),
                             preferred_element_type=F32)[:, 0:1]
    earlier = jnp.dot(routed.astype(BF16), tri_ref[...], preferred_element_type=F32)
    row_of = g0 + earlier
    slots = [jnp.sum(jnp.where(h, row_of, 0.0), axis=0, keepdims=True) for h in hits]

    prow = lax.broadcasted_iota(jnp.int32, (LANES, tm), 0)
    packed = jnp.zeros((LANES, tm), F32)
    for k, vec in enumerate(slots + weights):
        packed = jnp.where(prow == k, vec, packed)
    sw_ref[...] = packed.T

    group_row = lax.broadcasted_iota(jnp.int32, (MOE_GROUPED_ROWS, tm), 0).astype(F32)
    perm = jnp.zeros((MOE_GROUPED_ROWS, tm), F32)
    for k in range(TOP_K):
        perm = jnp.where(group_row == slots[k], 1.0, perm)
    perm = perm.astype(BF16)
    xg_ref[...] = jnp.dot(perm, xh, preferred_element_type=F32).astype(BF16)


def _post_even_kernel(xc_ref, xl_ref, ac_ref, al_ref, tc_ref, tl_ref, *refs, tm):
    is_ctx = pl.program_id(0) * tm < N_CTX_TOK
    mix = jnp.concatenate([jnp.where(is_ctx, ac_ref[...], al_ref[...]),
                           jnp.where(is_ctx, tc_ref[...], tl_ref[...])], axis=1)
    x = jnp.where(is_ctx, xc_ref[...].reshape(tm, D_MODEL), xl_ref[...].reshape(tm, D_MODEL))
    _post_mixer(x, mix, refs, tm)


def _post_odd_kernel(x_ref, yc_ref, yl_ref, *refs, tm):
    mix = jnp.where(pl.program_id(0) * tm < N_CTX_TOK, yc_ref[...], yl_ref[...])
    _post_mixer(x_ref[...], mix, refs, tm)


def _post_call(kernel_fn, name, mixer_specs, mixer_args, k_in, w_out, gain, mod, router_w, router_b):
    tm = ROW_TILE
    nsteps = N_TOK // tm
    row = lax.broadcasted_iota(jnp.int32, (tm, tm), 0)
    col = lax.broadcasted_iota(jnp.int32, (tm, tm), 1)
    tri = (row < col).astype(BF16)
    er = lax.broadcasted_iota(jnp.int32, (N_EXPERTS, N_EXPERTS), 0)
    ec = lax.broadcasted_iota(jnp.int32, (N_EXPERTS, N_EXPERTS), 1)
    low = (ec < er).astype(BF16)
    rw_t = router_w.T
    rw_hi = rw_t.astype(BF16)
    rw_mid = (rw_t - rw_hi.astype(F32)).astype(BF16)
    const = lambda shape: pl.BlockSpec(shape, lambda i: (0,) * len(shape))
    return pl.pallas_call(
        functools.partial(kernel_fn, tm=tm),
        grid=(nsteps,),
        in_specs=mixer_specs + [
            const((k_in, D_MODEL)),
            const((1, D_MODEL)),
            pl.BlockSpec((1, 6, D_MODEL), lambda i: (_cond_of_tile(i, tm), 0, 0)),
            const((N_EXPERTS, D_MODEL)), const((N_EXPERTS, D_MODEL)), const((N_EXPERTS, 1)),
            const((tm, tm)), const((N_EXPERTS, N_EXPERTS)),
        ],
        out_specs=[
            pl.BlockSpec((tm, D_MODEL), lambda i: (i, 0)),
            pl.BlockSpec((tm, LANES), lambda i: (i, 0)),
            pl.BlockSpec((1, N_EXPERTS, 1), lambda i: (i, 0, 0)),
            pl.BlockSpec((MOE_GROUPED_ROWS, D_MODEL), lambda i: (i, 0)),
        ],
        out_shape=[
            jax.ShapeDtypeStruct((N_TOK, D_MODEL), F32),
            jax.ShapeDtypeStruct((N_TOK, LANES), F32),
            jax.ShapeDtypeStruct((nsteps, N_EXPERTS, 1), jnp.int32),
            jax.ShapeDtypeStruct((nsteps * MOE_GROUPED_ROWS, D_MODEL), BF16),
        ],
        scratch_shapes=[pltpu.VMEM((k_in, D_MODEL), BF16)],
        compiler_params=_vmem(56),
        name=name,
    )(*mixer_args, w_out, gain.reshape(1, D_MODEL), mod, rw_hi, rw_mid, router_b.reshape(N_EXPERTS, 1), tri, low)


def _post_even(x_ctx, x_lat, a_ctx, a_lat, t_ctx, t_lat, w_out, gain, mod, router_w, router_b):
    tm = ROW_TILE
    nctx = N_CTX_TOK // tm
    ctx_map = lambda i: (jnp.minimum(i, nctx - 1), 0)
    lat_map = lambda i: (jnp.maximum(i - nctx, 0), 0)
    specs = _split_row_specs(tm, lambda i: i) + [
             pl.BlockSpec((tm, HY_WIDTH), ctx_map), pl.BlockSpec((tm, HY_WIDTH), lat_map),
             pl.BlockSpec((tm, ATT_WIDTH), ctx_map), pl.BlockSpec((tm, ATT_WIDTH), lat_map)]
    return _post_call(_post_even_kernel, "post_even", specs, (x_ctx, x_lat, a_ctx, a_lat, t_ctx, t_lat),
                      HY_WIDTH + ATT_WIDTH, w_out, gain, mod, router_w, router_b)


def _post_odd(x, y_ctx, y_lat, w_out, gain, mod, router_w, router_b):
    tm = ROW_TILE
    nctx = N_CTX_TOK // tm
    specs = [pl.BlockSpec((tm, D_MODEL), lambda i: (i, 0)),
             pl.BlockSpec((tm, RET_V_WIDTH), lambda i: (jnp.minimum(i, nctx - 1), 0)),
             pl.BlockSpec((tm, RET_V_WIDTH), lambda i: (jnp.maximum(i - nctx, 0), 0))]
    return _post_call(_post_odd_kernel, "post_odd", specs, (x, y_ctx, y_lat),
                      RET_V_WIDTH, w_out, gain, mod, router_w, router_b)


def _moe_plan(counts):
    t, grp, big = MOE_TILE, MOE_GROUP, MOE_GROUPED_ROWS
    n16 = _ceil_to(counts.reshape(-1, N_EXPERTS), grp)
    nsteps = n16.shape[0]
    g0 = jnp.cumsum(n16, axis=1) - n16
    e0 = jnp.cumsum(n16, axis=0) - n16
    rows_e = jnp.sum(n16, axis=0)
    ntiles = (rows_e + t - 1) // t
    tile_end = jnp.cumsum(ntiles)
    tile_first = tile_end - ntiles
    total = tile_end[-1]
    experts = jnp.arange(N_EXPERTS, dtype=jnp.int32)
    tiles = jnp.arange(MOE_NUM_TILES, dtype=jnp.int32)
    valid = tiles < total
    ti = jnp.minimum(tiles, total - 1)
    e_of = jnp.sum((tile_end[None, :] <= ti[:, None]).astype(jnp.int32), axis=1)
    onehot = (e_of[:, None] == experts[None, :]).astype(jnp.int32)
    pick = lambda v: jnp.sum(onehot * v[None, :], axis=1)
    pick2 = lambda m: jnp.sum(onehot[:, None, :] * m[None, :, :], axis=2)
    r = ((ti - pick(tile_first)) * t)[:, None] + grp * jnp.arange(t // grp, dtype=jnp.int32)[None, :]
    ends = pick2(e0 + n16)
    step = jnp.sum((ends[:, None, :] <= r[:, :, None]).astype(jnp.int32), axis=2)
    step = jnp.minimum(step, nsteps - 1)
    sel = (step[:, :, None] == jnp.arange(nsteps, dtype=jnp.int32)[None, None, :]).astype(jnp.int32)
    at_step = lambda m: jnp.sum(sel * pick2(m)[:, None, :], axis=2)
    src = step * big + at_step(g0) + r - at_step(e0)
    live = jnp.logical_and(valid[:, None], r < pick(rows_e)[:, None])
    moe_src = jnp.where(live, src, 0).reshape(-1)
    g = grp * jnp.arange(big // grp, dtype=jnp.int32)
    gend = g0 + n16
    ce = jnp.sum((gend[:, None, :] <= g[None, :, None]).astype(jnp.int32), axis=2)
    used = ce < N_EXPERTS
    ce = jnp.minimum(ce, N_EXPERTS - 1)
    csel = (ce[:, :, None] == experts[None, None, :]).astype(jnp.int32)
    of_e = lambda m: jnp.sum(csel * m[:, None, :], axis=2)
    base = jnp.sum(csel * (tile_first * t)[None, None, :], axis=2)
    csrc = base + of_e(e0) + g[None, :] - of_e(g0)
    comb_src = jnp.where(used, csrc, 0).reshape(-1)
    used_e = ntiles > 0
    rank = jnp.cumsum(used_e.astype(jnp.int32)) - 1
    later = jnp.logical_and(used_e[None, :], experts[None, :] > experts[:, None])
    next_e = jnp.min(jnp.where(later, experts[None, :], N_EXPERTS), axis=1)
    next_e = jnp.where(next_e == N_EXPERTS, -1, next_e)
    half = jnp.logical_and(valid, pick(rows_e) - (ti - pick(tile_first)) * t <= t // 2)
    wbuf = jnp.stack([pick(rank) % 2, pick(next_e), half.astype(jnp.int32)], axis=1).reshape(-1)
    as_i32 = lambda v: v.astype(jnp.int32)
    return as_i32(e_of), as_i32(valid), as_i32(wbuf), as_i32(moe_src), as_i32(comb_src)


def _moe_kernel(te_ref, tv_ref, nx_ref, src_ref, xg_hbm, w1_hbm, b1_ref, w2_hbm, b2_ref, y_ref,
                xbuf, w1f, w2f, w1b, w2b, sem_in, sem_w):
    i = pl.program_id(0)
    nt = pl.num_programs(0)
    t, grp = MOE_TILE, MOE_GROUP
    slot = i % 2

    def issue_gather(tile, sl):
        for c in range(t // grp):
            src = pl.multiple_of(src_ref[tile * (t // grp) + c], grp)
            pltpu.make_async_copy(xg_hbm.at[pl.ds(src, grp), :], xbuf.at[sl, pl.ds(c * grp, grp), :],
                                  sem_in.at[sl]).start()

    def weight_copies(e, ws):
        return (pltpu.make_async_copy(w1_hbm.at[e], w1f.at[ws], sem_w.at[ws]),
                pltpu.make_async_copy(w2_hbm.at[e], w2f.at[ws], sem_w.at[ws]))

    def valid(tile):
        return tv_ref[jnp.clip(tile, 0, nt - 1)] > 0

    @pl.when(i == 0)
    def _():
        issue_gather(0, 0)
        for cp in weight_copies(te_ref[0], 0):
            cp.start()

    @pl.when(valid(i))
    def _():
        pltpu.make_async_copy(xg_hbm.at[pl.ds(0, t), :], xbuf.at[slot], sem_in.at[slot]).wait()

        @pl.when(jnp.logical_and(i + 1 < nt, valid(i + 1)))
        def _():
            issue_gather(i + 1, 1 - slot)

        e = te_ref[i]
        first = jnp.logical_or(i == 0, e != te_ref[jnp.maximum(i - 1, 0)])
        ws = nx_ref[3 * i]
        nxt = nx_ref[3 * i + 1]
        half = nx_ref[3 * i + 2] > 0

        @pl.when(first)
        def _():
            for cp in weight_copies(e, ws):
                cp.wait()
            w1b[...] = w1f[ws].astype(BF16)
            w2b[...] = w2f[ws].astype(BF16)

            @pl.when(nxt >= 0)
            def _():
                for cp in weight_copies(nxt, 1 - ws):
                    cp.start()

        def expert_mlp(rows):
            h = jnp.dot(xbuf[slot, :rows], w1b[...], preferred_element_type=F32) + b1_ref[0]
            glu = jnp.minimum(h[:, :D_FF], SWIGLU_LIMIT)
            lin = jnp.clip(h[:, D_FF:], -SWIGLU_LIMIT, SWIGLU_LIMIT)
            act = (glu * _sigmoid(SWIGLU_ALPHA * glu) * (lin + 1.0)).astype(BF16)
            y = jnp.dot(act, w2b[...], preferred_element_type=F32) + b2_ref[0]
            y_ref[:rows, :] = y.astype(y_ref.dtype)

        @pl.when(jnp.logical_not(half))
        def _():
            expert_mlp(t)

        @pl.when(half)
        def _():
            expert_mlp(t // 2)
            y_ref[t // 2:, :] = jnp.zeros((t // 2, D_MODEL), y_ref.dtype)

    @pl.when(jnp.logical_not(valid(i)))
    def _():
        y_ref[...] = jnp.zeros_like(y_ref)


def _moe_experts(xg, plan, w1, b1, w2, b2):
    te, tv, nx, src, _ = plan
    t = MOE_TILE
    return pl.pallas_call(
        _moe_kernel,
        grid_spec=pltpu.PrefetchScalarGridSpec(
            num_scalar_prefetch=4,
            grid=(MOE_NUM_TILES,),
            in_specs=[
                pl.BlockSpec(memory_space=pl.ANY),
                pl.BlockSpec(memory_space=pl.ANY),
                pl.BlockSpec((1, 1, 2 * D_FF), lambda i, te, *_: (te[i], 0, 0)),
                pl.BlockSpec(memory_space=pl.ANY),
                pl.BlockSpec((1, 1, D_MODEL), lambda i, te, *_: (te[i], 0, 0)),
            ],
            out_specs=pl.BlockSpec((t, D_MODEL), lambda i, *_: (i, 0)),
            scratch_shapes=[
                pltpu.VMEM((2, t, D_MODEL), BF16),
                pltpu.VMEM((2, D_MODEL, 2 * D_FF), F32),
                pltpu.VMEM((2, D_FF, D_MODEL), F32),
                pltpu.VMEM((D_MODEL, 2 * D_FF), BF16),
                pltpu.VMEM((D_FF, D_MODEL), BF16),
                pltpu.SemaphoreType.DMA((2,)),
                pltpu.SemaphoreType.DMA((2,)),
            ],
        ),
        out_shape=jax.ShapeDtypeStruct((MOE_NUM_TILES * t, D_MODEL), BF16),
        compiler_params=_vmem(56),
        name="moe_experts",
    )(te, tv, nx, src, xg, w1, b1.reshape(N_EXPERTS, 1, 2 * D_FF), w2, b2.reshape(N_EXPERTS, 1, D_MODEL))


def _combine_value(src_ref, x1_ref, sw_ref, mod_ref, ys_hbm, ybuf, sem, tm):
    s = pl.program_id(0)
    nsteps = pl.num_programs(0)
    slot = s % 2
    grp, big = MOE_GROUP, MOE_GROUPED_ROWS
    nchunk = big // grp

    def issue_gather(step, sl):
        for c in range(nchunk):
            src = pl.multiple_of(src_ref[step * nchunk + c], grp)
            pltpu.make_async_copy(ys_hbm.at[pl.ds(src, grp), :], ybuf.at[sl, pl.ds(c * grp, grp), :],
                                  sem.at[sl]).start()

    @pl.when(s == 0)
    def _():
        issue_gather(0, 0)

    pltpu.make_async_copy(ys_hbm.at[pl.ds(0, big), :], ybuf.at[slot], sem.at[slot]).wait()

    @pl.when(s + 1 < nsteps)
    def _():
        issue_gather(s + 1, 1 - slot)

    sw = sw_ref[...]
    col = lax.broadcasted_iota(jnp.int32, (tm, big), 1).astype(F32)
    wmat = jnp.zeros((tm, big), F32)
    for k in range(TOP_K):
        wmat = jnp.where(col == sw[:, k:k + 1], sw[:, TOP_K + k:TOP_K + k + 1], wmat)
    moe = jnp.dot(wmat.astype(BF16), ybuf[slot], preferred_element_type=F32)
    return x1_ref[...] + mod_ref[0, 5:6, :] * moe


def _combine_kernel(src_ref, x1_ref, sw_ref, mod_ref, ys_hbm, o_ref, ybuf, sem, *, tm):
    o_ref[...] = _combine_value(src_ref, x1_ref, sw_ref, mod_ref, ys_hbm, ybuf, sem, tm)


def _combine_split_kernel(src_ref, x1_ref, sw_ref, mod_ref, ys_hbm, oc_ref, ol_ref, ybuf, sem, *, tm):
    val = _combine_value(src_ref, x1_ref, sw_ref, mod_ref, ys_hbm, ybuf, sem, tm)
    is_ctx = pl.program_id(0) * tm < N_CTX_TOK

    @pl.when(is_ctx)
    def _():
        oc_ref[...] = val.reshape(oc_ref.shape)

    @pl.when(jnp.logical_not(is_ctx))
    def _():
        ol_ref[...] = val.reshape(ol_ref.shape)


def _combine(x1, ys, sw, plan, mod, split):
    tm = ROW_TILE
    in_specs = [
        pl.BlockSpec((tm, D_MODEL), lambda i, *_: (i, 0)),
        pl.BlockSpec((tm, LANES), lambda i, *_: (i, 0)),
        pl.BlockSpec((1, 6, D_MODEL), lambda i, *_: (_cond_of_tile(i, tm), 0, 0)),
        pl.BlockSpec(memory_space=pl.ANY),
    ]
    scratch = [pltpu.VMEM((2, MOE_GROUPED_ROWS, D_MODEL), BF16), pltpu.SemaphoreType.DMA((2,))]
    if not split:
        kernel_fn, name = _combine_kernel, "moe_combine"
        out_specs = pl.BlockSpec((tm, D_MODEL), lambda i, *_: (i, 0))
        out_shape = jax.ShapeDtypeStruct((N_TOK, D_MODEL), F32)
    else:
        kernel_fn, name = _combine_split_kernel, "moe_combine_split"
        out_specs = _split_row_specs(tm, lambda i, *_: i)
        out_shape = [jax.ShapeDtypeStruct((N_CTX_SEQ, L_CTX, D_MODEL), F32),
                     jax.ShapeDtypeStruct((N_LAT_SEQ, L_LAT, D_MODEL), F32)]
    return pl.pallas_call(
        functools.partial(kernel_fn, tm=tm),
        grid_spec=pltpu.PrefetchScalarGridSpec(
            num_scalar_prefetch=1, grid=(N_TOK // tm,), in_specs=in_specs, out_specs=out_specs,
            scratch_shapes=scratch),
        out_shape=out_shape,
        compiler_params=_vmem(56),
        name=name,
    )(plan[4], x1, sw, mod, ys)


def _retention_kernel(lg_ref, q_ref, k_ref, v_ref, gf_ref, gb_ref, *rest, length, nh, has_s0, emit_state):
    rest = list(rest)
    s0_ref = rest.pop(0) if has_s0 else None
    o_ref = rest.pop(0)
    so_ref = rest.pop(0) if emit_state else None
    s_ref, yf_ref, yb_ref = rest
    c = RET_CHUNK
    nc = length // c
    ii = lax.broadcasted_iota(jnp.int32, (c, c), 0).astype(F32)
    jj = lax.broadcasted_iota(jnp.int32, (c, c), 1).astype(F32)
    ci = lax.broadcasted_iota(jnp.int32, (c, 1), 0).astype(F32)

    def decays(direction, hh):
        lg = -jnp.exp(lg_ref[direction, hh])
        lg1 = lg[:, 0:1]
        if direction == 0:
            diff = ii - jj
            q_decay = jnp.exp(lg1 * (ci + 1.0))
            k_decay = jnp.exp(lg1 * (c - 1.0 - ci))
        else:
            diff = jj - ii
            q_decay = jnp.exp(lg1 * (c - ci))
            k_decay = jnp.exp(lg1 * ci)
        scale = RET_DK ** -0.5
        inner = jnp.where(diff >= 0, jnp.exp(lg * jnp.maximum(diff, 0.0)), 0.0) * scale
        return inner, q_decay, k_decay * scale, jnp.exp(lg1 * float(c))

    def chunk(direction, hh, ch, consts, g_ref, y_ref):
        inner, q_decay, k_decay, chunk_decay = consts
        rows = pl.ds(pl.multiple_of(ch * c, c), c)
        kcols = slice(hh * RET_DK, (hh + 1) * RET_DK)
        vcols = slice(hh * RET_DV, (hh + 1) * RET_DV)
        qc = q_ref[rows, kcols]
        kc = k_ref[rows, kcols]
        vc = v_ref[rows, vcols]
        s = s_ref[direction * nh + hh]
        att = lax.dot_general(qc, kc, _DOT_NT, preferred_element_type=F32) * inner
        o = (jnp.dot(att.astype(BF16), vc, preferred_element_type=F32)
             + jnp.dot(qc, s.astype(BF16), preferred_element_type=F32) * q_decay)
        kd = (kc.astype(F32) * k_decay).T.astype(BF16)
        s_ref[direction * nh + hh] = s * chunk_decay + jnp.dot(kd, vc, preferred_element_type=F32)
        on = o * lax.rsqrt(jnp.mean(o * o, axis=-1, keepdims=True) + EPS)
        g = g_ref[rows, vcols].astype(F32)
        y_ref[rows, vcols] = g * _sigmoid(g) * on

    for direction in range(2):
        for hh in range(nh):
            if has_s0:
                s_ref[direction * nh + hh] = s0_ref[0, direction, hh]
            else:
                s_ref[direction * nh + hh] = jnp.zeros((RET_DK, RET_DV), F32)
    consts = [[decays(direction, hh) for hh in range(nh)] for direction in range(2)]

    def body(step, carry):
        for hh in range(nh):
            chunk(0, hh, step, consts[0][hh], gf_ref, yf_ref)
            chunk(1, hh, nc - 1 - step, consts[1][hh], gb_ref, yb_ref)
        return carry

    lax.fori_loop(0, nc, body, 0)
    o_ref[...] = (yf_ref[...] + yb_ref[...]).astype(o_ref.dtype)
    if emit_state:
        for direction in range(2):
            for hh in range(nh):
                so_ref[0, direction, hh] = s_ref[direction * nh + hh]


def _retention(u, first_seq, nseq, length, nh, decay_logit, s0, emit_state):
    row0 = first_seq
    lg = jnp.broadcast_to(decay_logit.astype(F32)[:, :, None, None], (2, RET_HEADS, 1, LANES))
    kcol = RET_QK_WIDTH // (nh * RET_DK)
    vcol = 2 * RET_QK_WIDTH // (nh * RET_DV)
    gfcol = vcol + RET_HEADS // nh
    gbcol = gfcol + RET_HEADS // nh
    in_specs = [
        pl.BlockSpec((2, nh, 1, LANES), lambda b, h: (0, h, 0, 0)),
        pl.BlockSpec((length, nh * RET_DK), lambda b, h: (row0 + b, h)),
        pl.BlockSpec((length, nh * RET_DK), lambda b, h: (row0 + b, kcol + h)),
        pl.BlockSpec((length, nh * RET_DV), lambda b, h: (row0 + b, vcol + h)),
        pl.BlockSpec((length, nh * RET_DV), lambda b, h: (row0 + b, gfcol + h)),
        pl.BlockSpec((length, nh * RET_DV), lambda b, h: (row0 + b, gbcol + h)),
    ]
    args = [lg, u, u, u, u, u]
    state_spec = pl.BlockSpec((1, 2, nh, RET_DK, RET_DV), lambda b, h: (b, 0, h, 0, 0))
    if s0 is not None:
        in_specs.append(state_spec)
        args.append(s0)
    out_specs = [pl.BlockSpec((length, nh * RET_DV), lambda b, h: (b, h))]
    out_shape = [jax.ShapeDtypeStruct((nseq * length, RET_V_WIDTH), BF16)]
    if emit_state:
        out_specs.append(state_spec)
        out_shape.append(jax.ShapeDtypeStruct((nseq, 2, RET_HEADS, RET_DK, RET_DV), F32))
    return pl.pallas_call(
        functools.partial(_retention_kernel, length=length, nh=nh, has_s0=s0 is not None, emit_state=emit_state),
        grid=(nseq, RET_HEADS // nh),
        in_specs=in_specs,
        out_specs=out_specs,
        out_shape=out_shape,
        scratch_shapes=[pltpu.VMEM((2 * nh, RET_DK, RET_DV), F32), pltpu.VMEM((length, nh * RET_DV), F32),
                        pltpu.VMEM((length, nh * RET_DV), F32)],
        compiler_params=_vmem(48),
        name=f"retention_{length}",
    )(*args)


def kernel(x_prompt, x_sample, cache_k0, cache_v0, state_ret1, c, c_ctx, l0_norm_mix, l0_ada_w, l0_ada_b, l0_w_in, l0_conv_w, l0_conv_b, l0_filt_w1, l0_filt_b1, l0_filt_freq, l0_filt_w2, l0_filt_b2, l0_filt_w3, l0_filt_deltas, l0_hy_skip, l0_q_gain, l0_k_gain, l0_sink, l0_w_out, l0_norm_ffn, l0_router_w, l0_router_b, l0_moe_w1, l0_moe_b1, l0_moe_w2, l0_moe_b2, l1_norm_mix, l1_ada_w, l1_ada_b, l1_w_in, l1_ret_decay_logit, l1_w_out, l1_norm_ffn, l1_router_w, l1_router_b, l1_moe_w1, l1_moe_b1, l1_moe_w2, l1_moe_b2):
    cond = jnp.zeros((SUBLANES, D_MODEL), F32).at[0].set(c_ctx).at[1:1 + N_LAT_SEQ].set(c)
    mod0 = _adaln(cond, l0_ada_w, l0_ada_b)
    mod1 = _adaln(cond, l1_ada_w, l1_ada_b)

    u = _in_proj((x_prompt, x_sample), l0_norm_mix, mod0, l0_w_in, EVEN_IN // 2, F32)
    filt = (l0_filt_w1, l0_filt_b1, l0_filt_freq, l0_filt_w2, l0_filt_b2, l0_filt_w3, l0_filt_deltas)
    hy = []
    for first_block, nseq, length in ((0, N_CTX_SEQ, L_CTX), (N_CTX_TOK // L_LAT // N_LAT_SEQ, N_LAT_SEQ, L_LAT)):
        cmat, smat, stmat = _dft_matrices(length)
        tc, ts = _hyena_filter(length, _filter_features(length), cmat, smat, *filt)
        hy.append(_hyena(u, first_block, nseq, length, l0_conv_w, l0_conv_b, l0_hy_skip, tc, ts, cmat, smat, stmat))
    head = lax.broadcasted_iota(jnp.int32, (ATT_WIDTH, ATT_WIDTH), 0) // HEAD_DIM
    seg = (head == head.T).astype(BF16)
    att_ctx, new_k, new_v = _ctx_attention(u, seg, l0_q_gain, l0_k_gain, l0_sink)
    att_lat = _lat_attention(u, cache_k0, cache_v0, seg, l0_q_gain, l0_k_gain, l0_sink)
    w_att = l0_w_out[HY_WIDTH:].reshape(ATT_KV_HEADS, ATT_GROUP, HEAD_DIM, D_MODEL).swapaxes(0, 1)
    w_out0 = jnp.concatenate([l0_w_out[:HY_WIDTH], w_att.reshape(ATT_WIDTH, D_MODEL)], axis=0)
    x1, sw, counts, xg = _post_even(x_prompt, x_sample, hy[0], hy[1], att_ctx, att_lat, w_out0, l0_norm_ffn, mod0,
                                    l0_router_w, l0_router_b)
    plan = _moe_plan(counts)
    ys = _moe_experts(xg, plan, l0_moe_w1, l0_moe_b1, l0_moe_w2, l0_moe_b2)
    x = _combine(x1, ys, sw, plan, mod0, split=False)

    u = _in_proj((x,), l1_norm_mix, mod1, l1_w_in, 2048, BF16)
    y_ctx, new_state = _retention(u, 0, N_CTX_SEQ, L_CTX, 2, l1_ret_decay_logit, None, True)
    (y_lat,) = _retention(u, N_CTX_TOK // L_LAT, N_LAT_SEQ, L_LAT, 1, l1_ret_decay_logit, state_ret1, False)
    x1, sw, counts, xg = _post_odd(x, y_ctx, y_lat, l1_w_out, l1_norm_ffn, mod1, l1_router_w, l1_router_b)
    plan = _moe_plan(counts)
    ys = _moe_experts(xg, plan, l1_moe_w1, l1_moe_b1, l1_moe_w2, l1_moe_b2)
    y_prompt, y_sample = _combine(x1, ys, sw, plan, mod1, split=True)

    return (y_prompt, y_sample,
            new_k.reshape(N_CTX_SEQ, L_CTX, ATT_KV_HEADS, HEAD_DIM),
            new_v.reshape(N_CTX_SEQ, L_CTX, ATT_KV_HEADS, HEAD_DIM), new_state)
```

```python
import functools
import math

import jax
import jax.numpy as jnp
from jax import lax
from jax.experimental import pallas as pl
from jax.experimental.pallas import tpu as pltpu

F32 = jnp.float32
BF16 = jnp.bfloat16

D_MODEL = 1024
N_CTX_SEQ, L_CTX = 16, 256
N_LAT_SEQ, L_LAT = 2, 2048
N_CTX_TOK = N_CTX_SEQ * L_CTX
N_TOK = N_CTX_TOK + N_LAT_SEQ * L_LAT
PAST_LEN = 512
EPS = 1e-6
NEG_BIG = -1e30

HY_WIDTH = 512
HY_BANDS = 16
HY_FILTER_HIDDEN = 64
HY_FEAT_PAD = 64

ATT_HEADS, ATT_KV_HEADS, HEAD_DIM = 8, 2, 64
ATT_GROUP = ATT_HEADS // ATT_KV_HEADS
ATT_WIDTH = ATT_HEADS * HEAD_DIM
KV_WIDTH = ATT_KV_HEADS * HEAD_DIM
ATT_SCALE = HEAD_DIM ** -0.5
WINDOW = 128
ATT_BLOCK = 128
ROPE_THETA = 10000.0
ROPE_FREQS = HEAD_DIM // 4
GRID_W = 64
EVEN_IN = 3 * HY_WIDTH + ATT_WIDTH + 2 * KV_WIDTH

RET_HEADS = 4
RET_DK = 256
RET_DV = 512
RET_CHUNK = 128
RET_QK_WIDTH = RET_HEADS * RET_DK
RET_V_WIDTH = RET_HEADS * RET_DV
ODD_IN = 2 * RET_QK_WIDTH + 3 * RET_V_WIDTH

N_EXPERTS = 32
TOP_K = 4
D_FF = 1024
SWIGLU_ALPHA = 1.702
SWIGLU_LIMIT = 7.0

SUBLANES = 8
LANES = 128

MOE_TILE = 512
MOE_GROUP = 16
ROW_TILE = 512
IN_PROJ_TILE = 1024
MOE_GROUPED_ROWS = 2560
MOE_NUM_TILES = 112


def _vmem(mib):
    return pltpu.CompilerParams(vmem_limit_bytes=mib * 1024 * 1024)


def _cond_of_tile(i, tm):
    row = i * tm
    return jnp.where(row < N_CTX_TOK, 0, 1 + (row - N_CTX_TOK) // L_LAT)


def _sigmoid(x):
    return 1.0 / (1.0 + jnp.exp(-x))


def _split_bf16(a):
    hi = a.astype(BF16)
    return hi, (a - hi.astype(F32)).astype(BF16)


def _dot3(a, b):
    a_hi, a_mid = _split_bf16(a)
    b_hi, b_mid = _split_bf16(b)
    return (jnp.dot(a_hi, b_hi, preferred_element_type=F32) + jnp.dot(a_hi, b_mid, preferred_element_type=F32)
            + jnp.dot(a_mid, b_hi, preferred_element_type=F32))


def _norm_mod(x, gain, shift, scale):
    ms = jnp.mean(x * x, axis=-1, keepdims=True)
    return (x * lax.rsqrt(ms + EPS) * gain) * (1.0 + scale) + shift


def _adaln_kernel(c_ref, w_ref, b_ref, o_ref):
    c = c_ref[...]
    s = c * _sigmoid(c)
    o_ref[...] = _dot3(s, w_ref[...]) + b_ref[...]


def _adaln(cond, w, b):
    n = w.shape[1]
    tn = 1024
    out = pl.pallas_call(
        _adaln_kernel,
        grid=(n // tn,),
        in_specs=[
            pl.BlockSpec((SUBLANES, D_MODEL), lambda j: (0, 0)),
            pl.BlockSpec((D_MODEL, tn), lambda j: (0, j)),
            pl.BlockSpec((1, tn), lambda j: (0, j)),
        ],
        out_specs=pl.BlockSpec((SUBLANES, tn), lambda j: (0, j)),
        out_shape=jax.ShapeDtypeStruct((SUBLANES, n), F32),
        name="adaln",
    )(cond, w, b.reshape(1, n))
    return out.reshape(SUBLANES, 6, D_MODEL)


def _in_proj_kernel(*refs, tm):
    *x_refs, gain_ref, mod_ref, w_ref, o_ref, wb_ref = refs
    i = pl.program_id(1)

    @pl.when(i == 0)
    def _():
        wb_ref[...] = w_ref[...].astype(BF16)

    if len(x_refs) == 2:
        x = jnp.where(i * tm < N_CTX_TOK, x_refs[0][...], x_refs[1][...])
    else:
        x = x_refs[0][...]
    h = _norm_mod(x, gain_ref[...], mod_ref[0, 0:1, :], mod_ref[0, 1:2, :])
    o_ref[...] = jnp.dot(h.astype(BF16), wb_ref[...], preferred_element_type=F32).astype(o_ref.dtype)


def _in_proj(xs, gain, mod, w, tn, out_dtype):
    n = w.shape[1]
    tm = IN_PROJ_TILE
    nctx = N_CTX_TOK // tm
    if len(xs) == 2:
        x_specs = [pl.BlockSpec((tm, D_MODEL), lambda j, i: (jnp.minimum(i, nctx - 1), 0)),
                   pl.BlockSpec((tm, D_MODEL), lambda j, i: (jnp.maximum(i - nctx, 0), 0))]
    else:
        x_specs = [pl.BlockSpec((tm, D_MODEL), lambda j, i: (i, 0))]
    return pl.pallas_call(
        functools.partial(_in_proj_kernel, tm=tm),
        grid=(n // tn, N_TOK // tm),
        in_specs=x_specs + [
            pl.BlockSpec((1, D_MODEL), lambda j, i: (0, 0)),
            pl.BlockSpec((1, 6, D_MODEL), lambda j, i: (_cond_of_tile(i, tm), 0, 0)),
            pl.BlockSpec((D_MODEL, tn), lambda j, i: (0, j)),
        ],
        out_specs=pl.BlockSpec((tm, tn), lambda j, i: (i, j)),
        out_shape=jax.ShapeDtypeStruct((N_TOK, n), out_dtype),
        scratch_shapes=[pltpu.VMEM((D_MODEL, tn), BF16)],
        compiler_params=_vmem(48),
        name="in_proj",
    )(*xs, gain.reshape(1, D_MODEL), mod, w)


def _dft_matrices(length):
    n = 2 * length
    lo = 16
    s = jnp.arange(length, dtype=jnp.int32)
    k1 = jnp.arange(length // lo, dtype=jnp.int32) * lo
    k0 = jnp.arange(lo, dtype=jnp.int32)
    ang1 = (2.0 * math.pi / n) * ((k1[:, None] * s[None, :]) % n).astype(F32)
    ang0 = (2.0 * math.pi / n) * ((k0[:, None] * s[None, :]) % n).astype(F32)
    c1, s1 = jnp.cos(ang1)[:, None, :], jnp.sin(ang1)[:, None, :]
    c0, s0 = jnp.cos(ang0)[None, :, :], jnp.sin(ang0)[None, :, :]
    cmat = (c1 * c0 - s1 * s0).reshape(length, length)
    smat = (s1 * c0 + c1 * s0).reshape(length, length)
    sign = jnp.where(s % 2 == 0, 1.0, -1.0).astype(F32)
    row = lax.broadcasted_iota(jnp.int32, (length, length), 0)
    col = lax.broadcasted_iota(jnp.int32, (length, length), 1)
    s_nyq = jnp.where(row == 0, sign[None, :], smat)
    st_nyq = jnp.where(col == 0, sign[:, None], smat)
    return cmat.astype(BF16), s_nyq.astype(BF16), st_nyq.astype(BF16)


def _filter_features(length):
    t = jnp.linspace(0.0, 1.0, length, dtype=F32)[:, None]
    w = 2.0 * math.pi * jnp.arange(length, dtype=F32)[:, None] / length
    f = jnp.linspace(1e-4, HY_BANDS - 1, HY_BANDS, dtype=F32)[None, :]
    z = jnp.concatenate([t, jnp.cos(f * w), -jnp.sin(f * w)], axis=-1)
    return jnp.pad(z, ((0, 0), (0, HY_FEAT_PAD - z.shape[1])))


def _filter_kernel(z_ref, w1_ref, b1_ref, fr_ref, w2_ref, b2_ref, w3_ref, dl_ref, c_ref, s_ref,
                   tc_ref, ts_ref, taps_ref, *, length, rb):
    r = pl.program_id(0)

    @pl.when(r == 0)
    def _():
        z = z_ref[...]
        fr = fr_ref[...]
        h = jnp.sin(fr * (_dot3(z, w1_ref[...]) + b1_ref[...]))
        h = jnp.sin(fr * (_dot3(h, w2_ref[...]) + b2_ref[...]))
        h = _dot3(h, w3_ref[...])
        win = jnp.exp(-z[:, 0:1] * jnp.abs(dl_ref[...]))
        hf = h[:, :HY_WIDTH] * win
        hb = h[:, HY_WIDTH:] * win
        row = lax.broadcasted_iota(jnp.int32, (length, HY_WIDTH), 0)
        hb = jnp.where(row == 0, 0.0, hb)
        l1 = jnp.sum(jnp.abs(hf), axis=0, keepdims=True) + jnp.sum(jnp.abs(hb), axis=0, keepdims=True)
        inv = 1.0 / l1
        taps_ref[:, :HY_WIDTH] = (hf * inv).astype(BF16)
        taps_ref[:, HY_WIDTH:] = (hb * inv).astype(BF16)

    taps = taps_ref[...]
    rc = jnp.dot(c_ref[...], taps, preferred_element_type=F32)
    rs = jnp.dot(s_ref[...], taps, preferred_element_type=F32)
    tc = rc[:, :HY_WIDTH] + rc[:, HY_WIDTH:]
    ts = rs[:, :HY_WIDTH] - rs[:, HY_WIDTH:]
    grow = r * rb + lax.broadcasted_iota(jnp.int32, (rb, HY_WIDTH), 0)
    is0 = grow == 0
    ts = jnp.where(is0, rs[:, :HY_WIDTH] + rs[:, HY_WIDTH:], ts)
    wgt = jnp.where(is0, 1.0 / (2 * length), 2.0 / (2 * length))
    tc_ref[...] = tc * wgt
    ts_ref[...] = ts * wgt


def _hyena_filter(length, feats, cmat, smat, w1, b1, freq, w2, b2, w3, deltas):
    rb = min(length, 512)
    w1p = jnp.pad(w1, ((0, HY_FEAT_PAD - w1.shape[0]), (0, 0)))
    full = lambda shape: pl.BlockSpec(shape, lambda r: (0,) * len(shape))
    hid = HY_FILTER_HIDDEN
    return pl.pallas_call(
        functools.partial(_filter_kernel, length=length, rb=rb),
        grid=(length // rb,),
        in_specs=[
            full((length, HY_FEAT_PAD)), full((HY_FEAT_PAD, hid)), full((1, hid)), full((1, hid)),
            full((hid, hid)), full((1, hid)), full((hid, 2 * HY_WIDTH)), full((1, HY_WIDTH)),
            pl.BlockSpec((rb, length), lambda r: (r, 0)),
            pl.BlockSpec((rb, length), lambda r: (r, 0)),
        ],
        out_specs=[pl.BlockSpec((rb, HY_WIDTH), lambda r: (r, 0)),
                   pl.BlockSpec((rb, HY_WIDTH), lambda r: (r, 0))],
        out_shape=[jax.ShapeDtypeStruct((length, HY_WIDTH), F32)] * 2,
        scratch_shapes=[pltpu.VMEM((length, 2 * HY_WIDTH), BF16)],
        compiler_params=_vmem(48),
        name=f"hyena_filter_{length}",
    )(feats, w1p, b1.reshape(1, hid), freq.reshape(1, hid), w2, b2.reshape(1, hid), w3,
      deltas.reshape(1, HY_WIDTH), cmat, smat)


def _hyena_kernel(x0_ref, x1_ref, v_ref, w0_ref, w1_ref, w2_ref, b0_ref, b1_ref, b2_ref, skip_ref,
                  tc_ref, ts_ref, c_ref, s_ref, ct_ref, st_ref, o_ref,
                  zb_ref, zs_ref, x0c_ref, acc_ref, *, nseq, length, cb, fb):
    f = pl.program_id(1)
    nf = pl.num_programs(1)

    def short_conv(u, w_ref, b_ref):
        row = lax.broadcasted_iota(jnp.int32, u.shape, 0)
        prev = jnp.where(row == 0, 0.0, pltpu.roll(u, 1, 0))
        nxt = jnp.where(row == length - 1, 0.0, pltpu.roll(u, length - 1, 0))
        return prev * w_ref[0:1, :] + u * w_ref[1:2, :] + nxt * w_ref[2:3, :] + b_ref[...]

    @pl.when(f == 0)
    def _():
        for b in range(nseq):
            cols = slice(b * cb, (b + 1) * cb)
            x0c_ref[:, cols] = short_conv(x0_ref[b].astype(F32), w0_ref, b0_ref)
            z = (short_conv(x1_ref[b].astype(F32), w1_ref, b1_ref)
                 * short_conv(v_ref[b].astype(F32), w2_ref, b2_ref))
            zb_ref[:, cols] = z.astype(BF16)
            zs_ref[:, cols] = z * skip_ref[...]
        acc_ref[...] = jnp.zeros_like(acc_ref)

    zb = zb_ref[...]
    zc = jnp.dot(c_ref[...], zb, preferred_element_type=F32)
    zsn = jnp.dot(s_ref[...], zb, preferred_element_type=F32)
    tc = jnp.concatenate([tc_ref[...]] * nseq, axis=1)
    ts = jnp.concatenate([ts_ref[...]] * nseq, axis=1)
    grow = f * fb + lax.broadcasted_iota(jnp.int32, zc.shape, 0)
    is0 = grow == 0
    yc = jnp.where(is0, zc * tc, zc * tc - zsn * ts)
    ys = jnp.where(is0, zsn * ts, zc * ts + zsn * tc)
    acc_ref[...] += (jnp.dot(ct_ref[...], yc.astype(BF16), preferred_element_type=F32)
                     + jnp.dot(st_ref[...], ys.astype(BF16), preferred_element_type=F32))

    @pl.when(f == nf - 1)
    def _():
        for b in range(nseq):
            cols = slice(b * cb, (b + 1) * cb)
            o_ref[b] = (x0c_ref[:, cols] * (acc_ref[:, cols] + zs_ref[:, cols])).astype(o_ref.dtype)


def _hyena(u, first_seq_block, nseq, length, conv_w, conv_b, skip, tc, ts, cmat, smat, stmat):
    cb = 128
    fb = min(length, 512)
    u3 = u.reshape(N_TOK // length, length, EVEN_IN)
    ncb = HY_WIDTH // cb
    width = nseq * cb

    def ublock(part):
        return pl.BlockSpec((nseq, length, cb), lambda c, f: (first_seq_block, 0, part * ncb + c))

    def wblock(part, rows):
        return pl.BlockSpec((rows, cb), lambda c, f: (0, part * ncb + c))

    return pl.pallas_call(
        functools.partial(_hyena_kernel, nseq=nseq, length=length, cb=cb, fb=fb),
        grid=(ncb, length // fb),
        in_specs=[
            ublock(0), ublock(1), ublock(2),
            wblock(0, 3), wblock(1, 3), wblock(2, 3),
            wblock(0, 1), wblock(1, 1), wblock(2, 1),
            pl.BlockSpec((1, cb), lambda c, f: (0, c)),
            pl.BlockSpec((fb, cb), lambda c, f: (f, c)),
            pl.BlockSpec((fb, cb), lambda c, f: (f, c)),
            pl.BlockSpec((fb, length), lambda c, f: (f, 0)),
            pl.BlockSpec((fb, length), lambda c, f: (f, 0)),
            pl.BlockSpec((length, fb), lambda c, f: (0, f)),
            pl.BlockSpec((length, fb), lambda c, f: (0, f)),
        ],
        out_specs=pl.BlockSpec((nseq, length, cb), lambda c, f: (0, 0, c)),
        out_shape=jax.ShapeDtypeStruct((nseq, length, HY_WIDTH), BF16),
        scratch_shapes=[pltpu.VMEM((length, width), BF16), pltpu.VMEM((length, width), F32),
                        pltpu.VMEM((length, width), F32), pltpu.VMEM((length, width), F32)],
        compiler_params=_vmem(48),
        name=f"hyena_{length}",
    )(u3, u3, u3, conv_w, conv_w, conv_w, conv_b.reshape(1, -1), conv_b.reshape(1, -1),
      conv_b.reshape(1, -1), skip.reshape(1, HY_WIDTH), tc, ts, cmat, smat, cmat, stmat
      ).reshape(nseq * length, HY_WIDTH)


def _head_rms(x, seg, gain):
    x2 = x * x
    hi = x2.astype(BF16)
    lo = (x2 - hi.astype(F32)).astype(BF16)
    ss = jnp.dot(hi, seg, preferred_element_type=F32) + jnp.dot(lo, seg, preferred_element_type=F32)
    return x * lax.rsqrt(ss * (1.0 / HEAD_DIM) + EPS) * gain


def _rope(x, cos, sin_signed):
    width = x.shape[1]
    lane = lax.broadcasted_iota(jnp.int32, x.shape, 1)
    first = (lane // ROPE_FREQS) % 2 == 0
    partner = jnp.where(first, pltpu.roll(x, width - ROPE_FREQS, 1), pltpu.roll(x, ROPE_FREQS, 1))
    return x * cos + partner * sin_signed


_DOT_NT = (((1,), (1,)), ((), ()))


def _softmax_over_keys(s, sink_row):
    m = jnp.maximum(jnp.max(s, axis=0, keepdims=True), sink_row)
    p = jnp.exp(s - m)
    return p, jnp.sum(p, axis=0, keepdims=True) + jnp.exp(sink_row - m)


def _ctx_attn_kernel(sink_ref, q_ref, k_ref, v_ref, seg_ref, qg_ref, kg_ref, o_ref, ko_ref, vo_ref, *, nseq):
    seg = seg_ref[...]
    for b in range(nseq):
        rows = slice(b * L_CTX, (b + 1) * L_CTX)
        qt = (_head_rms(q_ref[rows, :].astype(F32), seg, qg_ref[...]) * ATT_SCALE).T.astype(BF16)
        kn = _head_rms(k_ref[rows, :].astype(F32), seg[:KV_WIDTH, :KV_WIDTH], kg_ref[...])
        v = v_ref[rows, :].astype(F32)
        ko_ref[rows, :] = kn
        vo_ref[rows, :] = v
        kb = kn.astype(BF16)
        vt = v.T.astype(BF16)
        blocks = []
        for j in range(ATT_GROUP):
            outs = []
            for g in range(ATT_KV_HEADS):
                h = g * ATT_GROUP + j
                cols = slice(g * HEAD_DIM, (g + 1) * HEAD_DIM)
                s = jnp.dot(kb[:, cols], qt[h * HEAD_DIM:(h + 1) * HEAD_DIM, :], preferred_element_type=F32)
                p, den = _softmax_over_keys(s, jnp.full((1, L_CTX), sink_ref[h], F32))
                outs.append(jnp.dot(vt[cols, :], p.astype(BF16), preferred_element_type=F32) / den)
            blocks.append(jnp.concatenate(outs, axis=0).T)
        o_ref[rows, :] = jnp.concatenate(blocks, axis=1).astype(o_ref.dtype)


def _ctx_attention(u, seg, q_gain, k_gain, sink):
    qcol = 3 * HY_WIDTH // ATT_WIDTH
    kcol = (3 * HY_WIDTH + ATT_WIDTH) // KV_WIDTH
    nseq = 2
    rows = nseq * L_CTX
    return pl.pallas_call(
        functools.partial(_ctx_attn_kernel, nseq=nseq),
        grid_spec=pltpu.PrefetchScalarGridSpec(
            num_scalar_prefetch=1,
            grid=(N_CTX_SEQ // nseq,),
            in_specs=[
                pl.BlockSpec((rows, ATT_WIDTH), lambda b, s: (b, qcol)),
                pl.BlockSpec((rows, KV_WIDTH), lambda b, s: (b, kcol)),
                pl.BlockSpec((rows, KV_WIDTH), lambda b, s: (b, kcol + 1)),
                pl.BlockSpec((ATT_WIDTH, ATT_WIDTH), lambda b, s: (0, 0)),
                pl.BlockSpec((1, ATT_WIDTH), lambda b, s: (0, 0)),
                pl.BlockSpec((1, KV_WIDTH), lambda b, s: (0, 0)),
            ],
            out_specs=[
                pl.BlockSpec((rows, ATT_WIDTH), lambda b, s: (b, 0)),
                pl.BlockSpec((rows, KV_WIDTH), lambda b, s: (b, 0)),
                pl.BlockSpec((rows, KV_WIDTH), lambda b, s: (b, 0)),
            ],
        ),
        out_shape=[jax.ShapeDtypeStruct((N_CTX_TOK, ATT_WIDTH), BF16),
                   jax.ShapeDtypeStruct((N_CTX_TOK, KV_WIDTH), F32),
                   jax.ShapeDtypeStruct((N_CTX_TOK, KV_WIDTH), F32)],
        name="ctx_attention",
    )(sink, u, u, u, seg, jnp.tile(q_gain, ATT_HEADS).reshape(1, ATT_WIDTH),
      jnp.tile(k_gain, ATT_KV_HEADS).reshape(1, KV_WIDTH))


def _rope_tables():
    pos = jnp.arange(L_LAT, dtype=jnp.int32)
    row = (pos // GRID_W).astype(F32)
    col = (pos % GRID_W).astype(F32)
    inv = ROPE_THETA ** (-jnp.arange(ROPE_FREQS, dtype=F32) / ROPE_FREQS)
    ar, ac = row[:, None] * inv, col[:, None] * inv
    cos = jnp.concatenate([jnp.cos(ar), jnp.cos(ar), jnp.cos(ac), jnp.cos(ac)], axis=-1)
    sin = jnp.concatenate([-jnp.sin(ar), jnp.sin(ar), -jnp.sin(ac), jnp.sin(ac)], axis=-1)
    return jnp.tile(cos, (1, ATT_HEADS)), jnp.tile(sin, (1, ATT_HEADS))


def _lat_attn_kernel(sink_ref, q_ref, k_ref, v_ref, ck_ref, cv_ref, cosq_ref, sinq_ref, cosk_ref, sink_k_ref,
                     seg_ref, qg_ref, kg_ref, o_ref, kn_ref, *, nq):
    n = pl.program_id(1)
    seg = seg_ref[...]

    @pl.when(n == 0)
    def _():
        kn = _head_rms(k_ref[...].astype(F32), seg[:KV_WIDTH, :KV_WIDTH], kg_ref[...])
        kn_ref[...] = _rope(kn, cosk_ref[...], sink_k_ref[...]).astype(BF16)

    k_ctx = ck_ref[0].astype(BF16)
    vt_ctx = cv_ref[0].T.astype(BF16)
    span = 3 * ATT_BLOCK
    lanes = ATT_GROUP * ATT_BLOCK
    for j in range(nq):
        blk = n * nq + j
        qrows = slice(j * ATT_BLOCK, (j + 1) * ATT_BLOCK)
        qn = _head_rms(q_ref[qrows, :].astype(F32), seg, qg_ref[...])
        qt = (_rope(qn, cosq_ref[qrows, :], sinq_ref[qrows, :]) * ATT_SCALE).T.astype(BF16)
        start = pl.multiple_of(jnp.clip((blk - 1) * ATT_BLOCK, 0, L_LAT - span), ATT_BLOCK)
        q_pos = blk * ATT_BLOCK + (lax.broadcasted_iota(jnp.int32, (span, lanes), 1) % ATT_BLOCK)
        k_pos = start + lax.broadcasted_iota(jnp.int32, (span, lanes), 0)
        valid = jnp.abs(q_pos - k_pos) <= WINDOW
        k_loc = kn_ref[pl.ds(start, span), :]
        vt_loc = v_ref[pl.ds(start, span), :].astype(F32).T.astype(BF16)
        outs = []
        for g in range(ATT_KV_HEADS):
            cols = slice(g * HEAD_DIM, (g + 1) * HEAD_DIM)
            heads = range(g * ATT_GROUP, (g + 1) * ATT_GROUP)
            q = jnp.concatenate([qt[h * HEAD_DIM:(h + 1) * HEAD_DIM, :] for h in heads], axis=1)
            sink = jnp.concatenate([jnp.full((1, ATT_BLOCK), sink_ref[h], F32) for h in heads], axis=1)
            s_loc = jnp.where(valid, jnp.dot(k_loc[:, cols], q, preferred_element_type=F32), NEG_BIG)
            s_ctx = jnp.dot(k_ctx[:, cols], q, preferred_element_type=F32)
            m = jnp.maximum(jnp.maximum(jnp.max(s_loc, axis=0, keepdims=True),
                                        jnp.max(s_ctx, axis=0, keepdims=True)), sink)
            p_loc = jnp.exp(s_loc - m)
            p_ctx = jnp.exp(s_ctx - m)
            den = (jnp.sum(p_loc, axis=0, keepdims=True) + jnp.sum(p_ctx, axis=0, keepdims=True)
                   + jnp.exp(sink - m))
            o = (jnp.dot(vt_loc[cols, :], p_loc.astype(BF16), preferred_element_type=F32)
                 + jnp.dot(vt_ctx[cols, :], p_ctx.astype(BF16), preferred_element_type=F32))
            outs.append(o / den)
        both = jnp.concatenate(outs, axis=0)
        o_ref[qrows, :] = jnp.concatenate(
            [both[:, i * ATT_BLOCK:(i + 1) * ATT_BLOCK].T for i in range(ATT_GROUP)], axis=1).astype(o_ref.dtype)


def _lat_attention(u, cache_k, cache_v, seg, q_gain, k_gain, sink):
    qcol = 3 * HY_WIDTH // ATT_WIDTH
    kcol = (3 * HY_WIDTH + ATT_WIDTH) // KV_WIDTH
    nq = 2
    qrows = nq * ATT_BLOCK
    nblk = L_LAT // qrows
    first_q_block = N_CTX_TOK // qrows
    first_seq = N_CTX_TOK // L_LAT
    cos, sin = _rope_tables()
    return pl.pallas_call(
        functools.partial(_lat_attn_kernel, nq=nq),
        grid_spec=pltpu.PrefetchScalarGridSpec(
            num_scalar_prefetch=1,
            grid=(N_LAT_SEQ, nblk),
            in_specs=[
                pl.BlockSpec((qrows, ATT_WIDTH), lambda b, n, s: (first_q_block + b * nblk + n, qcol)),
                pl.BlockSpec((L_LAT, KV_WIDTH), lambda b, n, s: (first_seq + b, kcol)),
                pl.BlockSpec((L_LAT, KV_WIDTH), lambda b, n, s: (first_seq + b, kcol + 1)),
                pl.BlockSpec((1, PAST_LEN, KV_WIDTH), lambda b, n, s: (b, 0, 0)),
                pl.BlockSpec((1, PAST_LEN, KV_WIDTH), lambda b, n, s: (b, 0, 0)),
                pl.BlockSpec((qrows, ATT_WIDTH), lambda b, n, s: (n, 0)),
                pl.BlockSpec((qrows, ATT_WIDTH), lambda b, n, s: (n, 0)),
                pl.BlockSpec((L_LAT, KV_WIDTH), lambda b, n, s: (0, 0)),
                pl.BlockSpec((L_LAT, KV_WIDTH), lambda b, n, s: (0, 0)),
                pl.BlockSpec((ATT_WIDTH, ATT_WIDTH), lambda b, n, s: (0, 0)),
                pl.BlockSpec((1, ATT_WIDTH), lambda b, n, s: (0, 0)),
                pl.BlockSpec((1, KV_WIDTH), lambda b, n, s: (0, 0)),
            ],
            out_specs=pl.BlockSpec((qrows, ATT_WIDTH), lambda b, n, s: (b * nblk + n, 0)),
            scratch_shapes=[pltpu.VMEM((L_LAT, KV_WIDTH), BF16)],
        ),
        out_shape=jax.ShapeDtypeStruct((N_LAT_SEQ * L_LAT, ATT_WIDTH), BF16),
        name="lat_attention",
    )(sink, u, u, u, cache_k.reshape(N_LAT_SEQ, PAST_LEN, KV_WIDTH), cache_v.reshape(N_LAT_SEQ, PAST_LEN, KV_WIDTH),
      cos, sin, cos[:, :KV_WIDTH], sin[:, :KV_WIDTH], seg,
      jnp.tile(q_gain, ATT_HEADS).reshape(1, ATT_WIDTH), jnp.tile(k_gain, ATT_KV_HEADS).reshape(1, KV_WIDTH))


def _ceil_to(v, m):
    return ((v + (m - 1)) // m) * m


def _post_mixer(x, mix, refs, tm):
    (w_ref, gain_ref, mod_ref, rwh_ref, rwm_ref, rb_ref, tri_ref, low_ref,
     x1_ref, sw_ref, cnt_ref, xg_ref, wb_ref) = refs

    @pl.when(pl.program_id(0) == 0)
    def _():
        wb_ref[...] = w_ref[...].astype(BF16)

    y = jnp.dot(mix, wb_ref[...], preferred_element_type=F32)
    x1 = x + mod_ref[0, 2:3, :] * y
    x1_ref[...] = x1
    xt = _norm_mod(x1, gain_ref[...], mod_ref[0, 3:4, :], mod_ref[0, 4:5, :])
    xh = xt.astype(BF16)
    xm = (xt - xh.astype(F32)).astype(BF16)
    logits = (lax.dot_general(rwh_ref[...], xh, _DOT_NT, preferred_element_type=F32)
              + lax.dot_general(rwm_ref[...], xh, _DOT_NT, preferred_element_type=F32)
              + lax.dot_general(rwh_ref[...], xm, _DOT_NT, preferred_element_type=F32)) + rb_ref[...]
    expert = lax.broadcasted_iota(jnp.int32, logits.shape, 0)
    vals, hits = [], []
    for _ in range(TOP_K):
        m = jnp.max(logits, axis=0, keepdims=True)
        sel = jnp.min(jnp.where(logits == m, expert, N_EXPERTS), axis=0, keepdims=True)
        vals.append(m)
        hits.append(expert == sel)
        logits = jnp.where(expert == sel, -jnp.inf, logits)
    es = [jnp.exp(v - vals[0]) for v in vals]
    den = es[0] + es[1] + es[2] + es[3]
    weights = [e / den for e in es]

    routed = sum(h.astype(F32) for h in hits)
    counts = jnp.sum(routed, axis=1, keepdims=True)
    cnt_ref[0] = counts.astype(jnp.int32)
    units = (_ceil_to(counts.astype(jnp.int32), MOE_GROUP) // MOE_GROUP).astype(F32)
    g0 = MOE_GROUP * jnp.dot(low_ref[...], jnp.broadcast_to(units, (N_EXPERTS, LANES)).astype(BF16),
                             preferred_element_type=F32)[:, 0:1]
    earlier = jnp.dot(routed.astype(BF16), tri_ref[...], preferred_element_type=F32)
    row_of = g0 + earlier
    slots = [jnp.sum(jnp.where(h, row_of, 0.0), axis=0, keepdims=True) for h in hits]

    prow = lax.broadcasted_iota(jnp.int32, (LANES, tm), 0)
    packed = jnp.zeros((LANES, tm), F32)
    for k, vec in enumerate(slots + weights):
        packed = jnp.where(prow == k, vec, packed)
    sw_ref[...] = packed.T

    group_row = lax.broadcasted_iota(jnp.int32, (MOE_GROUPED_ROWS, tm), 0).astype(F32)
    perm = jnp.zeros((MOE_GROUPED_ROWS, tm), F32)
    for k in range(TOP_K):
        perm = jnp.where(group_row == slots[k], 1.0, perm)
    perm = perm.astype(BF16)
    xg_ref[...] = jnp.dot(perm, xh, preferred_element_type=F32).astype(BF16)


def _post_even_kernel(xc_ref, xl_ref, ac_ref, al_ref, tc_ref, tl_ref, *refs, tm):
    is_ctx = pl.program_id(0) * tm < N_CTX_TOK
    mix = jnp.concatenate([jnp.where(is_ctx, ac_ref[...], al_ref[...]),
                           jnp.where(is_ctx, tc_ref[...], tl_ref[...])], axis=1)
    _post_mixer(jnp.where(is_ctx, xc_ref[...], xl_ref[...]), mix, refs, tm)


def _post_odd_kernel(x_ref, yc_ref, yl_ref, *refs, tm):
    mix = jnp.where(pl.program_id(0) * tm < N_CTX_TOK, yc_ref[...], yl_ref[...])
    _post_mixer(x_ref[...], mix, refs, tm)


def _post_call(kernel_fn, name, mixer_specs, mixer_args, k_in, w_out, gain, mod, router_w, router_b):
    tm = ROW_TILE
    nsteps = N_TOK // tm
    row = lax.broadcasted_iota(jnp.int32, (tm, tm), 0)
    col = lax.broadcasted_iota(jnp.int32, (tm, tm), 1)
    tri = (row < col).astype(BF16)
    er = lax.broadcasted_iota(jnp.int32, (N_EXPERTS, N_EXPERTS), 0)
    ec = lax.broadcasted_iota(jnp.int32, (N_EXPERTS, N_EXPERTS), 1)
    low = (ec < er).astype(BF16)
    rw_t = router_w.T
    rw_hi = rw_t.astype(BF16)
    rw_mid = (rw_t - rw_hi.astype(F32)).astype(BF16)
    const = lambda shape: pl.BlockSpec(shape, lambda i: (0,) * len(shape))
    return pl.pallas_call(
        functools.partial(kernel_fn, tm=tm),
        grid=(nsteps,),
        in_specs=mixer_specs + [
            const((k_in, D_MODEL)),
            const((1, D_MODEL)),
            pl.BlockSpec((1, 6, D_MODEL), lambda i: (_cond_of_tile(i, tm), 0, 0)),
            const((N_EXPERTS, D_MODEL)), const((N_EXPERTS, D_MODEL)), const((N_EXPERTS, 1)),
            const((tm, tm)), const((N_EXPERTS, N_EXPERTS)),
        ],
        out_specs=[
            pl.BlockSpec((tm, D_MODEL), lambda i: (i, 0)),
            pl.BlockSpec((tm, LANES), lambda i: (i, 0)),
            pl.BlockSpec((1, N_EXPERTS, 1), lambda i: (i, 0, 0)),
            pl.BlockSpec((MOE_GROUPED_ROWS, D_MODEL), lambda i: (i, 0)),
        ],
        out_shape=[
            jax.ShapeDtypeStruct((N_TOK, D_MODEL), F32),
            jax.ShapeDtypeStruct((N_TOK, LANES), F32),
            jax.ShapeDtypeStruct((nsteps, N_EXPERTS, 1), jnp.int32),
            jax.ShapeDtypeStruct((nsteps * MOE_GROUPED_ROWS, D_MODEL), BF16),
        ],
        scratch_shapes=[pltpu.VMEM((k_in, D_MODEL), BF16)],
        compiler_params=_vmem(56),
        name=name,
    )(*mixer_args, w_out, gain.reshape(1, D_MODEL), mod, rw_hi, rw_mid, router_b.reshape(N_EXPERTS, 1), tri, low)


def _post_even(x_ctx, x_lat, a_ctx, a_lat, t_ctx, t_lat, w_out, gain, mod, router_w, router_b):
    tm = ROW_TILE
    nctx = N_CTX_TOK // tm
    ctx_map = lambda i: (jnp.minimum(i, nctx - 1), 0)
    lat_map = lambda i: (jnp.maximum(i - nctx, 0), 0)
    specs = [pl.BlockSpec((tm, D_MODEL), ctx_map), pl.BlockSpec((tm, D_MODEL), lat_map),
             pl.BlockSpec((tm, HY_WIDTH), ctx_map), pl.BlockSpec((tm, HY_WIDTH), lat_map),
             pl.BlockSpec((tm, ATT_WIDTH), ctx_map), pl.BlockSpec((tm, ATT_WIDTH), lat_map)]
    return _post_call(_post_even_kernel, "post_even", specs, (x_ctx, x_lat, a_ctx, a_lat, t_ctx, t_lat),
                      HY_WIDTH + ATT_WIDTH, w_out, gain, mod, router_w, router_b)


def _post_odd(x, y_ctx, y_lat, w_out, gain, mod, router_w, router_b):
    tm = ROW_TILE
    nctx = N_CTX_TOK // tm
    specs = [pl.BlockSpec((tm, D_MODEL), lambda i: (i, 0)),
             pl.BlockSpec((tm, RET_V_WIDTH), lambda i: (jnp.minimum(i, nctx - 1), 0)),
             pl.BlockSpec((tm, RET_V_WIDTH), lambda i: (jnp.maximum(i - nctx, 0), 0))]
    return _post_call(_post_odd_kernel, "post_odd", specs, (x, y_ctx, y_lat),
                      RET_V_WIDTH, w_out, gain, mod, router_w, router_b)


def _moe_plan(counts):
    t, grp, big = MOE_TILE, MOE_GROUP, MOE_GROUPED_ROWS
    n16 = _ceil_to(counts.reshape(-1, N_EXPERTS), grp)
    nsteps = n16.shape[0]
    g0 = jnp.cumsum(n16, axis=1) - n16
    e0 = jnp.cumsum(n16, axis=0) - n16
    rows_e = jnp.sum(n16, axis=0)
    ntiles = (rows_e + t - 1) // t
    tile_end = jnp.cumsum(ntiles)
    tile_first = tile_end - ntiles
    total = tile_end[-1]
    experts = jnp.arange(N_EXPERTS, dtype=jnp.int32)
    tiles = jnp.arange(MOE_NUM_TILES, dtype=jnp.int32)
    valid = tiles < total
    ti = jnp.minimum(tiles, total - 1)
    e_of = jnp.sum((tile_end[None, :] <= ti[:, None]).astype(jnp.int32), axis=1)
    onehot = (e_of[:, None] == experts[None, :]).astype(jnp.int32)
    pick = lambda v: jnp.sum(onehot * v[None, :], axis=1)
    pick2 = lambda m: jnp.sum(onehot[:, None, :] * m[None, :, :], axis=2)
    r = ((ti - pick(tile_first)) * t)[:, None] + grp * jnp.arange(t // grp, dtype=jnp.int32)[None, :]
    ends = pick2(e0 + n16)
    step = jnp.sum((ends[:, None, :] <= r[:, :, None]).astype(jnp.int32), axis=2)
    step = jnp.minimum(step, nsteps - 1)
    sel = (step[:, :, None] == jnp.arange(nsteps, dtype=jnp.int32)[None, None, :]).astype(jnp.int32)
    at_step = lambda m: jnp.sum(sel * pick2(m)[:, None, :], axis=2)
    src = step * big + at_step(g0) + r - at_step(e0)
    live = jnp.logical_and(valid[:, None], r < pick(rows_e)[:, None])
    moe_src = jnp.where(live, src, 0).reshape(-1)
    g = grp * jnp.arange(big // grp, dtype=jnp.int32)
    gend = g0 + n16
    ce = jnp.sum((gend[:, None, :] <= g[None, :, None]).astype(jnp.int32), axis=2)
    used = ce < N_EXPERTS
    ce = jnp.minimum(ce, N_EXPERTS - 1)
    csel = (ce[:, :, None] == experts[None, None, :]).astype(jnp.int32)
    of_e = lambda m: jnp.sum(csel * m[:, None, :], axis=2)
    base = jnp.sum(csel * (tile_first * t)[None, None, :], axis=2)
    csrc = base + of_e(e0) + g[None, :] - of_e(g0)
    comb_src = jnp.where(used, csrc, 0).reshape(-1)
    used_e = ntiles > 0
    rank = jnp.cumsum(used_e.astype(jnp.int32)) - 1
    later = jnp.logical_and(used_e[None, :], experts[None, :] > experts[:, None])
    next_e = jnp.min(jnp.where(later, experts[None, :], N_EXPERTS), axis=1)
    next_e = jnp.where(next_e == N_EXPERTS, -1, next_e)
    half = jnp.logical_and(valid, pick(rows_e) - (ti - pick(tile_first)) * t <= t // 2)
    wbuf = jnp.stack([pick(rank) % 2, pick(next_e), half.astype(jnp.int32)], axis=1).reshape(-1)
    as_i32 = lambda v: v.astype(jnp.int32)
    return as_i32(e_of), as_i32(valid), as_i32(wbuf), as_i32(moe_src), as_i32(comb_src)


def _moe_kernel(te_ref, tv_ref, nx_ref, src_ref, xg_hbm, w1_hbm, b1_ref, w2_hbm, b2_ref, y_ref,
                xbuf, w1f, w2f, w1b, w2b, sem_in, sem_w):
    i = pl.program_id(0)
    nt = pl.num_programs(0)
    t, grp = MOE_TILE, MOE_GROUP
    slot = i % 2

    def issue_gather(tile, sl):
        for c in range(t // grp):
            src = pl.multiple_of(src_ref[tile * (t // grp) + c], grp)
            pltpu.make_async_copy(xg_hbm.at[pl.ds(src, grp), :], xbuf.at[sl, pl.ds(c * grp, grp), :],
                                  sem_in.at[sl]).start()

    def weight_copies(e, ws):
        return (pltpu.make_async_copy(w1_hbm.at[e], w1f.at[ws], sem_w.at[ws]),
                pltpu.make_async_copy(w2_hbm.at[e], w2f.at[ws], sem_w.at[ws]))

    def valid(tile):
        return tv_ref[jnp.clip(tile, 0, nt - 1)] > 0

    @pl.when(i == 0)
    def _():
        issue_gather(0, 0)
        for cp in weight_copies(te_ref[0], 0):
            cp.start()

    @pl.when(valid(i))
    def _():
        pltpu.make_async_copy(xg_hbm.at[pl.ds(0, t), :], xbuf.at[slot], sem_in.at[slot]).wait()

        @pl.when(jnp.logical_and(i + 1 < nt, valid(i + 1)))
        def _():
            issue_gather(i + 1, 1 - slot)

        e = te_ref[i]
        first = jnp.logical_or(i == 0, e != te_ref[jnp.maximum(i - 1, 0)])
        ws = nx_ref[3 * i]
        nxt = nx_ref[3 * i + 1]
        half = nx_ref[3 * i + 2] > 0

        @pl.when(first)
        def _():
            for cp in weight_copies(e, ws):
                cp.wait()
            w1b[...] = w1f[ws].astype(BF16)
            w2b[...] = w2f[ws].astype(BF16)

            @pl.when(nxt >= 0)
            def _():
                for cp in weight_copies(nxt, 1 - ws):
                    cp.start()

        def expert_mlp(rows):
            h = jnp.dot(xbuf[slot, :rows], w1b[...], preferred_element_type=F32) + b1_ref[0]
            glu = jnp.minimum(h[:, :D_FF], SWIGLU_LIMIT)
            lin = jnp.clip(h[:, D_FF:], -SWIGLU_LIMIT, SWIGLU_LIMIT)
            act = (glu * _sigmoid(SWIGLU_ALPHA * glu) * (lin + 1.0)).astype(BF16)
            y = jnp.dot(act, w2b[...], preferred_element_type=F32) + b2_ref[0]
            y_ref[:rows, :] = y.astype(y_ref.dtype)

        @pl.when(jnp.logical_not(half))
        def _():
            expert_mlp(t)

        @pl.when(half)
        def _():
            expert_mlp(t // 2)
            y_ref[t // 2:, :] = jnp.zeros((t // 2, D_MODEL), y_ref.dtype)

    @pl.when(jnp.logical_not(valid(i)))
    def _():
        y_ref[...] = jnp.zeros_like(y_ref)


def _moe_experts(xg, plan, w1, b1, w2, b2):
    te, tv, nx, src, _ = plan
    t = MOE_TILE
    return pl.pallas_call(
        _moe_kernel,
        grid_spec=pltpu.PrefetchScalarGridSpec(
            num_scalar_prefetch=4,
            grid=(MOE_NUM_TILES,),
            in_specs=[
                pl.BlockSpec(memory_space=pl.ANY),
                pl.BlockSpec(memory_space=pl.ANY),
                pl.BlockSpec((1, 1, 2 * D_FF), lambda i, te, *_: (te[i], 0, 0)),
                pl.BlockSpec(memory_space=pl.ANY),
                pl.BlockSpec((1, 1, D_MODEL), lambda i, te, *_: (te[i], 0, 0)),
            ],
            out_specs=pl.BlockSpec((t, D_MODEL), lambda i, *_: (i, 0)),
            scratch_shapes=[
                pltpu.VMEM((2, t, D_MODEL), BF16),
                pltpu.VMEM((2, D_MODEL, 2 * D_FF), F32),
                pltpu.VMEM((2, D_FF, D_MODEL), F32),
                pltpu.VMEM((D_MODEL, 2 * D_FF), BF16),
                pltpu.VMEM((D_FF, D_MODEL), BF16),
                pltpu.SemaphoreType.DMA((2,)),
                pltpu.SemaphoreType.DMA((2,)),
            ],
        ),
        out_shape=jax.ShapeDtypeStruct((MOE_NUM_TILES * t, D_MODEL), BF16),
        compiler_params=_vmem(56),
        name="moe_experts",
    )(te, tv, nx, src, xg, w1, b1.reshape(N_EXPERTS, 1, 2 * D_FF), w2, b2.reshape(N_EXPERTS, 1, D_MODEL))


def _combine_value(src_ref, x1_ref, sw_ref, mod_ref, ys_hbm, ybuf, sem, tm):
    s = pl.program_id(0)
    nsteps = pl.num_programs(0)
    slot = s % 2
    grp, big = MOE_GROUP, MOE_GROUPED_ROWS
    nchunk = big // grp

    def issue_gather(step, sl):
        for c in range(nchunk):
            src = pl.multiple_of(src_ref[step * nchunk + c], grp)
            pltpu.make_async_copy(ys_hbm.at[pl.ds(src, grp), :], ybuf.at[sl, pl.ds(c * grp, grp), :],
                                  sem.at[sl]).start()

    @pl.when(s == 0)
    def _():
        issue_gather(0, 0)

    pltpu.make_async_copy(ys_hbm.at[pl.ds(0, big), :], ybuf.at[slot], sem.at[slot]).wait()

    @pl.when(s + 1 < nsteps)
    def _():
        issue_gather(s + 1, 1 - slot)

    sw = sw_ref[...]
    col = lax.broadcasted_iota(jnp.int32, (tm, big), 1).astype(F32)
    wmat = jnp.zeros((tm, big), F32)
    for k in range(TOP_K):
        wmat = jnp.where(col == sw[:, k:k + 1], sw[:, TOP_K + k:TOP_K + k + 1], wmat)
    moe = jnp.dot(wmat.astype(BF16), ybuf[slot], preferred_element_type=F32)
    return x1_ref[...] + mod_ref[0, 5:6, :] * moe


def _combine_kernel(src_ref, x1_ref, sw_ref, mod_ref, ys_hbm, o_ref, ybuf, sem, *, tm):
    o_ref[...] = _combine_value(src_ref, x1_ref, sw_ref, mod_ref, ys_hbm, ybuf, sem, tm)


def _combine_split_kernel(src_ref, x1_ref, sw_ref, mod_ref, ys_hbm, oc_ref, ol_ref, ybuf, sem, *, tm):
    val = _combine_value(src_ref, x1_ref, sw_ref, mod_ref, ys_hbm, ybuf, sem, tm)
    is_ctx = pl.program_id(0) * tm < N_CTX_TOK

    @pl.when(is_ctx)
    def _():
        oc_ref[...] = val

    @pl.when(jnp.logical_not(is_ctx))
    def _():
        ol_ref[...] = val


def _combine(x1, ys, sw, plan, mod, split):
    tm = ROW_TILE
    nctx = N_CTX_TOK // tm
    in_specs = [
        pl.BlockSpec((tm, D_MODEL), lambda i, *_: (i, 0)),
        pl.BlockSpec((tm, LANES), lambda i, *_: (i, 0)),
        pl.BlockSpec((1, 6, D_MODEL), lambda i, *_: (_cond_of_tile(i, tm), 0, 0)),
        pl.BlockSpec(memory_space=pl.ANY),
    ]
    scratch = [pltpu.VMEM((2, MOE_GROUPED_ROWS, D_MODEL), BF16), pltpu.SemaphoreType.DMA((2,))]
    if not split:
        kernel_fn, name = _combine_kernel, "moe_combine"
        out_specs = pl.BlockSpec((tm, D_MODEL), lambda i, *_: (i, 0))
        out_shape = jax.ShapeDtypeStruct((N_TOK, D_MODEL), F32)
    else:
        kernel_fn, name = _combine_split_kernel, "moe_combine_split"
        out_specs = [pl.BlockSpec((tm, D_MODEL), lambda i, *_: (jnp.minimum(i, nctx - 1), 0)),
                     pl.BlockSpec((tm, D_MODEL), lambda i, *_: (jnp.maximum(i - nctx, 0), 0))]
        out_shape = [jax.ShapeDtypeStruct((N_CTX_TOK, D_MODEL), F32),
                     jax.ShapeDtypeStruct((N_TOK - N_CTX_TOK, D_MODEL), F32)]
    return pl.pallas_call(
        functools.partial(kernel_fn, tm=tm),
        grid_spec=pltpu.PrefetchScalarGridSpec(
            num_scalar_prefetch=1, grid=(N_TOK // tm,), in_specs=in_specs, out_specs=out_specs,
            scratch_shapes=scratch),
        out_shape=out_shape,
        compiler_params=_vmem(56),
        name=name,
    )(plan[4], x1, sw, mod, ys)


def _retention_kernel(lg_ref, q_ref, k_ref, v_ref, gf_ref, gb_ref, *rest, length, has_s0, emit_state):
    rest = list(rest)
    s0_ref = rest.pop(0) if has_s0 else None
    o_ref = rest.pop(0)
    so_ref = rest.pop(0) if emit_state else None
    s_ref, yf_ref, yb_ref = rest
    c = RET_CHUNK
    nc = length // c
    ii = lax.broadcasted_iota(jnp.int32, (c, c), 0).astype(F32)
    jj = lax.broadcasted_iota(jnp.int32, (c, c), 1).astype(F32)
    ci = lax.broadcasted_iota(jnp.int32, (c, 1), 0).astype(F32)

    def decays(direction):
        lg = -jnp.exp(lg_ref[direction, 0])
        lg1 = lg[:, 0:1]
        if direction == 0:
            diff = ii - jj
            q_decay = jnp.exp(lg1 * (ci + 1.0))
            k_decay = jnp.exp(lg1 * (c - 1.0 - ci))
        else:
            diff = jj - ii
            q_decay = jnp.exp(lg1 * (c - ci))
            k_decay = jnp.exp(lg1 * ci)
        scale = RET_DK ** -0.5
        inner = jnp.where(diff >= 0, jnp.exp(lg * jnp.maximum(diff, 0.0)), 0.0) * scale
        return inner, q_decay, k_decay * scale, jnp.exp(lg1 * float(c))

    def chunk(direction, ch, consts, g_ref, y_ref):
        inner, q_decay, k_decay, chunk_decay = consts
        rows = pl.ds(pl.multiple_of(ch * c, c), c)
        qc = q_ref[rows, :]
        kc = k_ref[rows, :]
        vc = v_ref[rows, :]
        s = s_ref[direction]
        att = lax.dot_general(qc, kc, _DOT_NT, preferred_element_type=F32) * inner
        o = (jnp.dot(att.astype(BF16), vc, preferred_element_type=F32)
             + jnp.dot(qc, s.astype(BF16), preferred_element_type=F32) * q_decay)
        kd = (kc.astype(F32) * k_decay).T.astype(BF16)
        s_ref[direction] = s * chunk_decay + jnp.dot(kd, vc, preferred_element_type=F32)
        on = o * lax.rsqrt(jnp.mean(o * o, axis=-1, keepdims=True) + EPS)
        g = g_ref[rows, :].astype(F32)
        y_ref[rows, :] = g * _sigmoid(g) * on

    for direction in range(2):
        if has_s0:
            s_ref[direction] = s0_ref[0, direction, 0]
        else:
            s_ref[direction] = jnp.zeros((RET_DK, RET_DV), F32)
    forward, backward = decays(0), decays(1)

    def body(step, carry):
        chunk(0, step, forward, gf_ref, yf_ref)
        chunk(1, nc - 1 - step, backward, gb_ref, yb_ref)
        return carry

    lax.fori_loop(0, nc, body, 0)
    o_ref[...] = (yf_ref[...] + yb_ref[...]).astype(o_ref.dtype)
    if emit_state:
        for direction in range(2):
            so_ref[0, direction, 0] = s_ref[direction]


def _retention(u, first_seq, nseq, length, decay_logit, s0, emit_state):
    row0 = first_seq
    lg = jnp.broadcast_to(decay_logit.astype(F32)[:, :, None, None], (2, RET_HEADS, 1, LANES))
    kcol = RET_QK_WIDTH // RET_DK
    vcol = 2 * RET_QK_WIDTH // RET_DV
    gfcol = vcol + RET_HEADS
    gbcol = gfcol + RET_HEADS
    in_specs = [
        pl.BlockSpec((2, 1, 1, LANES), lambda b, h: (0, h, 0, 0)),
        pl.BlockSpec((length, RET_DK), lambda b, h: (row0 + b, h)),
        pl.BlockSpec((length, RET_DK), lambda b, h: (row0 + b, kcol + h)),
        pl.BlockSpec((length, RET_DV), lambda b, h: (row0 + b, vcol + h)),
        pl.BlockSpec((length, RET_DV), lambda b, h: (row0 + b, gfcol + h)),
        pl.BlockSpec((length, RET_DV), lambda b, h: (row0 + b, gbcol + h)),
    ]
    args = [lg, u, u, u, u, u]
    state_spec = pl.BlockSpec((1, 2, 1, RET_DK, RET_DV), lambda b, h: (b, 0, h, 0, 0))
    if s0 is not None:
        in_specs.append(state_spec)
        args.append(s0)
    out_specs = [pl.BlockSpec((length, RET_DV), lambda b, h: (b, h))]
    out_shape = [jax.ShapeDtypeStruct((nseq * length, RET_V_WIDTH), BF16)]
    if emit_state:
        out_specs.append(state_spec)
        out_shape.append(jax.ShapeDtypeStruct((nseq, 2, RET_HEADS, RET_DK, RET_DV), F32))
    return pl.pallas_call(
        functools.partial(_retention_kernel, length=length, has_s0=s0 is not None, emit_state=emit_state),
        grid=(nseq, RET_HEADS),
        in_specs=in_specs,
        out_specs=out_specs,
        out_shape=out_shape,
        scratch_shapes=[pltpu.VMEM((2, RET_DK, RET_DV), F32), pltpu.VMEM((length, RET_DV), F32),
                        pltpu.VMEM((length, RET_DV), F32)],
        compiler_params=_vmem(48),
        name=f"retention_{length}",
    )(*args)


def kernel(x_prompt, x_sample, cache_k0, cache_v0, state_ret1, c, c_ctx, l0_norm_mix, l0_ada_w, l0_ada_b, l0_w_in, l0_conv_w, l0_conv_b, l0_filt_w1, l0_filt_b1, l0_filt_freq, l0_filt_w2, l0_filt_b2, l0_filt_w3, l0_filt_deltas, l0_hy_skip, l0_q_gain, l0_k_gain, l0_sink, l0_w_out, l0_norm_ffn, l0_router_w, l0_router_b, l0_moe_w1, l0_moe_b1, l0_moe_w2, l0_moe_b2, l1_norm_mix, l1_ada_w, l1_ada_b, l1_w_in, l1_ret_decay_logit, l1_w_out, l1_norm_ffn, l1_router_w, l1_router_b, l1_moe_w1, l1_moe_b1, l1_moe_w2, l1_moe_b2):
    x_ctx = x_prompt.reshape(N_CTX_TOK, D_MODEL)
    x_lat = x_sample.reshape(N_TOK - N_CTX_TOK, D_MODEL)
    cond = jnp.zeros((SUBLANES, D_MODEL), F32).at[0].set(c_ctx).at[1:1 + N_LAT_SEQ].set(c)
    mod0 = _adaln(cond, l0_ada_w, l0_ada_b)
    mod1 = _adaln(cond, l1_ada_w, l1_ada_b)

    u = _in_proj((x_ctx, x_lat), l0_norm_mix, mod0, l0_w_in, EVEN_IN // 2, BF16)
    filt = (l0_filt_w1, l0_filt_b1, l0_filt_freq, l0_filt_w2, l0_filt_b2, l0_filt_w3, l0_filt_deltas)
    hy = []
    for first_block, nseq, length in ((0, N_CTX_SEQ, L_CTX), (N_CTX_TOK // L_LAT // N_LAT_SEQ, N_LAT_SEQ, L_LAT)):
        cmat, smat, stmat = _dft_matrices(length)
        tc, ts = _hyena_filter(length, _filter_features(length), cmat, smat, *filt)
        hy.append(_hyena(u, first_block, nseq, length, l0_conv_w, l0_conv_b, l0_hy_skip, tc, ts, cmat, smat, stmat))
    head = lax.broadcasted_iota(jnp.int32, (ATT_WIDTH, ATT_WIDTH), 0) // HEAD_DIM
    seg = (head == head.T).astype(BF16)
    att_ctx, new_k, new_v = _ctx_attention(u, seg, l0_q_gain, l0_k_gain, l0_sink)
    att_lat = _lat_attention(u, cache_k0, cache_v0, seg, l0_q_gain, l0_k_gain, l0_sink)
    w_att = l0_w_out[HY_WIDTH:].reshape(ATT_KV_HEADS, ATT_GROUP, HEAD_DIM, D_MODEL).swapaxes(0, 1)
    w_out0 = jnp.concatenate([l0_w_out[:HY_WIDTH], w_att.reshape(ATT_WIDTH, D_MODEL)], axis=0)
    x1, sw, counts, xg = _post_even(x_ctx, x_lat, hy[0], hy[1], att_ctx, att_lat, w_out0, l0_norm_ffn, mod0,
                                    l0_router_w, l0_router_b)
    plan = _moe_plan(counts)
    ys = _moe_experts(xg, plan, l0_moe_w1, l0_moe_b1, l0_moe_w2, l0_moe_b2)
    x = _combine(x1, ys, sw, plan, mod0, split=False)

    u = _in_proj((x,), l1_norm_mix, mod1, l1_w_in, 2048, BF16)
    y_ctx, new_state = _retention(u, 0, N_CTX_SEQ, L_CTX, l1_ret_decay_logit, None, True)
    (y_lat,) = _retention(u, N_CTX_TOK // L_LAT, N_LAT_SEQ, L_LAT, l1_ret_decay_logit, state_ret1, False)
    x1, sw, counts, xg = _post_odd(x, y_ctx, y_lat, l1_w_out, l1_norm_ffn, mod1, l1_router_w, l1_router_b)
    plan = _moe_plan(counts)
    ys = _moe_experts(xg, plan, l1_moe_w1, l1_moe_b1, l1_moe_w2, l1_moe_b2)
    y_prompt, y_sample = _combine(x1, ys, sw, plan, mod1, split=True)

    return (y_prompt.reshape(N_CTX_SEQ, L_CTX, D_MODEL), y_sample.reshape(N_LAT_SEQ, L_LAT, D_MODEL),
            new_k.reshape(N_CTX_SEQ, L_CTX, ATT_KV_HEADS, HEAD_DIM),
            new_v.reshape(N_CTX_SEQ, L_CTX, ATT_KV_HEADS, HEAD_DIM), new_state)
```
